```python
import math
import jax, jax.numpy as jnp
from jax import lax
import numpy as np

D_MODEL = 1024
BATCH = 8
SEQ = 4096
DEPTH = 4

HEAD_DIM = 64
N_HEADS = D_MODEL // HEAD_DIM
SWA_HEADS = N_HEADS // 4
SWA_KV_HEADS = SWA_HEADS // 2
SWA_WINDOW = 128
FOX_HEADS = N_HEADS // 4
NSA_HEADS = N_HEADS // 2
NSA_KV_HEADS = NSA_HEADS // 4
NSA_CMP_LEN = 32
NSA_CMP_STRIDE = 16
NSA_CMP_HIDDEN = 2 * HEAD_DIM
NSA_SLC_BLOCK = 64
NSA_TOPK = 16
NSA_WINDOW = 512
NSA_QUERY_BLOCK = 64
QUERY_BLOCK = 128
REL_BUCKETS = 32
REL_MAX_DISTANCE = 1024
BIAS_HEADS = SWA_HEADS + NSA_HEADS
FFN_HIDDEN = ((8 * D_MODEL + 3 * 256 - 1) // (3 * 256)) * 256
ADA_CHUNKS = 6
RMS_EPS = 1e-6
NEG_INF = -1e30
FORCE_SELECT = 1e30
SWA_WIDTH = SWA_HEADS * HEAD_DIM
FOX_WIDTH = FOX_HEADS * HEAD_DIM
NSA_WIDTH = NSA_HEADS * HEAD_DIM
NSA_KV_WIDTH = NSA_KV_HEADS * HEAD_DIM
IN_SPLITS = (
    SWA_WIDTH, SWA_KV_HEADS * HEAD_DIM, SWA_KV_HEADS * HEAD_DIM,
    FOX_WIDTH, FOX_WIDTH, FOX_WIDTH, FOX_HEADS,
    NSA_WIDTH,
    NSA_KV_WIDTH, NSA_KV_WIDTH,
    NSA_KV_WIDTH, NSA_KV_WIDTH,
    NSA_KV_WIDTH, NSA_KV_WIDTH,
    NSA_HEADS * 3,
)
N_IN = sum(IN_SPLITS)

kernel_name = "hybrid_swa_fox_nsa_block"


def rms_norm(x, gain):
    x32 = x.astype(jnp.float32)
    y = x32 * lax.rsqrt(jnp.mean(x32 * x32, axis=-1, keepdims=True) + RMS_EPS)
    return (y * gain.astype(jnp.float32)).astype(x.dtype)


def t5_bucket(dist):
    n = jnp.maximum(dist, 0)
    max_exact = REL_BUCKETS // 2
    nf = jnp.maximum(n, 1).astype(jnp.float32)
    large = max_exact + (jnp.log(nf / max_exact) / math.log(REL_MAX_DISTANCE / max_exact)
                         * (REL_BUCKETS - max_exact)).astype(jnp.int32)
    large = jnp.minimum(large, REL_BUCKETS - 1)
    return jnp.where(n < max_exact, n, large)


def split_heads(t, n_heads):
    return t.reshape(t.shape[0], t.shape[1], n_heads, HEAD_DIM)


def banded_attention(q, k, v, window, bias_table, sinks):
    b, s_len, h, d = q.shape
    hkv = k.shape[2]
    g = h // hkv
    n_blk = s_len // QUERY_BLOCK
    n_back = -(-(window - 1) // QUERY_BLOCK)
    pad = n_back * QUERY_BLOCK
    lk = pad + QUERY_BLOCK
    kp = jnp.pad(k, ((0, 0), (pad, 0), (0, 0), (0, 0)))
    vp = jnp.pad(v, ((0, 0), (pad, 0), (0, 0), (0, 0)))
    dist = jnp.arange(QUERY_BLOCK)[:, None] + pad - jnp.arange(lk)[None, :]
    in_window = (dist >= 0) & (dist < window)
    bias = bias_table[t5_bucket(dist)].astype(jnp.float32).transpose(2, 0, 1).reshape(hkv, g, QUERY_BLOCK, lk)
    q_blocks = jnp.moveaxis(q.reshape(b, n_blk, QUERY_BLOCK, hkv, g, d), 1, 0)
    scale = 1.0 / math.sqrt(d)

    def one_block(args):
        i, q_blk = args
        start = i * QUERY_BLOCK
        k_blk = lax.dynamic_slice_in_dim(kp, start, lk, axis=1)
        v_blk = lax.dynamic_slice_in_dim(vp, start, lk, axis=1)
        key_pos = start - pad + jnp.arange(lk)
        valid = in_window & (key_pos >= 0)[None, :]
        logits = jnp.einsum('bqhgd,bkhd->bhgqk', q_blk, k_blk).astype(jnp.float32) * scale + bias
        logits = jnp.where(valid, logits, NEG_INF)
        if sinks is None:
            p = jax.nn.softmax(logits, axis=-1)
        else:
            sink = sinks.astype(jnp.float32).reshape(1, hkv, g, 1, 1)
            m = jnp.maximum(jnp.max(logits, axis=-1, keepdims=True), sink)
            e = jnp.exp(logits - m)
            p = e / (jnp.sum(e, axis=-1, keepdims=True) + jnp.exp(sink - m))
        return jnp.einsum('bhgqk,bkhd->bqhgd', p.astype(v.dtype), v_blk)

    out = lax.map(one_block, (jnp.arange(n_blk), q_blocks))
    return jnp.moveaxis(out, 0, 1).reshape(b, s_len, h * d)


def forgetting_attention(q, k, v, log_f):
    b, s_len, h, d = q.shape
    n_blk = s_len // QUERY_BLOCK
    cum = lax.cumsum(log_f.astype(jnp.float32), axis=1)
    cum_k = cum.transpose(0, 2, 1)
    q_blocks = jnp.moveaxis(q.reshape(b, n_blk, QUERY_BLOCK, h, d), 1, 0)
    cum_q_blocks = jnp.moveaxis(cum_k.reshape(b, h, n_blk, QUERY_BLOCK), 2, 0)
    key_pos = jnp.arange(s_len)
    scale = 1.0 / math.sqrt(d)

    def one_block(args):
        i, q_blk, cq = args
        q_pos = i * QUERY_BLOCK + jnp.arange(QUERY_BLOCK)
        logits = (jnp.einsum('bqhd,bkhd->bhqk', q_blk, k).astype(jnp.float32) * scale
                  + (cq[..., None] - cum_k[:, :, None, :]))
        logits = jnp.where(key_pos[None, :] <= q_pos[:, None], logits, NEG_INF)
        p = jax.nn.softmax(logits, axis=-1)
        return jnp.einsum('bhqk,bkhd->bqhd', p.astype(v.dtype), v)

    out = lax.map(one_block, (jnp.arange(n_blk), q_blocks, cum_q_blocks))
    return jnp.moveaxis(out, 0, 1).reshape(b, s_len, h * d)


def compress_blocks(x, pe, w1, w2):
    b, s_len, hkv, d = x.shape
    n_c = (s_len - NSA_CMP_LEN) // NSA_CMP_STRIDE + 1
    idx = jnp.arange(n_c)[:, None] * NSA_CMP_STRIDE + jnp.arange(NSA_CMP_LEN)[None, :]
    blocks = x[:, idx] + pe[None, None, :, None, :]
    flat = blocks.transpose(0, 1, 3, 2, 4).reshape(b, n_c, hkv, NSA_CMP_LEN * d)
    return jax.nn.gelu(flat @ w1) @ w2


def nsa_attention(q, k_c, v_c, k_s, v_s, k_w, v_w, gates, cmp_pe, cmp_w1, cmp_w2, bias_table):
    b, s_len, h, d = q.shape
    hkv = k_s.shape[2]
    g = h // hkv
    scale = 1.0 / math.sqrt(d)
    qg = q.reshape(b, s_len, hkv, g, d)
    pos = jnp.arange(s_len)

    k_cmp = compress_blocks(k_c, cmp_pe[0], cmp_w1[0], cmp_w2[0])
    v_cmp = compress_blocks(v_c, cmp_pe[1], cmp_w1[1], cmp_w2[1])
    n_c = k_cmp.shape[1]
    blk_end = jnp.arange(n_c) * NSA_CMP_STRIDE + NSA_CMP_LEN - 1
    dist_c = pos[:, None] - blk_end[None, :]
    valid_c = dist_c >= 0
    bias_c = bias_table[t5_bucket(dist_c)].astype(jnp.float32).transpose(2, 0, 1).reshape(hkv, g, s_len, n_c)
    logits = jnp.einsum('bthgd,bnhd->bhgtn', qg, k_cmp).astype(jnp.float32) * scale + bias_c
    logits = jnp.where(valid_c, logits, NEG_INF)
    p_cmp = jnp.where(valid_c, jax.nn.softmax(logits, axis=-1), 0.0)
    o_cmp = jnp.einsum('bhgtn,bnhd->bthgd', p_cmp.astype(v_cmp.dtype), v_cmp)

    n_s = s_len // NSA_SLC_BLOCK
    top_k = min(NSA_TOPK, n_s)
    c_start = jnp.arange(n_c) * NSA_CMP_STRIDE
    s_start = jnp.arange(n_s) * NSA_SLC_BLOCK
    overlap = jnp.clip(jnp.minimum(c_start[:, None] + NSA_CMP_LEN, s_start[None, :] + NSA_SLC_BLOCK)
                       - jnp.maximum(c_start[:, None], s_start[None, :]), 0, None).astype(jnp.float32) / NSA_CMP_LEN
    importance = jnp.einsum('bhgtn,ns->bhts', p_cmp, overlap)
    q_blk_id = pos // NSA_SLC_BLOCK
    blk = jnp.arange(n_s)
    forced = (blk[None, :] == 0) | (blk[None, :] == q_blk_id[:, None]) | (blk[None, :] == q_blk_id[:, None] - 1)
    future = blk[None, :] > q_blk_id[:, None]
    importance = jnp.where(forced, FORCE_SELECT, jnp.where(future, NEG_INF, importance))
    _, sel = lax.top_k(importance, top_k)

    k_blocks = k_s.reshape(b, n_s, NSA_SLC_BLOCK, hkv, d).transpose(0, 3, 1, 2, 4)
    v_blocks = v_s.reshape(b, n_s, NSA_SLC_BLOCK, hkv, d).transpose(0, 3, 1, 2, 4)
    bias_sel_tab = bias_table.reshape(REL_BUCKETS, hkv, g).transpose(1, 0, 2)
    nq = s_len // NSA_QUERY_BLOCK
    q_blocks = jnp.moveaxis(qg.reshape(b, nq, NSA_QUERY_BLOCK, hkv, g, d), 1, 0)
    sel_blocks = jnp.moveaxis(sel.reshape(b, hkv, nq, NSA_QUERY_BLOCK, top_k), 2, 0)
    bi = jnp.arange(b)[:, None, None, None]
    hi = jnp.arange(hkv)[None, :, None, None]
    n_sel = top_k * NSA_SLC_BLOCK

    def one_block(args):
        i, q_blk, sel_blk = args
        k_g = k_blocks[bi, hi, sel_blk].reshape(b, hkv, NSA_QUERY_BLOCK, n_sel, d)
        v_g = v_blocks[bi, hi, sel_blk].reshape(b, hkv, NSA_QUERY_BLOCK, n_sel, d)
        tok_pos = (sel_blk[..., None] * NSA_SLC_BLOCK + jnp.arange(NSA_SLC_BLOCK)).reshape(b, hkv, NSA_QUERY_BLOCK, n_sel)
        q_pos = i * NSA_QUERY_BLOCK + jnp.arange(NSA_QUERY_BLOCK)
        dist = q_pos[None, None, :, None] - tok_pos
        bias = bias_sel_tab[hi, t5_bucket(dist)].astype(jnp.float32).transpose(0, 1, 4, 2, 3)
        logits = jnp.einsum('bqhgd,bhqnd->bhgqn', q_blk, k_g).astype(jnp.float32) * scale + bias
        logits = jnp.where((dist >= 0)[:, :, None], logits, NEG_INF)
        p = jax.nn.softmax(logits, axis=-1)
        return jnp.einsum('bhgqn,bhqnd->bqhgd', p.astype(v_g.dtype), v_g)

    o_slc = jnp.moveaxis(lax.map(one_block, (jnp.arange(nq), q_blocks, sel_blocks)), 0, 1)

    o_win = banded_attention(q, k_w, v_w, NSA_WINDOW, bias_table, None).reshape(b, s_len, h, d)

    o = (gates[..., 0:1] * o_cmp.reshape(b, s_len, h, d)
         + gates[..., 1:2] * o_slc.reshape(b, s_len, h, d)
         + gates[..., 2:3] * o_win)
    return o.reshape(b, s_len, h * d)


def setup_inputs(seed: int = 0) -> dict:
    key = jax.random.key(seed)
    ks = jax.random.split(key, 20)

    def normal(k, shape, scale):
        return jax.random.normal(k, shape, jnp.float32) * scale

    def gain(k):
        return 1.0 + normal(k, (DEPTH, D_MODEL), 0.05)

    return {
        "x": normal(ks[0], (BATCH, SEQ, D_MODEL), 1.0),
        "c": normal(ks[1], (BATCH, D_MODEL), 1.0),
        "rel_bias": normal(ks[2], (REL_BUCKETS, BIAS_HEADS), 0.5),
        "ada_w": normal(ks[3], (DEPTH, D_MODEL, ADA_CHUNKS * D_MODEL), D_MODEL ** -0.5),
        "ada_b": normal(ks[4], (DEPTH, ADA_CHUNKS * D_MODEL), 0.02),
        "attn_pre_norm": gain(ks[5]),
        "attn_post_norm": gain(ks[6]),
        "ffn_pre_norm": gain(ks[7]),
        "ffn_post_norm": gain(ks[8]),
        "w_in": normal(ks[9], (DEPTH, D_MODEL, N_IN), D_MODEL ** -0.5),
        "forget_bias": 3.0 + normal(ks[10], (DEPTH, FOX_HEADS), 0.5),
        "swa_sinks": normal(ks[11], (DEPTH, SWA_HEADS), 0.5),
        "cmp_pos": normal(ks[12], (DEPTH, 2, NSA_CMP_LEN, HEAD_DIM), 0.1),
        "cmp_w1": normal(ks[13], (DEPTH, 2, NSA_CMP_LEN * HEAD_DIM, NSA_CMP_HIDDEN), (NSA_CMP_LEN * HEAD_DIM) ** -0.5),
        "cmp_w2": normal(ks[14], (DEPTH, 2, NSA_CMP_HIDDEN, HEAD_DIM), NSA_CMP_HIDDEN ** -0.5),
        "group_norm": gain(ks[15]),
        "w_out": normal(ks[16], (DEPTH, D_MODEL, D_MODEL), D_MODEL ** -0.5),
        "ffn_w_gate": normal(ks[17], (DEPTH, D_MODEL, FFN_HIDDEN), D_MODEL ** -0.5),
        "ffn_w_up": normal(ks[18], (DEPTH, D_MODEL, FFN_HIDDEN), D_MODEL ** -0.5),
        "ffn_w_down": normal(ks[19], (DEPTH, FFN_HIDDEN, D_MODEL), FFN_HIDDEN ** -0.5),
    }


def reference(x, c, rel_bias, ada_w, ada_b, attn_pre_norm, attn_post_norm, ffn_pre_norm, ffn_post_norm,
              w_in, forget_bias, swa_sinks, cmp_pos, cmp_w1, cmp_w2, group_norm, w_out,
              ffn_w_gate, ffn_w_up, ffn_w_down):
    b, s_len, _ = x.shape
    offsets = np.cumsum(IN_SPLITS)[:-1].tolist()
    c_act = jax.nn.silu(c)
    bias_swa = rel_bias[:, :SWA_HEADS]
    bias_nsa = rel_bias[:, SWA_HEADS:]
    for layer in range(DEPTH):
        mod = c_act @ ada_w[layer] + ada_b[layer]
        shift_a, scale_a, gate_a, shift_f, scale_f, gate_f = [m[:, None, :] for m in jnp.split(mod, ADA_CHUNKS, axis=-1)]

        h = rms_norm(x, attn_pre_norm[layer]) * (1 + scale_a) + shift_a
        proj = h @ w_in[layer]
        qa, ka, va, qb, kb, vb, fb, qc, kc, vc, ksl, vsl, kw, vw, gc = jnp.split(proj, offsets, axis=-1)

        o_swa = banded_attention(split_heads(qa, SWA_HEADS), split_heads(ka, SWA_KV_HEADS),
                                 split_heads(va, SWA_KV_HEADS), SWA_WINDOW, bias_swa, swa_sinks[layer])
        log_f = jax.nn.log_sigmoid(fb.astype(jnp.float32) + forget_bias[layer].astype(jnp.float32))
        o_fox = forgetting_attention(split_heads(qb, FOX_HEADS), split_heads(kb, FOX_HEADS),
                                     split_heads(vb, FOX_HEADS), log_f)
        gates = jax.nn.sigmoid(gc.reshape(b, s_len, NSA_HEADS, 3))
        o_nsa = nsa_attention(split_heads(qc, NSA_HEADS),
                              split_heads(kc, NSA_KV_HEADS), split_heads(vc, NSA_KV_HEADS),
                              split_heads(ksl, NSA_KV_HEADS), split_heads(vsl, NSA_KV_HEADS),
                              split_heads(kw, NSA_KV_HEADS), split_heads(vw, NSA_KV_HEADS),
                              gates, cmp_pos[layer], cmp_w1[layer], cmp_w2[layer], bias_nsa)
        gn = group_norm[layer]
        mixed = jnp.concatenate([
            rms_norm(o_swa, gn[:SWA_WIDTH]),
            rms_norm(o_fox, gn[SWA_WIDTH:SWA_WIDTH + FOX_WIDTH]),
            rms_norm(o_nsa, gn[SWA_WIDTH + FOX_WIDTH:]),
        ], axis=-1)
        y = mixed @ w_out[layer]
        x = x + gate_a * rms_norm(y, attn_post_norm[layer])

        h = rms_norm(x, ffn_pre_norm[layer]) * (1 + scale_f) + shift_f
        y = (jax.nn.silu(h @ ffn_w_gate[layer]) * (h @ ffn_w_up[layer])) @ ffn_w_down[layer]
        x = x + gate_f * rms_norm(y, ffn_post_norm[layer])
    return x
```

```python
import functools
import math

import numpy as np
import jax
import jax.numpy as jnp
from jax import lax
from jax.experimental import pallas as pl
from jax.experimental.pallas import tpu as pltpu

F32 = jnp.float32
BF16 = jnp.bfloat16
HIGHEST = lax.Precision.HIGHEST

LANES = 128
VMEM_LIMIT = 56 * 1024 * 1024

HEAD_DIM = 64
SWA_HEADS = 4
SWA_WINDOW = 128
FOX_HEADS = 4
NSA_HEADS = 8
CMP_LEN = 32
CMP_STRIDE = 16
CMP_HIDDEN = 2 * HEAD_DIM
SLC_BLOCK = 64
TOPK = 16
NSA_WINDOW = 512
REL_BUCKETS = 32
REL_MAX_DISTANCE = 1024
RMS_EPS = 1e-6
NEG = -1e30
FORCE = 1e30
ADA_CHUNKS = 6

SWA_POS = (0, 2, 1, 3)
NSA_POS = (0, 4, 1, 5, 2, 6, 3, 7)

BAND_TQ = 128
FLASH_T = 256
SEL_TQ = 128
ROW_TILE = 512
FFN_CHUNK = 256

SEG_SWA = (0, 512)
SEG_FOX = (512, 1280)
SEG_NSAQ = (1280, 1792)
SEG_KC = (1792, 1920)
SEG_VC = (1920, 2048)
SEG_KV4 = (2048, 2560)
SEG_GATES = (2560, 4096)
SEG_MISC = (4096, 4224)


def _params(n_grid, vmem=VMEM_LIMIT):
    return pltpu.CompilerParams(dimension_semantics=("parallel",) * n_grid, vmem_limit_bytes=vmem)


def _dot_nt(a, b):
    return lax.dot_general(a, b, (((1,), (1,)), ((), ())), preferred_element_type=F32)


def _lane_lo(shape):
    return lax.broadcasted_iota(jnp.int32, shape, len(shape) - 1) < HEAD_DIM


def _adaln_kernel(c_ref, w_ref, b_ref, o_ref):
    c = c_ref[...]
    act = c * jax.nn.sigmoid(c)
    o_ref[0] = jnp.dot(act, w_ref[0], precision=HIGHEST, preferred_element_type=F32) + b_ref[0]


def _adaln(c, ada_w, ada_b):
    depth, d, n = ada_w.shape
    b = c.shape[0]
    return pl.pallas_call(
        _adaln_kernel,
        grid=(depth, n // d),
        in_specs=[pl.BlockSpec((b, d), lambda l, j: (0, 0)),
                  pl.BlockSpec((1, d, d), lambda l, j: (l, 0, j)),
                  pl.BlockSpec((1, 1, d), lambda l, j: (l, 0, j))],
        out_specs=pl.BlockSpec((1, b, d), lambda l, j: (l, 0, j)),
        out_shape=jax.ShapeDtypeStruct((depth, b, n), F32),
        compiler_params=_params(2),
    )(c, ada_w, ada_b.reshape(depth, 1, n))


def _t5_bucket(dist):
    n = jnp.maximum(dist, 0)
    max_exact = REL_BUCKETS // 2
    nf = jnp.maximum(n, 1).astype(jnp.float32)
    large = max_exact + (jnp.log(nf / max_exact) / math.log(REL_MAX_DISTANCE / max_exact)
                         * (REL_BUCKETS - max_exact)).astype(jnp.int32)
    large = jnp.minimum(large, REL_BUCKETS - 1)
    return jnp.where(n < max_exact, n, large)


def _bias_table_kernel(tab_ref, bucket_ref, o_ref, *, subtract_last):
    h = pl.program_id(0)
    bucket = bucket_ref[0]
    off = tab_ref[REL_BUCKETS - 1, h] if subtract_last else 0.0
    acc = jnp.full(bucket.shape, NEG, F32)
    for k in range(REL_BUCKETS):
        acc = jnp.where(bucket == k, tab_ref[k, h] - off, acc)
    o_ref[0, 0] = acc


def _bias_table(table, bucket, subtract_last=False):
    n_heads = table.shape[1]
    n, r, c = bucket.shape
    return pl.pallas_call(
        functools.partial(_bias_table_kernel, subtract_last=subtract_last),
        grid=(n_heads, n),
        in_specs=[pl.BlockSpec(memory_space=pltpu.SMEM),
                  pl.BlockSpec((1, r, c), lambda h, i: (i, 0, 0))],
        out_specs=pl.BlockSpec((1, 1, r, c), lambda h, i: (h, i, 0, 0)),
        out_shape=jax.ShapeDtypeStruct((n_heads, n, r, c), F32),
        compiler_params=_params(2),
    )(table, bucket)


def _band_buckets(tile, window):
    n_back = -(-(window - 1) // tile)
    t = jnp.arange(n_back + 1)[:, None, None]
    r = jnp.arange(tile)[None, :, None]
    c = jnp.arange(tile)[None, None, :]
    dist = r + (n_back - t) * tile - c
    return jnp.where((dist >= 0) & (dist < window), _t5_bucket(dist), -1).astype(jnp.int32)


def _toeplitz_buckets(tile, n_tiles):
    m = jnp.arange(n_tiles)[:, None, None]
    r = jnp.arange(tile)[None, :, None]
    c = jnp.arange(tile)[None, None, :]
    dist = m * tile + r - c
    return jnp.where(dist >= 0, _t5_bucket(dist), -1).astype(jnp.int32)


def _cmp_buckets(s_len, n_rows):
    n_c = n_rows - 1
    t = jnp.arange(s_len)[:, None]
    n = jnp.arange(n_rows)[None, :]
    dist = t - (n * CMP_STRIDE + CMP_LEN - 1)
    bucket = jnp.where((dist >= 0) & (n < n_c), _t5_bucket(dist), -1).astype(jnp.int32)
    return bucket.reshape(s_len // SEL_TQ, SEL_TQ, n_rows)


def _near_tiles(tile):
    max_exact = REL_BUCKETS // 2
    first_const = math.ceil(max_exact * (REL_MAX_DISTANCE / max_exact) ** ((max_exact - 1) / max_exact)) + 1
    m = 1
    while m * tile - (tile - 1) < first_const:
        m += 1
    return m


def _rms(x, gain):
    return x * lax.rsqrt(jnp.mean(x * x, axis=-1, keepdims=True) + RMS_EPS) * gain


def _in_proj_kernel(x_ref, mod_ref, gain_ref, w_ref, swa_ref, fox_ref, nsaq_ref, kc_ref, vc_ref, kv4_ref,
                    gates_ref, misc_ref):
    x = x_ref[0]
    h = _rms(x, gain_ref[...]) * (1.0 + mod_ref[0, 1:2, :]) + mod_ref[0, 0:1, :]
    hb = h.astype(BF16)

    def seg(bounds):
        return jnp.dot(hb, w_ref[:, bounds[0]:bounds[1]], preferred_element_type=F32)

    swa_ref[0] = seg(SEG_SWA).astype(BF16)
    fox_ref[0] = seg(SEG_FOX).astype(BF16)
    nsaq_ref[0] = seg(SEG_NSAQ).astype(BF16)
    kc_ref[0] = seg(SEG_KC).astype(BF16)
    vc_ref[0] = seg(SEG_VC).astype(BF16)
    kv4_ref[0] = seg(SEG_KV4).astype(BF16)
    gates_ref[0] = jax.nn.sigmoid(seg(SEG_GATES)).astype(BF16)
    misc_ref[0] = seg(SEG_MISC)


def _in_proj(x, mod, gain, w):
    b, s, d = x.shape
    n = w.shape[1]
    widths = [hi - lo for lo, hi in (SEG_SWA, SEG_FOX, SEG_NSAQ, SEG_KC, SEG_VC, SEG_KV4, SEG_GATES, SEG_MISC)]
    dtypes = [BF16] * 7 + [F32]
    return pl.pallas_call(
        _in_proj_kernel,
        grid=(b, s // ROW_TILE),
        in_specs=[pl.BlockSpec((1, ROW_TILE, d), lambda i, j: (i, j, 0)),
                  pl.BlockSpec((1, ADA_CHUNKS, d), lambda i, j: (i, 0, 0)),
                  pl.BlockSpec((1, d), lambda i, j: (0, 0)),
                  pl.BlockSpec((d, n), lambda i, j: (0, 0))],
        out_specs=[pl.BlockSpec((1, ROW_TILE, wd), lambda i, j: (i, j, 0)) for wd in widths],
        out_shape=[jax.ShapeDtypeStruct((b, s, wd), dt) for wd, dt in zip(widths, dtypes)],
        compiler_params=_params(2),
    )(x, mod, gain, w)


def _banded_kernel(*refs, n_back, n_groups, has_sink):
    if has_sink:
        sink_ref, q_ref, k_ref, v_ref, bias_ref, o_ref = refs
    else:
        q_ref, k_ref, v_ref, bias_ref, o_ref = refs
    tq = BAND_TQ
    i = pl.program_id(1)
    lo = _lane_lo((tq, LANES))
    k_tiles, v_tiles = [], []
    for t in range(n_back + 1):
        start = pl.multiple_of(jnp.maximum(i - n_back + t, 0) * tq, tq)
        k_tiles.append(k_ref[0, pl.ds(start, tq), :])
        v_tiles.append(v_ref[0, pl.ds(start, tq), :])
    for g in range(n_groups):
        qg = q_ref[0, :, g * LANES:(g + 1) * LANES]
        outs = []
        for half in range(2):
            pos = 2 * g + half
            qm = jnp.where(lo if half == 0 else jnp.logical_not(lo), qg, jnp.zeros_like(qg))
            scores = []
            for t in range(n_back + 1):
                sc = _dot_nt(qm, k_tiles[t]) + bias_ref[pos, t]
                if t < n_back:
                    sc = jnp.where(i - n_back + t >= 0, sc, NEG)
                scores.append(sc)
            m = scores[0].max(axis=-1, keepdims=True)
            for sc in scores[1:]:
                m = jnp.maximum(m, sc.max(axis=-1, keepdims=True))
            if has_sink:
                sink = sink_ref[pos]
                m = jnp.maximum(m, sink)
            denom = jnp.exp(sink - m) if has_sink else jnp.zeros_like(m)
            acc = jnp.zeros((tq, LANES), F32)
            for t in range(n_back + 1):
                e = jnp.exp(scores[t] - m)
                denom = denom + e.sum(axis=-1, keepdims=True)
                acc = acc + jnp.dot(e.astype(BF16), v_tiles[t], preferred_element_type=F32)
            outs.append(acc / denom)
        o_ref[0, :, g * LANES:(g + 1) * LANES] = jnp.where(lo, outs[0], outs[1])


def _banded_attention(q_arr, q_blk, k_arr, k_blk, v_arr, v_blk, bias, sinks=None):
    b, s, _ = q_arr.shape
    n_pos, n_tiles = bias.shape[0], bias.shape[1]
    width = n_pos * HEAD_DIM
    in_specs = [pl.BlockSpec((1, BAND_TQ, width), lambda i, j: (i, j, q_blk)),
                pl.BlockSpec((1, s, LANES), lambda i, j: (i, 0, k_blk)),
                pl.BlockSpec((1, s, LANES), lambda i, j: (i, 0, v_blk)),
                pl.BlockSpec(bias.shape, lambda i, j: (0, 0, 0, 0))]
    args = [q_arr, k_arr, v_arr, bias]
    if sinks is not None:
        in_specs = [pl.BlockSpec(memory_space=pltpu.SMEM)] + in_specs
        args = [sinks] + args
    return pl.pallas_call(
        functools.partial(_banded_kernel, n_back=n_tiles - 1, n_groups=n_pos // 2, has_sink=sinks is not None),
        grid=(b, s // BAND_TQ),
        in_specs=in_specs,
        out_specs=pl.BlockSpec((1, BAND_TQ, width), lambda i, j: (i, j, 0)),
        out_shape=jax.ShapeDtypeStruct((b, s, width), F32),
        compiler_params=_params(2),
    )(*args)


def _flash_init(m_scr, l_scr, acc_scr):
    m_scr[...] = jnp.full(m_scr.shape, NEG, F32)
    l_scr[...] = jnp.zeros(l_scr.shape, F32)
    acc_scr[...] = jnp.zeros(acc_scr.shape, F32)


def _flash_update(h, sc, v_tile, m_scr, l_scr, acc_scr):
    m_prev = m_scr[h]
    m_new = jnp.maximum(m_prev, sc.max(axis=-1, keepdims=True))
    alpha = jnp.exp(m_prev - m_new)
    p = jnp.exp(sc - m_new)
    l_scr[h] = alpha * l_scr[h] + p.sum(axis=-1, keepdims=True)
    acc_scr[h] = alpha * acc_scr[h] + jnp.dot(p.astype(BF16), v_tile, preferred_element_type=F32)
    m_scr[h] = m_new


def _flash_finish(o_ref, n_groups, l_scr, acc_scr):
    lo = _lane_lo((FLASH_T, LANES))
    for g in range(n_groups):
        a = acc_scr[2 * g] / l_scr[2 * g]
        b = acc_scr[2 * g + 1] / l_scr[2 * g + 1]
        o_ref[0, :, g * LANES:(g + 1) * LANES] = jnp.where(lo, a, b)


def _causal_mask(sc):
    row = lax.broadcasted_iota(jnp.int32, sc.shape, 0)
    col = lax.broadcasted_iota(jnp.int32, sc.shape, 1)
    return jnp.where(col <= row, sc, NEG)


def _fox_bias_kernel(misc_ref, fbias_ref, tri_ref, o_ref):
    s_len = misc_ref.shape[1]
    carry = jnp.zeros((8, 1), F32)
    for c in range(s_len // LANES):
        z = misc_ref[0, c * LANES:(c + 1) * LANES, :] + fbias_ref[...]
        log_f = jnp.minimum(z, 0.0) - jnp.log1p(jnp.exp(-jnp.abs(z)))
        log_f_t = log_f.T[:8, :]
        cum = jnp.dot(log_f_t, tri_ref[...], precision=HIGHEST, preferred_element_type=F32) + carry
        o_ref[0, :, c * LANES:(c + 1) * LANES] = -cum
        carry = cum[:, LANES - 1:LANES]


def _fox_key_bias(misc, forget_bias):
    b, s, _ = misc.shape
    fbias = jnp.zeros((1, LANES), F32).at[0, :FOX_HEADS].set(forget_bias.astype(F32))
    tri = jnp.asarray(np.triu(np.ones((LANES, LANES), np.float32)))
    return pl.pallas_call(
        _fox_bias_kernel,
        grid=(b,),
        in_specs=[pl.BlockSpec((1, s, LANES), lambda i: (i, 0, 0)),
                  pl.BlockSpec((1, LANES), lambda i: (0, 0)),
                  pl.BlockSpec((LANES, LANES), lambda i: (0, 0))],
        out_specs=pl.BlockSpec((1, 8, s), lambda i: (i, 0, 0)),
        out_shape=jax.ShapeDtypeStruct((b, 8, s), F32),
        compiler_params=_params(1),
    )(misc, fbias, tri)


def _fox_kernel(q_ref, k_ref, v_ref, kb_ref, o_ref, qm_scr, m_scr, l_scr, acc_scr):
    t = FLASH_T
    i = pl.program_id(1)
    lo = _lane_lo((t, LANES))
    n_groups = FOX_HEADS // 2
    for g in range(n_groups):
        qg = q_ref[0, :, g * LANES:(g + 1) * LANES]
        zero = jnp.zeros_like(qg)
        qm_scr[2 * g] = jnp.where(lo, qg, zero)
        qm_scr[2 * g + 1] = jnp.where(lo, zero, qg)
    _flash_init(m_scr, l_scr, acc_scr)

    def tile(j, masked):
        start = pl.multiple_of(j * t, t)
        for g in range(n_groups):
            k_tile = k_ref[0, pl.ds(start, t), g * LANES:(g + 1) * LANES]
            v_tile = v_ref[0, pl.ds(start, t), g * LANES:(g + 1) * LANES]
            for half in range(2):
                h = 2 * g + half
                sc = _dot_nt(qm_scr[h], k_tile) + kb_ref[0, h:h + 1, pl.ds(start, t)]
                if masked:
                    sc = _causal_mask(sc)
                _flash_update(h, sc, v_tile, m_scr, l_scr, acc_scr)

    def body(j, carry):
        tile(j, False)
        return carry

    lax.fori_loop(0, i, body, 0)
    tile(i, True)
    _flash_finish(o_ref, n_groups, l_scr, acc_scr)


def _fox_attention(fox_qkv, key_bias):
    b, s, _ = fox_qkv.shape
    width = FOX_HEADS * HEAD_DIM
    t = FLASH_T
    return pl.pallas_call(
        _fox_kernel,
        grid=(b, s // t),
        in_specs=[pl.BlockSpec((1, t, width), lambda i, j: (i, j, 0)),
                  pl.BlockSpec((1, s, width), lambda i, j: (i, 0, 1)),
                  pl.BlockSpec((1, s, width), lambda i, j: (i, 0, 2)),
                  pl.BlockSpec((1, 8, s), lambda i, j: (i, 0, 0))],
        out_specs=pl.BlockSpec((1, t, width), lambda i, j: (i, j, 0)),
        out_shape=jax.ShapeDtypeStruct((b, s, width), F32),
        scratch_shapes=[pltpu.VMEM((FOX_HEADS, t, LANES), BF16),
                        pltpu.VMEM((FOX_HEADS, t, 1), F32),
                        pltpu.VMEM((FOX_HEADS, t, 1), F32),
                        pltpu.VMEM((FOX_HEADS, t, LANES), F32)],
        compiler_params=_params(2),
    )(fox_qkv, fox_qkv, fox_qkv, key_bias)


def _compress_kernel(x_ref, pe_ref, w1a_ref, w1b_ref, w2_ref, o_ref):
    x = x_ref[0].astype(F32)
    n_rows = x.shape[0]

    def mm(a, w):
        return jnp.dot(a, w, precision=HIGHEST, preferred_element_type=F32)

    first = mm(x, w1a_ref[0])
    second = mm(x, w1b_ref[0])
    pe_term = (mm(pe_ref[0, 0], w1a_ref[0]) + mm(pe_ref[0, 1], w1b_ref[0]))[0:1, :]
    pre = first + pltpu.roll(second, n_rows - 1, 0) + pe_term
    hid = 0.5 * pre * (1.0 + jnp.tanh(math.sqrt(2.0 / math.pi) * (pre + 0.044715 * (pre * pre * pre))))
    o_ref[0, 0] = mm(hid, w2_ref[0])


def _compress(kc, vc, cmp_pos, cmp_w1, cmp_w2):
    b, s, _ = kc.shape
    n_rows = s // CMP_STRIDE
    half = CMP_LEN // 2
    feat = CMP_STRIDE * LANES
    x = jnp.stack([kc.reshape(b, n_rows, feat), vc.reshape(b, n_rows, feat)])
    eye = jnp.eye(2, dtype=F32)
    w1 = cmp_w1.astype(F32).reshape(2, CMP_LEN, HEAD_DIM, CMP_HIDDEN)
    w1a = jnp.einsum('wldj,hg->wlhdgj', w1[:, :half], eye).reshape(2, feat, 2 * CMP_HIDDEN)
    w1b = jnp.einsum('wldj,hg->wlhdgj', w1[:, half:], eye).reshape(2, feat, 2 * CMP_HIDDEN)
    w2 = jnp.einsum('wjd,hg->whjgd', cmp_w2.astype(F32), eye).reshape(2, 2 * CMP_HIDDEN, LANES)
    pe = jnp.broadcast_to(cmp_pos.astype(F32).reshape(2, 2, half, 1, HEAD_DIM), (2, 2, half, 2, HEAD_DIM))
    pe = jnp.broadcast_to(pe.reshape(2, 2, 1, feat), (2, 2, 8, feat))
    return pl.pallas_call(
        _compress_kernel,
        grid=(2, b),
        in_specs=[pl.BlockSpec((None, 1, n_rows, feat), lambda w, i: (w, i, 0, 0)),
                  pl.BlockSpec((1, 2, 8, feat), lambda w, i: (w, 0, 0, 0)),
                  pl.BlockSpec((1, feat, 2 * CMP_HIDDEN), lambda w, i: (w, 0, 0)),
                  pl.BlockSpec((1, feat, 2 * CMP_HIDDEN), lambda w, i: (w, 0, 0)),
                  pl.BlockSpec((1, 2 * CMP_HIDDEN, LANES), lambda w, i: (w, 0, 0))],
        out_specs=pl.BlockSpec((1, 1, n_rows, LANES), lambda w, i: (w, i, 0, 0)),
        out_shape=jax.ShapeDtypeStruct((2, b, n_rows, LANES), F32),
        compiler_params=_params(2),
    )(x, pe, w1a, w1b, w2)


def _select_kernel(q_ref, kc_ref, vc_ref, bias_ref, ovl_ref, o_ref, mb_ref):
    tq = SEL_TQ
    i = pl.program_id(0)
    lo = _lane_lo((tq, LANES))
    kc = kc_ref[0, 0]
    k_hi = kc.astype(BF16)
    k_lo = (kc - k_hi.astype(F32)).astype(BF16)
    vcb = vc_ref[0, 0].astype(BF16)
    n_rows = kc.shape[0]
    p_sum = [jnp.zeros((tq, n_rows), F32), jnp.zeros((tq, n_rows), F32)]
    for g in range(NSA_HEADS // 2):
        qg = q_ref[0, :, g * LANES:(g + 1) * LANES]
        outs = []
        for half in range(2):
            qm = jnp.where(lo if half == 0 else jnp.logical_not(lo), qg, jnp.zeros_like(qg))
            bias = bias_ref[2 * g + half, 0]
            sc = _dot_nt(qm, k_hi) + _dot_nt(qm, k_lo) + bias
            e = jnp.exp(sc - sc.max(axis=-1, keepdims=True))
            p = jnp.where(bias > 0.5 * NEG, e / e.sum(axis=-1, keepdims=True), 0.0)
            p_sum[half] = p_sum[half] + p
            outs.append(jnp.dot(p.astype(BF16), vcb, preferred_element_type=F32))
        o_ref[0, :, g * LANES:(g + 1) * LANES] = jnp.where(lo, outs[0], outs[1])

    n_blk = ovl_ref.shape[0]
    blk = lax.broadcasted_iota(jnp.int32, (n_blk, tq), 0)
    q_blk = (i * tq + lax.broadcasted_iota(jnp.int32, (n_blk, tq), 1)) // SLC_BLOCK
    forced = jnp.where(blk == 0, 1, 0) + jnp.where(blk == q_blk, 1, 0) + jnp.where(blk == q_blk - 1, 1, 0)
    masks = []
    for half in (1, 0):
        imp = lax.dot_general(ovl_ref[...], p_sum[half], (((1,), (1,)), ((), ())), precision=HIGHEST,
                              preferred_element_type=F32)
        imp = jnp.where(forced > 0, FORCE, jnp.where(blk > q_blk, NEG, imp))
        count = jnp.zeros((n_blk, tq), jnp.int32)
        for other in range(n_blk):
            row = imp[other:other + 1, :]
            count = count + jnp.where(blk > other, jnp.where(row >= imp, 1, 0), jnp.where(row > imp, 1, 0))
        masks.append(jnp.where(count < TOPK, 0.0, NEG))
    mb_ref[0] = jnp.concatenate(masks, axis=0).T.astype(BF16)


def _select(nsa_q, cmp_kv, bias_c, overlap_t):
    b, s, width = nsa_q.shape
    n_rows = cmp_kv.shape[2]
    n_blk = overlap_t.shape[0]
    assert n_blk == HEAD_DIM
    tq = SEL_TQ
    return pl.pallas_call(
        _select_kernel,
        grid=(s // tq, b),
        in_specs=[pl.BlockSpec((1, tq, width), lambda j, i: (i, j, 0)),
                  pl.BlockSpec((1, 1, n_rows, LANES), lambda j, i: (0, i, 0, 0)),
                  pl.BlockSpec((1, 1, n_rows, LANES), lambda j, i: (1, i, 0, 0)),
                  pl.BlockSpec((NSA_HEADS, 1, tq, n_rows), lambda j, i: (0, j, 0, 0)),
                  pl.BlockSpec((n_blk, n_rows), lambda j, i: (0, 0))],
        out_specs=[pl.BlockSpec((1, tq, width), lambda j, i: (i, j, 0)),
                   pl.BlockSpec((1, tq, LANES), lambda j, i: (i, j, 0))],
        out_shape=[jax.ShapeDtypeStruct((b, s, width), F32),
                   jax.ShapeDtypeStruct((b, s, LANES), BF16)],
        compiler_params=_params(2),
    )(nsa_q, cmp_kv, cmp_kv, bias_c, overlap_t)


def _overlap_t(s_len):
    n_rows = s_len // CMP_STRIDE
    c_start = np.arange(n_rows)[None, :] * CMP_STRIDE
    s_start = np.arange(HEAD_DIM)[:, None] * SLC_BLOCK
    ovl = np.clip(np.minimum(c_start + CMP_LEN, s_start + SLC_BLOCK) - np.maximum(c_start, s_start), 0, None)
    ovl = ovl.astype(np.float32) / CMP_LEN
    ovl[:, n_rows - 1] = 0.0
    ovl[s_len // SLC_BLOCK:, :] = 0.0
    return jnp.asarray(ovl)


def _slc_kernel(q_ref, mb_ref, k_ref, v_ref, e2_ref, bias_ref, o_ref, qs_scr, m_scr, l_scr, acc_scr, *, n_near):
    t = FLASH_T
    i = pl.program_id(1)
    lo = _lane_lo((t, LANES))
    n_groups = NSA_HEADS // 2
    mb = mb_ref[0]
    for g in range(n_groups):
        qg = q_ref[0, :, g * LANES:(g + 1) * LANES]
        qs_scr[2 * g] = jnp.where(lo, qg, mb)
        qs_scr[2 * g + 1] = jnp.where(lo, mb, qg)
    _flash_init(m_scr, l_scr, acc_scr)

    def tile(j, kind):
        start = pl.multiple_of(j * t, t)
        k_tile = k_ref[0, pl.ds(start, t), :]
        e_tile = e2_ref[pl.ds(start, t), :]
        v_tile = v_ref[0, pl.ds(start, t), :]
        k_sel = (jnp.where(lo, k_tile, e_tile), jnp.where(lo, e_tile, k_tile))
        for pos in range(NSA_HEADS):
            sc = _dot_nt(qs_scr[pos], k_sel[pos % 2])
            if kind == "near":
                sc = sc + bias_ref[pos, i - j]
            elif kind == "diag":
                sc = sc + bias_ref[pos, 0]
            _flash_update(pos, sc, v_tile, m_scr, l_scr, acc_scr)

    def far_body(j, carry):
        tile(j, "far")
        return carry

    def near_body(j, carry):
        tile(j, "near")
        return carry

    first_near = jnp.maximum(i - (n_near - 1), 0)
    lax.fori_loop(0, first_near, far_body, 0)
    lax.fori_loop(first_near, i, near_body, 0)
    tile(i, "diag")
    _flash_finish(o_ref, n_groups, l_scr, acc_scr)


def _slc_attention(nsa_q, mask_bias, kv4, e2, bias):
    b, s, width = nsa_q.shape
    t = FLASH_T
    n_near = bias.shape[1]
    return pl.pallas_call(
        functools.partial(_slc_kernel, n_near=n_near),
        grid=(b, s // t),
        in_specs=[pl.BlockSpec((1, t, width), lambda i, j: (i, j, 0)),
                  pl.BlockSpec((1, t, LANES), lambda i, j: (i, j, 0)),
                  pl.BlockSpec((1, s, LANES), lambda i, j: (i, 0, 0)),
                  pl.BlockSpec((1, s, LANES), lambda i, j: (i, 0, 1)),
                  pl.BlockSpec((s, LANES), lambda i, j: (0, 0)),
                  pl.BlockSpec(bias.shape, lambda i, j: (0, 0, 0, 0))],
        out_specs=pl.BlockSpec((1, t, width), lambda i, j: (i, j, 0)),
        out_shape=jax.ShapeDtypeStruct((b, s, width), F32),
        scratch_shapes=[pltpu.VMEM((NSA_HEADS, t, LANES), BF16),
                        pltpu.VMEM((NSA_HEADS, t, 1), F32),
                        pltpu.VMEM((NSA_HEADS, t, 1), F32),
                        pltpu.VMEM((NSA_HEADS, t, LANES), F32)],
        compiler_params=_params(2),
    )(nsa_q, mask_bias, kv4, kv4, e2, bias)


def _block_onehot(s_len):
    blk = np.arange(s_len)[:, None] // SLC_BLOCK
    lane = np.arange(LANES)[None, :] % HEAD_DIM
    return jnp.asarray((blk == lane).astype(np.float32), dtype=BF16)


def _out_proj_kernel(x_ref, mod_ref, swa_ref, fox_ref, cmp_ref, slc_ref, win_ref, gates_ref, gn_ref, w_ref,
                     post_ref, o_ref):
    n_swa = SWA_HEADS * HEAD_DIM
    n_fox = FOX_HEADS * HEAD_DIM
    n_nsa = NSA_HEADS * HEAD_DIM
    gates = gates_ref[0].astype(F32)
    o_nsa = (gates[:, 0:n_nsa] * cmp_ref[0] + gates[:, n_nsa:2 * n_nsa] * slc_ref[0]
             + gates[:, 2 * n_nsa:3 * n_nsa] * win_ref[0])
    a = _rms(swa_ref[0], gn_ref[:, 0:n_swa]).astype(BF16)
    b = _rms(fox_ref[0], gn_ref[:, n_swa:n_swa + n_fox]).astype(BF16)
    c = _rms(o_nsa, gn_ref[:, n_swa + n_fox:]).astype(BF16)
    y = (jnp.dot(a, w_ref[0:n_swa, :], preferred_element_type=F32)
         + jnp.dot(b, w_ref[n_swa:n_swa + n_fox, :], preferred_element_type=F32)
         + jnp.dot(c, w_ref[n_swa + n_fox:, :], preferred_element_type=F32))
    o_ref[0] = x_ref[0] + mod_ref[0, 2:3, :] * _rms(y, post_ref[...])


def _out_proj(x, mod, o_swa, o_fox, o_cmp, o_slc, o_win, gates, gn, w, post):
    b, s, d = x.shape

    def rows(width):
        return pl.BlockSpec((1, ROW_TILE, width), lambda i, j: (i, j, 0))

    return pl.pallas_call(
        _out_proj_kernel,
        grid=(b, s // ROW_TILE),
        in_specs=[rows(d),
                  pl.BlockSpec((1, ADA_CHUNKS, d), lambda i, j: (i, 0, 0)),
                  rows(o_swa.shape[2]), rows(o_fox.shape[2]), rows(o_cmp.shape[2]), rows(o_slc.shape[2]),
                  rows(o_win.shape[2]), rows(gates.shape[2]),
                  pl.BlockSpec((1, d), lambda i, j: (0, 0)),
                  pl.BlockSpec((d, d), lambda i, j: (0, 0)),
                  pl.BlockSpec((1, d), lambda i, j: (0, 0))],
        out_specs=rows(d),
        out_shape=jax.ShapeDtypeStruct((b, s, d), F32),
        compiler_params=_params(2),
    )(x, mod, o_swa, o_fox, o_cmp, o_slc, o_win, gates, gn, w, post)


def _ffn_kernel(x_ref, mod_ref, pre_ref, wg_ref, wu_ref, wd_ref, post_ref, o_ref):
    x = x_ref[0]
    h = (_rms(x, pre_ref[...]) * (1.0 + mod_ref[0, 4:5, :]) + mod_ref[0, 3:4, :]).astype(BF16)
    y = jnp.zeros(x.shape, F32)
    for c in range(wg_ref.shape[0]):
        gate = jnp.dot(h, wg_ref[c], preferred_element_type=F32)
        up = jnp.dot(h, wu_ref[c], preferred_element_type=F32)
        act = (gate * jax.nn.sigmoid(gate) * up).astype(BF16)
        y = y + jnp.dot(act, wd_ref[c], preferred_element_type=F32)
    o_ref[0] = x + mod_ref[0, 5:6, :] * _rms(y, post_ref[...])


def _ffn(x, mod, pre, wg, wu, wd, post):
    b, s, d = x.shape
    n_chunks = wg.shape[0]
    rows = pl.BlockSpec((1, ROW_TILE, d), lambda i, j: (i, j, 0))
    vec = pl.BlockSpec((1, d), lambda i, j: (0, 0))
    return pl.pallas_call(
        _ffn_kernel,
        grid=(b, s // ROW_TILE),
        in_specs=[rows,
                  pl.BlockSpec((1, ADA_CHUNKS, d), lambda i, j: (i, 0, 0)),
                  vec,
                  pl.BlockSpec((n_chunks, d, FFN_CHUNK), lambda i, j: (0, 0, 0)),
                  pl.BlockSpec((n_chunks, d, FFN_CHUNK), lambda i, j: (0, 0, 0)),
                  pl.BlockSpec((n_chunks, FFN_CHUNK, d), lambda i, j: (0, 0, 0)),
                  vec],
        out_specs=rows,
        out_shape=jax.ShapeDtypeStruct((b, s, d), F32),
        compiler_params=_params(2),
    )(x, mod, pre, wg, wu, wd, post)


def _in_proj_layout():
    d = HEAD_DIM
    o_qa, o_ka, o_qb, o_fb, o_qc, o_kc = 0, 256, 512, 1280, 1284, 1796
    o_gc = 2564
    scale = 1.0 / math.sqrt(d)

    def head_cols(base, heads):
        return np.concatenate([np.arange(base + h * d, base + (h + 1) * d) for h in heads])

    cols = [head_cols(o_qa, SWA_POS), np.arange(o_ka, o_qb),
            np.arange(o_qb, o_fb),
            head_cols(o_qc, NSA_POS),
            np.arange(o_kc, o_gc)]
    scales = [np.full(256, scale), np.ones(256), np.full(256, scale), np.ones(512), np.full(512, scale),
              np.ones(768)]
    for branch in range(3):
        cols.append(np.repeat(np.array([o_gc + h * 3 + branch for h in NSA_POS]), d))
        scales.append(np.ones(NSA_HEADS * d))
    cols.append(np.arange(o_fb, o_fb + FOX_HEADS))
    scales.append(np.ones(FOX_HEADS))
    return np.concatenate(cols), np.concatenate(scales).astype(np.float32)


def _head_perm(pos):
    return np.concatenate([np.arange(h * HEAD_DIM, (h + 1) * HEAD_DIM) for h in pos])


def kernel(x, c, rel_bias, ada_w, ada_b, attn_pre_norm, attn_post_norm, ffn_pre_norm, ffn_post_norm, w_in,
           forget_bias, swa_sinks, cmp_pos, cmp_w1, cmp_w2, group_norm, w_out, ffn_w_gate, ffn_w_up, ffn_w_down):
    b, s, d = x.shape
    depth = w_in.shape[0]
    hidden = ffn_w_gate.shape[2]
    assert s % (2 * FLASH_T) == 0 and s // SLC_BLOCK <= HEAD_DIM and hidden % FFN_CHUNK == 0

    cols, scales = _in_proj_layout()
    w_main = w_in[:, :, cols] * scales
    pad = SEG_MISC[1] - w_main.shape[2]
    w_all = jnp.pad(w_main, ((0, 0), (0, 0), (0, pad))).astype(BF16)
    swa_perm = _head_perm(SWA_POS)
    nsa_perm = _head_perm(NSA_POS)
    n_swa, n_fox = SWA_HEADS * HEAD_DIM, FOX_HEADS * HEAD_DIM
    mix_perm = np.concatenate([swa_perm, n_swa + np.arange(n_fox), n_swa + n_fox + nsa_perm])
    gn_all = group_norm[:, mix_perm].astype(F32)
    w_out_all = w_out[:, mix_perm, :].astype(BF16)
    n_chunks = hidden // FFN_CHUNK
    wg_all = ffn_w_gate.reshape(depth, d, n_chunks, FFN_CHUNK).transpose(0, 2, 1, 3).astype(BF16)
    wu_all = ffn_w_up.reshape(depth, d, n_chunks, FFN_CHUNK).transpose(0, 2, 1, 3).astype(BF16)
    wd_all = ffn_w_down.reshape(depth, n_chunks, FFN_CHUNK, d).astype(BF16)

    tab_swa = rel_bias[:, np.array(SWA_POS)].astype(F32)
    tab_nsa = rel_bias[:, SWA_HEADS + np.array(NSA_POS)].astype(F32)
    bias_swa = _bias_table(tab_swa, _band_buckets(BAND_TQ, SWA_WINDOW))
    bias_win = _bias_table(tab_nsa, _band_buckets(BAND_TQ, NSA_WINDOW))
    bias_slc = _bias_table(tab_nsa, _toeplitz_buckets(FLASH_T, _near_tiles(FLASH_T)), subtract_last=True)
    n_rows = s // CMP_STRIDE
    bias_cmp = _bias_table(tab_nsa, _cmp_buckets(s, n_rows))
    overlap_t = _overlap_t(s)
    e2 = _block_onehot(s)

    mod_all = _adaln(c.astype(F32), ada_w.astype(F32), ada_b.astype(F32)).reshape(depth, b, ADA_CHUNKS, d)

    for layer in range(depth):
        mod = mod_all[layer]
        swa_qkv, fox_qkv, nsa_q, kc, vc, kv4, gates, misc = _in_proj(
            x, mod, attn_pre_norm[layer].reshape(1, d).astype(F32), w_all[layer])
        o_swa = _banded_attention(swa_qkv, 0, swa_qkv, 2, swa_qkv, 3, bias_swa,
                                  sinks=swa_sinks[layer][np.array(SWA_POS)].astype(F32))
        o_fox = _fox_attention(fox_qkv, _fox_key_bias(misc, forget_bias[layer]))
        cmp_kv = _compress(kc, vc, cmp_pos[layer], cmp_w1[layer], cmp_w2[layer])
        o_cmp, mask_bias = _select(nsa_q, cmp_kv, bias_cmp, overlap_t)
        o_slc = _slc_attention(nsa_q, mask_bias, kv4, e2, bias_slc)
        o_win = _banded_attention(nsa_q, 0, kv4, 2, kv4, 3, bias_win)
        x = _out_proj(x, mod, o_swa, o_fox, o_cmp, o_slc, o_win, gates, gn_all[layer].reshape(1, d),
                      w_out_all[layer], attn_post_norm[layer].reshape(1, d).astype(F32))
        x = _ffn(x, mod, ffn_pre_norm[layer].reshape(1, d).astype(F32), wg_all[layer], wu_all[layer],
                 wd_all[layer], ffn_post_norm[layer].reshape(1, d).astype(F32))
    return x
```

```python
import functools
import math

import numpy as np
import jax
import jax.numpy as jnp
from jax import lax
from jax.experimental import pallas as pl
from jax.experimental.pallas import tpu as pltpu

F32 = jnp.float32
BF16 = jnp.bfloat16
HIGHEST = lax.Precision.HIGHEST

LANES = 128
SUBLANES = 8
VMEM_LIMIT = 56 * 1024 * 1024

HEAD_DIM = 64
SWA_HEADS = 4
SWA_WINDOW = 128
FOX_HEADS = 4
NSA_HEADS = 8
CMP_LEN = 32
CMP_STRIDE = 16
CMP_HIDDEN = 2 * HEAD_DIM
SLC_BLOCK = 64
TOPK = 16
NSA_WINDOW = 512
REL_BUCKETS = 32
REL_MAX_DISTANCE = 1024
RMS_EPS = 1e-6
NEG = -1e30
FORCE = 1e30
ADA_CHUNKS = 6
LOG2E = math.log2(math.e)

SWA_POS = (0, 2, 1, 3)
NSA_POS = (0, 4, 1, 5, 2, 6, 3, 7)

BAND_TQ = 128
FLASH_T = 256
SEL_TQ = 128
ROW_TILE = 512
FFN_CHUNK = 256
VT_ROWS = HEAD_DIM + 16
KEY_BIAS_TERMS = 3

SEG_SWA = (0, 512)
SEG_FOX = (512, 1280)
SEG_NSAQ = (1280, 1792)
SEG_KC = (1792, 1920)
SEG_VC = (1920, 2048)
SEG_KV4 = (2048, 2560)
SEG_GATES = (2560, 4096)
SEG_MISC = (4096, 4352)


def _params(n_grid, vmem=VMEM_LIMIT):
    return pltpu.CompilerParams(dimension_semantics=("parallel",) * n_grid, vmem_limit_bytes=vmem)


def _dot_nt(a, b):
    return lax.dot_general(a, b, (((1,), (1,)), ((), ())), preferred_element_type=F32)


def _lane_lo(shape):
    return lax.broadcasted_iota(jnp.int32, shape, len(shape) - 1) < HEAD_DIM


def _adaln_kernel(c_ref, w_ref, b_ref, o_ref):
    c = c_ref[...]
    act = c * jax.nn.sigmoid(c)
    o_ref[0] = jnp.dot(act, w_ref[0], precision=HIGHEST, preferred_element_type=F32) + b_ref[0]


def _adaln(c, ada_w, ada_b):
    depth, d, n = ada_w.shape
    b = c.shape[0]
    return pl.pallas_call(
        _adaln_kernel,
        grid=(depth, n // d),
        in_specs=[pl.BlockSpec((b, d), lambda l, j: (0, 0)),
                  pl.BlockSpec((1, d, d), lambda l, j: (l, 0, j)),
                  pl.BlockSpec((1, 1, d), lambda l, j: (l, 0, j))],
        out_specs=pl.BlockSpec((1, b, d), lambda l, j: (l, 0, j)),
        out_shape=jax.ShapeDtypeStruct((depth, b, n), F32),
        compiler_params=_params(2),
    )(c, ada_w, ada_b.reshape(depth, 1, n))


def _t5_bucket(dist):
    n = jnp.maximum(dist, 0)
    max_exact = REL_BUCKETS // 2
    nf = jnp.maximum(n, 1).astype(jnp.float32)
    large = max_exact + (jnp.log(nf / max_exact) / math.log(REL_MAX_DISTANCE / max_exact)
                         * (REL_BUCKETS - max_exact)).astype(jnp.int32)
    large = jnp.minimum(large, REL_BUCKETS - 1)
    return jnp.where(n < max_exact, n, large)


def _bias_table_kernel(tab_ref, bucket_ref, o_ref, *, subtract_last):
    h = pl.program_id(0)
    bucket = bucket_ref[0]
    off = tab_ref[REL_BUCKETS - 1, h] if subtract_last else 0.0
    acc = jnp.full(bucket.shape, NEG, F32)
    for k in range(REL_BUCKETS):
        acc = jnp.where(bucket == k, (tab_ref[k, h] - off) * LOG2E, acc)
    o_ref[0, 0] = acc


def _bias_table(table, bucket, subtract_last=False):
    n_heads = table.shape[1]
    n, r, c = bucket.shape
    return pl.pallas_call(
        functools.partial(_bias_table_kernel, subtract_last=subtract_last),
        grid=(n_heads, n),
        in_specs=[pl.BlockSpec(memory_space=pltpu.SMEM),
                  pl.BlockSpec((1, r, c), lambda h, i: (i, 0, 0))],
        out_specs=pl.BlockSpec((1, 1, r, c), lambda h, i: (h, i, 0, 0)),
        out_shape=jax.ShapeDtypeStruct((n_heads, n, r, c), F32),
        compiler_params=_params(2),
    )(table, bucket)


def _band_buckets(tile, window):
    n_back = -(-(window - 1) // tile)
    t = jnp.arange(n_back + 1)[:, None, None]
    r = jnp.arange(tile)[None, :, None]
    c = jnp.arange(tile)[None, None, :]
    dist = r + (n_back - t) * tile - c
    return jnp.where((dist >= 0) & (dist < window), _t5_bucket(dist), -1).astype(jnp.int32)


def _toeplitz_buckets_t(tile, n_tiles):
    m = jnp.arange(n_tiles)[:, None, None]
    key = jnp.arange(tile)[None, :, None]
    query = jnp.arange(tile)[None, None, :]
    dist = m * tile + query - key
    return jnp.where(dist >= 0, _t5_bucket(dist), -1).astype(jnp.int32)


def _cmp_buckets(s_len, n_rows):
    n_c = n_rows - 1
    t = jnp.arange(s_len)[:, None]
    n = jnp.arange(n_rows)[None, :]
    dist = t - (n * CMP_STRIDE + CMP_LEN - 1)
    bucket = jnp.where((dist >= 0) & (n < n_c), _t5_bucket(dist), -1).astype(jnp.int32)
    return bucket.reshape(s_len // SEL_TQ, SEL_TQ, n_rows)


def _near_tiles(tile):
    max_exact = REL_BUCKETS // 2
    first_const = math.ceil(max_exact * (REL_MAX_DISTANCE / max_exact) ** ((max_exact - 1) / max_exact)) + 1
    m = 1
    while m * tile - (tile - 1) < first_const:
        m += 1
    return m


def _rms(x, gain):
    return x * lax.rsqrt(jnp.mean(x * x, axis=-1, keepdims=True) + RMS_EPS) * gain


def _in_proj_kernel(x_ref, mod_ref, gain_ref, w_ref, swa_ref, fox_ref, nsaq_ref, kc_ref, vc_ref, kv4_ref,
                    gates_ref, misc_ref):
    x = x_ref[0]
    h = _rms(x, gain_ref[...]) * (1.0 + mod_ref[0, 1:2, :]) + mod_ref[0, 0:1, :]
    hb = h.astype(BF16)

    def seg(bounds):
        return jnp.dot(hb, w_ref[:, bounds[0]:bounds[1]], preferred_element_type=F32)

    swa_ref[0] = seg(SEG_SWA).astype(BF16)
    fox_ref[0] = seg(SEG_FOX).astype(BF16)
    nsaq_ref[0] = seg(SEG_NSAQ).astype(BF16)
    kc_ref[0] = seg(SEG_KC).astype(BF16)
    vc_ref[0] = seg(SEG_VC).astype(BF16)
    kv4_ref[0] = seg(SEG_KV4).astype(BF16)
    gates_ref[0] = jax.nn.sigmoid(seg(SEG_GATES)).astype(BF16)
    misc_ref[0] = seg(SEG_MISC)


def _in_proj(x, mod, gain, w):
    b, s, d = x.shape
    n = w.shape[1]
    widths = [hi - lo for lo, hi in (SEG_SWA, SEG_FOX, SEG_NSAQ, SEG_KC, SEG_VC, SEG_KV4, SEG_GATES, SEG_MISC)]
    dtypes = [BF16] * 7 + [F32]
    return pl.pallas_call(
        _in_proj_kernel,
        grid=(b, s // ROW_TILE),
        in_specs=[pl.BlockSpec((1, ROW_TILE, d), lambda i, j: (i, j, 0)),
                  pl.BlockSpec((1, ADA_CHUNKS, d), lambda i, j: (i, 0, 0)),
                  pl.BlockSpec((1, d), lambda i, j: (0, 0)),
                  pl.BlockSpec((d, n), lambda i, j: (0, 0))],
        out_specs=[pl.BlockSpec((1, ROW_TILE, wd), lambda i, j: (i, j, 0)) for wd in widths],
        out_shape=[jax.ShapeDtypeStruct((b, s, wd), dt) for wd, dt in zip(widths, dtypes)],
        compiler_params=_params(2),
    )(x, mod, gain, w)


def _banded_kernel(*refs, n_back, n_groups, has_sink):
    if has_sink:
        sink_ref, q_ref, k_ref, v_ref, bias_ref, o_ref = refs
    else:
        q_ref, k_ref, v_ref, bias_ref, o_ref = refs
    tq = BAND_TQ
    i = pl.program_id(1)
    lo = _lane_lo((tq, LANES))
    k_tiles, v_tiles = [], []
    for t in range(n_back + 1):
        start = pl.multiple_of(jnp.maximum(i - n_back + t, 0) * tq, tq)
        k_tiles.append(k_ref[0, pl.ds(start, tq), :])
        v_tiles.append(v_ref[0, pl.ds(start, tq), :])
    for g in range(n_groups):
        qg = q_ref[0, :, g * LANES:(g + 1) * LANES]
        outs = []
        for half in range(2):
            pos = 2 * g + half
            qm = jnp.where(lo if half == 0 else jnp.logical_not(lo), qg, jnp.zeros_like(qg))
            scores = []
            for t in range(n_back + 1):
                sc = _dot_nt(qm, k_tiles[t]) + bias_ref[pos, t]
                if t < n_back:
                    sc = jnp.where(i - n_back + t >= 0, sc, NEG)
                scores.append(sc)
            m = scores[0].max(axis=-1, keepdims=True)
            for sc in scores[1:]:
                m = jnp.maximum(m, sc.max(axis=-1, keepdims=True))
            if has_sink:
                sink = sink_ref[pos] * LOG2E
                m = jnp.maximum(m, sink)
            denom = jnp.exp2(sink - m) if has_sink else jnp.zeros_like(m)
            acc = jnp.zeros((tq, LANES), F32)
            for t in range(n_back + 1):
                e = jnp.exp2(scores[t] - m)
                denom = denom + e.sum(axis=-1, keepdims=True)
                acc = acc + jnp.dot(e.astype(BF16), v_tiles[t], preferred_element_type=F32)
            outs.append(acc / denom)
        o_ref[0, :, g * LANES:(g + 1) * LANES] = jnp.where(lo, outs[0], outs[1])


def _banded_attention(q_arr, q_blk, k_arr, k_blk, v_arr, v_blk, bias, sinks=None):
    b, s, _ = q_arr.shape
    n_pos, n_tiles = bias.shape[0], bias.shape[1]
    width = n_pos * HEAD_DIM
    in_specs = [pl.BlockSpec((1, BAND_TQ, width), lambda i, j: (i, j, q_blk)),
                pl.BlockSpec((1, s, LANES), lambda i, j: (i, 0, k_blk)),
                pl.BlockSpec((1, s, LANES), lambda i, j: (i, 0, v_blk)),
                pl.BlockSpec(bias.shape, lambda i, j: (0, 0, 0, 0))]
    args = [q_arr, k_arr, v_arr, bias]
    if sinks is not None:
        in_specs = [pl.BlockSpec(memory_space=pltpu.SMEM)] + in_specs
        args = [sinks] + args
    return pl.pallas_call(
        functools.partial(_banded_kernel, n_back=n_tiles - 1, n_groups=n_pos // 2, has_sink=sinks is not None),
        grid=(b, s // BAND_TQ),
        in_specs=in_specs,
        out_specs=pl.BlockSpec((1, BAND_TQ, width), lambda i, j: (i, j, 0)),
        out_shape=jax.ShapeDtypeStruct((b, s, width), F32),
        compiler_params=_params(2),
    )(*args)


def _all_sublanes(x, op):
    for shift in (4, 2, 1):
        x = op(x, pltpu.roll(x, shift, 0))
    return x


def _flash_init(m_scr, acc_scr):
    m_scr[...] = jnp.full(m_scr.shape, NEG, F32)
    acc_scr[...] = jnp.zeros(acc_scr.shape, F32)


def _flash_update(h, st, vt_h, m_scr, acc_scr):
    tk, tq = st.shape
    s3 = st.reshape(tk // SUBLANES, SUBLANES, tq)
    m_prev = m_scr[h]
    m_new = _all_sublanes(jnp.maximum(m_prev, s3.max(axis=0)), jnp.maximum)
    alpha = jnp.exp2(m_prev - m_new)
    p = jnp.exp2(s3 - m_new[None]).reshape(tk, tq).astype(BF16)
    acc = acc_scr[h].reshape(VT_ROWS // SUBLANES, SUBLANES, tq) * alpha[None]
    acc_scr[h] = acc.reshape(VT_ROWS, tq) + jnp.dot(vt_h, p, preferred_element_type=F32)
    m_scr[h] = m_new


def _flash_finish(o_ref, n_groups, acc_scr):
    tq = acc_scr.shape[2]
    for g in range(n_groups):
        pair = []
        for h in (2 * g, 2 * g + 1):
            denom = _all_sublanes(acc_scr[h, HEAD_DIM:HEAD_DIM + SUBLANES, :], jnp.add)
            out = acc_scr[h, 0:HEAD_DIM, :].reshape(HEAD_DIM // SUBLANES, SUBLANES, tq) / denom[None]
            pair.append(out.reshape(HEAD_DIM, tq))
        o_ref[0, :, g * LANES:(g + 1) * LANES] = jnp.concatenate(pair, axis=0).T


def _causal_mask_t(st):
    key = lax.broadcasted_iota(jnp.int32, st.shape, 0)
    query = lax.broadcasted_iota(jnp.int32, st.shape, 1)
    return jnp.where(key <= query, st, NEG)


def _value_slabs(v, n_heads):
    b, s, _ = v.shape
    vt = v.transpose(0, 2, 1).reshape(b, n_heads, HEAD_DIM, s)
    extra = np.zeros((1, 1, VT_ROWS - HEAD_DIM, 1), np.float32)
    extra[0, 0, 0, 0] = 1.0
    extra = jnp.broadcast_to(jnp.asarray(extra, dtype=v.dtype), (b, n_heads, VT_ROWS - HEAD_DIM, s))
    return jnp.concatenate([vt, extra], axis=2).reshape(b, n_heads * VT_ROWS, s)


def _fox_aug_kernel(misc_ref, fbias_ref, tri_ref, o_ref):
    s_len, width = misc_ref.shape[1], misc_ref.shape[2]
    term = lax.broadcasted_iota(jnp.int32, (LANES, width), 1) % HEAD_DIM
    carry = jnp.zeros((1, width), F32)
    for c in range(s_len // LANES):
        z = misc_ref[0, c * LANES:(c + 1) * LANES, :] + fbias_ref[...]
        log_f = jnp.minimum(z, 0.0) - jnp.log1p(jnp.exp(-jnp.abs(z)))
        cum = jnp.dot(tri_ref[...], log_f, precision=HIGHEST, preferred_element_type=F32) + carry
        carry = cum[LANES - 1:LANES, :]
        x = cum * (-LOG2E)
        hi = x.astype(BF16).astype(F32)
        rest = x - hi
        mid = rest.astype(BF16).astype(F32)
        low = rest - mid
        out = jnp.where(term == 0, hi, jnp.where(term == 1, mid, jnp.where(term == 2, low, 0.0)))
        o_ref[0, c * LANES:(c + 1) * LANES, :] = out.astype(BF16)


def _fox_key_terms(misc, fbias):
    b, s, width = misc.shape
    tri = jnp.asarray(np.tril(np.ones((LANES, LANES), np.float32)))
    return pl.pallas_call(
        _fox_aug_kernel,
        grid=(b,),
        in_specs=[pl.BlockSpec((1, s, width), lambda i: (i, 0, 0)),
                  pl.BlockSpec((1, width), lambda i: (0, 0)),
                  pl.BlockSpec((LANES, LANES), lambda i: (0, 0))],
        out_specs=pl.BlockSpec((1, s, width), lambda i: (i, 0, 0)),
        out_shape=jax.ShapeDtypeStruct((b, s, width), BF16),
        compiler_params=_params(1),
    )(misc, fbias, tri)


def _fox_kernel(q_ref, k_ref, aug_ref, vt_ref, o_ref, qs_scr, m_scr, acc_scr):
    t = FLASH_T
    i = pl.program_id(1)
    lo = _lane_lo((t, LANES))
    lane = lax.broadcasted_iota(jnp.int32, (t, LANES), 1)
    ones = jnp.where(lane % HEAD_DIM < KEY_BIAS_TERMS, 1.0, 0.0).astype(BF16)
    n_groups = FOX_HEADS // 2
    for g in range(n_groups):
        qg = q_ref[0, :, g * LANES:(g + 1) * LANES]
        qs_scr[2 * g] = jnp.where(lo, qg, ones)
        qs_scr[2 * g + 1] = jnp.where(lo, ones, qg)
    _flash_init(m_scr, acc_scr)

    def tile(j, masked):
        start = pl.multiple_of(j * t, t)
        scores = []
        for g in range(n_groups):
            k_tile = k_ref[0, pl.ds(start, t), g * LANES:(g + 1) * LANES]
            a_tile = aug_ref[0, pl.ds(start, t), g * LANES:(g + 1) * LANES]
            k_sel = (jnp.where(lo, k_tile, a_tile), jnp.where(lo, a_tile, k_tile))
            for half in range(2):
                scores.append(_dot_nt(k_sel[half], qs_scr[2 * g + half]))
        for h in range(FOX_HEADS):
            st = _causal_mask_t(scores[h]) if masked else scores[h]
            _flash_update(h, st, vt_ref[0, h * VT_ROWS:(h + 1) * VT_ROWS, pl.ds(start, t)], m_scr, acc_scr)

    def body(j, carry):
        tile(j, False)
        return carry

    lax.fori_loop(0, i, body, 0)
    tile(i, True)
    _flash_finish(o_ref, n_groups, acc_scr)


def _fox_attention(fox_qkv, key_terms):
    b, s, _ = fox_qkv.shape
    width = FOX_HEADS * HEAD_DIM
    t = FLASH_T
    vt = _value_slabs(fox_qkv[:, :, 2 * width:], FOX_HEADS)
    return pl.pallas_call(
        _fox_kernel,
        grid=(b, s // t),
        in_specs=[pl.BlockSpec((1, t, width), lambda i, j: (i, j, 0)),
                  pl.BlockSpec((1, s, width), lambda i, j: (i, 0, 1)),
                  pl.BlockSpec((1, s, width), lambda i, j: (i, 0, 0)),
                  pl.BlockSpec((1, FOX_HEADS * VT_ROWS, s), lambda i, j: (i, 0, 0))],
        out_specs=pl.BlockSpec((1, t, width), lambda i, j: (i, j, 0)),
        out_shape=jax.ShapeDtypeStruct((b, s, width), F32),
        scratch_shapes=[pltpu.VMEM((FOX_HEADS, t, LANES), BF16),
                        pltpu.VMEM((FOX_HEADS, SUBLANES, t), F32),
                        pltpu.VMEM((FOX_HEADS, VT_ROWS, t), F32)],
        compiler_params=_params(2),
    )(fox_qkv, fox_qkv, key_terms, vt)


def _compress_kernel(x_ref, pe_ref, w1a_ref, w1b_ref, w2_ref, o_ref):
    x = x_ref[0].astype(F32)
    n_rows = x.shape[0]

    def mm(a, w):
        return jnp.dot(a, w, precision=HIGHEST, preferred_element_type=F32)

    first = mm(x, w1a_ref[0])
    second = mm(x, w1b_ref[0])
    pe_term = (mm(pe_ref[0, 0], w1a_ref[0]) + mm(pe_ref[0, 1], w1b_ref[0]))[0:1, :]
    pre = first + pltpu.roll(second, n_rows - 1, 0) + pe_term
    hid = 0.5 * pre * (1.0 + jnp.tanh(math.sqrt(2.0 / math.pi) * (pre + 0.044715 * (pre * pre * pre))))
    o_ref[0, 0] = mm(hid, w2_ref[0])


def _compress(kc, vc, cmp_pos, cmp_w1, cmp_w2):
    b, s, _ = kc.shape
    n_rows = s // CMP_STRIDE
    half = CMP_LEN // 2
    feat = CMP_STRIDE * LANES
    x = jnp.stack([kc.reshape(b, n_rows, feat), vc.reshape(b, n_rows, feat)])
    eye = jnp.eye(2, dtype=F32)
    w1 = cmp_w1.astype(F32).reshape(2, CMP_LEN, HEAD_DIM, CMP_HIDDEN)
    w1a = jnp.einsum('wldj,hg->wlhdgj', w1[:, :half], eye).reshape(2, feat, 2 * CMP_HIDDEN)
    w1b = jnp.einsum('wldj,hg->wlhdgj', w1[:, half:], eye).reshape(2, feat, 2 * CMP_HIDDEN)
    w2 = jnp.einsum('wjd,hg->whjgd', cmp_w2.astype(F32), eye).reshape(2, 2 * CMP_HIDDEN, LANES)
    pe = jnp.broadcast_to(cmp_pos.astype(F32).reshape(2, 2, half, 1, HEAD_DIM), (2, 2, half, 2, HEAD_DIM))
    pe = jnp.broadcast_to(pe.reshape(2, 2, 1, feat), (2, 2, 8, feat))
    return pl.pallas_call(
        _compress_kernel,
        grid=(2, b),
        in_specs=[pl.BlockSpec((None, 1, n_rows, feat), lambda w, i: (w, i, 0, 0)),
                  pl.BlockSpec((1, 2, 8, feat), lambda w, i: (w, 0, 0, 0)),
                  pl.BlockSpec((1, feat, 2 * CMP_HIDDEN), lambda w, i: (w, 0, 0)),
                  pl.BlockSpec((1, feat, 2 * CMP_HIDDEN), lambda w, i: (w, 0, 0)),
                  pl.BlockSpec((1, 2 * CMP_HIDDEN, LANES), lambda w, i: (w, 0, 0))],
        out_specs=pl.BlockSpec((1, 1, n_rows, LANES), lambda w, i: (w, i, 0, 0)),
        out_shape=jax.ShapeDtypeStruct((2, b, n_rows, LANES), F32),
        compiler_params=_params(2),
    )(x, pe, w1a, w1b, w2)


def _select_kernel(q_ref, kc_ref, vc_ref, bias_ref, ovl_ref, o_ref, mb_ref):
    tq = SEL_TQ
    i = pl.program_id(0)
    lo = _lane_lo((tq, LANES))
    kc = kc_ref[0, 0]
    k_hi = kc.astype(BF16)
    k_lo = (kc - k_hi.astype(F32)).astype(BF16)
    vcb = vc_ref[0, 0].astype(BF16)
    n_rows = kc.shape[0]
    p_sum = [jnp.zeros((tq, n_rows), F32), jnp.zeros((tq, n_rows), F32)]
    for g in range(NSA_HEADS // 2):
        qg = q_ref[0, :, g * LANES:(g + 1) * LANES]
        outs = []
        for half in range(2):
            qm = jnp.where(lo if half == 0 else jnp.logical_not(lo), qg, jnp.zeros_like(qg))
            bias = bias_ref[2 * g + half, 0]
            sc = _dot_nt(qm, k_hi) + _dot_nt(qm, k_lo) + bias
            e = jnp.exp2(sc - sc.max(axis=-1, keepdims=True))
            p = jnp.where(bias > 0.5 * NEG, e / e.sum(axis=-1, keepdims=True), 0.0)
            p_sum[half] = p_sum[half] + p
            outs.append(jnp.dot(p.astype(BF16), vcb, preferred_element_type=F32))
        o_ref[0, :, g * LANES:(g + 1) * LANES] = jnp.where(lo, outs[0], outs[1])

    n_blk = ovl_ref.shape[0]
    blk = lax.broadcasted_iota(jnp.int32, (n_blk, tq), 0)
    q_blk = (i * tq + lax.broadcasted_iota(jnp.int32, (n_blk, tq), 1)) // SLC_BLOCK
    forced = jnp.where(blk == 0, 1, 0) + jnp.where(blk == q_blk, 1, 0) + jnp.where(blk == q_blk - 1, 1, 0)
    masks = []
    for half in (1, 0):
        imp = lax.dot_general(ovl_ref[...], p_sum[half], (((1,), (1,)), ((), ())), precision=HIGHEST,
                              preferred_element_type=F32)
        imp = jnp.where(forced > 0, FORCE, jnp.where(blk > q_blk, NEG, imp))
        count = jnp.zeros((n_blk, tq), jnp.int32)
        for other in range(n_blk):
            row = imp[other:other + 1, :]
            count = count + jnp.where(blk > other, jnp.where(row >= imp, 1, 0), jnp.where(row > imp, 1, 0))
        masks.append(jnp.where(count < TOPK, 0.0, NEG))
    mb_ref[0] = jnp.concatenate(masks, axis=0).T.astype(BF16)


def _select(nsa_q, cmp_kv, bias_c, overlap_t):
    b, s, width = nsa_q.shape
    n_rows = cmp_kv.shape[2]
    n_blk = overlap_t.shape[0]
    assert n_blk == HEAD_DIM
    tq = SEL_TQ
    return pl.pallas_call(
        _select_kernel,
        grid=(s // tq, b),
        in_specs=[pl.BlockSpec((1, tq, width), lambda j, i: (i, j, 0)),
                  pl.BlockSpec((1, 1, n_rows, LANES), lambda j, i: (0, i, 0, 0)),
                  pl.BlockSpec((1, 1, n_rows, LANES), lambda j, i: (1, i, 0, 0)),
                  pl.BlockSpec((NSA_HEADS, 1, tq, n_rows), lambda j, i: (0, j, 0, 0)),
                  pl.BlockSpec((n_blk, n_rows), lambda j, i: (0, 0))],
        out_specs=[pl.BlockSpec((1, tq, width), lambda j, i: (i, j, 0)),
                   pl.BlockSpec((1, tq, LANES), lambda j, i: (i, j, 0))],
        out_shape=[jax.ShapeDtypeStruct((b, s, width), F32),
                   jax.ShapeDtypeStruct((b, s, LANES), BF16)],
        compiler_params=_params(2),
    )(nsa_q, cmp_kv, cmp_kv, bias_c, overlap_t)


def _overlap_t(s_len):
    n_rows = s_len // CMP_STRIDE
    c_start = np.arange(n_rows)[None, :] * CMP_STRIDE
    s_start = np.arange(HEAD_DIM)[:, None] * SLC_BLOCK
    ovl = np.clip(np.minimum(c_start + CMP_LEN, s_start + SLC_BLOCK) - np.maximum(c_start, s_start), 0, None)
    ovl = ovl.astype(np.float32) / CMP_LEN
    ovl[:, n_rows - 1] = 0.0
    ovl[s_len // SLC_BLOCK:, :] = 0.0
    return jnp.asarray(ovl)


def _slc_kernel(q_ref, mb_ref, k_ref, e2_ref, vt_ref, bias_ref, o_ref, qs_scr, m_scr, acc_scr, *, n_near):
    t = FLASH_T
    i = pl.program_id(1)
    lo = _lane_lo((t, LANES))
    n_groups = NSA_HEADS // 2
    mb = mb_ref[0]
    for g in range(n_groups):
        qg = q_ref[0, :, g * LANES:(g + 1) * LANES]
        qs_scr[2 * g] = jnp.where(lo, qg, mb)
        qs_scr[2 * g + 1] = jnp.where(lo, mb, qg)
    _flash_init(m_scr, acc_scr)

    def tile(j, kind):
        start = pl.multiple_of(j * t, t)
        k_tile = k_ref[0, pl.ds(start, t), :]
        e_tile = e2_ref[pl.ds(start, t), :]
        k_sel = (jnp.where(lo, k_tile, e_tile), jnp.where(lo, e_tile, k_tile))
        scores = [_dot_nt(k_sel[pos % 2], qs_scr[pos]) for pos in range(NSA_HEADS)]
        for pos in range(NSA_HEADS):
            kv = pos % 2
            st = scores[pos]
            if kind == "near":
                st = bias_ref[pos, i - j] + st
            elif kind == "diag":
                st = bias_ref[pos, 0] + st
            _flash_update(pos, st, vt_ref[0, kv * VT_ROWS:(kv + 1) * VT_ROWS, pl.ds(start, t)], m_scr, acc_scr)

    def far_body(j, carry):
        tile(j, "far")
        return carry

    def near_body(j, carry):
        tile(j, "near")
        return carry

    first_near = jnp.maximum(i - (n_near - 1), 0)
    lax.fori_loop(0, first_near, far_body, 0)
    lax.fori_loop(first_near, i, near_body, 0)
    tile(i, "diag")
    _flash_finish(o_ref, n_groups, acc_scr)


def _slc_attention(nsa_q, mask_bias, kv4, e2, bias):
    b, s, width = nsa_q.shape
    t = FLASH_T
    n_near = bias.shape[1]
    vt = _value_slabs(kv4[:, :, LANES:2 * LANES], 2)
    return pl.pallas_call(
        functools.partial(_slc_kernel, n_near=n_near),
        grid=(b, s // t),
        in_specs=[pl.BlockSpec((1, t, width), lambda i, j: (i, j, 0)),
                  pl.BlockSpec((1, t, LANES), lambda i, j: (i, j, 0)),
                  pl.BlockSpec((1, s, LANES), lambda i, j: (i, 0, 0)),
                  pl.BlockSpec((s, LANES), lambda i, j: (0, 0)),
                  pl.BlockSpec((1, 2 * VT_ROWS, s), lambda i, j: (i, 0, 0)),
                  pl.BlockSpec(bias.shape, lambda i, j: (0, 0, 0, 0))],
        out_specs=pl.BlockSpec((1, t, width), lambda i, j: (i, j, 0)),
        out_shape=jax.ShapeDtypeStruct((b, s, width), F32),
        scratch_shapes=[pltpu.VMEM((NSA_HEADS, t, LANES), BF16),
                        pltpu.VMEM((NSA_HEADS, SUBLANES, t), F32),
                        pltpu.VMEM((NSA_HEADS, VT_ROWS, t), F32)],
        compiler_params=_params(2),
    )(nsa_q, mask_bias, kv4, e2, vt, bias)


def _block_onehot(s_len):
    blk = np.arange(s_len)[:, None] // SLC_BLOCK
    lane = np.arange(LANES)[None, :] % HEAD_DIM
    return jnp.asarray((blk == lane).astype(np.float32), dtype=BF16)


def _out_proj_kernel(x_ref, mod_ref, swa_ref, fox_ref, cmp_ref, slc_ref, win_ref, gates_ref, gn_ref, w_ref,
                     post_ref, o_ref):
    n_swa = SWA_HEADS * HEAD_DIM
    n_fox = FOX_HEADS * HEAD_DIM
    n_nsa = NSA_HEADS * HEAD_DIM
    gates = gates_ref[0].astype(F32)
    o_nsa = (gates[:, 0:n_nsa] * cmp_ref[0] + gates[:, n_nsa:2 * n_nsa] * slc_ref[0]
             + gates[:, 2 * n_nsa:3 * n_nsa] * win_ref[0])
    a = _rms(swa_ref[0], gn_ref[:, 0:n_swa]).astype(BF16)
    b = _rms(fox_ref[0], gn_ref[:, n_swa:n_swa + n_fox]).astype(BF16)
    c = _rms(o_nsa, gn_ref[:, n_swa + n_fox:]).astype(BF16)
    y = (jnp.dot(a, w_ref[0:n_swa, :], preferred_element_type=F32)
         + jnp.dot(b, w_ref[n_swa:n_swa + n_fox, :], preferred_element_type=F32)
         + jnp.dot(c, w_ref[n_swa + n_fox:, :], preferred_element_type=F32))
    o_ref[0] = x_ref[0] + mod_ref[0, 2:3, :] * _rms(y, post_ref[...])


def _out_proj(x, mod, o_swa, o_fox, o_cmp, o_slc, o_win, gates, gn, w, post):
    b, s, d = x.shape

    def rows(width):
        return pl.BlockSpec((1, ROW_TILE, width), lambda i, j: (i, j, 0))

    return pl.pallas_call(
        _out_proj_kernel,
        grid=(b, s // ROW_TILE),
        in_specs=[rows(d),
                  pl.BlockSpec((1, ADA_CHUNKS, d), lambda i, j: (i, 0, 0)),
                  rows(o_swa.shape[2]), rows(o_fox.shape[2]), rows(o_cmp.shape[2]), rows(o_slc.shape[2]),
                  rows(o_win.shape[2]), rows(gates.shape[2]),
                  pl.BlockSpec((1, d), lambda i, j: (0, 0)),
                  pl.BlockSpec((d, d), lambda i, j: (0, 0)),
                  pl.BlockSpec((1, d), lambda i, j: (0, 0))],
        out_specs=rows(d),
        out_shape=jax.ShapeDtypeStruct((b, s, d), F32),
        compiler_params=_params(2),
    )(x, mod, o_swa, o_fox, o_cmp, o_slc, o_win, gates, gn, w, post)


def _ffn_kernel(x_ref, mod_ref, pre_ref, wg_ref, wu_ref, wd_ref, post_ref, o_ref):
    x = x_ref[0]
    h = (_rms(x, pre_ref[...]) * (1.0 + mod_ref[0, 4:5, :]) + mod_ref[0, 3:4, :]).astype(BF16)
    y = jnp.zeros(x.shape, F32)
    for c in range(wg_ref.shape[0]):
        gate = jnp.dot(h, wg_ref[c], preferred_element_type=F32)
        up = jnp.dot(h, wu_ref[c], preferred_element_type=F32)
        act = (gate * jax.nn.sigmoid(gate) * up).astype(BF16)
        y = y + jnp.dot(act, wd_ref[c], preferred_element_type=F32)
    o_ref[0] = x + mod_ref[0, 5:6, :] * _rms(y, post_ref[...])


def _ffn(x, mod, pre, wg, wu, wd, post):
    b, s, d = x.shape
    n_chunks = wg.shape[0]
    rows = pl.BlockSpec((1, ROW_TILE, d), lambda i, j: (i, j, 0))
    vec = pl.BlockSpec((1, d), lambda i, j: (0, 0))
    return pl.pallas_call(
        _ffn_kernel,
        grid=(b, s // ROW_TILE),
        in_specs=[rows,
                  pl.BlockSpec((1, ADA_CHUNKS, d), lambda i, j: (i, 0, 0)),
                  vec,
                  pl.BlockSpec((n_chunks, d, FFN_CHUNK), lambda i, j: (0, 0, 0)),
                  pl.BlockSpec((n_chunks, d, FFN_CHUNK), lambda i, j: (0, 0, 0)),
                  pl.BlockSpec((n_chunks, FFN_CHUNK, d), lambda i, j: (0, 0, 0)),
                  vec],
        out_specs=rows,
        out_shape=jax.ShapeDtypeStruct((b, s, d), F32),
        compiler_params=_params(2),
    )(x, mod, pre, wg, wu, wd, post)


def _forget_lanes():
    lanes, heads = [], []
    for h in range(FOX_HEADS):
        base = (h // 2) * LANES + (HEAD_DIM if h % 2 == 0 else 0)
        for j in range(KEY_BIAS_TERMS):
            lanes.append(base + j)
            heads.append(h)
    return np.array(lanes), np.array(heads)


def _in_proj_layout():
    d = HEAD_DIM
    o_qa, o_ka, o_qb, o_fb, o_qc, o_kc = 0, 256, 512, 1280, 1284, 1796
    o_gc = 2564
    scale = LOG2E / math.sqrt(d)

    def head_cols(base, heads):
        return np.concatenate([np.arange(base + h * d, base + (h + 1) * d) for h in heads])

    cols = [head_cols(o_qa, SWA_POS), np.arange(o_ka, o_qb),
            np.arange(o_qb, o_fb),
            head_cols(o_qc, NSA_POS),
            np.arange(o_kc, o_gc)]
    scales = [np.full(256, scale), np.ones(256), np.full(256, scale), np.ones(512), np.full(512, scale),
              np.ones(768)]
    for branch in range(3):
        cols.append(np.repeat(np.array([o_gc + h * 3 + branch for h in NSA_POS]), d))
        scales.append(np.ones(NSA_HEADS * d))
    lanes, heads = _forget_lanes()
    misc_cols = np.zeros(SEG_MISC[1] - SEG_MISC[0], np.int64)
    misc_scale = np.zeros(SEG_MISC[1] - SEG_MISC[0])
    misc_cols[lanes] = o_fb + heads
    misc_scale[lanes] = 1.0
    cols.append(misc_cols)
    scales.append(misc_scale)
    return np.concatenate(cols), np.concatenate(scales).astype(np.float32)


def _head_perm(pos):
    return np.concatenate([np.arange(h * HEAD_DIM, (h + 1) * HEAD_DIM) for h in pos])


def kernel(x, c, rel_bias, ada_w, ada_b, attn_pre_norm, attn_post_norm, ffn_pre_norm, ffn_post_norm, w_in,
           forget_bias, swa_sinks, cmp_pos, cmp_w1, cmp_w2, group_norm, w_out, ffn_w_gate, ffn_w_up, ffn_w_down):
    b, s, d = x.shape
    depth = w_in.shape[0]
    hidden = ffn_w_gate.shape[2]
    assert s % (2 * FLASH_T) == 0 and s // SLC_BLOCK <= HEAD_DIM and hidden % FFN_CHUNK == 0

    cols, scales = _in_proj_layout()
    w_all = (w_in[:, :, cols] * scales).astype(BF16)
    lanes, heads = _forget_lanes()
    fbias_all = jnp.zeros((depth, 1, SEG_MISC[1] - SEG_MISC[0]), F32).at[:, 0, lanes].set(
        forget_bias[:, heads].astype(F32))
    swa_perm = _head_perm(SWA_POS)
    nsa_perm = _head_perm(NSA_POS)
    n_swa, n_fox = SWA_HEADS * HEAD_DIM, FOX_HEADS * HEAD_DIM
    mix_perm = np.concatenate([swa_perm, n_swa + np.arange(n_fox), n_swa + n_fox + nsa_perm])
    gn_all = group_norm[:, mix_perm].astype(F32)
    w_out_all = w_out[:, mix_perm, :].astype(BF16)
    n_chunks = hidden // FFN_CHUNK
    wg_all = ffn_w_gate.reshape(depth, d, n_chunks, FFN_CHUNK).transpose(0, 2, 1, 3).astype(BF16)
    wu_all = ffn_w_up.reshape(depth, d, n_chunks, FFN_CHUNK).transpose(0, 2, 1, 3).astype(BF16)
    wd_all = ffn_w_down.reshape(depth, n_chunks, FFN_CHUNK, d).astype(BF16)

    tab_swa = rel_bias[:, np.array(SWA_POS)].astype(F32)
    tab_nsa = rel_bias[:, SWA_HEADS + np.array(NSA_POS)].astype(F32)
    bias_swa = _bias_table(tab_swa, _band_buckets(BAND_TQ, SWA_WINDOW))
    bias_win = _bias_table(tab_nsa, _band_buckets(BAND_TQ, NSA_WINDOW))
    bias_slc = _bias_table(tab_nsa, _toeplitz_buckets_t(FLASH_T, _near_tiles(FLASH_T)), subtract_last=True)
    n_rows = s // CMP_STRIDE
    bias_cmp = _bias_table(tab_nsa, _cmp_buckets(s, n_rows))
    overlap_t = _overlap_t(s)
    e2 = _block_onehot(s)

    mod_all = _adaln(c.astype(F32), ada_w.astype(F32), ada_b.astype(F32)).reshape(depth, b, ADA_CHUNKS, d)

    for layer in range(depth):
        mod = mod_all[layer]
        swa_qkv, fox_qkv, nsa_q, kc, vc, kv4, gates, misc = _in_proj(
            x, mod, attn_pre_norm[layer].reshape(1, d).astype(F32), w_all[layer])
        o_swa = _banded_attention(swa_qkv, 0, swa_qkv, 2, swa_qkv, 3, bias_swa,
                                  sinks=swa_sinks[layer][np.array(SWA_POS)].astype(F32))
        o_fox = _fox_attention(fox_qkv, _fox_key_terms(misc, fbias_all[layer]))
        cmp_kv = _compress(kc, vc, cmp_pos[layer], cmp_w1[layer], cmp_w2[layer])
        o_cmp, mask_bias = _select(nsa_q, cmp_kv, bias_cmp, overlap_t)
        o_slc = _slc_attention(nsa_q, mask_bias, kv4, e2, bias_slc)
        o_win = _banded_attention(nsa_q, 0, kv4, 2, kv4, 3, bias_win)
        x = _out_proj(x, mod, o_swa, o_fox, o_cmp, o_slc, o_win, gates, gn_all[layer].reshape(1, d),
                      w_out_all[layer], attn_post_norm[layer].reshape(1, d).astype(F32))
        x = _ffn(x, mod, ffn_pre_norm[layer].reshape(1, d).astype(F32), wg_all[layer], wu_all[layer],
                 wd_all[layer], ffn_post_norm[layer].reshape(1, d).astype(F32))
    return x
```

```python
import functools
import math

import numpy as np
import jax
import jax.numpy as jnp
from jax import lax
from jax.experimental import pallas as pl
from jax.experimental.pallas import tpu as pltpu

F32 = jnp.float32
BF16 = jnp.bfloat16
HIGHEST = lax.Precision.HIGHEST

LANES = 128
SUBLANES = 8
VMEM_LIMIT = 56 * 1024 * 1024

HEAD_DIM = 64
SWA_HEADS = 4
SWA_WINDOW = 128
FOX_HEADS = 4
NSA_HEADS = 8
CMP_LEN = 32
CMP_STRIDE = 16
CMP_HIDDEN = 2 * HEAD_DIM
SLC_BLOCK = 64
TOPK = 16
NSA_WINDOW = 512
REL_BUCKETS = 32
REL_MAX_DISTANCE = 1024
RMS_EPS = 1e-6
NEG = -1e30
FORCE = 1e30
ADA_CHUNKS = 6
LOG2E = math.log2(math.e)

SWA_POS = (0, 2, 1, 3)
NSA_POS = (0, 4, 1, 5, 2, 6, 3, 7)

SWA_TILE = 128
WIN_TILE = 256
FLASH_T = 256
SEL_TQ = 256
ROW_TILE = 512
FFN_CHUNK = 256
VT_ROWS = HEAD_DIM + 16
KEY_BIAS_TERMS = 3

SEG_SWA = (0, 512)
SEG_FOX = (512, 1280)
SEG_NSAQ = (1280, 1792)
SEG_KC = (1792, 1920)
SEG_VC = (1920, 2048)
SEG_KV4 = (2048, 2560)
SEG_GATES = (2560, 4096)
SEG_MISC = (4096, 4352)


def _params(n_grid, vmem=VMEM_LIMIT):
    return pltpu.CompilerParams(dimension_semantics=("parallel",) * n_grid, vmem_limit_bytes=vmem)


def _dot_nt(a, b):
    return lax.dot_general(a, b, (((1,), (1,)), ((), ())), preferred_element_type=F32)


def _lane_lo(shape):
    return lax.broadcasted_iota(jnp.int32, shape, len(shape) - 1) < HEAD_DIM


def _adaln_kernel(c_ref, w_ref, b_ref, o_ref):
    c = c_ref[...]
    act = c * jax.nn.sigmoid(c)
    o_ref[0] = jnp.dot(act, w_ref[0], precision=HIGHEST, preferred_element_type=F32) + b_ref[0]


def _adaln(c, ada_w, ada_b):
    depth, d, n = ada_w.shape
    b = c.shape[0]
    return pl.pallas_call(
        _adaln_kernel,
        grid=(depth, n // d),
        in_specs=[pl.BlockSpec((b, d), lambda l, j: (0, 0)),
                  pl.BlockSpec((1, d, d), lambda l, j: (l, 0, j)),
                  pl.BlockSpec((1, 1, d), lambda l, j: (l, 0, j))],
        out_specs=pl.BlockSpec((1, b, d), lambda l, j: (l, 0, j)),
        out_shape=jax.ShapeDtypeStruct((depth, b, n), F32),
        compiler_params=_params(2),
    )(c, ada_w, ada_b.reshape(depth, 1, n))


def _t5_bucket(dist):
    n = jnp.maximum(dist, 0)
    max_exact = REL_BUCKETS // 2
    nf = jnp.maximum(n, 1).astype(jnp.float32)
    large = max_exact + (jnp.log(nf / max_exact) / math.log(REL_MAX_DISTANCE / max_exact)
                         * (REL_BUCKETS - max_exact)).astype(jnp.int32)
    large = jnp.minimum(large, REL_BUCKETS - 1)
    return jnp.where(n < max_exact, n, large)


def _bias_table_kernel(tab_ref, bucket_ref, o_ref, *, subtract_last):
    h = pl.program_id(0)
    bucket = bucket_ref[0]
    off = tab_ref[REL_BUCKETS - 1, h] if subtract_last else 0.0
    acc = jnp.full(bucket.shape, NEG, F32)
    for k in range(REL_BUCKETS):
        acc = jnp.where(bucket == k, (tab_ref[k, h] - off) * LOG2E, acc)
    o_ref[0, 0] = acc


def _bias_table(table, bucket, subtract_last=False):
    n_heads = table.shape[1]
    n, r, c = bucket.shape
    return pl.pallas_call(
        functools.partial(_bias_table_kernel, subtract_last=subtract_last),
        grid=(n_heads, n),
        in_specs=[pl.BlockSpec(memory_space=pltpu.SMEM),
                  pl.BlockSpec((1, r, c), lambda h, i: (i, 0, 0))],
        out_specs=pl.BlockSpec((1, 1, r, c), lambda h, i: (h, i, 0, 0)),
        out_shape=jax.ShapeDtypeStruct((n_heads, n, r, c), F32),
        compiler_params=_params(2),
    )(table, bucket)


def _band_buckets_t(tile, window):
    n_back = -(-(window - 1) // tile)
    t = jnp.arange(n_back + 1)[:, None, None]
    key = jnp.arange(tile)[None, :, None]
    query = jnp.arange(tile)[None, None, :]
    dist = query + (n_back - t) * tile - key
    return jnp.where((dist >= 0) & (dist < window), _t5_bucket(dist), -1).astype(jnp.int32)


def _toeplitz_buckets_t(tile, n_tiles):
    m = jnp.arange(n_tiles)[:, None, None]
    key = jnp.arange(tile)[None, :, None]
    query = jnp.arange(tile)[None, None, :]
    dist = m * tile + query - key
    return jnp.where(dist >= 0, _t5_bucket(dist), -1).astype(jnp.int32)


def _cmp_buckets_t(s_len, n_rows):
    n_c = n_rows - 1
    tile = jnp.arange(s_len // SEL_TQ)[:, None, None]
    n = jnp.arange(n_rows)[None, :, None]
    t = tile * SEL_TQ + jnp.arange(SEL_TQ)[None, None, :]
    dist = t - (n * CMP_STRIDE + CMP_LEN - 1)
    return jnp.where((dist >= 0) & (n < n_c), _t5_bucket(dist), -1).astype(jnp.int32)


def _near_tiles(tile):
    max_exact = REL_BUCKETS // 2
    first_const = math.ceil(max_exact * (REL_MAX_DISTANCE / max_exact) ** ((max_exact - 1) / max_exact)) + 1
    m = 1
    while m * tile - (tile - 1) < first_const:
        m += 1
    return m


def _rms(x, gain):
    return x * lax.rsqrt(jnp.mean(x * x, axis=-1, keepdims=True) + RMS_EPS) * gain


def _in_proj_kernel(x_ref, mod_ref, gain_ref, w_ref, swa_ref, fox_ref, nsaq_ref, kc_ref, vc_ref, kv4_ref,
                    gates_ref, misc_ref):
    x = x_ref[0]
    h = _rms(x, gain_ref[...]) * (1.0 + mod_ref[0, 1:2, :]) + mod_ref[0, 0:1, :]
    hb = h.astype(BF16)

    def seg(bounds):
        return jnp.dot(hb, w_ref[:, bounds[0]:bounds[1]], preferred_element_type=F32)

    swa_ref[0] = seg(SEG_SWA).astype(BF16)
    fox_ref[0] = seg(SEG_FOX).astype(BF16)
    nsaq_ref[0] = seg(SEG_NSAQ).astype(BF16)
    kc_ref[0] = seg(SEG_KC).astype(BF16)
    vc_ref[0] = seg(SEG_VC).astype(BF16)
    kv4_ref[0] = seg(SEG_KV4).astype(BF16)
    gates_ref[0] = jax.nn.sigmoid(seg(SEG_GATES)).astype(BF16)
    misc_ref[0] = seg(SEG_MISC)


def _in_proj(x, mod, gain, w):
    b, s, d = x.shape
    n = w.shape[1]
    widths = [hi - lo for lo, hi in (SEG_SWA, SEG_FOX, SEG_NSAQ, SEG_KC, SEG_VC, SEG_KV4, SEG_GATES, SEG_MISC)]
    dtypes = [BF16] * 7 + [F32]
    return pl.pallas_call(
        _in_proj_kernel,
        grid=(b, s // ROW_TILE),
        in_specs=[pl.BlockSpec((1, ROW_TILE, d), lambda i, j: (i, j, 0)),
                  pl.BlockSpec((1, ADA_CHUNKS, d), lambda i, j: (i, 0, 0)),
                  pl.BlockSpec((1, d), lambda i, j: (0, 0)),
                  pl.BlockSpec((d, n), lambda i, j: (0, 0))],
        out_specs=[pl.BlockSpec((1, ROW_TILE, wd), lambda i, j: (i, j, 0)) for wd in widths],
        out_shape=[jax.ShapeDtypeStruct((b, s, wd), dt) for wd, dt in zip(widths, dtypes)],
        compiler_params=_params(2),
    )(x, mod, gain, w)


def _banded_kernel(*refs, n_back, n_groups, has_sink, t):
    if has_sink:
        sink_ref, q_ref, k_ref, vt_ref, bias_ref, o_ref = refs
    else:
        q_ref, k_ref, vt_ref, bias_ref, o_ref = refs
    i = pl.program_id(1)
    lo = _lane_lo((t, LANES))
    n_tiles = n_back + 1

    def run(all_valid):
        starts = [pl.multiple_of(jnp.maximum(i - n_back + tt, 0) * t, t) for tt in range(n_tiles)]
        k_tiles = [k_ref[0, pl.ds(start, t), :] for start in starts]

        def scores(g):
            qg = q_ref[0, :, g * LANES:(g + 1) * LANES]
            zero = jnp.zeros_like(qg)
            qms = (jnp.where(lo, qg, zero), jnp.where(lo, zero, qg))
            return [[bias_ref[2 * g + half, tt] + _dot_nt(k_tiles[tt], qms[half]) for tt in range(n_tiles)]
                    for half in range(2)]

        def softmax_pv(g, sts):
            pair = []
            for half in range(2):
                tiles = sts[half]
                if not all_valid:
                    tiles = [jnp.where(i - n_back + tt >= 0, st, NEG) if tt < n_back else st
                             for tt, st in enumerate(tiles)]
                m = None
                for st in tiles:
                    part = st.reshape(t // SUBLANES, SUBLANES, t).max(axis=0)
                    m = part if m is None else jnp.maximum(m, part)
                m = _all_sublanes(m, jnp.maximum)
                if has_sink:
                    sink = sink_ref[2 * g + half] * LOG2E
                    m = jnp.maximum(m, sink)
                acc = None
                for tt, st in enumerate(tiles):
                    p = jnp.exp2(st.reshape(t // SUBLANES, SUBLANES, t) - m[None]).reshape(t, t).astype(BF16)
                    part = jnp.dot(vt_ref[0, half * VT_ROWS:(half + 1) * VT_ROWS, pl.ds(starts[tt], t)], p,
                                   preferred_element_type=F32)
                    acc = part if acc is None else acc + part
                denom = _all_sublanes(acc[HEAD_DIM:HEAD_DIM + SUBLANES, :], jnp.add)
                if has_sink:
                    denom = denom + jnp.exp2(sink - m)
                out = acc[0:HEAD_DIM, :].reshape(HEAD_DIM // SUBLANES, SUBLANES, t) / denom[None]
                pair.append(out.reshape(HEAD_DIM, t))
            o_ref[0, :, g * LANES:(g + 1) * LANES] = jnp.concatenate(pair, axis=0).T

        pending = scores(0)
        for g in range(n_groups):
            current = pending
            if g + 1 < n_groups:
                pending = scores(g + 1)
            softmax_pv(g, current)

    @pl.when(i >= n_back)
    def _():
        run(True)

    @pl.when(i < n_back)
    def _():
        run(False)


def _banded_attention(q_arr, k_arr, k_blk, v, bias, sinks=None):
    b, s, _ = q_arr.shape
    n_pos, n_tiles, t = bias.shape[0], bias.shape[1], bias.shape[2]
    width = n_pos * HEAD_DIM
    vt = _value_slabs(v, 2)
    in_specs = [pl.BlockSpec((1, t, width), lambda i, j: (i, j, 0)),
                pl.BlockSpec((1, s, LANES), lambda i, j: (i, 0, k_blk)),
                pl.BlockSpec((1, 2 * VT_ROWS, s), lambda i, j: (i, 0, 0)),
                pl.BlockSpec(bias.shape, lambda i, j: (0, 0, 0, 0))]
    args = [q_arr, k_arr, vt, bias]
    if sinks is not None:
        in_specs = [pl.BlockSpec(memory_space=pltpu.SMEM)] + in_specs
        args = [sinks] + args
    return pl.pallas_call(
        functools.partial(_banded_kernel, n_back=n_tiles - 1, n_groups=n_pos // 2, has_sink=sinks is not None, t=t),
        grid=(b, s // t),
        in_specs=in_specs,
        out_specs=pl.BlockSpec((1, t, width), lambda i, j: (i, j, 0)),
        out_shape=jax.ShapeDtypeStruct((b, s, width), F32),
        compiler_params=_params(2),
    )(*args)


def _all_sublanes(x, op):
    for shift in (4, 2, 1):
        x = op(x, pltpu.roll(x, shift, 0))
    return x


def _flash_init(m_scr, acc_scr):
    m_scr[...] = jnp.full(m_scr.shape, NEG, F32)
    acc_scr[...] = jnp.zeros(acc_scr.shape, F32)


def _flash_update(h, st, vt_h, m_scr, acc_scr):
    tk, tq = st.shape
    s3 = st.reshape(tk // SUBLANES, SUBLANES, tq)
    m_prev = m_scr[h]
    m_new = _all_sublanes(jnp.maximum(m_prev, s3.max(axis=0)), jnp.maximum)
    alpha = jnp.exp2(m_prev - m_new)
    p = jnp.exp2(s3 - m_new[None]).reshape(tk, tq).astype(BF16)
    acc = acc_scr[h].reshape(VT_ROWS // SUBLANES, SUBLANES, tq) * alpha[None]
    acc_scr[h] = acc.reshape(VT_ROWS, tq) + jnp.dot(vt_h, p, preferred_element_type=F32)
    m_scr[h] = m_new


def _flash_finish(o_ref, n_groups, acc_scr):
    tq = acc_scr.shape[2]
    for g in range(n_groups):
        pair = []
        for h in (2 * g, 2 * g + 1):
            denom = _all_sublanes(acc_scr[h, HEAD_DIM:HEAD_DIM + SUBLANES, :], jnp.add)
            out = acc_scr[h, 0:HEAD_DIM, :].reshape(HEAD_DIM // SUBLANES, SUBLANES, tq) / denom[None]
            pair.append(out.reshape(HEAD_DIM, tq))
        o_ref[0, :, g * LANES:(g + 1) * LANES] = jnp.concatenate(pair, axis=0).T


def _causal_mask_t(st):
    key = lax.broadcasted_iota(jnp.int32, st.shape, 0)
    query = lax.broadcasted_iota(jnp.int32, st.shape, 1)
    return jnp.where(key <= query, st, NEG)


def _value_slabs(v, n_heads):
    b, s, _ = v.shape
    vt = v.transpose(0, 2, 1).reshape(b, n_heads, HEAD_DIM, s)
    extra = np.zeros((1, 1, VT_ROWS - HEAD_DIM, 1), np.float32)
    extra[0, 0, 0, 0] = 1.0
    extra = jnp.broadcast_to(jnp.asarray(extra, dtype=v.dtype), (b, n_heads, VT_ROWS - HEAD_DIM, s))
    return jnp.concatenate([vt, extra], axis=2).reshape(b, n_heads * VT_ROWS, s)


def _fox_aug_kernel(misc_ref, fbias_ref, tri_ref, o_ref):
    s_len, width = misc_ref.shape[1], misc_ref.shape[2]
    term = lax.broadcasted_iota(jnp.int32, (LANES, width), 1) % HEAD_DIM
    carry = jnp.zeros((1, width), F32)
    for c in range(s_len // LANES):
        z = misc_ref[0, c * LANES:(c + 1) * LANES, :] + fbias_ref[...]
        log_f = jnp.minimum(z, 0.0) - jnp.log1p(jnp.exp(-jnp.abs(z)))
        cum = jnp.dot(tri_ref[...], log_f, precision=HIGHEST, preferred_element_type=F32) + carry
        carry = cum[LANES - 1:LANES, :]
        x = cum * (-LOG2E)
        hi = x.astype(BF16).astype(F32)
        rest = x - hi
        mid = rest.astype(BF16).astype(F32)
        low = rest - mid
        out = jnp.where(term == 0, hi, jnp.where(term == 1, mid, jnp.where(term == 2, low, 0.0)))
        o_ref[0, c * LANES:(c + 1) * LANES, :] = out.astype(BF16)


def _fox_key_terms(misc, fbias):
    b, s, width = misc.shape
    tri = jnp.asarray(np.tril(np.ones((LANES, LANES), np.float32)))
    return pl.pallas_call(
        _fox_aug_kernel,
        grid=(b,),
        in_specs=[pl.BlockSpec((1, s, width), lambda i: (i, 0, 0)),
                  pl.BlockSpec((1, width), lambda i: (0, 0)),
                  pl.BlockSpec((LANES, LANES), lambda i: (0, 0))],
        out_specs=pl.BlockSpec((1, s, width), lambda i: (i, 0, 0)),
        out_shape=jax.ShapeDtypeStruct((b, s, width), BF16),
        compiler_params=_params(1),
    )(misc, fbias, tri)


def _fox_kernel(q_ref, k_ref, aug_ref, vt_ref, o_ref, qs_scr, m_scr, acc_scr):
    t = FLASH_T
    i = pl.program_id(1)
    lo = _lane_lo((t, LANES))
    lane = lax.broadcasted_iota(jnp.int32, (t, LANES), 1)
    ones = jnp.where(lane % HEAD_DIM < KEY_BIAS_TERMS, 1.0, 0.0).astype(BF16)
    n_groups = FOX_HEADS // 2
    for g in range(n_groups):
        qg = q_ref[0, :, g * LANES:(g + 1) * LANES]
        qs_scr[2 * g] = jnp.where(lo, qg, ones)
        qs_scr[2 * g + 1] = jnp.where(lo, ones, qg)
    _flash_init(m_scr, acc_scr)

    def tile(j, masked):
        start = pl.multiple_of(j * t, t)
        scores = []
        for g in range(n_groups):
            k_tile = k_ref[0, pl.ds(start, t), g * LANES:(g + 1) * LANES]
            a_tile = aug_ref[0, pl.ds(start, t), g * LANES:(g + 1) * LANES]
            k_sel = (jnp.where(lo, k_tile, a_tile), jnp.where(lo, a_tile, k_tile))
            for half in range(2):
                scores.append(_dot_nt(k_sel[half], qs_scr[2 * g + half]))
        for h in range(FOX_HEADS):
            st = _causal_mask_t(scores[h]) if masked else scores[h]
            _flash_update(h, st, vt_ref[0, h * VT_ROWS:(h + 1) * VT_ROWS, pl.ds(start, t)], m_scr, acc_scr)

    def body(j, carry):
        tile(j, False)
        return carry

    lax.fori_loop(0, i, body, 0)
    tile(i, True)
    _flash_finish(o_ref, n_groups, acc_scr)


def _fox_attention(fox_qkv, key_terms):
    b, s, _ = fox_qkv.shape
    width = FOX_HEADS * HEAD_DIM
    t = FLASH_T
    vt = _value_slabs(fox_qkv[:, :, 2 * width:], FOX_HEADS)
    return pl.pallas_call(
        _fox_kernel,
        grid=(b, s // t),
        in_specs=[pl.BlockSpec((1, t, width), lambda i, j: (i, j, 0)),
                  pl.BlockSpec((1, s, width), lambda i, j: (i, 0, 1)),
                  pl.BlockSpec((1, s, width), lambda i, j: (i, 0, 0)),
                  pl.BlockSpec((1, FOX_HEADS * VT_ROWS, s), lambda i, j: (i, 0, 0))],
        out_specs=pl.BlockSpec((1, t, width), lambda i, j: (i, j, 0)),
        out_shape=jax.ShapeDtypeStruct((b, s, width), F32),
        scratch_shapes=[pltpu.VMEM((FOX_HEADS, t, LANES), BF16),
                        pltpu.VMEM((FOX_HEADS, SUBLANES, t), F32),
                        pltpu.VMEM((FOX_HEADS, VT_ROWS, t), F32)],
        compiler_params=_params(2),
    )(fox_qkv, fox_qkv, key_terms, vt)


def _compress_kernel(x_ref, pe_ref, w1a_ref, w1b_ref, w2_ref, o_ref):
    x = x_ref[0].astype(F32)
    n_rows = x.shape[0]

    def mm(a, w):
        return jnp.dot(a, w, precision=HIGHEST, preferred_element_type=F32)

    first = mm(x, w1a_ref[0])
    second = mm(x, w1b_ref[0])
    pe_term = (mm(pe_ref[0, 0], w1a_ref[0]) + mm(pe_ref[0, 1], w1b_ref[0]))[0:1, :]
    pre = first + pltpu.roll(second, n_rows - 1, 0) + pe_term
    hid = 0.5 * pre * (1.0 + jnp.tanh(math.sqrt(2.0 / math.pi) * (pre + 0.044715 * (pre * pre * pre))))
    o_ref[0, 0] = mm(hid, w2_ref[0])


def _compress(kc, vc, cmp_pos, cmp_w1, cmp_w2):
    b, s, _ = kc.shape
    n_rows = s // CMP_STRIDE
    half = CMP_LEN // 2
    feat = CMP_STRIDE * LANES
    x = jnp.stack([kc.reshape(b, n_rows, feat), vc.reshape(b, n_rows, feat)])
    eye = jnp.eye(2, dtype=F32)
    w1 = cmp_w1.astype(F32).reshape(2, CMP_LEN, HEAD_DIM, CMP_HIDDEN)
    w1a = jnp.einsum('wldj,hg->wlhdgj', w1[:, :half], eye).reshape(2, feat, 2 * CMP_HIDDEN)
    w1b = jnp.einsum('wldj,hg->wlhdgj', w1[:, half:], eye).reshape(2, feat, 2 * CMP_HIDDEN)
    w2 = jnp.einsum('wjd,hg->whjgd', cmp_w2.astype(F32), eye).reshape(2, 2 * CMP_HIDDEN, LANES)
    w2 = jnp.concatenate([w2, jnp.roll(w2, HEAD_DIM, axis=2)], axis=2)
    pe = jnp.broadcast_to(cmp_pos.astype(F32).reshape(2, 2, half, 1, HEAD_DIM), (2, 2, half, 2, HEAD_DIM))
    pe = jnp.broadcast_to(pe.reshape(2, 2, 1, feat), (2, 2, 8, feat))
    return pl.pallas_call(
        _compress_kernel,
        grid=(2, b),
        in_specs=[pl.BlockSpec((None, 1, n_rows, feat), lambda w, i: (w, i, 0, 0)),
                  pl.BlockSpec((1, 2, 8, feat), lambda w, i: (w, 0, 0, 0)),
                  pl.BlockSpec((1, feat, 2 * CMP_HIDDEN), lambda w, i: (w, 0, 0)),
                  pl.BlockSpec((1, feat, 2 * CMP_HIDDEN), lambda w, i: (w, 0, 0)),
                  pl.BlockSpec((1, 2 * CMP_HIDDEN, 2 * LANES), lambda w, i: (w, 0, 0))],
        out_specs=pl.BlockSpec((1, 1, n_rows, 2 * LANES), lambda w, i: (w, i, 0, 0)),
        out_shape=jax.ShapeDtypeStruct((2, b, n_rows, 2 * LANES), F32),
        compiler_params=_params(2),
    )(x, pe, w1a, w1b, w2)


def _select_kernel(q_ref, kc_ref, vct_ref, bias_ref, ovl_ref, o_ref, mb_ref):
    tq = SEL_TQ
    i = pl.program_id(0)
    lo = _lane_lo((tq, LANES))
    n_rows = kc_ref.shape[2]
    lo_k = _lane_lo((n_rows, LANES))
    n_grp = n_rows // SUBLANES

    k_own = kc_ref[0, 0, :, 0:LANES]
    k_swapped = kc_ref[0, 0, :, LANES:2 * LANES]
    hi = k_own.astype(BF16)
    low = (k_swapped - k_swapped.astype(BF16).astype(F32)).astype(BF16)
    k_sel = (jnp.where(lo_k, hi, low), jnp.where(lo_k, low, hi))

    def scores(g):
        qg = q_ref[0, :, g * LANES:(g + 1) * LANES]
        swapped = pltpu.roll(qg.astype(F32), HEAD_DIM, 1).astype(BF16)
        q_dup = (jnp.where(lo, qg, swapped), jnp.where(lo, swapped, qg))
        return [bias_ref[2 * g + half, 0] + _dot_nt(k_sel[half], q_dup[half]) for half in range(2)]

    query = i * tq + lax.broadcasted_iota(jnp.int32, (SUBLANES, tq), 1)
    has_keys = query >= CMP_LEN - 1
    p_sum = [None, None]

    def softmax_pv(g, sts):
        pair = []
        for half in range(2):
            s3 = sts[half].reshape(n_grp, SUBLANES, tq)
            m = _all_sublanes(s3.max(axis=0), jnp.maximum)
            e = jnp.exp2(s3 - m[None])
            inv = jnp.where(has_keys, 1.0 / _all_sublanes(e.sum(axis=0), jnp.add), 0.0)
            p = e * inv[None]
            p_sum[half] = p if p_sum[half] is None else p_sum[half] + p
            pair.append(jnp.dot(vct_ref[0, half * HEAD_DIM:(half + 1) * HEAD_DIM, :],
                                p.reshape(n_rows, tq).astype(BF16), preferred_element_type=F32))
        o_ref[0, :, g * LANES:(g + 1) * LANES] = jnp.concatenate(pair, axis=0).T

    n_groups = NSA_HEADS // 2
    pending = scores(0)
    for g in range(n_groups):
        current = pending
        if g + 1 < n_groups:
            pending = scores(g + 1)
        softmax_pv(g, current)

    n_blk = ovl_ref.shape[0]
    blk_grp = n_blk // SUBLANES
    sub = lax.broadcasted_iota(jnp.int32, (SUBLANES, tq), 0)
    q_blk = query // SLC_BLOCK
    masks = []
    for half in (1, 0):
        imp = jnp.dot(ovl_ref[...], p_sum[half].reshape(n_rows, tq), precision=HIGHEST,
                      preferred_element_type=F32)
        rows = []
        for r in range(blk_grp):
            blk = sub + r * SUBLANES
            forced = jnp.where(blk == 0, 1, 0) + jnp.where(blk == q_blk, 1, 0) + jnp.where(blk == q_blk - 1, 1, 0)
            rows.append(jnp.where(forced > 0, FORCE,
                                  jnp.where(blk > q_blk, NEG, imp[r * SUBLANES:(r + 1) * SUBLANES, :])))
        counts = [jnp.zeros((SUBLANES, tq), jnp.int32) for _ in range(blk_grp)]
        for other in range(n_blk):
            r_other, s_other = divmod(other, SUBLANES)
            row = jnp.broadcast_to(rows[r_other][s_other:s_other + 1, :], (SUBLANES, tq))
            for r in range(blk_grp):
                if r > r_other:
                    beats = jnp.where(row >= rows[r], 1, 0)
                elif r < r_other:
                    beats = jnp.where(row > rows[r], 1, 0)
                else:
                    beats = jnp.where(sub > s_other, jnp.where(row >= rows[r], 1, 0), jnp.where(row > rows[r], 1, 0))
                counts[r] = counts[r] + beats
        masks.extend(jnp.where(cnt < TOPK, 0.0, NEG) for cnt in counts)
    for c in range(tq // LANES):
        mb_ref[0, c * LANES:(c + 1) * LANES, :] = jnp.concatenate(
            [mk[:, c * LANES:(c + 1) * LANES] for mk in masks], axis=0).T.astype(BF16)


def _select(nsa_q, cmp_kv, bias_c, overlap_t):
    b, s, width = nsa_q.shape
    n_rows = cmp_kv.shape[2]
    n_blk = overlap_t.shape[0]
    assert n_blk == HEAD_DIM
    tq = SEL_TQ
    vct = cmp_kv[1, :, :, 0:LANES].transpose(0, 2, 1).astype(BF16)
    return pl.pallas_call(
        _select_kernel,
        grid=(s // tq, b),
        in_specs=[pl.BlockSpec((1, tq, width), lambda j, i: (i, j, 0)),
                  pl.BlockSpec((1, 1, n_rows, 2 * LANES), lambda j, i: (0, i, 0, 0)),
                  pl.BlockSpec((1, LANES, n_rows), lambda j, i: (i, 0, 0)),
                  pl.BlockSpec((NSA_HEADS, 1, n_rows, tq), lambda j, i: (0, j, 0, 0)),
                  pl.BlockSpec((n_blk, n_rows), lambda j, i: (0, 0))],
        out_specs=[pl.BlockSpec((1, tq, width), lambda j, i: (i, j, 0)),
                   pl.BlockSpec((1, tq, LANES), lambda j, i: (i, j, 0))],
        out_shape=[jax.ShapeDtypeStruct((b, s, width), F32),
                   jax.ShapeDtypeStruct((b, s, LANES), BF16)],
        compiler_params=_params(2),
    )(nsa_q, cmp_kv, vct, bias_c, overlap_t)


def _overlap_t(s_len):
    n_rows = s_len // CMP_STRIDE
    c_start = np.arange(n_rows)[None, :] * CMP_STRIDE
    s_start = np.arange(HEAD_DIM)[:, None] * SLC_BLOCK
    ovl = np.clip(np.minimum(c_start + CMP_LEN, s_start + SLC_BLOCK) - np.maximum(c_start, s_start), 0, None)
    ovl = ovl.astype(np.float32) / CMP_LEN
    ovl[:, n_rows - 1] = 0.0
    ovl[s_len // SLC_BLOCK:, :] = 0.0
    return jnp.asarray(ovl)


def _slc_kernel(q_ref, mb_ref, k_ref, e2_ref, vt_ref, bias_ref, o_ref, qs_scr, m_scr, acc_scr, *, n_near):
    t = FLASH_T
    i = pl.program_id(1)
    lo = _lane_lo((t, LANES))
    n_groups = NSA_HEADS // 2
    mb = mb_ref[0]
    for g in range(n_groups):
        qg = q_ref[0, :, g * LANES:(g + 1) * LANES]
        qs_scr[2 * g] = jnp.where(lo, qg, mb)
        qs_scr[2 * g + 1] = jnp.where(lo, mb, qg)
    _flash_init(m_scr, acc_scr)

    def tile(j, kind):
        start = pl.multiple_of(j * t, t)
        k_tile = k_ref[0, pl.ds(start, t), :]
        e_tile = e2_ref[pl.ds(start, t), :]
        k_sel = (jnp.where(lo, k_tile, e_tile), jnp.where(lo, e_tile, k_tile))
        scores = [_dot_nt(k_sel[pos % 2], qs_scr[pos]) for pos in range(NSA_HEADS)]
        for pos in range(NSA_HEADS):
            kv = pos % 2
            st = scores[pos]
            if kind == "near":
                st = bias_ref[pos, i - j] + st
            elif kind == "diag":
                st = bias_ref[pos, 0] + st
            _flash_update(pos, st, vt_ref[0, kv * VT_ROWS:(kv + 1) * VT_ROWS, pl.ds(start, t)], m_scr, acc_scr)

    def far_body(j, carry):
        tile(j, "far")
        return carry

    def near_body(j, carry):
        tile(j, "near")
        return carry

    first_near = jnp.maximum(i - (n_near - 1), 0)
    lax.fori_loop(0, first_near, far_body, 0)
    lax.fori_loop(first_near, i, near_body, 0)
    tile(i, "diag")
    _flash_finish(o_ref, n_groups, acc_scr)


def _slc_attention(nsa_q, mask_bias, kv4, e2, bias):
    b, s, width = nsa_q.shape
    t = FLASH_T
    n_near = bias.shape[1]
    vt = _value_slabs(kv4[:, :, LANES:2 * LANES], 2)
    return pl.pallas_call(
        functools.partial(_slc_kernel, n_near=n_near),
        grid=(b, s // t),
        in_specs=[pl.BlockSpec((1, t, width), lambda i, j: (i, j, 0)),
                  pl.BlockSpec((1, t, LANES), lambda i, j: (i, j, 0)),
                  pl.BlockSpec((1, s, LANES), lambda i, j: (i, 0, 0)),
                  pl.BlockSpec((s, LANES), lambda i, j: (0, 0)),
                  pl.BlockSpec((1, 2 * VT_ROWS, s), lambda i, j: (i, 0, 0)),
                  pl.BlockSpec(bias.shape, lambda i, j: (0, 0, 0, 0))],
        out_specs=pl.BlockSpec((1, t, width), lambda i, j: (i, j, 0)),
        out_shape=jax.ShapeDtypeStruct((b, s, width), F32),
        scratch_shapes=[pltpu.VMEM((NSA_HEADS, t, LANES), BF16),
                        pltpu.VMEM((NSA_HEADS, SUBLANES, t), F32),
                        pltpu.VMEM((NSA_HEADS, VT_ROWS, t), F32)],
        compiler_params=_params(2),
    )(nsa_q, mask_bias, kv4, e2, vt, bias)


def _block_onehot(s_len):
    blk = np.arange(s_len)[:, None] // SLC_BLOCK
    lane = np.arange(LANES)[None, :] % HEAD_DIM
    return jnp.asarray((blk == lane).astype(np.float32), dtype=BF16)


def _out_proj_kernel(x_ref, mod_ref, swa_ref, fox_ref, cmp_ref, slc_ref, win_ref, gates_ref, gn_ref, w_ref,
                     post_ref, o_ref):
    n_swa = SWA_HEADS * HEAD_DIM
    n_fox = FOX_HEADS * HEAD_DIM
    n_nsa = NSA_HEADS * HEAD_DIM
    gates = gates_ref[0].astype(F32)
    o_nsa = (gates[:, 0:n_nsa] * cmp_ref[0] + gates[:, n_nsa:2 * n_nsa] * slc_ref[0]
             + gates[:, 2 * n_nsa:3 * n_nsa] * win_ref[0])
    a = _rms(swa_ref[0], gn_ref[:, 0:n_swa]).astype(BF16)
    b = _rms(fox_ref[0], gn_ref[:, n_swa:n_swa + n_fox]).astype(BF16)
    c = _rms(o_nsa, gn_ref[:, n_swa + n_fox:]).astype(BF16)
    y = (jnp.dot(a, w_ref[0:n_swa, :], preferred_element_type=F32)
         + jnp.dot(b, w_ref[n_swa:n_swa + n_fox, :], preferred_element_type=F32)
         + jnp.dot(c, w_ref[n_swa + n_fox:, :], preferred_element_type=F32))
    o_ref[0] = x_ref[0] + mod_ref[0, 2:3, :] * _rms(y, post_ref[...])


def _out_proj(x, mod, o_swa, o_fox, o_cmp, o_slc, o_win, gates, gn, w, post):
    b, s, d = x.shape

    def rows(width):
        return pl.BlockSpec((1, ROW_TILE, width), lambda i, j: (i, j, 0))

    return pl.pallas_call(
        _out_proj_kernel,
        grid=(b, s // ROW_TILE),
        in_specs=[rows(d),
                  pl.BlockSpec((1, ADA_CHUNKS, d), lambda i, j: (i, 0, 0)),
                  rows(o_swa.shape[2]), rows(o_fox.shape[2]), rows(o_cmp.shape[2]), rows(o_slc.shape[2]),
                  rows(o_win.shape[2]), rows(gates.shape[2]),
                  pl.BlockSpec((1, d), lambda i, j: (0, 0)),
                  pl.BlockSpec((d, d), lambda i, j: (0, 0)),
                  pl.BlockSpec((1, d), lambda i, j: (0, 0))],
        out_specs=rows(d),
        out_shape=jax.ShapeDtypeStruct((b, s, d), F32),
        compiler_params=_params(2),
    )(x, mod, o_swa, o_fox, o_cmp, o_slc, o_win, gates, gn, w, post)


def _ffn_kernel(x_ref, mod_ref, pre_ref, wg_ref, wu_ref, wd_ref, post_ref, o_ref):
    x = x_ref[0]
    h = (_rms(x, pre_ref[...]) * (1.0 + mod_ref[0, 4:5, :]) + mod_ref[0, 3:4, :]).astype(BF16)
    y = jnp.zeros(x.shape, F32)
    for c in range(wg_ref.shape[0]):
        gate = jnp.dot(h, wg_ref[c], preferred_element_type=F32)
        up = jnp.dot(h, wu_ref[c], preferred_element_type=F32)
        act = (gate * jax.nn.sigmoid(gate) * up).astype(BF16)
        y = y + jnp.dot(act, wd_ref[c], preferred_element_type=F32)
    o_ref[0] = x + mod_ref[0, 5:6, :] * _rms(y, post_ref[...])


def _ffn(x, mod, pre, wg, wu, wd, post):
    b, s, d = x.shape
    n_chunks = wg.shape[0]
    rows = pl.BlockSpec((1, ROW_TILE, d), lambda i, j: (i, j, 0))
    vec = pl.BlockSpec((1, d), lambda i, j: (0, 0))
    return pl.pallas_call(
        _ffn_kernel,
        grid=(b, s // ROW_TILE),
        in_specs=[rows,
                  pl.BlockSpec((1, ADA_CHUNKS, d), lambda i, j: (i, 0, 0)),
                  vec,
                  pl.BlockSpec((n_chunks, d, FFN_CHUNK), lambda i, j: (0, 0, 0)),
                  pl.BlockSpec((n_chunks, d, FFN_CHUNK), lambda i, j: (0, 0, 0)),
                  pl.BlockSpec((n_chunks, FFN_CHUNK, d), lambda i, j: (0, 0, 0)),
                  vec],
        out_specs=rows,
        out_shape=jax.ShapeDtypeStruct((b, s, d), F32),
        compiler_params=_params(2),
    )(x, mod, pre, wg, wu, wd, post)


def _forget_lanes():
    lanes, heads = [], []
    for h in range(FOX_HEADS):
        base = (h // 2) * LANES + (HEAD_DIM if h % 2 == 0 else 0)
        for j in range(KEY_BIAS_TERMS):
            lanes.append(base + j)
            heads.append(h)
    return np.array(lanes), np.array(heads)


def _in_proj_layout():
    d = HEAD_DIM
    o_qa, o_ka, o_qb, o_fb, o_qc, o_kc = 0, 256, 512, 1280, 1284, 1796
    o_gc = 2564
    scale = LOG2E / math.sqrt(d)

    def head_cols(base, heads):
        return np.concatenate([np.arange(base + h * d, base + (h + 1) * d) for h in heads])

    cols = [head_cols(o_qa, SWA_POS), np.arange(o_ka, o_qb),
            np.arange(o_qb, o_fb),
            head_cols(o_qc, NSA_POS),
            np.arange(o_kc, o_gc)]
    scales = [np.full(256, scale), np.ones(256), np.full(256, scale), np.ones(512), np.full(512, scale),
              np.ones(768)]
    for branch in range(3):
        cols.append(np.repeat(np.array([o_gc + h * 3 + branch for h in NSA_POS]), d))
        scales.append(np.ones(NSA_HEADS * d))
    lanes, heads = _forget_lanes()
    misc_cols = np.zeros(SEG_MISC[1] - SEG_MISC[0], np.int64)
    misc_scale = np.zeros(SEG_MISC[1] - SEG_MISC[0])
    misc_cols[lanes] = o_fb + heads
    misc_scale[lanes] = 1.0
    cols.append(misc_cols)
    scales.append(misc_scale)
    return np.concatenate(cols), np.concatenate(scales).astype(np.float32)


def _head_perm(pos):
    return np.concatenate([np.arange(h * HEAD_DIM, (h + 1) * HEAD_DIM) for h in pos])


def kernel(x, c, rel_bias, ada_w, ada_b, attn_pre_norm, attn_post_norm, ffn_pre_norm, ffn_post_norm, w_in,
           forget_bias, swa_sinks, cmp_pos, cmp_w1, cmp_w2, group_norm, w_out, ffn_w_gate, ffn_w_up, ffn_w_down):
    b, s, d = x.shape
    depth = w_in.shape[0]
    hidden = ffn_w_gate.shape[2]
    assert s % (2 * FLASH_T) == 0 and s // SLC_BLOCK <= HEAD_DIM and hidden % FFN_CHUNK == 0

    cols, scales = _in_proj_layout()
    w_all = (w_in[:, :, cols] * scales).astype(BF16)
    lanes, heads = _forget_lanes()
    fbias_all = jnp.zeros((depth, 1, SEG_MISC[1] - SEG_MISC[0]), F32).at[:, 0, lanes].set(
        forget_bias[:, heads].astype(F32))
    swa_perm = _head_perm(SWA_POS)
    nsa_perm = _head_perm(NSA_POS)
    n_swa, n_fox = SWA_HEADS * HEAD_DIM, FOX_HEADS * HEAD_DIM
    mix_perm = np.concatenate([swa_perm, n_swa + np.arange(n_fox), n_swa + n_fox + nsa_perm])
    gn_all = group_norm[:, mix_perm].astype(F32)
    w_out_all = w_out[:, mix_perm, :].astype(BF16)
    n_chunks = hidden // FFN_CHUNK
    wg_all = ffn_w_gate.reshape(depth, d, n_chunks, FFN_CHUNK).transpose(0, 2, 1, 3).astype(BF16)
    wu_all = ffn_w_up.reshape(depth, d, n_chunks, FFN_CHUNK).transpose(0, 2, 1, 3).astype(BF16)
    wd_all = ffn_w_down.reshape(depth, n_chunks, FFN_CHUNK, d).astype(BF16)

    tab_swa = rel_bias[:, np.array(SWA_POS)].astype(F32)
    tab_nsa = rel_bias[:, SWA_HEADS + np.array(NSA_POS)].astype(F32)
    bias_swa = _bias_table(tab_swa, _band_buckets_t(SWA_TILE, SWA_WINDOW))
    bias_win = _bias_table(tab_nsa, _band_buckets_t(WIN_TILE, NSA_WINDOW))
    bias_slc = _bias_table(tab_nsa, _toeplitz_buckets_t(FLASH_T, _near_tiles(FLASH_T)), subtract_last=True)
    n_rows = s // CMP_STRIDE
    bias_cmp = _bias_table(tab_nsa, _cmp_buckets_t(s, n_rows))
    overlap_t = _overlap_t(s)
    e2 = _block_onehot(s)

    mod_all = _adaln(c.astype(F32), ada_w.astype(F32), ada_b.astype(F32)).reshape(depth, b, ADA_CHUNKS, d)

    for layer in range(depth):
        mod = mod_all[layer]
        swa_qkv, fox_qkv, nsa_q, kc, vc, kv4, gates, misc = _in_proj(
            x, mod, attn_pre_norm[layer].reshape(1, d).astype(F32), w_all[layer])
        o_swa = _banded_attention(swa_qkv, swa_qkv, 2, swa_qkv[:, :, 3 * LANES:], bias_swa,
                                  sinks=swa_sinks[layer][np.array(SWA_POS)].astype(F32))
        o_fox = _fox_attention(fox_qkv, _fox_key_terms(misc, fbias_all[layer]))
        cmp_kv = _compress(kc, vc, cmp_pos[layer], cmp_w1[layer], cmp_w2[layer])
        o_cmp, mask_bias = _select(nsa_q, cmp_kv, bias_cmp, overlap_t)
        o_slc = _slc_attention(nsa_q, mask_bias, kv4, e2, bias_slc)
        o_win = _banded_attention(nsa_q, kv4, 2, kv4[:, :, 3 * LANES:], bias_win)
        x = _out_proj(x, mod, o_swa, o_fox, o_cmp, o_slc, o_win, gates, gn_all[layer].reshape(1, d),
                      w_out_all[layer], attn_post_norm[layer].reshape(1, d).astype(F32))
        x = _ffn(x, mod, ffn_pre_norm[layer].reshape(1, d).astype(F32), wg_all[layer], wu_all[layer],
                 wd_all[layer], ffn_post_norm[layer].reshape(1, d).astype(F32))
    return x
```

```python
import functools
import math

import numpy as np
import jax
import jax.numpy as jnp
from jax import lax
from jax.experimental import pallas as pl
from jax.experimental.pallas import tpu as pltpu

F32 = jnp.float32
BF16 = jnp.bfloat16
HIGHEST = lax.Precision.HIGHEST

LANES = 128
SUBLANES = 8
VMEM_LIMIT = 56 * 1024 * 1024

HEAD_DIM = 64
SWA_HEADS = 4
SWA_WINDOW = 128
FOX_HEADS = 4
NSA_HEADS = 8
CMP_LEN = 32
CMP_STRIDE = 16
CMP_HIDDEN = 2 * HEAD_DIM
SLC_BLOCK = 64
TOPK = 16
NSA_WINDOW = 512
REL_BUCKETS = 32
REL_MAX_DISTANCE = 1024
ZERO_BUCKET = -2
RMS_EPS = 1e-6
NEG = -1e30
FORCE = 1e30
ADA_CHUNKS = 6
LOG2E = math.log2(math.e)

SWA_POS = (0, 2, 1, 3)
NSA_POS = (0, 4, 1, 5, 2, 6, 3, 7)

SWA_TILE = 128
WIN_TILE = 256
FLASH_T = 256
SEL_TQ = 256
ROW_TILE = 512
FFN_CHUNK = 256
VT_ROWS = HEAD_DIM + 16
KEY_BIAS_TERMS = 3

SEG_SWA = (0, 512)
SEG_FOX = (512, 1280)
SEG_NSAQ = (1280, 1792)
SEG_KC = (1792, 1920)
SEG_VC = (1920, 2048)
SEG_KV4 = (2048, 2560)
SEG_GATES = (2560, 4096)
SEG_MISC = (4096, 4352)


def _params(n_grid, vmem=VMEM_LIMIT):
    return pltpu.CompilerParams(dimension_semantics=("parallel",) * n_grid, vmem_limit_bytes=vmem)


def _dot_nt(a, b):
    return lax.dot_general(a, b, (((1,), (1,)), ((), ())), preferred_element_type=F32)


def _lane_lo(shape):
    return lax.broadcasted_iota(jnp.int32, shape, len(shape) - 1) < HEAD_DIM


def _adaln_kernel(c_ref, w_ref, b_ref, o_ref):
    c = c_ref[...]
    act = c * jax.nn.sigmoid(c)
    o_ref[0] = jnp.dot(act, w_ref[0], precision=HIGHEST, preferred_element_type=F32) + b_ref[0]


def _adaln(c, ada_w, ada_b):
    depth, d, n = ada_w.shape
    b = c.shape[0]
    return pl.pallas_call(
        _adaln_kernel,
        grid=(depth, n // d),
        in_specs=[pl.BlockSpec((b, d), lambda l, j: (0, 0)),
                  pl.BlockSpec((1, d, d), lambda l, j: (l, 0, j)),
                  pl.BlockSpec((1, 1, d), lambda l, j: (l, 0, j))],
        out_specs=pl.BlockSpec((1, b, d), lambda l, j: (l, 0, j)),
        out_shape=jax.ShapeDtypeStruct((depth, b, n), F32),
        compiler_params=_params(2),
    )(c, ada_w, ada_b.reshape(depth, 1, n))


def _t5_bucket(dist):
    n = jnp.maximum(dist, 0)
    max_exact = REL_BUCKETS // 2
    nf = jnp.maximum(n, 1).astype(jnp.float32)
    large = max_exact + (jnp.log(nf / max_exact) / math.log(REL_MAX_DISTANCE / max_exact)
                         * (REL_BUCKETS - max_exact)).astype(jnp.int32)
    large = jnp.minimum(large, REL_BUCKETS - 1)
    return jnp.where(n < max_exact, n, large)


def _bias_table_kernel(tab_ref, bucket_ref, o_ref, *, subtract_last):
    h = pl.program_id(0)
    bucket = bucket_ref[0]
    off = tab_ref[REL_BUCKETS - 1, h] if subtract_last else 0.0
    acc = jnp.full(bucket.shape, NEG, F32)
    for k in range(REL_BUCKETS):
        acc = jnp.where(bucket == k, (tab_ref[k, h] - off) * LOG2E, acc)
    o_ref[0, 0] = jnp.where(bucket == ZERO_BUCKET, 0.0, acc)


def _bias_table(table, bucket, subtract_last=False):
    n_heads = table.shape[1]
    n, r, c = bucket.shape
    return pl.pallas_call(
        functools.partial(_bias_table_kernel, subtract_last=subtract_last),
        grid=(n_heads, n),
        in_specs=[pl.BlockSpec(memory_space=pltpu.SMEM),
                  pl.BlockSpec((1, r, c), lambda h, i: (i, 0, 0))],
        out_specs=pl.BlockSpec((1, 1, r, c), lambda h, i: (h, i, 0, 0)),
        out_shape=jax.ShapeDtypeStruct((n_heads, n, r, c), F32),
        compiler_params=_params(2),
    )(table, bucket)


def _band_buckets_t(tile, window):
    n_back = -(-(window - 1) // tile)
    t = jnp.arange(n_back + 1)[:, None, None]
    key = jnp.arange(tile)[None, :, None]
    query = jnp.arange(tile)[None, None, :]
    dist = query + (n_back - t) * tile - key
    return jnp.where((dist >= 0) & (dist < window), _t5_bucket(dist), -1).astype(jnp.int32)


def _toeplitz_buckets_t(tile, n_tiles):
    m = jnp.arange(n_tiles)[:, None, None]
    key = jnp.arange(tile)[None, :, None]
    query = jnp.arange(tile)[None, None, :]
    dist = m * tile + query - key
    near = jnp.where(dist >= 0, _t5_bucket(dist), -1).astype(jnp.int32)
    return jnp.concatenate([near, jnp.full((1, tile, tile), ZERO_BUCKET, jnp.int32),
                            jnp.full((1, tile, tile), -1, jnp.int32)])


def _cmp_buckets_t(s_len, n_rows):
    n_c = n_rows - 1
    tile = jnp.arange(s_len // SEL_TQ)[:, None, None]
    n = jnp.arange(n_rows)[None, :, None]
    t = tile * SEL_TQ + jnp.arange(SEL_TQ)[None, None, :]
    dist = t - (n * CMP_STRIDE + CMP_LEN - 1)
    return jnp.where((dist >= 0) & (n < n_c), _t5_bucket(dist), -1).astype(jnp.int32)


def _near_tiles(tile):
    max_exact = REL_BUCKETS // 2
    first_const = math.ceil(max_exact * (REL_MAX_DISTANCE / max_exact) ** ((max_exact - 1) / max_exact)) + 1
    m = 1
    while m * tile - (tile - 1) < first_const:
        m += 1
    return m


def _rms(x, gain):
    return x * lax.rsqrt(jnp.mean(x * x, axis=-1, keepdims=True) + RMS_EPS) * gain


def _in_proj_kernel(x_ref, mod_ref, gain_ref, w_ref, swa_ref, fox_ref, nsaq_ref, kc_ref, vc_ref, kv4_ref,
                    gates_ref, misc_ref):
    x = x_ref[0]
    h = _rms(x, gain_ref[...]) * (1.0 + mod_ref[0, 1:2, :]) + mod_ref[0, 0:1, :]
    hb = h.astype(BF16)

    def seg(bounds):
        return jnp.dot(hb, w_ref[:, bounds[0]:bounds[1]], preferred_element_type=F32)

    swa_ref[0] = seg(SEG_SWA).astype(BF16)
    fox_ref[0] = seg(SEG_FOX).astype(BF16)
    nsaq_ref[0] = seg(SEG_NSAQ).astype(BF16)
    kc_ref[0] = seg(SEG_KC).astype(BF16)
    vc_ref[0] = seg(SEG_VC).astype(BF16)
    kv4_ref[0] = seg(SEG_KV4).astype(BF16)
    gates_ref[0] = jax.nn.sigmoid(seg(SEG_GATES)).astype(BF16)
    misc_ref[0] = seg(SEG_MISC)


def _in_proj(x, mod, gain, w):
    b, s, d = x.shape
    n = w.shape[1]
    widths = [hi - lo for lo, hi in (SEG_SWA, SEG_FOX, SEG_NSAQ, SEG_KC, SEG_VC, SEG_KV4, SEG_GATES, SEG_MISC)]
    dtypes = [BF16] * 7 + [F32]
    return pl.pallas_call(
        _in_proj_kernel,
        grid=(b, s // ROW_TILE),
        in_specs=[pl.BlockSpec((1, ROW_TILE, d), lambda i, j: (i, j, 0)),
                  pl.BlockSpec((1, ADA_CHUNKS, d), lambda i, j: (i, 0, 0)),
                  pl.BlockSpec((1, d), lambda i, j: (0, 0)),
                  pl.BlockSpec((d, n), lambda i, j: (0, 0))],
        out_specs=[pl.BlockSpec((1, ROW_TILE, wd), lambda i, j: (i, j, 0)) for wd in widths],
        out_shape=[jax.ShapeDtypeStruct((b, s, wd), dt) for wd, dt in zip(widths, dtypes)],
        compiler_params=_params(2),
    )(x, mod, gain, w)


def _banded_kernel(*refs, n_back, n_groups, has_sink, t):
    if has_sink:
        sink_ref, q_ref, k_ref, vt_ref, bias_ref, o_ref = refs
    else:
        q_ref, k_ref, vt_ref, bias_ref, o_ref = refs
    i = pl.program_id(1)
    lo = _lane_lo((t, LANES))
    n_tiles = n_back + 1

    def run(all_valid):
        starts = [pl.multiple_of(jnp.maximum(i - n_back + tt, 0) * t, t) for tt in range(n_tiles)]
        k_tiles = [k_ref[0, pl.ds(start, t), :] for start in starts]

        def scores(g):
            qg = q_ref[0, :, g * LANES:(g + 1) * LANES]
            zero = jnp.zeros_like(qg)
            qms = (jnp.where(lo, qg, zero), jnp.where(lo, zero, qg))
            return [[bias_ref[2 * g + half, tt] + _dot_nt(k_tiles[tt], qms[half]) for tt in range(n_tiles)]
                    for half in range(2)]

        def softmax_pv(g, sts):
            pair = []
            for half in range(2):
                tiles = sts[half]
                if not all_valid:
                    tiles = [jnp.where(i - n_back + tt >= 0, st, NEG) if tt < n_back else st
                             for tt, st in enumerate(tiles)]
                m = None
                for st in tiles:
                    part = st.reshape(t // SUBLANES, SUBLANES, t).max(axis=0)
                    m = part if m is None else jnp.maximum(m, part)
                m = _all_sublanes(m, jnp.maximum)
                if has_sink:
                    sink = sink_ref[2 * g + half] * LOG2E
                    m = jnp.maximum(m, sink)
                acc = None
                for tt, st in enumerate(tiles):
                    p = jnp.exp2(st.reshape(t // SUBLANES, SUBLANES, t) - m[None]).reshape(t, t).astype(BF16)
                    part = jnp.dot(vt_ref[0, half * VT_ROWS:(half + 1) * VT_ROWS, pl.ds(starts[tt], t)], p,
                                   preferred_element_type=F32)
                    acc = part if acc is None else acc + part
                denom = _all_sublanes(acc[HEAD_DIM:HEAD_DIM + SUBLANES, :], jnp.add)
                if has_sink:
                    denom = denom + jnp.exp2(sink - m)
                out = acc[0:HEAD_DIM, :].reshape(HEAD_DIM // SUBLANES, SUBLANES, t) / denom[None]
                pair.append(out.reshape(HEAD_DIM, t))
            o_ref[0, :, g * LANES:(g + 1) * LANES] = jnp.concatenate(pair, axis=0).T

        pending = scores(0)
        for g in range(n_groups):
            current = pending
            if g + 1 < n_groups:
                pending = scores(g + 1)
            softmax_pv(g, current)

    @pl.when(i >= n_back)
    def _():
        run(True)

    @pl.when(i < n_back)
    def _():
        run(False)


def _banded_attention(q_arr, k_arr, k_blk, v, bias, sinks=None):
    b, s, _ = q_arr.shape
    n_pos, n_tiles, t = bias.shape[0], bias.shape[1], bias.shape[2]
    width = n_pos * HEAD_DIM
    vt = _value_slabs(v, 2)
    in_specs = [pl.BlockSpec((1, t, width), lambda i, j: (i, j, 0)),
                pl.BlockSpec((1, s, LANES), lambda i, j: (i, 0, k_blk)),
                pl.BlockSpec((1, 2 * VT_ROWS, s), lambda i, j: (i, 0, 0)),
                pl.BlockSpec(bias.shape, lambda i, j: (0, 0, 0, 0))]
    args = [q_arr, k_arr, vt, bias]
    if sinks is not None:
        in_specs = [pl.BlockSpec(memory_space=pltpu.SMEM)] + in_specs
        args = [sinks] + args
    return pl.pallas_call(
        functools.partial(_banded_kernel, n_back=n_tiles - 1, n_groups=n_pos // 2, has_sink=sinks is not None, t=t),
        grid=(b, s // t),
        in_specs=in_specs,
        out_specs=pl.BlockSpec((1, t, width), lambda i, j: (i, j, 0)),
        out_shape=jax.ShapeDtypeStruct((b, s, width), F32),
        compiler_params=_params(2),
    )(*args)


def _all_sublanes(x, op):
    for shift in (4, 2, 1):
        x = op(x, pltpu.roll(x, shift, 0))
    return x


def _flash_init(m_scr, acc_scr):
    m_scr[...] = jnp.full(m_scr.shape, NEG, F32)
    acc_scr[...] = jnp.zeros(acc_scr.shape, F32)


def _flash_update(h, st_ref, tile_max, vt_h, m_scr, acc_scr):
    tk, tq = st_ref.shape
    m_prev = m_scr[h]
    m_new = _all_sublanes(jnp.maximum(m_prev, tile_max), jnp.maximum)
    alpha = jnp.exp2(m_prev - m_new)
    p = jnp.exp2(st_ref[...].reshape(tk // SUBLANES, SUBLANES, tq) - m_new[None]).reshape(tk, tq).astype(BF16)
    acc = acc_scr[h].reshape(VT_ROWS // SUBLANES, SUBLANES, tq) * alpha[None]
    acc_scr[h] = acc.reshape(VT_ROWS, tq) + jnp.dot(vt_h, p, preferred_element_type=F32)
    m_scr[h] = m_new


def _flash_finish(o_ref, n_groups, acc_scr):
    tq = acc_scr.shape[2]
    for g in range(n_groups):
        pair = []
        for h in (2 * g, 2 * g + 1):
            denom = _all_sublanes(acc_scr[h, HEAD_DIM:HEAD_DIM + SUBLANES, :], jnp.add)
            out = acc_scr[h, 0:HEAD_DIM, :].reshape(HEAD_DIM // SUBLANES, SUBLANES, tq) / denom[None]
            pair.append(out.reshape(HEAD_DIM, tq))
        o_ref[0, :, g * LANES:(g + 1) * LANES] = jnp.concatenate(pair, axis=0).T


def _flash_pipeline(i, n_heads, qk_scores, bias_tile, vt_slab, s_scr, tmax_scr, m_scr, acc_scr):
    def qk(j, slot):
        for h, sc in enumerate(qk_scores(jnp.minimum(j, i))):
            st = sc + bias_tile(h, j)
            s_scr[slot, h] = st
            tmax_scr[slot, h] = st.reshape(st.shape[0] // SUBLANES, SUBLANES, st.shape[1]).max(axis=0)

    def softmax(j, slot):
        for h in range(n_heads):
            _flash_update(h, s_scr.at[slot, h], tmax_scr[slot, h], vt_slab(h, jnp.minimum(j, i)), m_scr, acc_scr)

    qk(0, 0)

    def body(trip, carry):
        j = 2 * trip
        qk(j + 1, 1)
        softmax(j, 0)
        qk(j + 2, 0)
        softmax(j + 1, 1)
        return carry

    lax.fori_loop(0, (i + 2) // 2, body, 0)


def _value_slabs(v, n_heads):
    b, s, _ = v.shape
    vt = v.transpose(0, 2, 1).reshape(b, n_heads, HEAD_DIM, s)
    extra = np.zeros((1, 1, VT_ROWS - HEAD_DIM, 1), np.float32)
    extra[0, 0, 0, 0] = 1.0
    extra = jnp.broadcast_to(jnp.asarray(extra, dtype=v.dtype), (b, n_heads, VT_ROWS - HEAD_DIM, s))
    return jnp.concatenate([vt, extra], axis=2).reshape(b, n_heads * VT_ROWS, s)


def _fox_aug_kernel(misc_ref, fbias_ref, tri_ref, o_ref):
    s_len, width = misc_ref.shape[1], misc_ref.shape[2]
    term = lax.broadcasted_iota(jnp.int32, (LANES, width), 1) % HEAD_DIM
    carry = jnp.zeros((1, width), F32)
    for c in range(s_len // LANES):
        z = misc_ref[0, c * LANES:(c + 1) * LANES, :] + fbias_ref[...]
        log_f = jnp.minimum(z, 0.0) - jnp.log1p(jnp.exp(-jnp.abs(z)))
        cum = jnp.dot(tri_ref[...], log_f, precision=HIGHEST, preferred_element_type=F32) + carry
        carry = cum[LANES - 1:LANES, :]
        x = cum * (-LOG2E)
        hi = x.astype(BF16).astype(F32)
        rest = x - hi
        mid = rest.astype(BF16).astype(F32)
        low = rest - mid
        out = jnp.where(term == 0, hi, jnp.where(term == 1, mid, jnp.where(term == 2, low, 0.0)))
        o_ref[0, c * LANES:(c + 1) * LANES, :] = out.astype(BF16)


def _fox_key_terms(misc, fbias):
    b, s, width = misc.shape
    tri = jnp.asarray(np.tril(np.ones((LANES, LANES), np.float32)))
    return pl.pallas_call(
        _fox_aug_kernel,
        grid=(b,),
        in_specs=[pl.BlockSpec((1, s, width), lambda i: (i, 0, 0)),
                  pl.BlockSpec((1, width), lambda i: (0, 0)),
                  pl.BlockSpec((LANES, LANES), lambda i: (0, 0))],
        out_specs=pl.BlockSpec((1, s, width), lambda i: (i, 0, 0)),
        out_shape=jax.ShapeDtypeStruct((b, s, width), BF16),
        compiler_params=_params(1),
    )(misc, fbias, tri)


def _fox_kernel(q_ref, k_ref, aug_ref, vt_ref, mask_ref, o_ref, qs_scr, s_scr, tmax_scr, m_scr, acc_scr):
    t = FLASH_T
    i = pl.program_id(1)
    lo = _lane_lo((t, LANES))
    lane = lax.broadcasted_iota(jnp.int32, (t, LANES), 1)
    ones = jnp.where(lane % HEAD_DIM < KEY_BIAS_TERMS, 1.0, 0.0).astype(BF16)
    n_groups = FOX_HEADS // 2
    for g in range(n_groups):
        qg = q_ref[0, :, g * LANES:(g + 1) * LANES]
        qs_scr[2 * g] = jnp.where(lo, qg, ones)
        qs_scr[2 * g + 1] = jnp.where(lo, ones, qg)
    _flash_init(m_scr, acc_scr)

    def qk_scores(j):
        start = pl.multiple_of(j * t, t)
        scores = []
        for g in range(n_groups):
            k_tile = k_ref[0, pl.ds(start, t), g * LANES:(g + 1) * LANES]
            a_tile = aug_ref[0, pl.ds(start, t), g * LANES:(g + 1) * LANES]
            k_sel = (jnp.where(lo, k_tile, a_tile), jnp.where(lo, a_tile, k_tile))
            for half in range(2):
                scores.append(_dot_nt(k_sel[half], qs_scr[2 * g + half]))
        return scores

    def bias_tile(h, j):
        return mask_ref[jnp.where(j > i, 2, jnp.minimum(i - j, 1))]

    def vt_slab(h, j):
        return vt_ref[0, h * VT_ROWS:(h + 1) * VT_ROWS, pl.ds(pl.multiple_of(j * t, t), t)]

    _flash_pipeline(i, FOX_HEADS, qk_scores, bias_tile, vt_slab, s_scr, tmax_scr, m_scr, acc_scr)
    _flash_finish(o_ref, n_groups, acc_scr)


def _fox_attention(fox_qkv, key_terms):
    b, s, _ = fox_qkv.shape
    width = FOX_HEADS * HEAD_DIM
    t = FLASH_T
    vt = _value_slabs(fox_qkv[:, :, 2 * width:], FOX_HEADS)
    idx = np.arange(t)
    diag = np.where(idx[:, None] <= idx[None, :], 0.0, NEG)
    masks = jnp.asarray(np.stack([diag, np.zeros((t, t)), np.full((t, t), NEG)]).astype(np.float32))
    return pl.pallas_call(
        _fox_kernel,
        grid=(b, s // t),
        in_specs=[pl.BlockSpec((1, t, width), lambda i, j: (i, j, 0)),
                  pl.BlockSpec((1, s, width), lambda i, j: (i, 0, 1)),
                  pl.BlockSpec((1, s, width), lambda i, j: (i, 0, 0)),
                  pl.BlockSpec((1, FOX_HEADS * VT_ROWS, s), lambda i, j: (i, 0, 0)),
                  pl.BlockSpec(masks.shape, lambda i, j: (0, 0, 0))],
        out_specs=pl.BlockSpec((1, t, width), lambda i, j: (i, j, 0)),
        out_shape=jax.ShapeDtypeStruct((b, s, width), F32),
        scratch_shapes=[pltpu.VMEM((FOX_HEADS, t, LANES), BF16),
                        pltpu.VMEM((2, FOX_HEADS, t, t), F32),
                        pltpu.VMEM((2, FOX_HEADS, SUBLANES, t), F32),
                        pltpu.VMEM((FOX_HEADS, SUBLANES, t), F32),
                        pltpu.VMEM((FOX_HEADS, VT_ROWS, t), F32)],
        compiler_params=_params(2),
    )(fox_qkv, fox_qkv, key_terms, vt, masks)


def _compress_kernel(x_ref, pe_ref, w1a_ref, w1b_ref, w2_ref, o_ref):
    x = x_ref[0].astype(F32)
    n_rows = x.shape[0]

    def mm(a, w):
        return jnp.dot(a, w, precision=HIGHEST, preferred_element_type=F32)

    first = mm(x, w1a_ref[0])
    second = mm(x, w1b_ref[0])
    pe_term = (mm(pe_ref[0, 0], w1a_ref[0]) + mm(pe_ref[0, 1], w1b_ref[0]))[0:1, :]
    pre = first + pltpu.roll(second, n_rows - 1, 0) + pe_term
    hid = 0.5 * pre * (1.0 + jnp.tanh(math.sqrt(2.0 / math.pi) * (pre + 0.044715 * (pre * pre * pre))))
    o_ref[0, 0] = mm(hid, w2_ref[0])


def _compress(kc, vc, cmp_pos, cmp_w1, cmp_w2):
    b, s, _ = kc.shape
    n_rows = s // CMP_STRIDE
    half = CMP_LEN // 2
    feat = CMP_STRIDE * LANES
    x = jnp.stack([kc.reshape(b, n_rows, feat), vc.reshape(b, n_rows, feat)])
    eye = jnp.eye(2, dtype=F32)
    w1 = cmp_w1.astype(F32).reshape(2, CMP_LEN, HEAD_DIM, CMP_HIDDEN)
    w1a = jnp.einsum('wldj,hg->wlhdgj', w1[:, :half], eye).reshape(2, feat, 2 * CMP_HIDDEN)
    w1b = jnp.einsum('wldj,hg->wlhdgj', w1[:, half:], eye).reshape(2, feat, 2 * CMP_HIDDEN)
    w2 = jnp.einsum('wjd,hg->whjgd', cmp_w2.astype(F32), eye).reshape(2, 2 * CMP_HIDDEN, LANES)
    w2 = jnp.concatenate([w2, jnp.roll(w2, HEAD_DIM, axis=2)], axis=2)
    pe = jnp.broadcast_to(cmp_pos.astype(F32).reshape(2, 2, half, 1, HEAD_DIM), (2, 2, half, 2, HEAD_DIM))
    pe = jnp.broadcast_to(pe.reshape(2, 2, 1, feat), (2, 2, 8, feat))
    return pl.pallas_call(
        _compress_kernel,
        grid=(2, b),
        in_specs=[pl.BlockSpec((None, 1, n_rows, feat), lambda w, i: (w, i, 0, 0)),
                  pl.BlockSpec((1, 2, 8, feat), lambda w, i: (w, 0, 0, 0)),
                  pl.BlockSpec((1, feat, 2 * CMP_HIDDEN), lambda w, i: (w, 0, 0)),
                  pl.BlockSpec((1, feat, 2 * CMP_HIDDEN), lambda w, i: (w, 0, 0)),
                  pl.BlockSpec((1, 2 * CMP_HIDDEN, 2 * LANES), lambda w, i: (w, 0, 0))],
        out_specs=pl.BlockSpec((1, 1, n_rows, 2 * LANES), lambda w, i: (w, i, 0, 0)),
        out_shape=jax.ShapeDtypeStruct((2, b, n_rows, 2 * LANES), F32),
        compiler_params=_params(2),
    )(x, pe, w1a, w1b, w2)


def _select_kernel(q_ref, kc_ref, vct_ref, bias_ref, ovl_ref, o_ref, mb_ref):
    tq = SEL_TQ
    i = pl.program_id(0)
    lo = _lane_lo((tq, LANES))
    n_rows = kc_ref.shape[2]
    lo_k = _lane_lo((n_rows, LANES))
    n_grp = n_rows // SUBLANES

    k_own = kc_ref[0, 0, :, 0:LANES]
    k_swapped = kc_ref[0, 0, :, LANES:2 * LANES]
    hi = k_own.astype(BF16)
    low = (k_swapped - k_swapped.astype(BF16).astype(F32)).astype(BF16)
    k_sel = (jnp.where(lo_k, hi, low), jnp.where(lo_k, low, hi))

    def scores(g):
        qg = q_ref[0, :, g * LANES:(g + 1) * LANES]
        swapped = pltpu.roll(qg.astype(F32), HEAD_DIM, 1).astype(BF16)
        q_dup = (jnp.where(lo, qg, swapped), jnp.where(lo, swapped, qg))
        return [bias_ref[2 * g + half, 0] + _dot_nt(k_sel[half], q_dup[half]) for half in range(2)]

    query = i * tq + lax.broadcasted_iota(jnp.int32, (SUBLANES, tq), 1)
    has_keys = query >= CMP_LEN - 1
    p_sum = [None, None]

    def softmax_pv(g, sts):
        pair = []
        for half in range(2):
            s3 = sts[half].reshape(n_grp, SUBLANES, tq)
            m = _all_sublanes(s3.max(axis=0), jnp.maximum)
            e = jnp.exp2(s3 - m[None])
            inv = jnp.where(has_keys, 1.0 / _all_sublanes(e.sum(axis=0), jnp.add), 0.0)
            p = e * inv[None]
            p_sum[half] = p if p_sum[half] is None else p_sum[half] + p
            pair.append(jnp.dot(vct_ref[0, half * HEAD_DIM:(half + 1) * HEAD_DIM, :],
                                p.reshape(n_rows, tq).astype(BF16), preferred_element_type=F32))
        o_ref[0, :, g * LANES:(g + 1) * LANES] = jnp.concatenate(pair, axis=0).T

    n_groups = NSA_HEADS // 2
    pending = scores(0)
    for g in range(n_groups):
        current = pending
        if g + 1 < n_groups:
            pending = scores(g + 1)
        softmax_pv(g, current)

    n_blk = ovl_ref.shape[0]
    blk_grp = n_blk // SUBLANES
    sub = lax.broadcasted_iota(jnp.int32, (SUBLANES, tq), 0)
    q_blk = query // SLC_BLOCK
    masks = []
    for half in (1, 0):
        imp = jnp.dot(ovl_ref[...], p_sum[half].reshape(n_rows, tq), precision=HIGHEST,
                      preferred_element_type=F32)
        rows = []
        for r in range(blk_grp):
            blk = sub + r * SUBLANES
            forced = jnp.where(blk == 0, 1, 0) + jnp.where(blk == q_blk, 1, 0) + jnp.where(blk == q_blk - 1, 1, 0)
            rows.append(jnp.where(forced > 0, FORCE,
                                  jnp.where(blk > q_blk, NEG, imp[r * SUBLANES:(r + 1) * SUBLANES, :])))
        counts = [jnp.zeros((SUBLANES, tq), jnp.int32) for _ in range(blk_grp)]
        for other in range(n_blk):
            r_other, s_other = divmod(other, SUBLANES)
            row = jnp.broadcast_to(rows[r_other][s_other:s_other + 1, :], (SUBLANES, tq))
            for r in range(blk_grp):
                if r > r_other:
                    beats = jnp.where(row >= rows[r], 1, 0)
                elif r < r_other:
                    beats = jnp.where(row > rows[r], 1, 0)
                else:
                    beats = jnp.where(sub > s_other, jnp.where(row >= rows[r], 1, 0), jnp.where(row > rows[r], 1, 0))
                counts[r] = counts[r] + beats
        masks.extend(jnp.where(cnt < TOPK, 0.0, NEG) for cnt in counts)
    for c in range(tq // LANES):
        mb_ref[0, c * LANES:(c + 1) * LANES, :] = jnp.concatenate(
            [mk[:, c * LANES:(c + 1) * LANES] for mk in masks], axis=0).T.astype(BF16)


def _select(nsa_q, cmp_kv, bias_c, overlap_t):
    b, s, width = nsa_q.shape
    n_rows = cmp_kv.shape[2]
    n_blk = overlap_t.shape[0]
    assert n_blk == HEAD_DIM
    tq = SEL_TQ
    vct = cmp_kv[1, :, :, 0:LANES].transpose(0, 2, 1).astype(BF16)
    return pl.pallas_call(
        _select_kernel,
        grid=(s // tq, b),
        in_specs=[pl.BlockSpec((1, tq, width), lambda j, i: (i, j, 0)),
                  pl.BlockSpec((1, 1, n_rows, 2 * LANES), lambda j, i: (0, i, 0, 0)),
                  pl.BlockSpec((1, LANES, n_rows), lambda j, i: (i, 0, 0)),
                  pl.BlockSpec((NSA_HEADS, 1, n_rows, tq), lambda j, i: (0, j, 0, 0)),
                  pl.BlockSpec((n_blk, n_rows), lambda j, i: (0, 0))],
        out_specs=[pl.BlockSpec((1, tq, width), lambda j, i: (i, j, 0)),
                   pl.BlockSpec((1, tq, LANES), lambda j, i: (i, j, 0))],
        out_shape=[jax.ShapeDtypeStruct((b, s, width), F32),
                   jax.ShapeDtypeStruct((b, s, LANES), BF16)],
        compiler_params=_params(2),
    )(nsa_q, cmp_kv, vct, bias_c, overlap_t)


def _overlap_t(s_len):
    n_rows = s_len // CMP_STRIDE
    c_start = np.arange(n_rows)[None, :] * CMP_STRIDE
    s_start = np.arange(HEAD_DIM)[:, None] * SLC_BLOCK
    ovl = np.clip(np.minimum(c_start + CMP_LEN, s_start + SLC_BLOCK) - np.maximum(c_start, s_start), 0, None)
    ovl = ovl.astype(np.float32) / CMP_LEN
    ovl[:, n_rows - 1] = 0.0
    ovl[s_len // SLC_BLOCK:, :] = 0.0
    return jnp.asarray(ovl)


def _slc_kernel(q_ref, mb_ref, k_ref, e2_ref, vt_ref, bias_ref, o_ref, qs_scr, s_scr, tmax_scr, m_scr, acc_scr, *,
                n_near):
    t = FLASH_T
    i = pl.program_id(1)
    lo = _lane_lo((t, LANES))
    n_groups = NSA_HEADS // 2
    mb = mb_ref[0]
    for g in range(n_groups):
        qg = q_ref[0, :, g * LANES:(g + 1) * LANES]
        qs_scr[2 * g] = jnp.where(lo, qg, mb)
        qs_scr[2 * g + 1] = jnp.where(lo, mb, qg)
    _flash_init(m_scr, acc_scr)

    def qk_scores(j):
        start = pl.multiple_of(j * t, t)
        k_tile = k_ref[0, pl.ds(start, t), :]
        e_tile = e2_ref[pl.ds(start, t), :]
        k_sel = (jnp.where(lo, k_tile, e_tile), jnp.where(lo, e_tile, k_tile))
        return [_dot_nt(k_sel[pos % 2], qs_scr[pos]) for pos in range(NSA_HEADS)]

    def bias_tile(pos, j):
        return bias_ref[pos, jnp.where(j > i, n_near + 1, jnp.minimum(i - j, n_near))]

    def vt_slab(pos, j):
        kv = pos % 2
        return vt_ref[0, kv * VT_ROWS:(kv + 1) * VT_ROWS, pl.ds(pl.multiple_of(j * t, t), t)]

    _flash_pipeline(i, NSA_HEADS, qk_scores, bias_tile, vt_slab, s_scr, tmax_scr, m_scr, acc_scr)
    _flash_finish(o_ref, n_groups, acc_scr)


def _slc_attention(nsa_q, mask_bias, kv4, e2, bias):
    b, s, width = nsa_q.shape
    t = FLASH_T
    n_near = bias.shape[1] - 2
    vt = _value_slabs(kv4[:, :, LANES:2 * LANES], 2)
    return pl.pallas_call(
        functools.partial(_slc_kernel, n_near=n_near),
        grid=(b, s // t),
        in_specs=[pl.BlockSpec((1, t, width), lambda i, j: (i, j, 0)),
                  pl.BlockSpec((1, t, LANES), lambda i, j: (i, j, 0)),
                  pl.BlockSpec((1, s, LANES), lambda i, j: (i, 0, 0)),
                  pl.BlockSpec((s, LANES), lambda i, j: (0, 0)),
                  pl.BlockSpec((1, 2 * VT_ROWS, s), lambda i, j: (i, 0, 0)),
                  pl.BlockSpec(bias.shape, lambda i, j: (0, 0, 0, 0))],
        out_specs=pl.BlockSpec((1, t, width), lambda i, j: (i, j, 0)),
        out_shape=jax.ShapeDtypeStruct((b, s, width), F32),
        scratch_shapes=[pltpu.VMEM((NSA_HEADS, t, LANES), BF16),
                        pltpu.VMEM((2, NSA_HEADS, t, t), F32),
                        pltpu.VMEM((2, NSA_HEADS, SUBLANES, t), F32),
                        pltpu.VMEM((NSA_HEADS, SUBLANES, t), F32),
                        pltpu.VMEM((NSA_HEADS, VT_ROWS, t), F32)],
        compiler_params=_params(2),
    )(nsa_q, mask_bias, kv4, e2, vt, bias)


def _block_onehot(s_len):
    blk = np.arange(s_len)[:, None] // SLC_BLOCK
    lane = np.arange(LANES)[None, :] % HEAD_DIM
    return jnp.asarray((blk == lane).astype(np.float32), dtype=BF16)


def _out_proj_kernel(x_ref, mod_ref, swa_ref, fox_ref, cmp_ref, slc_ref, win_ref, gates_ref, gn_ref, w_ref,
                     post_ref, o_ref):
    n_swa = SWA_HEADS * HEAD_DIM
    n_fox = FOX_HEADS * HEAD_DIM
    n_nsa = NSA_HEADS * HEAD_DIM
    gates = gates_ref[0].astype(F32)
    o_nsa = (gates[:, 0:n_nsa] * cmp_ref[0] + gates[:, n_nsa:2 * n_nsa] * slc_ref[0]
             + gates[:, 2 * n_nsa:3 * n_nsa] * win_ref[0])
    a = _rms(swa_ref[0], gn_ref[:, 0:n_swa]).astype(BF16)
    b = _rms(fox_ref[0], gn_ref[:, n_swa:n_swa + n_fox]).astype(BF16)
    c = _rms(o_nsa, gn_ref[:, n_swa + n_fox:]).astype(BF16)
    y = (jnp.dot(a, w_ref[0:n_swa, :], preferred_element_type=F32)
         + jnp.dot(b, w_ref[n_swa:n_swa + n_fox, :], preferred_element_type=F32)
         + jnp.dot(c, w_ref[n_swa + n_fox:, :], preferred_element_type=F32))
    o_ref[0] = x_ref[0] + mod_ref[0, 2:3, :] * _rms(y, post_ref[...])


def _out_proj(x, mod, o_swa, o_fox, o_cmp, o_slc, o_win, gates, gn, w, post):
    b, s, d = x.shape

    def rows(width):
        return pl.BlockSpec((1, ROW_TILE, width), lambda i, j: (i, j, 0))

    return pl.pallas_call(
        _out_proj_kernel,
        grid=(b, s // ROW_TILE),
        in_specs=[rows(d),
                  pl.BlockSpec((1, ADA_CHUNKS, d), lambda i, j: (i, 0, 0)),
                  rows(o_swa.shape[2]), rows(o_fox.shape[2]), rows(o_cmp.shape[2]), rows(o_slc.shape[2]),
                  rows(o_win.shape[2]), rows(gates.shape[2]),
                  pl.BlockSpec((1, d), lambda i, j: (0, 0)),
                  pl.BlockSpec((d, d), lambda i, j: (0, 0)),
                  pl.BlockSpec((1, d), lambda i, j: (0, 0))],
        out_specs=rows(d),
        out_shape=jax.ShapeDtypeStruct((b, s, d), F32),
        compiler_params=_params(2),
    )(x, mod, o_swa, o_fox, o_cmp, o_slc, o_win, gates, gn, w, post)


def _ffn_kernel(x_ref, mod_ref, pre_ref, wg_ref, wu_ref, wd_ref, post_ref, o_ref):
    x = x_ref[0]
    h = (_rms(x, pre_ref[...]) * (1.0 + mod_ref[0, 4:5, :]) + mod_ref[0, 3:4, :]).astype(BF16)
    y = jnp.zeros(x.shape, F32)
    for c in range(wg_ref.shape[0]):
        gate = jnp.dot(h, wg_ref[c], preferred_element_type=F32)
        up = jnp.dot(h, wu_ref[c], preferred_element_type=F32)
        act = (gate * jax.nn.sigmoid(gate) * up).astype(BF16)
        y = y + jnp.dot(act, wd_ref[c], preferred_element_type=F32)
    o_ref[0] = x + mod_ref[0, 5:6, :] * _rms(y, post_ref[...])


def _ffn(x, mod, pre, wg, wu, wd, post):
    b, s, d = x.shape
    n_chunks = wg.shape[0]
    rows = pl.BlockSpec((1, ROW_TILE, d), lambda i, j: (i, j, 0))
    vec = pl.BlockSpec((1, d), lambda i, j: (0, 0))
    return pl.pallas_call(
        _ffn_kernel,
        grid=(b, s // ROW_TILE),
        in_specs=[rows,
                  pl.BlockSpec((1, ADA_CHUNKS, d), lambda i, j: (i, 0, 0)),
                  vec,
                  pl.BlockSpec((n_chunks, d, FFN_CHUNK), lambda i, j: (0, 0, 0)),
                  pl.BlockSpec((n_chunks, d, FFN_CHUNK), lambda i, j: (0, 0, 0)),
                  pl.BlockSpec((n_chunks, FFN_CHUNK, d), lambda i, j: (0, 0, 0)),
                  vec],
        out_specs=rows,
        out_shape=jax.ShapeDtypeStruct((b, s, d), F32),
        compiler_params=_params(2),
    )(x, mod, pre, wg, wu, wd, post)


def _forget_lanes():
    lanes, heads = [], []
    for h in range(FOX_HEADS):
        base = (h // 2) * LANES + (HEAD_DIM if h % 2 == 0 else 0)
        for j in range(KEY_BIAS_TERMS):
            lanes.append(base + j)
            heads.append(h)
    return np.array(lanes), np.array(heads)


def _in_proj_layout():
    d = HEAD_DIM
    o_qa, o_ka, o_qb, o_fb, o_qc, o_kc = 0, 256, 512, 1280, 1284, 1796
    o_gc = 2564
    scale = LOG2E / math.sqrt(d)

    def head_cols(base, heads):
        return np.concatenate([np.arange(base + h * d, base + (h + 1) * d) for h in heads])

    cols = [head_cols(o_qa, SWA_POS), np.arange(o_ka, o_qb),
            np.arange(o_qb, o_fb),
            head_cols(o_qc, NSA_POS),
            np.arange(o_kc, o_gc)]
    scales = [np.full(256, scale), np.ones(256), np.full(256, scale), np.ones(512), np.full(512, scale),
              np.ones(768)]
    for branch in range(3):
        cols.append(np.repeat(np.array([o_gc + h * 3 + branch for h in NSA_POS]), d))
        scales.append(np.ones(NSA_HEADS * d))
    lanes, heads = _forget_lanes()
    misc_cols = np.zeros(SEG_MISC[1] - SEG_MISC[0], np.int64)
    misc_scale = np.zeros(SEG_MISC[1] - SEG_MISC[0])
    misc_cols[lanes] = o_fb + heads
    misc_scale[lanes] = 1.0
    cols.append(misc_cols)
    scales.append(misc_scale)
    return np.concatenate(cols), np.concatenate(scales).astype(np.float32)


def _head_perm(pos):
    return np.concatenate([np.arange(h * HEAD_DIM, (h + 1) * HEAD_DIM) for h in pos])


def kernel(x, c, rel_bias, ada_w, ada_b, attn_pre_norm, attn_post_norm, ffn_pre_norm, ffn_post_norm, w_in,
           forget_bias, swa_sinks, cmp_pos, cmp_w1, cmp_w2, group_norm, w_out, ffn_w_gate, ffn_w_up, ffn_w_down):
    b, s, d = x.shape
    depth = w_in.shape[0]
    hidden = ffn_w_gate.shape[2]
    assert s % (2 * FLASH_T) == 0 and s // SLC_BLOCK <= HEAD_DIM and hidden % FFN_CHUNK == 0

    cols, scales = _in_proj_layout()
    w_all = (w_in[:, :, cols] * scales).astype(BF16)
    lanes, heads = _forget_lanes()
    fbias_all = jnp.zeros((depth, 1, SEG_MISC[1] - SEG_MISC[0]), F32).at[:, 0, lanes].set(
        forget_bias[:, heads].astype(F32))
    swa_perm = _head_perm(SWA_POS)
    nsa_perm = _head_perm(NSA_POS)
    n_swa, n_fox = SWA_HEADS * HEAD_DIM, FOX_HEADS * HEAD_DIM
    mix_perm = np.concatenate([swa_perm, n_swa + np.arange(n_fox), n_swa + n_fox + nsa_perm])
    gn_all = group_norm[:, mix_perm].astype(F32)
    w_out_all = w_out[:, mix_perm, :].astype(BF16)
    n_chunks = hidden // FFN_CHUNK
    wg_all = ffn_w_gate.reshape(depth, d, n_chunks, FFN_CHUNK).transpose(0, 2, 1, 3).astype(BF16)
    wu_all = ffn_w_up.reshape(depth, d, n_chunks, FFN_CHUNK).transpose(0, 2, 1, 3).astype(BF16)
    wd_all = ffn_w_down.reshape(depth, n_chunks, FFN_CHUNK, d).astype(BF16)

    tab_swa = rel_bias[:, np.array(SWA_POS)].astype(F32)
    tab_nsa = rel_bias[:, SWA_HEADS + np.array(NSA_POS)].astype(F32)
    bias_swa = _bias_table(tab_swa, _band_buckets_t(SWA_TILE, SWA_WINDOW))
    bias_win = _bias_table(tab_nsa, _band_buckets_t(WIN_TILE, NSA_WINDOW))
    bias_slc = _bias_table(tab_nsa, _toeplitz_buckets_t(FLASH_T, _near_tiles(FLASH_T)), subtract_last=True)
    n_rows = s // CMP_STRIDE
    bias_cmp = _bias_table(tab_nsa, _cmp_buckets_t(s, n_rows))
    overlap_t = _overlap_t(s)
    e2 = _block_onehot(s)

    mod_all = _adaln(c.astype(F32), ada_w.astype(F32), ada_b.astype(F32)).reshape(depth, b, ADA_CHUNKS, d)

    for layer in range(depth):
        mod = mod_all[layer]
        swa_qkv, fox_qkv, nsa_q, kc, vc, kv4, gates, misc = _in_proj(
            x, mod, attn_pre_norm[layer].reshape(1, d).astype(F32), w_all[layer])
        o_swa = _banded_attention(swa_qkv, swa_qkv, 2, swa_qkv[:, :, 3 * LANES:], bias_swa,
                                  sinks=swa_sinks[layer][np.array(SWA_POS)].astype(F32))
        o_fox = _fox_attention(fox_qkv, _fox_key_terms(misc, fbias_all[layer]))
        cmp_kv = _compress(kc, vc, cmp_pos[layer], cmp_w1[layer], cmp_w2[layer])
        o_cmp, mask_bias = _select(nsa_q, cmp_kv, bias_cmp, overlap_t)
        o_slc = _slc_attention(nsa_q, mask_bias, kv4, e2, bias_slc)
        o_win = _banded_attention(nsa_q, kv4, 2, kv4[:, :, 3 * LANES:], bias_win)
        x = _out_proj(x, mod, o_swa, o_fox, o_cmp, o_slc, o_win, gates, gn_all[layer].reshape(1, d),
                      w_out_all[layer], attn_post_norm[layer].reshape(1, d).astype(F32))
        x = _ffn(x, mod, ffn_pre_norm[layer].reshape(1, d).astype(F32), wg_all[layer], wu_all[layer],
                 wd_all[layer], ffn_post_norm[layer].reshape(1, d).astype(F32))
    return x
```

```python
import functools
import math

import numpy as np
import jax
import jax.numpy as jnp
from jax import lax
from jax.experimental import pallas as pl
from jax.experimental.pallas import tpu as pltpu

F32 = jnp.float32
BF16 = jnp.bfloat16
HIGHEST = lax.Precision.HIGHEST

LANES = 128
SUBLANES = 8
VMEM_LIMIT = 56 * 1024 * 1024

HEAD_DIM = 64
SWA_HEADS = 4
SWA_WINDOW = 128
FOX_HEADS = 4
NSA_HEADS = 8
CMP_LEN = 32
CMP_STRIDE = 16
CMP_HIDDEN = 2 * HEAD_DIM
SLC_BLOCK = 64
TOPK = 16
NSA_WINDOW = 512
REL_BUCKETS = 32
REL_MAX_DISTANCE = 1024
ZERO_BUCKET = -2
RMS_EPS = 1e-6
NEG = -1e30
FORCE = 1e30
ADA_CHUNKS = 6
LOG2E = math.log2(math.e)

SWA_POS = (0, 2, 1, 3)
NSA_POS = (0, 4, 1, 5, 2, 6, 3, 7)

SWA_TILE = 128
WIN_TILE = 256
FLASH_T = 256
SEL_TQ = 256
ROW_TILE = 512
FFN_CHUNK = 256
VT_ROWS = HEAD_DIM + 16
KEY_BIAS_TERMS = 3

SEG_SWA = (0, 384)
SEG_FOX = (384, 896)
SEG_NSAQ = (896, 1408)
SEG_KC = (1408, 1536)
SEG_VC = (1536, 1664)
SEG_K2 = (1664, 1920)
SEG_V = (1920, 2560)
SEG_MISC = (2560, 2816)
GATE_LANE = 8
VT_FOX_BLOCK, VT_SWA_BLOCK, VT_SLC_BLOCK, VT_WIN_BLOCK = 0, 2, 3, 4


def _params(n_grid, vmem=VMEM_LIMIT):
    return pltpu.CompilerParams(dimension_semantics=("parallel",) * n_grid, vmem_limit_bytes=vmem)


def _dot_nt(a, b):
    return lax.dot_general(a, b, (((1,), (1,)), ((), ())), preferred_element_type=F32)


def _lane_lo(shape):
    return lax.broadcasted_iota(jnp.int32, shape, len(shape) - 1) < HEAD_DIM


def _adaln_kernel(c_ref, w_ref, b_ref, o_ref):
    c = c_ref[...]
    act = c * jax.nn.sigmoid(c)
    o_ref[0] = jnp.dot(act, w_ref[0], precision=HIGHEST, preferred_element_type=F32) + b_ref[0]


def _adaln(c, ada_w, ada_b):
    depth, d, n = ada_w.shape
    b = c.shape[0]
    return pl.pallas_call(
        _adaln_kernel,
        grid=(depth, n // d),
        in_specs=[pl.BlockSpec((b, d), lambda l, j: (0, 0)),
                  pl.BlockSpec((1, d, d), lambda l, j: (l, 0, j)),
                  pl.BlockSpec((1, 1, d), lambda l, j: (l, 0, j))],
        out_specs=pl.BlockSpec((1, b, d), lambda l, j: (l, 0, j)),
        out_shape=jax.ShapeDtypeStruct((depth, b, n), F32),
        compiler_params=_params(2),
    )(c, ada_w, ada_b.reshape(depth, 1, n))


def _t5_bucket(dist):
    n = jnp.maximum(dist, 0)
    max_exact = REL_BUCKETS // 2
    nf = jnp.maximum(n, 1).astype(jnp.float32)
    large = max_exact + (jnp.log(nf / max_exact) / math.log(REL_MAX_DISTANCE / max_exact)
                         * (REL_BUCKETS - max_exact)).astype(jnp.int32)
    large = jnp.minimum(large, REL_BUCKETS - 1)
    return jnp.where(n < max_exact, n, large)


def _bias_table_kernel(tab_ref, bucket_ref, o_ref, *, subtract_last):
    h = pl.program_id(0)
    bucket = bucket_ref[0]
    off = tab_ref[REL_BUCKETS - 1, h] if subtract_last else 0.0
    acc = jnp.full(bucket.shape, NEG, F32)
    for k in range(REL_BUCKETS):
        acc = jnp.where(bucket == k, (tab_ref[k, h] - off) * LOG2E, acc)
    o_ref[0, 0] = jnp.where(bucket == ZERO_BUCKET, 0.0, acc)


def _bias_table(table, bucket, subtract_last=False):
    n_heads = table.shape[1]
    n, r, c = bucket.shape
    return pl.pallas_call(
        functools.partial(_bias_table_kernel, subtract_last=subtract_last),
        grid=(n_heads, n),
        in_specs=[pl.BlockSpec(memory_space=pltpu.SMEM),
                  pl.BlockSpec((1, r, c), lambda h, i: (i, 0, 0))],
        out_specs=pl.BlockSpec((1, 1, r, c), lambda h, i: (h, i, 0, 0)),
        out_shape=jax.ShapeDtypeStruct((n_heads, n, r, c), F32),
        compiler_params=_params(2),
    )(table, bucket)


def _band_buckets_t(tile, window):
    n_back = -(-(window - 1) // tile)
    t = jnp.arange(n_back + 1)[:, None, None]
    key = jnp.arange(tile)[None, :, None]
    query = jnp.arange(tile)[None, None, :]
    dist = query + (n_back - t) * tile - key
    return jnp.where((dist >= 0) & (dist < window), _t5_bucket(dist), -1).astype(jnp.int32)


def _toeplitz_buckets_t(tile, n_tiles):
    m = jnp.arange(n_tiles)[:, None, None]
    key = jnp.arange(tile)[None, :, None]
    query = jnp.arange(tile)[None, None, :]
    dist = m * tile + query - key
    near = jnp.where(dist >= 0, _t5_bucket(dist), -1).astype(jnp.int32)
    return jnp.concatenate([near, jnp.full((1, tile, tile), ZERO_BUCKET, jnp.int32),
                            jnp.full((1, tile, tile), -1, jnp.int32)])


def _cmp_buckets_t(s_len, n_rows):
    n_c = n_rows - 1
    tile = jnp.arange(s_len // SEL_TQ)[:, None, None]
    n = jnp.arange(n_rows)[None, :, None]
    t = tile * SEL_TQ + jnp.arange(SEL_TQ)[None, None, :]
    dist = t - (n * CMP_STRIDE + CMP_LEN - 1)
    return jnp.where((dist >= 0) & (n < n_c), _t5_bucket(dist), -1).astype(jnp.int32)


def _near_tiles(tile):
    max_exact = REL_BUCKETS // 2
    first_const = math.ceil(max_exact * (REL_MAX_DISTANCE / max_exact) ** ((max_exact - 1) / max_exact)) + 1
    m = 1
    while m * tile - (tile - 1) < first_const:
        m += 1
    return m


def _rms(x, gain):
    return x * lax.rsqrt(jnp.mean(x * x, axis=-1, keepdims=True) + RMS_EPS) * gain


def _in_proj_kernel(x_ref, mod_ref, gain_ref, w_ref, swa_ref, fox_ref, nsaq_ref, kc_ref, vc_ref, k2_ref,
                    misc_ref, vt_ref):
    x = x_ref[0]
    h = _rms(x, gain_ref[...]) * (1.0 + mod_ref[0, 1:2, :]) + mod_ref[0, 0:1, :]
    hb = h.astype(BF16)

    def seg(bounds):
        return jnp.dot(hb, w_ref[:, bounds[0]:bounds[1]], preferred_element_type=F32)

    swa_ref[0] = seg(SEG_SWA).astype(BF16)
    fox_ref[0] = seg(SEG_FOX).astype(BF16)
    nsaq_ref[0] = seg(SEG_NSAQ).astype(BF16)
    kc_ref[0] = seg(SEG_KC).astype(BF16)
    vc_ref[0] = seg(SEG_VC).astype(BF16)
    k2_ref[0] = seg(SEG_K2).astype(BF16)
    misc_ref[0] = seg(SEG_MISC)

    rows = x.shape[0]
    extra_row = lax.broadcasted_iota(jnp.int32, (VT_ROWS - HEAD_DIM, rows), 0)
    extra = jnp.where(extra_row == 0, 1.0, 0.0).astype(BF16)
    values = seg(SEG_V)
    for c in range(values.shape[1] // LANES):
        vt = values[:, c * LANES:(c + 1) * LANES].T.astype(BF16)
        for half in range(2):
            base = (2 * c + half) * VT_ROWS
            vt_ref[0, base:base + HEAD_DIM, :] = vt[half * HEAD_DIM:(half + 1) * HEAD_DIM, :]
            vt_ref[0, base + HEAD_DIM:base + VT_ROWS, :] = extra


def _in_proj(x, mod, gain, w):
    b, s, d = x.shape
    n = w.shape[1]
    widths = [hi - lo for lo, hi in (SEG_SWA, SEG_FOX, SEG_NSAQ, SEG_KC, SEG_VC, SEG_K2, SEG_MISC)]
    dtypes = [BF16] * 6 + [F32]
    vt_rows = (SEG_V[1] - SEG_V[0]) // HEAD_DIM * VT_ROWS
    return pl.pallas_call(
        _in_proj_kernel,
        grid=(b, s // ROW_TILE),
        in_specs=[pl.BlockSpec((1, ROW_TILE, d), lambda i, j: (i, j, 0)),
                  pl.BlockSpec((1, ADA_CHUNKS, d), lambda i, j: (i, 0, 0)),
                  pl.BlockSpec((1, d), lambda i, j: (0, 0)),
                  pl.BlockSpec((d, n), lambda i, j: (0, 0))],
        out_specs=[pl.BlockSpec((1, ROW_TILE, wd), lambda i, j: (i, j, 0)) for wd in widths]
        + [pl.BlockSpec((1, vt_rows, ROW_TILE), lambda i, j: (i, 0, j))],
        out_shape=[jax.ShapeDtypeStruct((b, s, wd), dt) for wd, dt in zip(widths, dtypes)]
        + [jax.ShapeDtypeStruct((b, vt_rows, s), BF16)],
        compiler_params=_params(2),
    )(x, mod, gain, w)


def _banded_kernel(*refs, n_back, n_groups, has_sink, t):
    if has_sink:
        sink_ref, q_ref, k_ref, vt_ref, bias_ref, o_ref = refs
    else:
        q_ref, k_ref, vt_ref, bias_ref, o_ref = refs
    i = pl.program_id(1)
    lo = _lane_lo((t, LANES))
    n_tiles = n_back + 1

    def run(all_valid):
        starts = [pl.multiple_of(jnp.maximum(i - n_back + tt, 0) * t, t) for tt in range(n_tiles)]
        k_tiles = [k_ref[0, pl.ds(start, t), :] for start in starts]

        def scores(g):
            qg = q_ref[0, :, g * LANES:(g + 1) * LANES]
            zero = jnp.zeros_like(qg)
            qms = (jnp.where(lo, qg, zero), jnp.where(lo, zero, qg))
            return [[bias_ref[2 * g + half, tt] + _dot_nt(k_tiles[tt], qms[half]) for tt in range(n_tiles)]
                    for half in range(2)]

        def softmax_pv(g, sts):
            pair = []
            for half in range(2):
                tiles = sts[half]
                if not all_valid:
                    tiles = [jnp.where(i - n_back + tt >= 0, st, NEG) if tt < n_back else st
                             for tt, st in enumerate(tiles)]
                m = None
                for st in tiles:
                    part = st.reshape(t // SUBLANES, SUBLANES, t).max(axis=0)
                    m = part if m is None else jnp.maximum(m, part)
                m = _all_sublanes(m, jnp.maximum)
                if has_sink:
                    sink = sink_ref[2 * g + half] * LOG2E
                    m = jnp.maximum(m, sink)
                acc = None
                for tt, st in enumerate(tiles):
                    p = jnp.exp2(st.reshape(t // SUBLANES, SUBLANES, t) - m[None]).reshape(t, t).astype(BF16)
                    part = jnp.dot(vt_ref[0, half * VT_ROWS:(half + 1) * VT_ROWS, pl.ds(starts[tt], t)], p,
                                   preferred_element_type=F32)
                    acc = part if acc is None else acc + part
                denom = _all_sublanes(acc[HEAD_DIM:HEAD_DIM + SUBLANES, :], jnp.add)
                if has_sink:
                    denom = denom + jnp.exp2(sink - m)
                out = acc[0:HEAD_DIM, :].reshape(HEAD_DIM // SUBLANES, SUBLANES, t) / denom[None]
                pair.append(out.reshape(HEAD_DIM, t))
            o_ref[0, :, g * LANES:(g + 1) * LANES] = jnp.concatenate(pair, axis=0).T.astype(o_ref.dtype)

        pending = scores(0)
        for g in range(n_groups):
            current = pending
            if g + 1 < n_groups:
                pending = scores(g + 1)
            softmax_pv(g, current)

    @pl.when(i >= n_back)
    def _():
        run(True)

    @pl.when(i < n_back)
    def _():
        run(False)


def _banded_attention(q_arr, k_arr, k_blk, vt, vt_blk, bias, sinks=None):
    b, s, _ = q_arr.shape
    n_pos, n_tiles, t = bias.shape[0], bias.shape[1], bias.shape[2]
    width = n_pos * HEAD_DIM
    in_specs = [pl.BlockSpec((1, t, width), lambda i, j: (i, j, 0)),
                pl.BlockSpec((1, s, LANES), lambda i, j: (i, 0, k_blk)),
                pl.BlockSpec((1, 2 * VT_ROWS, s), lambda i, j: (i, vt_blk, 0)),
                pl.BlockSpec(bias.shape, lambda i, j: (0, 0, 0, 0))]
    args = [q_arr, k_arr, vt, bias]
    if sinks is not None:
        in_specs = [pl.BlockSpec(memory_space=pltpu.SMEM)] + in_specs
        args = [sinks] + args
    return pl.pallas_call(
        functools.partial(_banded_kernel, n_back=n_tiles - 1, n_groups=n_pos // 2, has_sink=sinks is not None, t=t),
        grid=(b, s // t),
        in_specs=in_specs,
        out_specs=pl.BlockSpec((1, t, width), lambda i, j: (i, j, 0)),
        out_shape=jax.ShapeDtypeStruct((b, s, width), BF16),
        compiler_params=_params(2),
    )(*args)


def _all_sublanes(x, op):
    for shift in (4, 2, 1):
        x = op(x, pltpu.roll(x, shift, 0))
    return x


def _flash_init(m_scr, acc_scr):
    m_scr[...] = jnp.full(m_scr.shape, NEG, F32)
    acc_scr[...] = jnp.zeros(acc_scr.shape, F32)


def _flash_update(h, st_ref, tile_max, vt_h, m_scr, acc_scr):
    tk, tq = st_ref.shape
    m_prev = m_scr[h]
    m_new = _all_sublanes(jnp.maximum(m_prev, tile_max), jnp.maximum)
    alpha = jnp.exp2(m_prev - m_new)
    p = jnp.exp2(st_ref[...].reshape(tk // SUBLANES, SUBLANES, tq) - m_new[None]).reshape(tk, tq).astype(BF16)
    acc = acc_scr[h].reshape(VT_ROWS // SUBLANES, SUBLANES, tq) * alpha[None]
    acc_scr[h] = acc.reshape(VT_ROWS, tq) + jnp.dot(vt_h, p, preferred_element_type=F32)
    m_scr[h] = m_new


def _flash_finish(o_ref, n_groups, acc_scr):
    tq = acc_scr.shape[2]
    for g in range(n_groups):
        pair = []
        for h in (2 * g, 2 * g + 1):
            denom = _all_sublanes(acc_scr[h, HEAD_DIM:HEAD_DIM + SUBLANES, :], jnp.add)
            out = acc_scr[h, 0:HEAD_DIM, :].reshape(HEAD_DIM // SUBLANES, SUBLANES, tq) / denom[None]
            pair.append(out.reshape(HEAD_DIM, tq))
        o_ref[0, :, g * LANES:(g + 1) * LANES] = jnp.concatenate(pair, axis=0).T.astype(o_ref.dtype)


def _flash_pipeline(i, n_heads, qk_scores, bias_tile, vt_slab, s_scr, tmax_scr, m_scr, acc_scr):
    def qk(j, slot):
        for h, sc in enumerate(qk_scores(jnp.minimum(j, i))):
            st = sc + bias_tile(h, j)
            s_scr[slot, h] = st
            tmax_scr[slot, h] = st.reshape(st.shape[0] // SUBLANES, SUBLANES, st.shape[1]).max(axis=0)

    def softmax(j, slot):
        for h in range(n_heads):
            _flash_update(h, s_scr.at[slot, h], tmax_scr[slot, h], vt_slab(h, jnp.minimum(j, i)), m_scr, acc_scr)

    qk(0, 0)

    def body(trip, carry):
        j = 2 * trip
        qk(j + 1, 1)
        softmax(j, 0)
        qk(j + 2, 0)
        softmax(j + 1, 1)
        return carry

    lax.fori_loop(0, (i + 2) // 2, body, 0)


def _fox_aug_kernel(misc_ref, fbias_ref, tri_ref, o_ref):
    s_len, width = misc_ref.shape[1], misc_ref.shape[2]
    term = lax.broadcasted_iota(jnp.int32, (LANES, width), 1) % HEAD_DIM
    carry = jnp.zeros((1, width), F32)
    for c in range(s_len // LANES):
        z = misc_ref[0, c * LANES:(c + 1) * LANES, :] + fbias_ref[...]
        log_f = jnp.minimum(z, 0.0) - jnp.log1p(jnp.exp(-jnp.abs(z)))
        cum = jnp.dot(tri_ref[...], log_f, precision=HIGHEST, preferred_element_type=F32) + carry
        carry = cum[LANES - 1:LANES, :]
        x = cum * (-LOG2E)
        hi = x.astype(BF16).astype(F32)
        rest = x - hi
        mid = rest.astype(BF16).astype(F32)
        low = rest - mid
        out = jnp.where(term == 0, hi, jnp.where(term == 1, mid, jnp.where(term == 2, low, 0.0)))
        o_ref[0, c * LANES:(c + 1) * LANES, :] = out.astype(BF16)


def _fox_key_terms(misc, fbias):
    b, s, width = misc.shape
    tri = jnp.asarray(np.tril(np.ones((LANES, LANES), np.float32)))
    return pl.pallas_call(
        _fox_aug_kernel,
        grid=(b,),
        in_specs=[pl.BlockSpec((1, s, width), lambda i: (i, 0, 0)),
                  pl.BlockSpec((1, width), lambda i: (0, 0)),
                  pl.BlockSpec((LANES, LANES), lambda i: (0, 0))],
        out_specs=pl.BlockSpec((1, s, width), lambda i: (i, 0, 0)),
        out_shape=jax.ShapeDtypeStruct((b, s, width), BF16),
        compiler_params=_params(1),
    )(misc, fbias, tri)


def _fox_kernel(q_ref, k_ref, aug_ref, vt_ref, mask_ref, o_ref, qs_scr, s_scr, tmax_scr, m_scr, acc_scr):
    t = FLASH_T
    i = pl.program_id(1)
    lo = _lane_lo((t, LANES))
    lane = lax.broadcasted_iota(jnp.int32, (t, LANES), 1)
    ones = jnp.where(lane % HEAD_DIM < KEY_BIAS_TERMS, 1.0, 0.0).astype(BF16)
    n_groups = FOX_HEADS // 2
    for g in range(n_groups):
        qg = q_ref[0, :, g * LANES:(g + 1) * LANES]
        qs_scr[2 * g] = jnp.where(lo, qg, ones)
        qs_scr[2 * g + 1] = jnp.where(lo, ones, qg)
    _flash_init(m_scr, acc_scr)

    def qk_scores(j):
        start = pl.multiple_of(j * t, t)
        scores = []
        for g in range(n_groups):
            k_tile = k_ref[0, pl.ds(start, t), g * LANES:(g + 1) * LANES]
            a_tile = aug_ref[0, pl.ds(start, t), g * LANES:(g + 1) * LANES]
            k_sel = (jnp.where(lo, k_tile, a_tile), jnp.where(lo, a_tile, k_tile))
            for half in range(2):
                scores.append(_dot_nt(k_sel[half], qs_scr[2 * g + half]))
        return scores

    def bias_tile(h, j):
        return mask_ref[jnp.where(j > i, 2, jnp.minimum(i - j, 1))]

    def vt_slab(h, j):
        return vt_ref[0, h * VT_ROWS:(h + 1) * VT_ROWS, pl.ds(pl.multiple_of(j * t, t), t)]

    _flash_pipeline(i, FOX_HEADS, qk_scores, bias_tile, vt_slab, s_scr, tmax_scr, m_scr, acc_scr)
    _flash_finish(o_ref, n_groups, acc_scr)


def _fox_attention(fox_qk, key_terms, vt):
    b, s, _ = fox_qk.shape
    width = FOX_HEADS * HEAD_DIM
    t = FLASH_T
    idx = np.arange(t)
    diag = np.where(idx[:, None] <= idx[None, :], 0.0, NEG)
    masks = jnp.asarray(np.stack([diag, np.zeros((t, t)), np.full((t, t), NEG)]).astype(np.float32))
    return pl.pallas_call(
        _fox_kernel,
        grid=(b, s // t),
        in_specs=[pl.BlockSpec((1, t, width), lambda i, j: (i, j, 0)),
                  pl.BlockSpec((1, s, width), lambda i, j: (i, 0, 1)),
                  pl.BlockSpec((1, s, width), lambda i, j: (i, 0, 0)),
                  pl.BlockSpec((1, FOX_HEADS * VT_ROWS, s), lambda i, j: (i, VT_FOX_BLOCK, 0)),
                  pl.BlockSpec(masks.shape, lambda i, j: (0, 0, 0))],
        out_specs=pl.BlockSpec((1, t, width), lambda i, j: (i, j, 0)),
        out_shape=jax.ShapeDtypeStruct((b, s, width), BF16),
        scratch_shapes=[pltpu.VMEM((FOX_HEADS, t, LANES), BF16),
                        pltpu.VMEM((2, FOX_HEADS, t, t), F32),
                        pltpu.VMEM((2, FOX_HEADS, SUBLANES, t), F32),
                        pltpu.VMEM((FOX_HEADS, SUBLANES, t), F32),
                        pltpu.VMEM((FOX_HEADS, VT_ROWS, t), F32)],
        compiler_params=_params(2),
    )(fox_qk, fox_qk, key_terms, vt, masks)


def _compress_kernel(x_ref, pe_ref, w1a_ref, w1b_ref, w2_ref, o_ref):
    x = x_ref[0].astype(F32)
    n_rows = x.shape[0]

    def mm(a, w):
        return jnp.dot(a, w, precision=HIGHEST, preferred_element_type=F32)

    first = mm(x, w1a_ref[0])
    second = mm(x, w1b_ref[0])
    pe_term = (mm(pe_ref[0, 0], w1a_ref[0]) + mm(pe_ref[0, 1], w1b_ref[0]))[0:1, :]
    pre = first + pltpu.roll(second, n_rows - 1, 0) + pe_term
    hid = 0.5 * pre * (1.0 + jnp.tanh(math.sqrt(2.0 / math.pi) * (pre + 0.044715 * (pre * pre * pre))))
    o_ref[0, 0] = mm(hid, w2_ref[0])


def _compress(kc, vc, cmp_pos, cmp_w1, cmp_w2):
    b, s, _ = kc.shape
    n_rows = s // CMP_STRIDE
    half = CMP_LEN // 2
    feat = CMP_STRIDE * LANES
    x = jnp.stack([kc.reshape(b, n_rows, feat), vc.reshape(b, n_rows, feat)])
    eye = jnp.eye(2, dtype=F32)
    w1 = cmp_w1.astype(F32).reshape(2, CMP_LEN, HEAD_DIM, CMP_HIDDEN)
    w1a = jnp.einsum('wldj,hg->wlhdgj', w1[:, :half], eye).reshape(2, feat, 2 * CMP_HIDDEN)
    w1b = jnp.einsum('wldj,hg->wlhdgj', w1[:, half:], eye).reshape(2, feat, 2 * CMP_HIDDEN)
    w2 = jnp.einsum('wjd,hg->whjgd', cmp_w2.astype(F32), eye).reshape(2, 2 * CMP_HIDDEN, LANES)
    w2 = jnp.concatenate([w2, jnp.roll(w2, HEAD_DIM, axis=2)], axis=2)
    pe = jnp.broadcast_to(cmp_pos.astype(F32).reshape(2, 2, half, 1, HEAD_DIM), (2, 2, half, 2, HEAD_DIM))
    pe = jnp.broadcast_to(pe.reshape(2, 2, 1, feat), (2, 2, 8, feat))
    return pl.pallas_call(
        _compress_kernel,
        grid=(2, b),
        in_specs=[pl.BlockSpec((None, 1, n_rows, feat), lambda w, i: (w, i, 0, 0)),
                  pl.BlockSpec((1, 2, 8, feat), lambda w, i: (w, 0, 0, 0)),
                  pl.BlockSpec((1, feat, 2 * CMP_HIDDEN), lambda w, i: (w, 0, 0)),
                  pl.BlockSpec((1, feat, 2 * CMP_HIDDEN), lambda w, i: (w, 0, 0)),
                  pl.BlockSpec((1, 2 * CMP_HIDDEN, 2 * LANES), lambda w, i: (w, 0, 0))],
        out_specs=pl.BlockSpec((1, 1, n_rows, 2 * LANES), lambda w, i: (w, i, 0, 0)),
        out_shape=jax.ShapeDtypeStruct((2, b, n_rows, 2 * LANES), F32),
        compiler_params=_params(2),
    )(x, pe, w1a, w1b, w2)


def _select_kernel(q_ref, kc_ref, vct_ref, bias_ref, ovl_ref, o_ref, mb_ref):
    tq = SEL_TQ
    i = pl.program_id(0)
    lo = _lane_lo((tq, LANES))
    n_rows = kc_ref.shape[2]
    lo_k = _lane_lo((n_rows, LANES))
    n_grp = n_rows // SUBLANES

    k_own = kc_ref[0, 0, :, 0:LANES]
    k_swapped = kc_ref[0, 0, :, LANES:2 * LANES]
    hi = k_own.astype(BF16)
    low = (k_swapped - k_swapped.astype(BF16).astype(F32)).astype(BF16)
    k_sel = (jnp.where(lo_k, hi, low), jnp.where(lo_k, low, hi))

    def scores(g):
        qg = q_ref[0, :, g * LANES:(g + 1) * LANES]
        swapped = pltpu.roll(qg.astype(F32), HEAD_DIM, 1).astype(BF16)
        q_dup = (jnp.where(lo, qg, swapped), jnp.where(lo, swapped, qg))
        return [bias_ref[2 * g + half, 0] + _dot_nt(k_sel[half], q_dup[half]) for half in range(2)]

    query = i * tq + lax.broadcasted_iota(jnp.int32, (SUBLANES, tq), 1)
    has_keys = query >= CMP_LEN - 1
    p_sum = [None, None]

    def softmax_pv(g, sts):
        pair = []
        for half in range(2):
            s3 = sts[half].reshape(n_grp, SUBLANES, tq)
            m = _all_sublanes(s3.max(axis=0), jnp.maximum)
            e = jnp.exp2(s3 - m[None])
            inv = jnp.where(has_keys, 1.0 / _all_sublanes(e.sum(axis=0), jnp.add), 0.0)
            p = e * inv[None]
            p_sum[half] = p if p_sum[half] is None else p_sum[half] + p
            pair.append(jnp.dot(vct_ref[0, half * HEAD_DIM:(half + 1) * HEAD_DIM, :],
                                p.reshape(n_rows, tq).astype(BF16), preferred_element_type=F32))
        o_ref[0, :, g * LANES:(g + 1) * LANES] = jnp.concatenate(pair, axis=0).T.astype(o_ref.dtype)

    n_groups = NSA_HEADS // 2
    pending = scores(0)
    for g in range(n_groups):
        current = pending
        if g + 1 < n_groups:
            pending = scores(g + 1)
        softmax_pv(g, current)

    n_blk = ovl_ref.shape[0]
    blk_grp = n_blk // SUBLANES
    sub = lax.broadcasted_iota(jnp.int32, (SUBLANES, tq), 0)
    q_blk = query // SLC_BLOCK
    masks = []
    for half in (1, 0):
        imp = jnp.dot(ovl_ref[...], p_sum[half].reshape(n_rows, tq), precision=HIGHEST,
                      preferred_element_type=F32)
        rows = []
        for r in range(blk_grp):
            blk = sub + r * SUBLANES
            forced = jnp.where(blk == 0, 1, 0) + jnp.where(blk == q_blk, 1, 0) + jnp.where(blk == q_blk - 1, 1, 0)
            rows.append(jnp.where(forced > 0, FORCE,
                                  jnp.where(blk > q_blk, NEG, imp[r * SUBLANES:(r + 1) * SUBLANES, :])))
        counts = [jnp.zeros((SUBLANES, tq), jnp.int32) for _ in range(blk_grp)]
        for other in range(n_blk):
            r_other, s_other = divmod(other, SUBLANES)
            row = jnp.broadcast_to(rows[r_other][s_other:s_other + 1, :], (SUBLANES, tq))
            for r in range(blk_grp):
                if r > r_other:
                    beats = jnp.where(row >= rows[r], 1, 0)
                elif r < r_other:
                    beats = jnp.where(row > rows[r], 1, 0)
                else:
                    beats = jnp.where(sub > s_other, jnp.where(row >= rows[r], 1, 0), jnp.where(row > rows[r], 1, 0))
                counts[r] = counts[r] + beats
        masks.extend(jnp.where(cnt < TOPK, 0.0, NEG) for cnt in counts)
    for c in range(tq // LANES):
        mb_ref[0, c * LANES:(c + 1) * LANES, :] = jnp.concatenate(
            [mk[:, c * LANES:(c + 1) * LANES] for mk in masks], axis=0).T.astype(BF16)


def _select(nsa_q, cmp_kv, bias_c, overlap_t):
    b, s, width = nsa_q.shape
    n_rows = cmp_kv.shape[2]
    n_blk = overlap_t.shape[0]
    assert n_blk == HEAD_DIM
    tq = SEL_TQ
    vct = cmp_kv[1, :, :, 0:LANES].transpose(0, 2, 1).astype(BF16)
    return pl.pallas_call(
        _select_kernel,
        grid=(s // tq, b),
        in_specs=[pl.BlockSpec((1, tq, width), lambda j, i: (i, j, 0)),
                  pl.BlockSpec((1, 1, n_rows, 2 * LANES), lambda j, i: (0, i, 0, 0)),
                  pl.BlockSpec((1, LANES, n_rows), lambda j, i: (i, 0, 0)),
                  pl.BlockSpec((NSA_HEADS, 1, n_rows, tq), lambda j, i: (0, j, 0, 0)),
                  pl.BlockSpec((n_blk, n_rows), lambda j, i: (0, 0))],
        out_specs=[pl.BlockSpec((1, tq, width), lambda j, i: (i, j, 0)),
                   pl.BlockSpec((1, tq, LANES), lambda j, i: (i, j, 0))],
        out_shape=[jax.ShapeDtypeStruct((b, s, width), BF16),
                   jax.ShapeDtypeStruct((b, s, LANES), BF16)],
        compiler_params=_params(2),
    )(nsa_q, cmp_kv, vct, bias_c, overlap_t)


def _overlap_t(s_len):
    n_rows = s_len // CMP_STRIDE
    c_start = np.arange(n_rows)[None, :] * CMP_STRIDE
    s_start = np.arange(HEAD_DIM)[:, None] * SLC_BLOCK
    ovl = np.clip(np.minimum(c_start + CMP_LEN, s_start + SLC_BLOCK) - np.maximum(c_start, s_start), 0, None)
    ovl = ovl.astype(np.float32) / CMP_LEN
    ovl[:, n_rows - 1] = 0.0
    ovl[s_len // SLC_BLOCK:, :] = 0.0
    return jnp.asarray(ovl)


def _slc_kernel(q_ref, mb_ref, k_ref, e2_ref, vt_ref, bias_ref, o_ref, qs_scr, s_scr, tmax_scr, m_scr, acc_scr, *,
                n_near):
    t = FLASH_T
    i = pl.program_id(1)
    lo = _lane_lo((t, LANES))
    n_groups = NSA_HEADS // 2
    mb = mb_ref[0]
    for g in range(n_groups):
        qg = q_ref[0, :, g * LANES:(g + 1) * LANES]
        qs_scr[2 * g] = jnp.where(lo, qg, mb)
        qs_scr[2 * g + 1] = jnp.where(lo, mb, qg)
    _flash_init(m_scr, acc_scr)

    def qk_scores(j):
        start = pl.multiple_of(j * t, t)
        k_tile = k_ref[0, pl.ds(start, t), :]
        e_tile = e2_ref[pl.ds(start, t), :]
        k_sel = (jnp.where(lo, k_tile, e_tile), jnp.where(lo, e_tile, k_tile))
        return [_dot_nt(k_sel[pos % 2], qs_scr[pos]) for pos in range(NSA_HEADS)]

    def bias_tile(pos, j):
        return bias_ref[pos, jnp.where(j > i, n_near + 1, jnp.minimum(i - j, n_near))]

    def vt_slab(pos, j):
        kv = pos % 2
        return vt_ref[0, kv * VT_ROWS:(kv + 1) * VT_ROWS, pl.ds(pl.multiple_of(j * t, t), t)]

    _flash_pipeline(i, NSA_HEADS, qk_scores, bias_tile, vt_slab, s_scr, tmax_scr, m_scr, acc_scr)
    _flash_finish(o_ref, n_groups, acc_scr)


def _slc_attention(nsa_q, mask_bias, k2, e2, vt, bias):
    b, s, width = nsa_q.shape
    t = FLASH_T
    n_near = bias.shape[1] - 2
    return pl.pallas_call(
        functools.partial(_slc_kernel, n_near=n_near),
        grid=(b, s // t),
        in_specs=[pl.BlockSpec((1, t, width), lambda i, j: (i, j, 0)),
                  pl.BlockSpec((1, t, LANES), lambda i, j: (i, j, 0)),
                  pl.BlockSpec((1, s, LANES), lambda i, j: (i, 0, 0)),
                  pl.BlockSpec((s, LANES), lambda i, j: (0, 0)),
                  pl.BlockSpec((1, 2 * VT_ROWS, s), lambda i, j: (i, VT_SLC_BLOCK, 0)),
                  pl.BlockSpec(bias.shape, lambda i, j: (0, 0, 0, 0))],
        out_specs=pl.BlockSpec((1, t, width), lambda i, j: (i, j, 0)),
        out_shape=jax.ShapeDtypeStruct((b, s, width), BF16),
        scratch_shapes=[pltpu.VMEM((NSA_HEADS, t, LANES), BF16),
                        pltpu.VMEM((2, NSA_HEADS, t, t), F32),
                        pltpu.VMEM((2, NSA_HEADS, SUBLANES, t), F32),
                        pltpu.VMEM((NSA_HEADS, SUBLANES, t), F32),
                        pltpu.VMEM((NSA_HEADS, VT_ROWS, t), F32)],
        compiler_params=_params(2),
    )(nsa_q, mask_bias, k2, e2, vt, bias)


def _block_onehot(s_len):
    blk = np.arange(s_len)[:, None] // SLC_BLOCK
    lane = np.arange(LANES)[None, :] % HEAD_DIM
    return jnp.asarray((blk == lane).astype(np.float32), dtype=BF16)


def _out_proj_kernel(x_ref, mod_ref, swa_ref, fox_ref, cmp_ref, slc_ref, win_ref, misc_ref, expand_ref, gn_ref,
                     w_ref, post_ref, o_ref):
    n_swa = SWA_HEADS * HEAD_DIM
    n_fox = FOX_HEADS * HEAD_DIM
    n_nsa = NSA_HEADS * HEAD_DIM
    gate = jax.nn.sigmoid(misc_ref[0])
    gate_hi = gate.astype(BF16)
    gate_lo = (gate - gate_hi.astype(F32)).astype(BF16)
    gates = (jnp.dot(gate_hi, expand_ref[...], preferred_element_type=F32)
             + jnp.dot(gate_lo, expand_ref[...], preferred_element_type=F32))
    o_nsa = (gates[:, 0:n_nsa] * cmp_ref[0].astype(F32) + gates[:, n_nsa:2 * n_nsa] * slc_ref[0].astype(F32)
             + gates[:, 2 * n_nsa:3 * n_nsa] * win_ref[0].astype(F32))
    a = _rms(swa_ref[0].astype(F32), gn_ref[:, 0:n_swa]).astype(BF16)
    b = _rms(fox_ref[0].astype(F32), gn_ref[:, n_swa:n_swa + n_fox]).astype(BF16)
    c = _rms(o_nsa, gn_ref[:, n_swa + n_fox:]).astype(BF16)
    y = (jnp.dot(a, w_ref[0:n_swa, :], preferred_element_type=F32)
         + jnp.dot(b, w_ref[n_swa:n_swa + n_fox, :], preferred_element_type=F32)
         + jnp.dot(c, w_ref[n_swa + n_fox:, :], preferred_element_type=F32))
    o_ref[0] = x_ref[0] + mod_ref[0, 2:3, :] * _rms(y, post_ref[...])


def _gate_expansion():
    expand = np.zeros((LANES, 3 * NSA_HEADS * HEAD_DIM), np.float32)
    for branch in range(3):
        for p in range(NSA_HEADS):
            col = (branch * NSA_HEADS + p) * HEAD_DIM
            expand[GATE_LANE + 8 * branch + p, col:col + HEAD_DIM] = 1.0
    return jnp.asarray(expand, dtype=BF16)


def _out_proj(x, mod, o_swa, o_fox, o_cmp, o_slc, o_win, misc, gn, w, post):
    b, s, d = x.shape
    expand = _gate_expansion()

    def rows(width):
        return pl.BlockSpec((1, ROW_TILE, width), lambda i, j: (i, j, 0))

    return pl.pallas_call(
        _out_proj_kernel,
        grid=(b, s // ROW_TILE),
        in_specs=[rows(d),
                  pl.BlockSpec((1, ADA_CHUNKS, d), lambda i, j: (i, 0, 0)),
                  rows(o_swa.shape[2]), rows(o_fox.shape[2]), rows(o_cmp.shape[2]), rows(o_slc.shape[2]),
                  rows(o_win.shape[2]), rows(LANES),
                  pl.BlockSpec(expand.shape, lambda i, j: (0, 0)),
                  pl.BlockSpec((1, d), lambda i, j: (0, 0)),
                  pl.BlockSpec((d, d), lambda i, j: (0, 0)),
                  pl.BlockSpec((1, d), lambda i, j: (0, 0))],
        out_specs=rows(d),
        out_shape=jax.ShapeDtypeStruct((b, s, d), F32),
        compiler_params=_params(2),
    )(x, mod, o_swa, o_fox, o_cmp, o_slc, o_win, misc, expand, gn, w, post)


def _ffn_kernel(x_ref, mod_ref, pre_ref, wg_ref, wu_ref, wd_ref, post_ref, o_ref):
    x = x_ref[0]
    h = (_rms(x, pre_ref[...]) * (1.0 + mod_ref[0, 4:5, :]) + mod_ref[0, 3:4, :]).astype(BF16)
    y = jnp.zeros(x.shape, F32)
    for c in range(wg_ref.shape[0]):
        gate = jnp.dot(h, wg_ref[c], preferred_element_type=F32)
        up = jnp.dot(h, wu_ref[c], preferred_element_type=F32)
        act = (gate * jax.nn.sigmoid(gate) * up).astype(BF16)
        y = y + jnp.dot(act, wd_ref[c], preferred_element_type=F32)
    o_ref[0] = x + mod_ref[0, 5:6, :] * _rms(y, post_ref[...])


def _ffn(x, mod, pre, wg, wu, wd, post):
    b, s, d = x.shape
    n_chunks = wg.shape[0]
    rows = pl.BlockSpec((1, ROW_TILE, d), lambda i, j: (i, j, 0))
    vec = pl.BlockSpec((1, d), lambda i, j: (0, 0))
    return pl.pallas_call(
        _ffn_kernel,
        grid=(b, s // ROW_TILE),
        in_specs=[rows,
                  pl.BlockSpec((1, ADA_CHUNKS, d), lambda i, j: (i, 0, 0)),
                  vec,
                  pl.BlockSpec((n_chunks, d, FFN_CHUNK), lambda i, j: (0, 0, 0)),
                  pl.BlockSpec((n_chunks, d, FFN_CHUNK), lambda i, j: (0, 0, 0)),
                  pl.BlockSpec((n_chunks, FFN_CHUNK, d), lambda i, j: (0, 0, 0)),
                  vec],
        out_specs=rows,
        out_shape=jax.ShapeDtypeStruct((b, s, d), F32),
        compiler_params=_params(2),
    )(x, mod, pre, wg, wu, wd, post)


def _forget_lanes():
    lanes, heads = [], []
    for h in range(FOX_HEADS):
        base = (h // 2) * LANES + (HEAD_DIM if h % 2 == 0 else 0)
        for j in range(KEY_BIAS_TERMS):
            lanes.append(base + j)
            heads.append(h)
    return np.array(lanes), np.array(heads)


def _in_proj_layout():
    d = HEAD_DIM
    o_qa, o_ka, o_va, o_qb, o_kb, o_vb, o_fb, o_qc = 0, 256, 384, 512, 768, 1024, 1280, 1284
    o_kc, o_vc, o_ksl, o_vsl, o_kw, o_vw, o_gc = 1796, 1924, 2052, 2180, 2308, 2436, 2564
    scale = LOG2E / math.sqrt(d)

    def head_cols(base, heads):
        return np.concatenate([np.arange(base + h * d, base + (h + 1) * d) for h in heads])

    def span(base, width):
        return np.arange(base, base + width)

    cols = [head_cols(o_qa, SWA_POS), span(o_ka, 128),
            span(o_qb, 256), span(o_kb, 256),
            head_cols(o_qc, NSA_POS),
            span(o_kc, 128), span(o_vc, 128),
            span(o_ksl, 128), span(o_kw, 128),
            span(o_vb, 256), span(o_va, 128), span(o_vsl, 128), span(o_vw, 128)]
    scales = [np.full(256, scale), np.ones(128), np.full(256, scale), np.ones(256), np.full(512, scale),
              np.ones(256), np.ones(256), np.ones(640)]
    lanes, heads = _forget_lanes()
    misc_cols = np.zeros(SEG_MISC[1] - SEG_MISC[0], np.int64)
    misc_scale = np.zeros(SEG_MISC[1] - SEG_MISC[0])
    misc_cols[lanes] = o_fb + heads
    misc_scale[lanes] = 1.0
    for branch in range(3):
        for p, h in enumerate(NSA_POS):
            misc_cols[GATE_LANE + 8 * branch + p] = o_gc + h * 3 + branch
            misc_scale[GATE_LANE + 8 * branch + p] = 1.0
    cols.append(misc_cols)
    scales.append(misc_scale)
    return np.concatenate(cols), np.concatenate(scales).astype(np.float32)


def _head_perm(pos):
    return np.concatenate([np.arange(h * HEAD_DIM, (h + 1) * HEAD_DIM) for h in pos])


def kernel(x, c, rel_bias, ada_w, ada_b, attn_pre_norm, attn_post_norm, ffn_pre_norm, ffn_post_norm, w_in,
           forget_bias, swa_sinks, cmp_pos, cmp_w1, cmp_w2, group_norm, w_out, ffn_w_gate, ffn_w_up, ffn_w_down):
    b, s, d = x.shape
    depth = w_in.shape[0]
    hidden = ffn_w_gate.shape[2]
    assert s % (2 * FLASH_T) == 0 and s // SLC_BLOCK <= HEAD_DIM and hidden % FFN_CHUNK == 0

    cols, scales = _in_proj_layout()
    w_all = (w_in[:, :, cols] * scales).astype(BF16)
    lanes, heads = _forget_lanes()
    fbias_all = jnp.zeros((depth, 1, SEG_MISC[1] - SEG_MISC[0]), F32).at[:, 0, lanes].set(
        forget_bias[:, heads].astype(F32))
    swa_perm = _head_perm(SWA_POS)
    nsa_perm = _head_perm(NSA_POS)
    n_swa, n_fox = SWA_HEADS * HEAD_DIM, FOX_HEADS * HEAD_DIM
    mix_perm = np.concatenate([swa_perm, n_swa + np.arange(n_fox), n_swa + n_fox + nsa_perm])
    gn_all = group_norm[:, mix_perm].astype(F32)
    w_out_all = w_out[:, mix_perm, :].astype(BF16)
    n_chunks = hidden // FFN_CHUNK
    wg_all = ffn_w_gate.reshape(depth, d, n_chunks, FFN_CHUNK).transpose(0, 2, 1, 3).astype(BF16)
    wu_all = ffn_w_up.reshape(depth, d, n_chunks, FFN_CHUNK).transpose(0, 2, 1, 3).astype(BF16)
    wd_all = ffn_w_down.reshape(depth, n_chunks, FFN_CHUNK, d).astype(BF16)

    tab_swa = rel_bias[:, np.array(SWA_POS)].astype(F32)
    tab_nsa = rel_bias[:, SWA_HEADS + np.array(NSA_POS)].astype(F32)
    bias_swa = _bias_table(tab_swa, _band_buckets_t(SWA_TILE, SWA_WINDOW))
    bias_win = _bias_table(tab_nsa, _band_buckets_t(WIN_TILE, NSA_WINDOW))
    bias_slc = _bias_table(tab_nsa, _toeplitz_buckets_t(FLASH_T, _near_tiles(FLASH_T)), subtract_last=True)
    n_rows = s // CMP_STRIDE
    bias_cmp = _bias_table(tab_nsa, _cmp_buckets_t(s, n_rows))
    overlap_t = _overlap_t(s)
    e2 = _block_onehot(s)

    mod_all = _adaln(c.astype(F32), ada_w.astype(F32), ada_b.astype(F32)).reshape(depth, b, ADA_CHUNKS, d)

    for layer in range(depth):
        mod = mod_all[layer]
        swa_qk, fox_qk, nsa_q, kc, vc, k2, misc, vt = _in_proj(
            x, mod, attn_pre_norm[layer].reshape(1, d).astype(F32), w_all[layer])
        o_swa = _banded_attention(swa_qk, swa_qk, 2, vt, VT_SWA_BLOCK, bias_swa,
                                  sinks=swa_sinks[layer][np.array(SWA_POS)].astype(F32))
        o_fox = _fox_attention(fox_qk, _fox_key_terms(misc, fbias_all[layer]), vt)
        cmp_kv = _compress(kc, vc, cmp_pos[layer], cmp_w1[layer], cmp_w2[layer])
        o_cmp, mask_bias = _select(nsa_q, cmp_kv, bias_cmp, overlap_t)
        o_slc = _slc_attention(nsa_q, mask_bias, k2, e2, vt, bias_slc)
        o_win = _banded_attention(nsa_q, k2, 1, vt, VT_WIN_BLOCK, bias_win)
        x = _out_proj(x, mod, o_swa, o_fox, o_cmp, o_slc, o_win, misc, gn_all[layer].reshape(1, d),
                      w_out_all[layer], attn_post_norm[layer].reshape(1, d).astype(F32))
        x = _ffn(x, mod, ffn_pre_norm[layer].reshape(1, d).astype(F32), wg_all[layer], wu_all[layer],
                 wd_all[layer], ffn_post_norm[layer].reshape(1, d).astype(F32))
    return x
```

```python
import functools
import math

import numpy as np
import jax
import jax.numpy as jnp
from jax import lax
from jax.experimental import pallas as pl
from jax.experimental.pallas import tpu as pltpu

F32 = jnp.float32
BF16 = jnp.bfloat16
HIGHEST = lax.Precision.HIGHEST

LANES = 128
SUBLANES = 8
VMEM_LIMIT = 56 * 1024 * 1024

HEAD_DIM = 64
SWA_HEADS = 4
SWA_WINDOW = 128
FOX_HEADS = 4
NSA_HEADS = 8
CMP_LEN = 32
CMP_STRIDE = 16
CMP_HIDDEN = 2 * HEAD_DIM
SLC_BLOCK = 64
TOPK = 16
NSA_WINDOW = 512
REL_BUCKETS = 32
REL_MAX_DISTANCE = 1024
ZERO_BUCKET = -2
RMS_EPS = 1e-6
NEG = -1e30
FORCE = 1e30
ADA_CHUNKS = 6
LOG2E = math.log2(math.e)

SWA_POS = (0, 2, 1, 3)
NSA_POS = (0, 4, 1, 5, 2, 6, 3, 7)

SWA_TILE = 256
WIN_TILE = 256
FLASH_T = 256
SEL_TQ = 256
ROW_TILE = 512
FFN_CHUNK = 256
VT_ROWS = HEAD_DIM + 16
KEY_BIAS_TERMS = 3

SEG_SWA = (0, 384)
SEG_FOX = (384, 896)
SEG_NSAQ = (896, 1408)
SEG_KC = (1408, 1536)
SEG_VC = (1536, 1664)
SEG_K2 = (1664, 1920)
SEG_V = (1920, 2560)
SEG_MISC = (2560, 2816)
GATE_LANE = 8
VT_FOX_BLOCK, VT_SWA_BLOCK, VT_SLC_BLOCK, VT_WIN_BLOCK = 0, 2, 3, 4


def _params(n_grid, vmem=VMEM_LIMIT):
    return pltpu.CompilerParams(dimension_semantics=("parallel",) * n_grid, vmem_limit_bytes=vmem)


def _dot_nt(a, b):
    return lax.dot_general(a, b, (((1,), (1,)), ((), ())), preferred_element_type=F32)


def _lane_lo(shape):
    return lax.broadcasted_iota(jnp.int32, shape, len(shape) - 1) < HEAD_DIM


def _adaln_kernel(c_ref, w_ref, b_ref, o_ref):
    c = c_ref[...]
    act = c * jax.nn.sigmoid(c)
    o_ref[0] = jnp.dot(act, w_ref[0], precision=HIGHEST, preferred_element_type=F32) + b_ref[0]


def _adaln(c, ada_w, ada_b):
    depth, d, n = ada_w.shape
    b = c.shape[0]
    return pl.pallas_call(
        _adaln_kernel,
        grid=(depth, n // d),
        in_specs=[pl.BlockSpec((b, d), lambda l, j: (0, 0)),
                  pl.BlockSpec((1, d, d), lambda l, j: (l, 0, j)),
                  pl.BlockSpec((1, 1, d), lambda l, j: (l, 0, j))],
        out_specs=pl.BlockSpec((1, b, d), lambda l, j: (l, 0, j)),
        out_shape=jax.ShapeDtypeStruct((depth, b, n), F32),
        compiler_params=_params(2),
    )(c, ada_w, ada_b.reshape(depth, 1, n))


def _t5_bucket(dist):
    n = jnp.maximum(dist, 0)
    max_exact = REL_BUCKETS // 2
    nf = jnp.maximum(n, 1).astype(jnp.float32)
    large = max_exact + (jnp.log(nf / max_exact) / math.log(REL_MAX_DISTANCE / max_exact)
                         * (REL_BUCKETS - max_exact)).astype(jnp.int32)
    large = jnp.minimum(large, REL_BUCKETS - 1)
    return jnp.where(n < max_exact, n, large)


def _bias_table_kernel(tab_ref, bucket_ref, o_ref, *, subtract_last):
    h = pl.program_id(0)
    bucket = bucket_ref[0]
    off = tab_ref[REL_BUCKETS - 1, h] if subtract_last else 0.0
    acc = jnp.full(bucket.shape, NEG, F32)
    for k in range(REL_BUCKETS):
        acc = jnp.where(bucket == k, (tab_ref[k, h] - off) * LOG2E, acc)
    o_ref[0, 0] = jnp.where(bucket == ZERO_BUCKET, 0.0, acc)


def _bias_table(table, bucket, subtract_last=False):
    n_heads = table.shape[1]
    n, r, c = bucket.shape
    return pl.pallas_call(
        functools.partial(_bias_table_kernel, subtract_last=subtract_last),
        grid=(n_heads, n),
        in_specs=[pl.BlockSpec(memory_space=pltpu.SMEM),
                  pl.BlockSpec((1, r, c), lambda h, i: (i, 0, 0))],
        out_specs=pl.BlockSpec((1, 1, r, c), lambda h, i: (h, i, 0, 0)),
        out_shape=jax.ShapeDtypeStruct((n_heads, n, r, c), F32),
        compiler_params=_params(2),
    )(table, bucket)


def _band_buckets_t(tile, window):
    n_back = -(-(window - 1) // tile)
    t = jnp.arange(n_back + 1)[:, None, None]
    key = jnp.arange(tile)[None, :, None]
    query = jnp.arange(tile)[None, None, :]
    dist = query + (n_back - t) * tile - key
    return jnp.where((dist >= 0) & (dist < window), _t5_bucket(dist), -1).astype(jnp.int32)


def _toeplitz_buckets_t(tile, n_tiles):
    m = jnp.arange(n_tiles)[:, None, None]
    key = jnp.arange(tile)[None, :, None]
    query = jnp.arange(tile)[None, None, :]
    dist = m * tile + query - key
    near = jnp.where(dist >= 0, _t5_bucket(dist), -1).astype(jnp.int32)
    return jnp.concatenate([near, jnp.full((1, tile, tile), ZERO_BUCKET, jnp.int32),
                            jnp.full((1, tile, tile), -1, jnp.int32)])


def _cmp_buckets_t(s_len, n_rows):
    n_c = n_rows - 1
    tile = jnp.arange(s_len // SEL_TQ)[:, None, None]
    n = jnp.arange(n_rows)[None, :, None]
    t = tile * SEL_TQ + jnp.arange(SEL_TQ)[None, None, :]
    dist = t - (n * CMP_STRIDE + CMP_LEN - 1)
    return jnp.where((dist >= 0) & (n < n_c), _t5_bucket(dist), -1).astype(jnp.int32)


def _near_tiles(tile):
    max_exact = REL_BUCKETS // 2
    first_const = math.ceil(max_exact * (REL_MAX_DISTANCE / max_exact) ** ((max_exact - 1) / max_exact)) + 1
    m = 1
    while m * tile - (tile - 1) < first_const:
        m += 1
    return m


def _rms(x, gain):
    return x * lax.rsqrt(jnp.mean(x * x, axis=-1, keepdims=True) + RMS_EPS) * gain


def _in_proj_kernel(x_ref, mod_ref, gain_ref, w_ref, swa_ref, fox_ref, nsaq_ref, kc_ref, vc_ref, k2_ref,
                    misc_ref, vt_ref):
    x = x_ref[0]
    h = _rms(x, gain_ref[...]) * (1.0 + mod_ref[0, 1:2, :]) + mod_ref[0, 0:1, :]
    hb = h.astype(BF16)

    def seg(bounds):
        return jnp.dot(hb, w_ref[:, bounds[0]:bounds[1]], preferred_element_type=F32)

    swa_ref[0] = seg(SEG_SWA).astype(BF16)
    fox_ref[0] = seg(SEG_FOX).astype(BF16)
    nsaq_ref[0] = seg(SEG_NSAQ).astype(BF16)
    kc_ref[0] = seg(SEG_KC).astype(BF16)
    vc_ref[0] = seg(SEG_VC).astype(BF16)
    k2_ref[0] = seg(SEG_K2).astype(BF16)
    misc_ref[0] = seg(SEG_MISC)

    rows = x.shape[0]
    extra_row = lax.broadcasted_iota(jnp.int32, (VT_ROWS - HEAD_DIM, rows), 0)
    extra = jnp.where(extra_row == 0, 1.0, 0.0).astype(BF16)
    values = seg(SEG_V)
    for c in range(values.shape[1] // LANES):
        vt = values[:, c * LANES:(c + 1) * LANES].T.astype(BF16)
        for half in range(2):
            base = (2 * c + half) * VT_ROWS
            vt_ref[0, base:base + HEAD_DIM, :] = vt[half * HEAD_DIM:(half + 1) * HEAD_DIM, :]
            vt_ref[0, base + HEAD_DIM:base + VT_ROWS, :] = extra


def _in_proj(x, mod, gain, w):
    b, s, d = x.shape
    n = w.shape[1]
    widths = [hi - lo for lo, hi in (SEG_SWA, SEG_FOX, SEG_NSAQ, SEG_KC, SEG_VC, SEG_K2, SEG_MISC)]
    dtypes = [BF16] * 6 + [F32]
    vt_rows = (SEG_V[1] - SEG_V[0]) // HEAD_DIM * VT_ROWS
    return pl.pallas_call(
        _in_proj_kernel,
        grid=(b, s // ROW_TILE),
        in_specs=[pl.BlockSpec((1, ROW_TILE, d), lambda i, j: (i, j, 0)),
                  pl.BlockSpec((1, ADA_CHUNKS, d), lambda i, j: (i, 0, 0)),
                  pl.BlockSpec((1, d), lambda i, j: (0, 0)),
                  pl.BlockSpec((d, n), lambda i, j: (0, 0))],
        out_specs=[pl.BlockSpec((1, ROW_TILE, wd), lambda i, j: (i, j, 0)) for wd in widths]
        + [pl.BlockSpec((1, vt_rows, ROW_TILE), lambda i, j: (i, 0, j))],
        out_shape=[jax.ShapeDtypeStruct((b, s, wd), dt) for wd, dt in zip(widths, dtypes)]
        + [jax.ShapeDtypeStruct((b, vt_rows, s), BF16)],
        compiler_params=_params(2),
    )(x, mod, gain, w)


def _banded_kernel(*refs, n_back, n_groups, has_sink, t):
    if has_sink:
        sink_ref, q_ref, k_ref, vt_ref, bias_ref, o_ref = refs
    else:
        q_ref, k_ref, vt_ref, bias_ref, o_ref = refs
    i = pl.program_id(1)
    lo = _lane_lo((t, LANES))
    n_tiles = n_back + 1

    def run(all_valid):
        starts = [pl.multiple_of(jnp.maximum(i - n_back + tt, 0) * t, t) for tt in range(n_tiles)]
        k_tiles = [k_ref[0, pl.ds(start, t), :] for start in starts]

        def scores(g):
            qg = q_ref[0, :, g * LANES:(g + 1) * LANES]
            zero = jnp.zeros_like(qg)
            qms = (jnp.where(lo, qg, zero), jnp.where(lo, zero, qg))
            return [[bias_ref[2 * g + half, tt] + _dot_nt(k_tiles[tt], qms[half]) for tt in range(n_tiles)]
                    for half in range(2)]

        def softmax_pv(g, sts):
            pair = []
            for half in range(2):
                tiles = sts[half]
                if not all_valid:
                    tiles = [jnp.where(i - n_back + tt >= 0, st, NEG) if tt < n_back else st
                             for tt, st in enumerate(tiles)]
                m = None
                for st in tiles:
                    part = st.reshape(t // SUBLANES, SUBLANES, t).max(axis=0)
                    m = part if m is None else jnp.maximum(m, part)
                m = _all_sublanes(m, jnp.maximum)
                if has_sink:
                    sink = sink_ref[2 * g + half] * LOG2E
                    m = jnp.maximum(m, sink)
                acc = None
                for tt, st in enumerate(tiles):
                    p = jnp.exp2((st.reshape(t // SUBLANES, SUBLANES, t) - m[None]).reshape(t, t).astype(BF16))
                    part = jnp.dot(vt_ref[0, half * VT_ROWS:(half + 1) * VT_ROWS, pl.ds(starts[tt], t)], p,
                                   preferred_element_type=F32)
                    acc = part if acc is None else acc + part
                denom = _all_sublanes(acc[HEAD_DIM:HEAD_DIM + SUBLANES, :], jnp.add)
                if has_sink:
                    denom = denom + jnp.exp2(sink - m)
                out = acc[0:HEAD_DIM, :].reshape(HEAD_DIM // SUBLANES, SUBLANES, t) / denom[None]
                pair.append(out.reshape(HEAD_DIM, t))
            o_ref[0, :, g * LANES:(g + 1) * LANES] = jnp.concatenate(pair, axis=0).T.astype(o_ref.dtype)

        pending = scores(0)
        for g in range(n_groups):
            current = pending
            if g + 1 < n_groups:
                pending = scores(g + 1)
            softmax_pv(g, current)

    @pl.when(i >= n_back)
    def _():
        run(True)

    @pl.when(i < n_back)
    def _():
        run(False)


def _banded_attention(q_arr, k_arr, k_blk, vt, vt_blk, bias, sinks=None):
    b, s, _ = q_arr.shape
    n_pos, n_tiles, t = bias.shape[0], bias.shape[1], bias.shape[2]
    width = n_pos * HEAD_DIM
    in_specs = [pl.BlockSpec((1, t, width), lambda i, j: (i, j, 0)),
                pl.BlockSpec((1, s, LANES), lambda i, j: (i, 0, k_blk)),
                pl.BlockSpec((1, 2 * VT_ROWS, s), lambda i, j: (i, vt_blk, 0)),
                pl.BlockSpec(bias.shape, lambda i, j: (0, 0, 0, 0))]
    args = [q_arr, k_arr, vt, bias]
    if sinks is not None:
        in_specs = [pl.BlockSpec(memory_space=pltpu.SMEM)] + in_specs
        args = [sinks] + args
    return pl.pallas_call(
        functools.partial(_banded_kernel, n_back=n_tiles - 1, n_groups=n_pos // 2, has_sink=sinks is not None, t=t),
        grid=(b, s // t),
        in_specs=in_specs,
        out_specs=pl.BlockSpec((1, t, width), lambda i, j: (i, j, 0)),
        out_shape=jax.ShapeDtypeStruct((b, s, width), BF16),
        compiler_params=_params(2),
    )(*args)


def _all_sublanes(x, op):
    for shift in (4, 2, 1):
        x = op(x, pltpu.roll(x, shift, 0))
    return x


def _flash_init(m_scr, acc_scr):
    m_scr[...] = jnp.full(m_scr.shape, NEG, F32)
    acc_scr[...] = jnp.zeros(acc_scr.shape, F32)


def _flash_update(h, st_ref, tile_max, vt_h, m_scr, acc_scr):
    tk, tq = st_ref.shape
    m_prev = m_scr[h]
    m_new = _all_sublanes(jnp.maximum(m_prev, tile_max), jnp.maximum)
    alpha = jnp.exp2(m_prev - m_new)
    p = jnp.exp2((st_ref[...].reshape(tk // SUBLANES, SUBLANES, tq) - m_new[None]).reshape(tk, tq).astype(BF16))
    acc = acc_scr[h].reshape(VT_ROWS // SUBLANES, SUBLANES, tq) * alpha[None]
    acc_scr[h] = acc.reshape(VT_ROWS, tq) + jnp.dot(vt_h, p, preferred_element_type=F32)
    m_scr[h] = m_new


def _flash_finish(o_ref, n_groups, acc_scr):
    tq = acc_scr.shape[2]
    for g in range(n_groups):
        pair = []
        for h in (2 * g, 2 * g + 1):
            denom = _all_sublanes(acc_scr[h, HEAD_DIM:HEAD_DIM + SUBLANES, :], jnp.add)
            out = acc_scr[h, 0:HEAD_DIM, :].reshape(HEAD_DIM // SUBLANES, SUBLANES, tq) / denom[None]
            pair.append(out.reshape(HEAD_DIM, tq))
        o_ref[0, :, g * LANES:(g + 1) * LANES] = jnp.concatenate(pair, axis=0).T.astype(o_ref.dtype)


def _flash_pipeline(i, n_heads, qk_scores, bias_tile, vt_slab, s_scr, tmax_scr, m_scr, acc_scr):
    def qk_head(thunk, h, j, slot):
        st = thunk() + bias_tile(h, j)
        s_scr[slot, h] = st
        tmax_scr[slot, h] = st.reshape(st.shape[0] // SUBLANES, SUBLANES, st.shape[1]).max(axis=0)

    def softmax_head(h, j, slot):
        _flash_update(h, s_scr.at[slot, h], tmax_scr[slot, h], vt_slab(h, jnp.minimum(j, i)), m_scr, acc_scr)

    def stage(j_qk, slot_qk, j_sm, slot_sm):
        thunks = qk_scores(jnp.minimum(j_qk, i))
        for h in range(n_heads):
            qk_head(thunks[h], h, j_qk, slot_qk)
            softmax_head(h, j_sm, slot_sm)

    for h, thunk in enumerate(qk_scores(0)):
        qk_head(thunk, h, 0, 0)

    def body(trip, carry):
        j = 2 * trip
        stage(j + 1, 1, j, 0)
        stage(j + 2, 0, j + 1, 1)
        return carry

    lax.fori_loop(0, (i + 2) // 2, body, 0)


def _fox_aug_kernel(misc_ref, fbias_ref, tri_ref, o_ref):
    s_len, width = misc_ref.shape[1], misc_ref.shape[2]
    term = lax.broadcasted_iota(jnp.int32, (LANES, width), 1) % HEAD_DIM
    carry = jnp.zeros((1, width), F32)
    for c in range(s_len // LANES):
        z = misc_ref[0, c * LANES:(c + 1) * LANES, :] + fbias_ref[...]
        log_f = jnp.minimum(z, 0.0) - jnp.log1p(jnp.exp(-jnp.abs(z)))
        cum = jnp.dot(tri_ref[...], log_f, precision=HIGHEST, preferred_element_type=F32) + carry
        carry = cum[LANES - 1:LANES, :]
        x = cum * (-LOG2E)
        hi = x.astype(BF16).astype(F32)
        rest = x - hi
        mid = rest.astype(BF16).astype(F32)
        low = rest - mid
        out = jnp.where(term == 0, hi, jnp.where(term == 1, mid, jnp.where(term == 2, low, 0.0)))
        o_ref[0, c * LANES:(c + 1) * LANES, :] = out.astype(BF16)


def _fox_key_terms(misc, fbias):
    b, s, width = misc.shape
    tri = jnp.asarray(np.tril(np.ones((LANES, LANES), np.float32)))
    return pl.pallas_call(
        _fox_aug_kernel,
        grid=(b,),
        in_specs=[pl.BlockSpec((1, s, width), lambda i: (i, 0, 0)),
                  pl.BlockSpec((1, width), lambda i: (0, 0)),
                  pl.BlockSpec((LANES, LANES), lambda i: (0, 0))],
        out_specs=pl.BlockSpec((1, s, width), lambda i: (i, 0, 0)),
        out_shape=jax.ShapeDtypeStruct((b, s, width), BF16),
        compiler_params=_params(1),
    )(misc, fbias, tri)


def _fox_kernel(q_ref, k_ref, aug_ref, vt_ref, mask_ref, o_ref, qs_scr, s_scr, tmax_scr, m_scr, acc_scr):
    t = FLASH_T
    i = pl.program_id(1)
    lo = _lane_lo((t, LANES))
    lane = lax.broadcasted_iota(jnp.int32, (t, LANES), 1)
    ones = jnp.where(lane % HEAD_DIM < KEY_BIAS_TERMS, 1.0, 0.0).astype(BF16)
    n_groups = FOX_HEADS // 2
    for g in range(n_groups):
        qg = q_ref[0, :, g * LANES:(g + 1) * LANES]
        qs_scr[2 * g] = jnp.where(lo, qg, ones)
        qs_scr[2 * g + 1] = jnp.where(lo, ones, qg)
    _flash_init(m_scr, acc_scr)

    def qk_scores(j):
        start = pl.multiple_of(j * t, t)
        scores = []
        for g in range(n_groups):
            k_tile = k_ref[0, pl.ds(start, t), g * LANES:(g + 1) * LANES]
            a_tile = aug_ref[0, pl.ds(start, t), g * LANES:(g + 1) * LANES]
            k_sel = (jnp.where(lo, k_tile, a_tile), jnp.where(lo, a_tile, k_tile))
            for half in range(2):
                scores.append(functools.partial(lambda k, h: _dot_nt(k, qs_scr[h]), k_sel[half], 2 * g + half))
        return scores

    def bias_tile(h, j):
        return mask_ref[jnp.where(j > i, 2, jnp.minimum(i - j, 1))]

    def vt_slab(h, j):
        return vt_ref[0, h * VT_ROWS:(h + 1) * VT_ROWS, pl.ds(pl.multiple_of(j * t, t), t)]

    _flash_pipeline(i, FOX_HEADS, qk_scores, bias_tile, vt_slab, s_scr, tmax_scr, m_scr, acc_scr)
    _flash_finish(o_ref, n_groups, acc_scr)


def _fox_attention(fox_qk, key_terms, vt):
    b, s, _ = fox_qk.shape
    width = FOX_HEADS * HEAD_DIM
    t = FLASH_T
    idx = np.arange(t)
    diag = np.where(idx[:, None] <= idx[None, :], 0.0, NEG)
    masks = jnp.asarray(np.stack([diag, np.zeros((t, t)), np.full((t, t), NEG)]).astype(np.float32))
    return pl.pallas_call(
        _fox_kernel,
        grid=(b, s // t),
        in_specs=[pl.BlockSpec((1, t, width), lambda i, j: (i, j, 0)),
                  pl.BlockSpec((1, s, width), lambda i, j: (i, 0, 1)),
                  pl.BlockSpec((1, s, width), lambda i, j: (i, 0, 0)),
                  pl.BlockSpec((1, FOX_HEADS * VT_ROWS, s), lambda i, j: (i, VT_FOX_BLOCK, 0)),
                  pl.BlockSpec(masks.shape, lambda i, j: (0, 0, 0))],
        out_specs=pl.BlockSpec((1, t, width), lambda i, j: (i, j, 0)),
        out_shape=jax.ShapeDtypeStruct((b, s, width), BF16),
        scratch_shapes=[pltpu.VMEM((FOX_HEADS, t, LANES), BF16),
                        pltpu.VMEM((2, FOX_HEADS, t, t), F32),
                        pltpu.VMEM((2, FOX_HEADS, SUBLANES, t), F32),
                        pltpu.VMEM((FOX_HEADS, SUBLANES, t), F32),
                        pltpu.VMEM((FOX_HEADS, VT_ROWS, t), F32)],
        compiler_params=_params(2),
    )(fox_qk, fox_qk, key_terms, vt, masks)


def _compress_kernel(x_ref, pe_ref, w1a_ref, w1b_ref, w2_ref, o_ref):
    x = x_ref[0].astype(F32)
    n_rows = x.shape[0]

    def mm(a, w):
        return jnp.dot(a, w, precision=HIGHEST, preferred_element_type=F32)

    first = mm(x, w1a_ref[0])
    second = mm(x, w1b_ref[0])
    pe_term = (mm(pe_ref[0, 0], w1a_ref[0]) + mm(pe_ref[0, 1], w1b_ref[0]))[0:1, :]
    pre = first + pltpu.roll(second, n_rows - 1, 0) + pe_term
    hid = 0.5 * pre * (1.0 + jnp.tanh(math.sqrt(2.0 / math.pi) * (pre + 0.044715 * (pre * pre * pre))))
    o_ref[0, 0] = mm(hid, w2_ref[0])


def _compress(kc, vc, cmp_pos, cmp_w1, cmp_w2):
    b, s, _ = kc.shape
    n_rows = s // CMP_STRIDE
    half = CMP_LEN // 2
    feat = CMP_STRIDE * LANES
    x = jnp.stack([kc.reshape(b, n_rows, feat), vc.reshape(b, n_rows, feat)])
    eye = jnp.eye(2, dtype=F32)
    w1 = cmp_w1.astype(F32).reshape(2, CMP_LEN, HEAD_DIM, CMP_HIDDEN)
    w1a = jnp.einsum('wldj,hg->wlhdgj', w1[:, :half], eye).reshape(2, feat, 2 * CMP_HIDDEN)
    w1b = jnp.einsum('wldj,hg->wlhdgj', w1[:, half:], eye).reshape(2, feat, 2 * CMP_HIDDEN)
    w2 = jnp.einsum('wjd,hg->whjgd', cmp_w2.astype(F32), eye).reshape(2, 2 * CMP_HIDDEN, LANES)
    w2 = jnp.concatenate([w2, jnp.roll(w2, HEAD_DIM, axis=2)], axis=2)
    pe = jnp.broadcast_to(cmp_pos.astype(F32).reshape(2, 2, half, 1, HEAD_DIM), (2, 2, half, 2, HEAD_DIM))
    pe = jnp.broadcast_to(pe.reshape(2, 2, 1, feat), (2, 2, 8, feat))
    return pl.pallas_call(
        _compress_kernel,
        grid=(2, b),
        in_specs=[pl.BlockSpec((None, 1, n_rows, feat), lambda w, i: (w, i, 0, 0)),
                  pl.BlockSpec((1, 2, 8, feat), lambda w, i: (w, 0, 0, 0)),
                  pl.BlockSpec((1, feat, 2 * CMP_HIDDEN), lambda w, i: (w, 0, 0)),
                  pl.BlockSpec((1, feat, 2 * CMP_HIDDEN), lambda w, i: (w, 0, 0)),
                  pl.BlockSpec((1, 2 * CMP_HIDDEN, 2 * LANES), lambda w, i: (w, 0, 0))],
        out_specs=pl.BlockSpec((1, 1, n_rows, 2 * LANES), lambda w, i: (w, i, 0, 0)),
        out_shape=jax.ShapeDtypeStruct((2, b, n_rows, 2 * LANES), F32),
        compiler_params=_params(2),
    )(x, pe, w1a, w1b, w2)


def _select_kernel(q_ref, kc_ref, vct_ref, bias_ref, ovl_ref, o_ref, mb_ref):
    tq = SEL_TQ
    i = pl.program_id(0)
    lo = _lane_lo((tq, LANES))
    n_rows = kc_ref.shape[2]
    lo_k = _lane_lo((n_rows, LANES))
    n_grp = n_rows // SUBLANES

    k_own = kc_ref[0, 0, :, 0:LANES]
    k_swapped = kc_ref[0, 0, :, LANES:2 * LANES]
    hi = k_own.astype(BF16)
    low = (k_swapped - k_swapped.astype(BF16).astype(F32)).astype(BF16)
    k_sel = (jnp.where(lo_k, hi, low), jnp.where(lo_k, low, hi))

    def scores(g):
        qg = q_ref[0, :, g * LANES:(g + 1) * LANES]
        swapped = pltpu.roll(qg.astype(F32), HEAD_DIM, 1).astype(BF16)
        q_dup = (jnp.where(lo, qg, swapped), jnp.where(lo, swapped, qg))
        return [bias_ref[2 * g + half, 0] + _dot_nt(k_sel[half], q_dup[half]) for half in range(2)]

    query = i * tq + lax.broadcasted_iota(jnp.int32, (SUBLANES, tq), 1)
    has_keys = query >= CMP_LEN - 1
    p_sum = [None, None]

    def softmax_pv(g, sts):
        pair = []
        for half in range(2):
            s3 = sts[half].reshape(n_grp, SUBLANES, tq)
            m = _all_sublanes(s3.max(axis=0), jnp.maximum)
            e = jnp.exp2(s3 - m[None])
            inv = jnp.where(has_keys, 1.0 / _all_sublanes(e.sum(axis=0), jnp.add), 0.0)
            p = e * inv[None]
            p_sum[half] = p if p_sum[half] is None else p_sum[half] + p
            pair.append(jnp.dot(vct_ref[0, half * HEAD_DIM:(half + 1) * HEAD_DIM, :],
                                p.reshape(n_rows, tq).astype(BF16), preferred_element_type=F32))
        o_ref[0, :, g * LANES:(g + 1) * LANES] = jnp.concatenate(pair, axis=0).T.astype(o_ref.dtype)

    n_groups = NSA_HEADS // 2
    pending = scores(0)
    for g in range(n_groups):
        current = pending
        if g + 1 < n_groups:
            pending = scores(g + 1)
        softmax_pv(g, current)

    n_blk = ovl_ref.shape[0]
    blk_grp = n_blk // SUBLANES
    sub = lax.broadcasted_iota(jnp.int32, (SUBLANES, tq), 0)
    q_blk = query // SLC_BLOCK
    masks = []
    for half in (1, 0):
        imp = jnp.dot(ovl_ref[...], p_sum[half].reshape(n_rows, tq), precision=HIGHEST,
                      preferred_element_type=F32)
        rows = []
        for r in range(blk_grp):
            blk = sub + r * SUBLANES
            forced = jnp.where(blk == 0, 1, 0) + jnp.where(blk == q_blk, 1, 0) + jnp.where(blk == q_blk - 1, 1, 0)
            rows.append(jnp.where(forced > 0, FORCE,
                                  jnp.where(blk > q_blk, NEG, imp[r * SUBLANES:(r + 1) * SUBLANES, :])))
        counts = [jnp.zeros((SUBLANES, tq), jnp.int32) for _ in range(blk_grp)]
        for other in range(n_blk):
            r_other, s_other = divmod(other, SUBLANES)
            row = jnp.broadcast_to(rows[r_other][s_other:s_other + 1, :], (SUBLANES, tq))
            for r in range(blk_grp):
                if r > r_other:
                    beats = jnp.where(row >= rows[r], 1, 0)
                elif r < r_other:
                    beats = jnp.where(row > rows[r], 1, 0)
                else:
                    beats = jnp.where(sub > s_other, jnp.where(row >= rows[r], 1, 0), jnp.where(row > rows[r], 1, 0))
                counts[r] = counts[r] + beats
        masks.extend(jnp.where(cnt < TOPK, 0.0, NEG) for cnt in counts)
    for c in range(tq // LANES):
        mb_ref[0, c * LANES:(c + 1) * LANES, :] = jnp.concatenate(
            [mk[:, c * LANES:(c + 1) * LANES] for mk in masks], axis=0).T.astype(BF16)


def _select(nsa_q, cmp_kv, bias_c, overlap_t):
    b, s, width = nsa_q.shape
    n_rows = cmp_kv.shape[2]
    n_blk = overlap_t.shape[0]
    assert n_blk == HEAD_DIM
    tq = SEL_TQ
    vct = cmp_kv[1, :, :, 0:LANES].transpose(0, 2, 1).astype(BF16)
    return pl.pallas_call(
        _select_kernel,
        grid=(s // tq, b),
        in_specs=[pl.BlockSpec((1, tq, width), lambda j, i: (i, j, 0)),
                  pl.BlockSpec((1, 1, n_rows, 2 * LANES), lambda j, i: (0, i, 0, 0)),
                  pl.BlockSpec((1, LANES, n_rows), lambda j, i: (i, 0, 0)),
                  pl.BlockSpec((NSA_HEADS, 1, n_rows, tq), lambda j, i: (0, j, 0, 0)),
                  pl.BlockSpec((n_blk, n_rows), lambda j, i: (0, 0))],
        out_specs=[pl.BlockSpec((1, tq, width), lambda j, i: (i, j, 0)),
                   pl.BlockSpec((1, tq, LANES), lambda j, i: (i, j, 0))],
        out_shape=[jax.ShapeDtypeStruct((b, s, width), BF16),
                   jax.ShapeDtypeStruct((b, s, LANES), BF16)],
        compiler_params=_params(2),
    )(nsa_q, cmp_kv, vct, bias_c, overlap_t)


def _overlap_t(s_len):
    n_rows = s_len // CMP_STRIDE
    c_start = np.arange(n_rows)[None, :] * CMP_STRIDE
    s_start = np.arange(HEAD_DIM)[:, None] * SLC_BLOCK
    ovl = np.clip(np.minimum(c_start + CMP_LEN, s_start + SLC_BLOCK) - np.maximum(c_start, s_start), 0, None)
    ovl = ovl.astype(np.float32) / CMP_LEN
    ovl[:, n_rows - 1] = 0.0
    ovl[s_len // SLC_BLOCK:, :] = 0.0
    return jnp.asarray(ovl)


def _slc_kernel(q_ref, mb_ref, k_ref, e2_ref, vt_ref, bias_ref, o_ref, qs_scr, s_scr, tmax_scr, m_scr, acc_scr, *,
                n_near):
    t = FLASH_T
    i = pl.program_id(1)
    lo = _lane_lo((t, LANES))
    n_groups = NSA_HEADS // 2
    mb = mb_ref[0]
    for g in range(n_groups):
        qg = q_ref[0, :, g * LANES:(g + 1) * LANES]
        qs_scr[2 * g] = jnp.where(lo, qg, mb)
        qs_scr[2 * g + 1] = jnp.where(lo, mb, qg)
    _flash_init(m_scr, acc_scr)

    def qk_scores(j):
        start = pl.multiple_of(j * t, t)
        k_tile = k_ref[0, pl.ds(start, t), :]
        e_tile = e2_ref[pl.ds(start, t), :]
        k_sel = (jnp.where(lo, k_tile, e_tile), jnp.where(lo, e_tile, k_tile))
        return [functools.partial(lambda k, pos: _dot_nt(k, qs_scr[pos]), k_sel[pos % 2], pos)
                for pos in range(NSA_HEADS)]

    def bias_tile(pos, j):
        return bias_ref[pos, jnp.where(j > i, n_near + 1, jnp.minimum(i - j, n_near))]

    def vt_slab(pos, j):
        kv = pos % 2
        return vt_ref[0, kv * VT_ROWS:(kv + 1) * VT_ROWS, pl.ds(pl.multiple_of(j * t, t), t)]

    _flash_pipeline(i, NSA_HEADS, qk_scores, bias_tile, vt_slab, s_scr, tmax_scr, m_scr, acc_scr)
    _flash_finish(o_ref, n_groups, acc_scr)


def _slc_attention(nsa_q, mask_bias, k2, e2, vt, bias):
    b, s, width = nsa_q.shape
    t = FLASH_T
    n_near = bias.shape[1] - 2
    return pl.pallas_call(
        functools.partial(_slc_kernel, n_near=n_near),
        grid=(b, s // t),
        in_specs=[pl.BlockSpec((1, t, width), lambda i, j: (i, j, 0)),
                  pl.BlockSpec((1, t, LANES), lambda i, j: (i, j, 0)),
                  pl.BlockSpec((1, s, LANES), lambda i, j: (i, 0, 0)),
                  pl.BlockSpec((s, LANES), lambda i, j: (0, 0)),
                  pl.BlockSpec((1, 2 * VT_ROWS, s), lambda i, j: (i, VT_SLC_BLOCK, 0)),
                  pl.BlockSpec(bias.shape, lambda i, j: (0, 0, 0, 0))],
        out_specs=pl.BlockSpec((1, t, width), lambda i, j: (i, j, 0)),
        out_shape=jax.ShapeDtypeStruct((b, s, width), BF16),
        scratch_shapes=[pltpu.VMEM((NSA_HEADS, t, LANES), BF16),
                        pltpu.VMEM((2, NSA_HEADS, t, t), F32),
                        pltpu.VMEM((2, NSA_HEADS, SUBLANES, t), F32),
                        pltpu.VMEM((NSA_HEADS, SUBLANES, t), F32),
                        pltpu.VMEM((NSA_HEADS, VT_ROWS, t), F32)],
        compiler_params=_params(2),
    )(nsa_q, mask_bias, k2, e2, vt, bias)


def _block_onehot(s_len):
    blk = np.arange(s_len)[:, None] // SLC_BLOCK
    lane = np.arange(LANES)[None, :] % HEAD_DIM
    return jnp.asarray((blk == lane).astype(np.float32), dtype=BF16)


def _out_proj_kernel(x_ref, mod_ref, swa_ref, fox_ref, cmp_ref, slc_ref, win_ref, misc_ref, expand_ref, gn_ref,
                     w_ref, post_ref, o_ref):
    n_swa = SWA_HEADS * HEAD_DIM
    n_fox = FOX_HEADS * HEAD_DIM
    n_nsa = NSA_HEADS * HEAD_DIM
    gate = jax.nn.sigmoid(misc_ref[0])
    gate_hi = gate.astype(BF16)
    gate_lo = (gate - gate_hi.astype(F32)).astype(BF16)
    gates = (jnp.dot(gate_hi, expand_ref[...], preferred_element_type=F32)
             + jnp.dot(gate_lo, expand_ref[...], preferred_element_type=F32))
    o_nsa = (gates[:, 0:n_nsa] * cmp_ref[0].astype(F32) + gates[:, n_nsa:2 * n_nsa] * slc_ref[0].astype(F32)
             + gates[:, 2 * n_nsa:3 * n_nsa] * win_ref[0].astype(F32))
    a = _rms(swa_ref[0].astype(F32), gn_ref[:, 0:n_swa]).astype(BF16)
    b = _rms(fox_ref[0].astype(F32), gn_ref[:, n_swa:n_swa + n_fox]).astype(BF16)
    c = _rms(o_nsa, gn_ref[:, n_swa + n_fox:]).astype(BF16)
    y = (jnp.dot(a, w_ref[0:n_swa, :], preferred_element_type=F32)
         + jnp.dot(b, w_ref[n_swa:n_swa + n_fox, :], preferred_element_type=F32)
         + jnp.dot(c, w_ref[n_swa + n_fox:, :], preferred_element_type=F32))
    o_ref[0] = x_ref[0] + mod_ref[0, 2:3, :] * _rms(y, post_ref[...])


def _gate_expansion():
    expand = np.zeros((LANES, 3 * NSA_HEADS * HEAD_DIM), np.float32)
    for branch in range(3):
        for p in range(NSA_HEADS):
            col = (branch * NSA_HEADS + p) * HEAD_DIM
            expand[GATE_LANE + 8 * branch + p, col:col + HEAD_DIM] = 1.0
    return jnp.asarray(expand, dtype=BF16)


def _out_proj(x, mod, o_swa, o_fox, o_cmp, o_slc, o_win, misc, gn, w, post):
    b, s, d = x.shape
    expand = _gate_expansion()

    def rows(width):
        return pl.BlockSpec((1, ROW_TILE, width), lambda i, j: (i, j, 0))

    return pl.pallas_call(
        _out_proj_kernel,
        grid=(b, s // ROW_TILE),
        in_specs=[rows(d),
                  pl.BlockSpec((1, ADA_CHUNKS, d), lambda i, j: (i, 0, 0)),
                  rows(o_swa.shape[2]), rows(o_fox.shape[2]), rows(o_cmp.shape[2]), rows(o_slc.shape[2]),
                  rows(o_win.shape[2]), rows(LANES),
                  pl.BlockSpec(expand.shape, lambda i, j: (0, 0)),
                  pl.BlockSpec((1, d), lambda i, j: (0, 0)),
                  pl.BlockSpec((d, d), lambda i, j: (0, 0)),
                  pl.BlockSpec((1, d), lambda i, j: (0, 0))],
        out_specs=rows(d),
        out_shape=jax.ShapeDtypeStruct((b, s, d), F32),
        compiler_params=_params(2),
    )(x, mod, o_swa, o_fox, o_cmp, o_slc, o_win, misc, expand, gn, w, post)


def _ffn_kernel(x_ref, mod_ref, pre_ref, wg_ref, wu_ref, wd_ref, post_ref, o_ref):
    x = x_ref[0]
    h = (_rms(x, pre_ref[...]) * (1.0 + mod_ref[0, 4:5, :]) + mod_ref[0, 3:4, :]).astype(BF16)
    y = jnp.zeros(x.shape, F32)
    for c in range(wg_ref.shape[0]):
        gate = jnp.dot(h, wg_ref[c], preferred_element_type=F32)
        up = jnp.dot(h, wu_ref[c], preferred_element_type=F32)
        act = (gate * jax.nn.sigmoid(gate) * up).astype(BF16)
        y = y + jnp.dot(act, wd_ref[c], preferred_element_type=F32)
    o_ref[0] = x + mod_ref[0, 5:6, :] * _rms(y, post_ref[...])


def _ffn(x, mod, pre, wg, wu, wd, post):
    b, s, d = x.shape
    n_chunks = wg.shape[0]
    rows = pl.BlockSpec((1, ROW_TILE, d), lambda i, j: (i, j, 0))
    vec = pl.BlockSpec((1, d), lambda i, j: (0, 0))
    return pl.pallas_call(
        _ffn_kernel,
        grid=(b, s // ROW_TILE),
        in_specs=[rows,
                  pl.BlockSpec((1, ADA_CHUNKS, d), lambda i, j: (i, 0, 0)),
                  vec,
                  pl.BlockSpec((n_chunks, d, FFN_CHUNK), lambda i, j: (0, 0, 0)),
                  pl.BlockSpec((n_chunks, d, FFN_CHUNK), lambda i, j: (0, 0, 0)),
                  pl.BlockSpec((n_chunks, FFN_CHUNK, d), lambda i, j: (0, 0, 0)),
                  vec],
        out_specs=rows,
        out_shape=jax.ShapeDtypeStruct((b, s, d), F32),
        compiler_params=_params(2),
    )(x, mod, pre, wg, wu, wd, post)


def _forget_lanes():
    lanes, heads = [], []
    for h in range(FOX_HEADS):
        base = (h // 2) * LANES + (HEAD_DIM if h % 2 == 0 else 0)
        for j in range(KEY_BIAS_TERMS):
            lanes.append(base + j)
            heads.append(h)
    return np.array(lanes), np.array(heads)


def _in_proj_layout():
    d = HEAD_DIM
    o_qa, o_ka, o_va, o_qb, o_kb, o_vb, o_fb, o_qc = 0, 256, 384, 512, 768, 1024, 1280, 1284
    o_kc, o_vc, o_ksl, o_vsl, o_kw, o_vw, o_gc = 1796, 1924, 2052, 2180, 2308, 2436, 2564
    scale = LOG2E / math.sqrt(d)

    def head_cols(base, heads):
        return np.concatenate([np.arange(base + h * d, base + (h + 1) * d) for h in heads])

    def span(base, width):
        return np.arange(base, base + width)

    cols = [head_cols(o_qa, SWA_POS), span(o_ka, 128),
            span(o_qb, 256), span(o_kb, 256),
            head_cols(o_qc, NSA_POS),
            span(o_kc, 128), span(o_vc, 128),
            span(o_ksl, 128), span(o_kw, 128),
            span(o_vb, 256), span(o_va, 128), span(o_vsl, 128), span(o_vw, 128)]
    scales = [np.full(256, scale), np.ones(128), np.full(256, scale), np.ones(256), np.full(512, scale),
              np.ones(256), np.ones(256), np.ones(640)]
    lanes, heads = _forget_lanes()
    misc_cols = np.zeros(SEG_MISC[1] - SEG_MISC[0], np.int64)
    misc_scale = np.zeros(SEG_MISC[1] - SEG_MISC[0])
    misc_cols[lanes] = o_fb + heads
    misc_scale[lanes] = 1.0
    for branch in range(3):
        for p, h in enumerate(NSA_POS):
            misc_cols[GATE_LANE + 8 * branch + p] = o_gc + h * 3 + branch
            misc_scale[GATE_LANE + 8 * branch + p] = 1.0
    cols.append(misc_cols)
    scales.append(misc_scale)
    return np.concatenate(cols), np.concatenate(scales).astype(np.float32)


def _head_perm(pos):
    return np.concatenate([np.arange(h * HEAD_DIM, (h + 1) * HEAD_DIM) for h in pos])


def kernel(x, c, rel_bias, ada_w, ada_b, attn_pre_norm, attn_post_norm, ffn_pre_norm, ffn_post_norm, w_in,
           forget_bias, swa_sinks, cmp_pos, cmp_w1, cmp_w2, group_norm, w_out, ffn_w_gate, ffn_w_up, ffn_w_down):
    b, s, d = x.shape
    depth = w_in.shape[0]
    hidden = ffn_w_gate.shape[2]
    assert s % (2 * FLASH_T) == 0 and s // SLC_BLOCK <= HEAD_DIM and hidden % FFN_CHUNK == 0

    cols, scales = _in_proj_layout()
    w_all = (w_in[:, :, cols] * scales).astype(BF16)
    lanes, heads = _forget_lanes()
    fbias_all = jnp.zeros((depth, 1, SEG_MISC[1] - SEG_MISC[0]), F32).at[:, 0, lanes].set(
        forget_bias[:, heads].astype(F32))
    swa_perm = _head_perm(SWA_POS)
    nsa_perm = _head_perm(NSA_POS)
    n_swa, n_fox = SWA_HEADS * HEAD_DIM, FOX_HEADS * HEAD_DIM
    mix_perm = np.concatenate([swa_perm, n_swa + np.arange(n_fox), n_swa + n_fox + nsa_perm])
    gn_all = group_norm[:, mix_perm].astype(F32)
    w_out_all = w_out[:, mix_perm, :].astype(BF16)
    n_chunks = hidden // FFN_CHUNK
    wg_all = ffn_w_gate.reshape(depth, d, n_chunks, FFN_CHUNK).transpose(0, 2, 1, 3).astype(BF16)
    wu_all = ffn_w_up.reshape(depth, d, n_chunks, FFN_CHUNK).transpose(0, 2, 1, 3).astype(BF16)
    wd_all = ffn_w_down.reshape(depth, n_chunks, FFN_CHUNK, d).astype(BF16)

    tab_swa = rel_bias[:, np.array(SWA_POS)].astype(F32)
    tab_nsa = rel_bias[:, SWA_HEADS + np.array(NSA_POS)].astype(F32)
    bias_swa = _bias_table(tab_swa, _band_buckets_t(SWA_TILE, SWA_WINDOW))
    bias_win = _bias_table(tab_nsa, _band_buckets_t(WIN_TILE, NSA_WINDOW))
    bias_slc = _bias_table(tab_nsa, _toeplitz_buckets_t(FLASH_T, _near_tiles(FLASH_T)), subtract_last=True)
    n_rows = s // CMP_STRIDE
    bias_cmp = _bias_table(tab_nsa, _cmp_buckets_t(s, n_rows))
    overlap_t = _overlap_t(s)
    e2 = _block_onehot(s)

    mod_all = _adaln(c.astype(F32), ada_w.astype(F32), ada_b.astype(F32)).reshape(depth, b, ADA_CHUNKS, d)

    for layer in range(depth):
        mod = mod_all[layer]
        swa_qk, fox_qk, nsa_q, kc, vc, k2, misc, vt = _in_proj(
            x, mod, attn_pre_norm[layer].reshape(1, d).astype(F32), w_all[layer])
        o_swa = _banded_attention(swa_qk, swa_qk, 2, vt, VT_SWA_BLOCK, bias_swa,
                                  sinks=swa_sinks[layer][np.array(SWA_POS)].astype(F32))
        o_fox = _fox_attention(fox_qk, _fox_key_terms(misc, fbias_all[layer]), vt)
        cmp_kv = _compress(kc, vc, cmp_pos[layer], cmp_w1[layer], cmp_w2[layer])
        o_cmp, mask_bias = _select(nsa_q, cmp_kv, bias_cmp, overlap_t)
        o_slc = _slc_attention(nsa_q, mask_bias, k2, e2, vt, bias_slc)
        o_win = _banded_attention(nsa_q, k2, 1, vt, VT_WIN_BLOCK, bias_win)
        x = _out_proj(x, mod, o_swa, o_fox, o_cmp, o_slc, o_win, misc, gn_all[layer].reshape(1, d),
                      w_out_all[layer], attn_post_norm[layer].reshape(1, d).astype(F32))
        x = _ffn(x, mod, ffn_pre_norm[layer].reshape(1, d).astype(F32), wg_all[layer], wu_all[layer],
                 wd_all[layer], ffn_post_norm[layer].reshape(1, d).astype(F32))
    return x
```

```python
import functools
import math

import numpy as np
import jax
import jax.numpy as jnp
from jax import lax
from jax.experimental import pallas as pl
from jax.experimental.pallas import tpu as pltpu

F32 = jnp.float32
BF16 = jnp.bfloat16
HIGHEST = lax.Precision.HIGHEST

LANES = 128
SUBLANES = 8
VMEM_LIMIT = 56 * 1024 * 1024

HEAD_DIM = 64
SWA_HEADS = 4
SWA_WINDOW = 128
FOX_HEADS = 4
NSA_HEADS = 8
CMP_LEN = 32
CMP_STRIDE = 16
CMP_HIDDEN = 2 * HEAD_DIM
SLC_BLOCK = 64
TOPK = 16
NSA_WINDOW = 512
REL_BUCKETS = 32
REL_MAX_DISTANCE = 1024
ZERO_BUCKET = -2
RMS_EPS = 1e-6
NEG = -1e30
FORCE = 1e30
ADA_CHUNKS = 6
LOG2E = math.log2(math.e)

SWA_POS = (0, 2, 1, 3)
NSA_POS = (0, 4, 1, 5, 2, 6, 3, 7)

SWA_TILE = 256
WIN_TILE = 256
FLASH_T = 256
SEL_TQ = 256
ROW_TILE = 512
FFN_CHUNK = 256
VT_ROWS = HEAD_DIM + 16
KEY_BIAS_TERMS = 3

SEG_SWA = (0, 384)
SEG_FOX = (384, 896)
SEG_NSAQ = (896, 1408)
SEG_KC = (1408, 1536)
SEG_VC = (1536, 1664)
SEG_K2 = (1664, 1920)
SEG_V = (1920, 2560)
SEG_MISC = (2560, 2816)
GATE_LANE = 8
VT_FOX_BLOCK, VT_SWA_BLOCK, VT_SLC_BLOCK, VT_WIN_BLOCK = 0, 2, 3, 4


def _params(n_grid, vmem=VMEM_LIMIT):
    return pltpu.CompilerParams(dimension_semantics=("parallel",) * n_grid, vmem_limit_bytes=vmem)


def _dot_nt(a, b):
    return lax.dot_general(a, b, (((1,), (1,)), ((), ())), preferred_element_type=F32)


def _lane_lo(shape):
    return lax.broadcasted_iota(jnp.int32, shape, len(shape) - 1) < HEAD_DIM


def _adaln_kernel(c_ref, w_ref, b_ref, o_ref):
    c = c_ref[...]
    act = c * jax.nn.sigmoid(c)
    o_ref[0] = jnp.dot(act, w_ref[0], precision=HIGHEST, preferred_element_type=F32) + b_ref[0]


def _adaln(c, ada_w, ada_b):
    depth, d, n = ada_w.shape
    b = c.shape[0]
    return pl.pallas_call(
        _adaln_kernel,
        grid=(depth, n // d),
        in_specs=[pl.BlockSpec((b, d), lambda l, j: (0, 0)),
                  pl.BlockSpec((1, d, d), lambda l, j: (l, 0, j)),
                  pl.BlockSpec((1, 1, d), lambda l, j: (l, 0, j))],
        out_specs=pl.BlockSpec((1, b, d), lambda l, j: (l, 0, j)),
        out_shape=jax.ShapeDtypeStruct((depth, b, n), F32),
        compiler_params=_params(2),
    )(c, ada_w, ada_b.reshape(depth, 1, n))


def _t5_bucket(dist):
    n = jnp.maximum(dist, 0)
    max_exact = REL_BUCKETS // 2
    nf = jnp.maximum(n, 1).astype(jnp.float32)
    large = max_exact + (jnp.log(nf / max_exact) / math.log(REL_MAX_DISTANCE / max_exact)
                         * (REL_BUCKETS - max_exact)).astype(jnp.int32)
    large = jnp.minimum(large, REL_BUCKETS - 1)
    return jnp.where(n < max_exact, n, large)


def _bias_table_kernel(tab_ref, bucket_ref, o_ref, *, subtract_last):
    h = pl.program_id(0)
    bucket = bucket_ref[0]
    off = tab_ref[REL_BUCKETS - 1, h] if subtract_last else 0.0
    acc = jnp.full(bucket.shape, NEG, F32)
    for k in range(REL_BUCKETS):
        acc = jnp.where(bucket == k, (tab_ref[k, h] - off) * LOG2E, acc)
    o_ref[0, 0] = jnp.where(bucket == ZERO_BUCKET, 0.0, acc)


def _bias_table(table, bucket, subtract_last=False):
    n_heads = table.shape[1]
    n, r, c = bucket.shape
    return pl.pallas_call(
        functools.partial(_bias_table_kernel, subtract_last=subtract_last),
        grid=(n_heads, n),
        in_specs=[pl.BlockSpec(memory_space=pltpu.SMEM),
                  pl.BlockSpec((1, r, c), lambda h, i: (i, 0, 0))],
        out_specs=pl.BlockSpec((1, 1, r, c), lambda h, i: (h, i, 0, 0)),
        out_shape=jax.ShapeDtypeStruct((n_heads, n, r, c), F32),
        compiler_params=_params(2),
    )(table, bucket)


def _band_buckets_t(tile, window):
    n_back = -(-(window - 1) // tile)
    t = jnp.arange(n_back + 1)[:, None, None]
    key = jnp.arange(tile)[None, :, None]
    query = jnp.arange(tile)[None, None, :]
    dist = query + (n_back - t) * tile - key
    return jnp.where((dist >= 0) & (dist < window), _t5_bucket(dist), -1).astype(jnp.int32)


def _toeplitz_buckets_t(tile, n_tiles):
    m = jnp.arange(n_tiles)[:, None, None]
    key = jnp.arange(tile)[None, :, None]
    query = jnp.arange(tile)[None, None, :]
    dist = m * tile + query - key
    near = jnp.where(dist >= 0, _t5_bucket(dist), -1).astype(jnp.int32)
    return jnp.concatenate([near, jnp.full((1, tile, tile), ZERO_BUCKET, jnp.int32)])


def _cmp_buckets_t(s_len, n_rows):
    n_c = n_rows - 1
    tile = jnp.arange(s_len // SEL_TQ)[:, None, None]
    n = jnp.arange(n_rows)[None, :, None]
    t = tile * SEL_TQ + jnp.arange(SEL_TQ)[None, None, :]
    dist = t - (n * CMP_STRIDE + CMP_LEN - 1)
    return jnp.where((dist >= 0) & (n < n_c), _t5_bucket(dist), -1).astype(jnp.int32)


def _near_tiles(tile):
    max_exact = REL_BUCKETS // 2
    first_const = math.ceil(max_exact * (REL_MAX_DISTANCE / max_exact) ** ((max_exact - 1) / max_exact)) + 1
    m = 1
    while m * tile - (tile - 1) < first_const:
        m += 1
    return m


def _rms(x, gain):
    return x * lax.rsqrt(jnp.mean(x * x, axis=-1, keepdims=True) + RMS_EPS) * gain


def _in_proj_kernel(x_ref, mod_ref, gain_ref, w_ref, swa_ref, fox_ref, nsaq_ref, kc_ref, vc_ref, k2_ref,
                    misc_ref, vt_ref):
    x = x_ref[0]
    h = _rms(x, gain_ref[...]) * (1.0 + mod_ref[0, 1:2, :]) + mod_ref[0, 0:1, :]
    hb = h.astype(BF16)

    def seg(bounds):
        return jnp.dot(hb, w_ref[:, bounds[0]:bounds[1]], preferred_element_type=F32)

    swa_ref[0] = seg(SEG_SWA).astype(BF16)
    fox_ref[0] = seg(SEG_FOX).astype(BF16)
    nsaq_ref[0] = seg(SEG_NSAQ).astype(BF16)
    kc_ref[0] = seg(SEG_KC).astype(BF16)
    vc_ref[0] = seg(SEG_VC).astype(BF16)
    k2_ref[0] = seg(SEG_K2).astype(BF16)
    misc_ref[0] = seg(SEG_MISC)

    rows = x.shape[0]
    extra_row = lax.broadcasted_iota(jnp.int32, (VT_ROWS - HEAD_DIM, rows), 0)
    extra = jnp.where(extra_row == 0, 1.0, 0.0).astype(BF16)
    values = seg(SEG_V)
    for c in range(values.shape[1] // LANES):
        vt = values[:, c * LANES:(c + 1) * LANES].T.astype(BF16)
        for half in range(2):
            base = (2 * c + half) * VT_ROWS
            vt_ref[0, base:base + HEAD_DIM, :] = vt[half * HEAD_DIM:(half + 1) * HEAD_DIM, :]
            vt_ref[0, base + HEAD_DIM:base + VT_ROWS, :] = extra


def _in_proj(x, mod, gain, w):
    b, s, d = x.shape
    n = w.shape[1]
    widths = [hi - lo for lo, hi in (SEG_SWA, SEG_FOX, SEG_NSAQ, SEG_KC, SEG_VC, SEG_K2, SEG_MISC)]
    dtypes = [BF16] * 6 + [F32]
    vt_rows = (SEG_V[1] - SEG_V[0]) // HEAD_DIM * VT_ROWS
    return pl.pallas_call(
        _in_proj_kernel,
        grid=(b, s // ROW_TILE),
        in_specs=[pl.BlockSpec((1, ROW_TILE, d), lambda i, j: (i, j, 0)),
                  pl.BlockSpec((1, ADA_CHUNKS, d), lambda i, j: (i, 0, 0)),
                  pl.BlockSpec((1, d), lambda i, j: (0, 0)),
                  pl.BlockSpec((d, n), lambda i, j: (0, 0))],
        out_specs=[pl.BlockSpec((1, ROW_TILE, wd), lambda i, j: (i, j, 0)) for wd in widths]
        + [pl.BlockSpec((1, vt_rows, ROW_TILE), lambda i, j: (i, 0, j))],
        out_shape=[jax.ShapeDtypeStruct((b, s, wd), dt) for wd, dt in zip(widths, dtypes)]
        + [jax.ShapeDtypeStruct((b, vt_rows, s), BF16)],
        compiler_params=_params(2),
    )(x, mod, gain, w)


def _banded_kernel(*refs, n_back, n_groups, has_sink, t):
    if has_sink:
        sink_ref, q_ref, k_ref, vt_ref, bias_ref, o_ref = refs
    else:
        q_ref, k_ref, vt_ref, bias_ref, o_ref = refs
    i = pl.program_id(1)
    lo = _lane_lo((t, LANES))
    n_tiles = n_back + 1

    def run(all_valid):
        starts = [pl.multiple_of(jnp.maximum(i - n_back + tt, 0) * t, t) for tt in range(n_tiles)]
        k_tiles = [k_ref[0, pl.ds(start, t), :] for start in starts]

        def scores(g):
            qg = q_ref[0, :, g * LANES:(g + 1) * LANES]
            zero = jnp.zeros_like(qg)
            qms = (jnp.where(lo, qg, zero), jnp.where(lo, zero, qg))
            return [[bias_ref[2 * g + half, tt] + _dot_nt(k_tiles[tt], qms[half]) for tt in range(n_tiles)]
                    for half in range(2)]

        def softmax_pv(g, sts):
            pair = []
            for half in range(2):
                tiles = sts[half]
                if not all_valid:
                    tiles = [jnp.where(i - n_back + tt >= 0, st, NEG) if tt < n_back else st
                             for tt, st in enumerate(tiles)]
                m = None
                for st in tiles:
                    part = st.reshape(t // SUBLANES, SUBLANES, t).max(axis=0)
                    m = part if m is None else jnp.maximum(m, part)
                m = _all_sublanes(m, jnp.maximum)
                if has_sink:
                    sink = sink_ref[2 * g + half] * LOG2E
                    m = jnp.maximum(m, sink)
                acc = None
                for tt, st in enumerate(tiles):
                    p = jnp.exp2((st.reshape(t // SUBLANES, SUBLANES, t) - m[None]).reshape(t, t).astype(BF16))
                    part = jnp.dot(vt_ref[0, half * VT_ROWS:(half + 1) * VT_ROWS, pl.ds(starts[tt], t)], p,
                                   preferred_element_type=F32)
                    acc = part if acc is None else acc + part
                denom = _all_sublanes(acc[HEAD_DIM:HEAD_DIM + SUBLANES, :], jnp.add)
                if has_sink:
                    denom = denom + jnp.exp2(sink - m)
                out = acc[0:HEAD_DIM, :].reshape(HEAD_DIM // SUBLANES, SUBLANES, t) / denom[None]
                pair.append(out.reshape(HEAD_DIM, t))
            o_ref[0, :, g * LANES:(g + 1) * LANES] = jnp.concatenate(pair, axis=0).T.astype(o_ref.dtype)

        pending = scores(0)
        for g in range(n_groups):
            current = pending
            if g + 1 < n_groups:
                pending = scores(g + 1)
            softmax_pv(g, current)

    @pl.when(i >= n_back)
    def _():
        run(True)

    @pl.when(i < n_back)
    def _():
        run(False)


def _banded_attention(q_arr, k_arr, k_blk, vt, vt_blk, bias, sinks=None):
    b, s, _ = q_arr.shape
    n_pos, n_tiles, t = bias.shape[0], bias.shape[1], bias.shape[2]
    width = n_pos * HEAD_DIM
    in_specs = [pl.BlockSpec((1, t, width), lambda i, j: (i, j, 0)),
                pl.BlockSpec((1, s, LANES), lambda i, j: (i, 0, k_blk)),
                pl.BlockSpec((1, 2 * VT_ROWS, s), lambda i, j: (i, vt_blk, 0)),
                pl.BlockSpec(bias.shape, lambda i, j: (0, 0, 0, 0))]
    args = [q_arr, k_arr, vt, bias]
    if sinks is not None:
        in_specs = [pl.BlockSpec(memory_space=pltpu.SMEM)] + in_specs
        args = [sinks] + args
    return pl.pallas_call(
        functools.partial(_banded_kernel, n_back=n_tiles - 1, n_groups=n_pos // 2, has_sink=sinks is not None, t=t),
        grid=(b, s // t),
        in_specs=in_specs,
        out_specs=pl.BlockSpec((1, t, width), lambda i, j: (i, j, 0)),
        out_shape=jax.ShapeDtypeStruct((b, s, width), BF16),
        compiler_params=_params(2),
    )(*args)


def _all_sublanes(x, op):
    for shift in (4, 2, 1):
        x = op(x, pltpu.roll(x, shift, 0))
    return x


def _flash_init(m_scr, acc_scr):
    m_scr[...] = jnp.full(m_scr.shape, NEG, F32)
    acc_scr[...] = jnp.zeros(acc_scr.shape, F32)


def _flash_update(h, st_ref, tile_max, vt_h, m_scr, acc_scr):
    tk, tq = st_ref.shape
    m_prev = m_scr[h]
    m_new = _all_sublanes(jnp.maximum(m_prev, tile_max), jnp.maximum)
    alpha = jnp.exp2(m_prev - m_new)
    p = jnp.exp2((st_ref[...].reshape(tk // SUBLANES, SUBLANES, tq) - m_new[None]).reshape(tk, tq).astype(BF16))
    acc = acc_scr[h].reshape(VT_ROWS // SUBLANES, SUBLANES, tq) * alpha[None]
    acc_scr[h] = acc.reshape(VT_ROWS, tq) + jnp.dot(vt_h, p, preferred_element_type=F32)
    m_scr[h] = m_new


def _flash_finish(o_ref, n_groups, acc_scr):
    tq = acc_scr.shape[2]
    for g in range(n_groups):
        pair = []
        for h in (2 * g, 2 * g + 1):
            denom = _all_sublanes(acc_scr[h, HEAD_DIM:HEAD_DIM + SUBLANES, :], jnp.add)
            out = acc_scr[h, 0:HEAD_DIM, :].reshape(HEAD_DIM // SUBLANES, SUBLANES, tq) / denom[None]
            pair.append(out.reshape(HEAD_DIM, tq))
        o_ref[0, :, g * LANES:(g + 1) * LANES] = jnp.concatenate(pair, axis=0).T.astype(o_ref.dtype)


def _flash_pipeline(i, n_heads, qk_scores, bias_tile, vt_slab, s_scr, tmax_scr, m_scr, acc_scr):
    def qk_head(thunk, h, j, slot):
        st = thunk() + bias_tile(h, j)
        s_scr[slot, h] = st
        tmax_scr[slot, h] = st.reshape(st.shape[0] // SUBLANES, SUBLANES, st.shape[1]).max(axis=0)

    def softmax_head(h, j, slot):
        _flash_update(h, s_scr.at[slot, h], tmax_scr[slot, h], vt_slab(h, j), m_scr, acc_scr)

    def softmax_all(j, slot):
        for h in range(n_heads):
            softmax_head(h, j, slot)

    def stage(j_qk, slot_qk, j_sm, slot_sm):
        thunks = qk_scores(j_qk)
        for h in range(n_heads):
            qk_head(thunks[h], h, j_qk, slot_qk)
            softmax_head(h, j_sm, slot_sm)

    for h, thunk in enumerate(qk_scores(0)):
        qk_head(thunk, h, 0, 0)

    def body(trip, carry):
        j = 2 * trip
        stage(j + 1, 1, j, 0)
        stage(j + 2, 0, j + 1, 1)
        return carry

    lax.fori_loop(0, i // 2, body, 0)
    last = 2 * (i // 2)

    @pl.when(i % 2 == 0)
    def _():
        softmax_all(last, 0)

    @pl.when(i % 2 == 1)
    def _():
        stage(last + 1, 1, last, 0)
        softmax_all(last + 1, 1)


def _fox_aug_kernel(misc_ref, fbias_ref, tri_ref, o_ref):
    s_len, width = misc_ref.shape[1], misc_ref.shape[2]
    term = lax.broadcasted_iota(jnp.int32, (LANES, width), 1) % HEAD_DIM
    carry = jnp.zeros((1, width), F32)
    for c in range(s_len // LANES):
        z = misc_ref[0, c * LANES:(c + 1) * LANES, :] + fbias_ref[...]
        log_f = jnp.minimum(z, 0.0) - jnp.log1p(jnp.exp(-jnp.abs(z)))
        cum = jnp.dot(tri_ref[...], log_f, precision=HIGHEST, preferred_element_type=F32) + carry
        carry = cum[LANES - 1:LANES, :]
        x = cum * (-LOG2E)
        hi = x.astype(BF16).astype(F32)
        rest = x - hi
        mid = rest.astype(BF16).astype(F32)
        low = rest - mid
        out = jnp.where(term == 0, hi, jnp.where(term == 1, mid, jnp.where(term == 2, low, 0.0)))
        o_ref[0, c * LANES:(c + 1) * LANES, :] = out.astype(BF16)


def _fox_key_terms(misc, fbias):
    b, s, width = misc.shape
    tri = jnp.asarray(np.tril(np.ones((LANES, LANES), np.float32)))
    return pl.pallas_call(
        _fox_aug_kernel,
        grid=(b,),
        in_specs=[pl.BlockSpec((1, s, width), lambda i: (i, 0, 0)),
                  pl.BlockSpec((1, width), lambda i: (0, 0)),
                  pl.BlockSpec((LANES, LANES), lambda i: (0, 0))],
        out_specs=pl.BlockSpec((1, s, width), lambda i: (i, 0, 0)),
        out_shape=jax.ShapeDtypeStruct((b, s, width), BF16),
        compiler_params=_params(1),
    )(misc, fbias, tri)


def _fox_kernel(q_ref, k_ref, aug_ref, vt_ref, mask_ref, o_ref, qs_scr, s_scr, tmax_scr, m_scr, acc_scr):
    t = FLASH_T
    i = pl.program_id(1)
    lo = _lane_lo((t, LANES))
    lane = lax.broadcasted_iota(jnp.int32, (t, LANES), 1)
    ones = jnp.where(lane % HEAD_DIM < KEY_BIAS_TERMS, 1.0, 0.0).astype(BF16)
    n_groups = FOX_HEADS // 2
    for g in range(n_groups):
        qg = q_ref[0, :, g * LANES:(g + 1) * LANES]
        qs_scr[2 * g] = jnp.where(lo, qg, ones)
        qs_scr[2 * g + 1] = jnp.where(lo, ones, qg)
    _flash_init(m_scr, acc_scr)

    def qk_scores(j):
        start = pl.multiple_of(j * t, t)
        scores = []
        for g in range(n_groups):
            k_tile = k_ref[0, pl.ds(start, t), g * LANES:(g + 1) * LANES]
            a_tile = aug_ref[0, pl.ds(start, t), g * LANES:(g + 1) * LANES]
            k_sel = (jnp.where(lo, k_tile, a_tile), jnp.where(lo, a_tile, k_tile))
            for half in range(2):
                scores.append(functools.partial(lambda k, h: _dot_nt(k, qs_scr[h]), k_sel[half], 2 * g + half))
        return scores

    def bias_tile(h, j):
        return mask_ref[jnp.minimum(i - j, 1)]

    def vt_slab(h, j):
        return vt_ref[0, h * VT_ROWS:(h + 1) * VT_ROWS, pl.ds(pl.multiple_of(j * t, t), t)]

    _flash_pipeline(i, FOX_HEADS, qk_scores, bias_tile, vt_slab, s_scr, tmax_scr, m_scr, acc_scr)
    _flash_finish(o_ref, n_groups, acc_scr)


def _fox_attention(fox_qk, key_terms, vt):
    b, s, _ = fox_qk.shape
    width = FOX_HEADS * HEAD_DIM
    t = FLASH_T
    idx = np.arange(t)
    diag = np.where(idx[:, None] <= idx[None, :], 0.0, NEG)
    masks = jnp.asarray(np.stack([diag, np.zeros((t, t))]).astype(np.float32))
    return pl.pallas_call(
        _fox_kernel,
        grid=(b, s // t),
        in_specs=[pl.BlockSpec((1, t, width), lambda i, j: (i, j, 0)),
                  pl.BlockSpec((1, s, width), lambda i, j: (i, 0, 1)),
                  pl.BlockSpec((1, s, width), lambda i, j: (i, 0, 0)),
                  pl.BlockSpec((1, FOX_HEADS * VT_ROWS, s), lambda i, j: (i, VT_FOX_BLOCK, 0)),
                  pl.BlockSpec(masks.shape, lambda i, j: (0, 0, 0))],
        out_specs=pl.BlockSpec((1, t, width), lambda i, j: (i, j, 0)),
        out_shape=jax.ShapeDtypeStruct((b, s, width), BF16),
        scratch_shapes=[pltpu.VMEM((FOX_HEADS, t, LANES), BF16),
                        pltpu.VMEM((2, FOX_HEADS, t, t), F32),
                        pltpu.VMEM((2, FOX_HEADS, SUBLANES, t), F32),
                        pltpu.VMEM((FOX_HEADS, SUBLANES, t), F32),
                        pltpu.VMEM((FOX_HEADS, VT_ROWS, t), F32)],
        compiler_params=_params(2),
    )(fox_qk, fox_qk, key_terms, vt, masks)


def _compress_kernel(x_ref, pe_ref, w1a_ref, w1b_ref, w2_ref, o_ref):
    x = x_ref[0].astype(F32)
    n_rows = x.shape[0]

    def mm(a, w):
        return jnp.dot(a, w, precision=HIGHEST, preferred_element_type=F32)

    first = mm(x, w1a_ref[0])
    second = mm(x, w1b_ref[0])
    pe_term = (mm(pe_ref[0, 0], w1a_ref[0]) + mm(pe_ref[0, 1], w1b_ref[0]))[0:1, :]
    pre = first + pltpu.roll(second, n_rows - 1, 0) + pe_term
    hid = 0.5 * pre * (1.0 + jnp.tanh(math.sqrt(2.0 / math.pi) * (pre + 0.044715 * (pre * pre * pre))))
    o_ref[0, 0] = mm(hid, w2_ref[0])


def _compress(kc, vc, cmp_pos, cmp_w1, cmp_w2):
    b, s, _ = kc.shape
    n_rows = s // CMP_STRIDE
    half = CMP_LEN // 2
    feat = CMP_STRIDE * LANES
    x = jnp.stack([kc.reshape(b, n_rows, feat), vc.reshape(b, n_rows, feat)])
    eye = jnp.eye(2, dtype=F32)
    w1 = cmp_w1.astype(F32).reshape(2, CMP_LEN, HEAD_DIM, CMP_HIDDEN)
    w1a = jnp.einsum('wldj,hg->wlhdgj', w1[:, :half], eye).reshape(2, feat, 2 * CMP_HIDDEN)
    w1b = jnp.einsum('wldj,hg->wlhdgj', w1[:, half:], eye).reshape(2, feat, 2 * CMP_HIDDEN)
    w2 = jnp.einsum('wjd,hg->whjgd', cmp_w2.astype(F32), eye).reshape(2, 2 * CMP_HIDDEN, LANES)
    w2 = jnp.concatenate([w2, jnp.roll(w2, HEAD_DIM, axis=2)], axis=2)
    pe = jnp.broadcast_to(cmp_pos.astype(F32).reshape(2, 2, half, 1, HEAD_DIM), (2, 2, half, 2, HEAD_DIM))
    pe = jnp.broadcast_to(pe.reshape(2, 2, 1, feat), (2, 2, 8, feat))
    return pl.pallas_call(
        _compress_kernel,
        grid=(2, b),
        in_specs=[pl.BlockSpec((None, 1, n_rows, feat), lambda w, i: (w, i, 0, 0)),
                  pl.BlockSpec((1, 2, 8, feat), lambda w, i: (w, 0, 0, 0)),
                  pl.BlockSpec((1, feat, 2 * CMP_HIDDEN), lambda w, i: (w, 0, 0)),
                  pl.BlockSpec((1, feat, 2 * CMP_HIDDEN), lambda w, i: (w, 0, 0)),
                  pl.BlockSpec((1, 2 * CMP_HIDDEN, 2 * LANES), lambda w, i: (w, 0, 0))],
        out_specs=pl.BlockSpec((1, 1, n_rows, 2 * LANES), lambda w, i: (w, i, 0, 0)),
        out_shape=jax.ShapeDtypeStruct((2, b, n_rows, 2 * LANES), F32),
        compiler_params=_params(2),
    )(x, pe, w1a, w1b, w2)


def _select_kernel(q_ref, kc_ref, vct_ref, bias_ref, ovl_ref, o_ref, mb_ref, count_scr):
    tq = SEL_TQ
    i = pl.program_id(0)
    lo = _lane_lo((tq, LANES))
    n_rows = kc_ref.shape[2]
    lo_k = _lane_lo((n_rows, LANES))
    n_grp = n_rows // SUBLANES

    k_own = kc_ref[0, 0, :, 0:LANES]
    k_swapped = kc_ref[0, 0, :, LANES:2 * LANES]
    hi = k_own.astype(BF16)
    low = (k_swapped - k_swapped.astype(BF16).astype(F32)).astype(BF16)
    k_sel = (jnp.where(lo_k, hi, low), jnp.where(lo_k, low, hi))

    def scores(g):
        qg = q_ref[0, :, g * LANES:(g + 1) * LANES]
        swapped = pltpu.roll(qg.astype(F32), HEAD_DIM, 1).astype(BF16)
        q_dup = (jnp.where(lo, qg, swapped), jnp.where(lo, swapped, qg))
        return [bias_ref[2 * g + half, 0] + _dot_nt(k_sel[half], q_dup[half]) for half in range(2)]

    query = i * tq + lax.broadcasted_iota(jnp.int32, (SUBLANES, tq), 1)
    has_keys = query >= CMP_LEN - 1
    p_sum = [None, None]

    def softmax_pv(g, sts):
        pair = []
        for half in range(2):
            s3 = sts[half].reshape(n_grp, SUBLANES, tq)
            m = _all_sublanes(s3.max(axis=0), jnp.maximum)
            e = jnp.exp2(s3 - m[None])
            inv = jnp.where(has_keys, 1.0 / _all_sublanes(e.sum(axis=0), jnp.add), 0.0)
            p = e * inv[None]
            p_sum[half] = p if p_sum[half] is None else p_sum[half] + p
            pair.append(jnp.dot(vct_ref[0, half * HEAD_DIM:(half + 1) * HEAD_DIM, :],
                                p.reshape(n_rows, tq).astype(BF16), preferred_element_type=F32))
        o_ref[0, :, g * LANES:(g + 1) * LANES] = jnp.concatenate(pair, axis=0).T.astype(o_ref.dtype)

    n_groups = NSA_HEADS // 2
    pending = scores(0)
    for g in range(n_groups):
        current = pending
        if g + 1 < n_groups:
            pending = scores(g + 1)
        softmax_pv(g, current)

    n_blk = ovl_ref.shape[0]
    blk_grp = n_blk // SUBLANES
    sub = lax.broadcasted_iota(jnp.int32, (SUBLANES, tq), 0)
    q_blk = query // SLC_BLOCK
    masks = []
    for half in (1, 0):
        imp = jnp.dot(ovl_ref[...], p_sum[half].reshape(n_rows, tq), precision=HIGHEST,
                      preferred_element_type=F32)
        rows = []
        for r in range(blk_grp):
            blk = sub + r * SUBLANES
            forced = jnp.where(blk == 0, 1, 0) + jnp.where(blk == q_blk, 1, 0) + jnp.where(blk == q_blk - 1, 1, 0)
            rows.append(jnp.where(forced > 0, FORCE,
                                  jnp.where(blk > q_blk, NEG, imp[r * SUBLANES:(r + 1) * SUBLANES, :])))
        count_scr[...] = jnp.zeros(count_scr.shape, jnp.int32)
        for r_other in range(blk_grp):
            @pl.when(r_other * SUBLANES * SLC_BLOCK < (i + 1) * tq)
            def _(r_other=r_other, rows=rows):
                counts = [None] * blk_grp
                for s_other in range(SUBLANES):
                    row = jnp.broadcast_to(rows[r_other][s_other:s_other + 1, :], (SUBLANES, tq))
                    for r in range(blk_grp):
                        if r > r_other:
                            beats = jnp.where(row >= rows[r], 1, 0)
                        elif r < r_other:
                            beats = jnp.where(row > rows[r], 1, 0)
                        else:
                            beats = jnp.where(sub > s_other, jnp.where(row >= rows[r], 1, 0),
                                              jnp.where(row > rows[r], 1, 0))
                        counts[r] = beats if counts[r] is None else counts[r] + beats
                for r in range(blk_grp):
                    count_scr[r] = count_scr[r] + counts[r]
        masks.extend(jnp.where(count_scr[r] < TOPK, 0.0, NEG) for r in range(blk_grp))
    for c in range(tq // LANES):
        mb_ref[0, c * LANES:(c + 1) * LANES, :] = jnp.concatenate(
            [mk[:, c * LANES:(c + 1) * LANES] for mk in masks], axis=0).T.astype(BF16)


def _select(nsa_q, cmp_kv, bias_c, overlap_t):
    b, s, width = nsa_q.shape
    n_rows = cmp_kv.shape[2]
    n_blk = overlap_t.shape[0]
    assert n_blk == HEAD_DIM
    tq = SEL_TQ
    vct = cmp_kv[1, :, :, 0:LANES].transpose(0, 2, 1).astype(BF16)
    return pl.pallas_call(
        _select_kernel,
        grid=(s // tq, b),
        in_specs=[pl.BlockSpec((1, tq, width), lambda j, i: (i, j, 0)),
                  pl.BlockSpec((1, 1, n_rows, 2 * LANES), lambda j, i: (0, i, 0, 0)),
                  pl.BlockSpec((1, LANES, n_rows), lambda j, i: (i, 0, 0)),
                  pl.BlockSpec((NSA_HEADS, 1, n_rows, tq), lambda j, i: (0, j, 0, 0)),
                  pl.BlockSpec((n_blk, n_rows), lambda j, i: (0, 0))],
        out_specs=[pl.BlockSpec((1, tq, width), lambda j, i: (i, j, 0)),
                   pl.BlockSpec((1, tq, LANES), lambda j, i: (i, j, 0))],
        out_shape=[jax.ShapeDtypeStruct((b, s, width), BF16),
                   jax.ShapeDtypeStruct((b, s, LANES), BF16)],
        scratch_shapes=[pltpu.VMEM((n_blk // SUBLANES, SUBLANES, tq), jnp.int32)],
        compiler_params=_params(2),
    )(nsa_q, cmp_kv, vct, bias_c, overlap_t)


def _overlap_t(s_len):
    n_rows = s_len // CMP_STRIDE
    c_start = np.arange(n_rows)[None, :] * CMP_STRIDE
    s_start = np.arange(HEAD_DIM)[:, None] * SLC_BLOCK
    ovl = np.clip(np.minimum(c_start + CMP_LEN, s_start + SLC_BLOCK) - np.maximum(c_start, s_start), 0, None)
    ovl = ovl.astype(np.float32) / CMP_LEN
    ovl[:, n_rows - 1] = 0.0
    ovl[s_len // SLC_BLOCK:, :] = 0.0
    return jnp.asarray(ovl)


def _slc_kernel(q_ref, mb_ref, k_ref, e2_ref, vt_ref, bias_ref, o_ref, qs_scr, s_scr, tmax_scr, m_scr, acc_scr, *,
                n_near):
    t = FLASH_T
    i = pl.program_id(1)
    lo = _lane_lo((t, LANES))
    n_groups = NSA_HEADS // 2
    mb = mb_ref[0]
    for g in range(n_groups):
        qg = q_ref[0, :, g * LANES:(g + 1) * LANES]
        qs_scr[2 * g] = jnp.where(lo, qg, mb)
        qs_scr[2 * g + 1] = jnp.where(lo, mb, qg)
    _flash_init(m_scr, acc_scr)

    def qk_scores(j):
        start = pl.multiple_of(j * t, t)
        k_tile = k_ref[0, pl.ds(start, t), :]
        e_tile = e2_ref[pl.ds(start, t), :]
        k_sel = (jnp.where(lo, k_tile, e_tile), jnp.where(lo, e_tile, k_tile))
        return [functools.partial(lambda k, pos: _dot_nt(k, qs_scr[pos]), k_sel[pos % 2], pos)
                for pos in range(NSA_HEADS)]

    def bias_tile(pos, j):
        return bias_ref[pos, jnp.minimum(i - j, n_near)]

    def vt_slab(pos, j):
        kv = pos % 2
        return vt_ref[0, kv * VT_ROWS:(kv + 1) * VT_ROWS, pl.ds(pl.multiple_of(j * t, t), t)]

    _flash_pipeline(i, NSA_HEADS, qk_scores, bias_tile, vt_slab, s_scr, tmax_scr, m_scr, acc_scr)
    _flash_finish(o_ref, n_groups, acc_scr)


def _slc_attention(nsa_q, mask_bias, k2, e2, vt, bias):
    b, s, width = nsa_q.shape
    t = FLASH_T
    n_near = bias.shape[1] - 1
    return pl.pallas_call(
        functools.partial(_slc_kernel, n_near=n_near),
        grid=(b, s // t),
        in_specs=[pl.BlockSpec((1, t, width), lambda i, j: (i, j, 0)),
                  pl.BlockSpec((1, t, LANES), lambda i, j: (i, j, 0)),
                  pl.BlockSpec((1, s, LANES), lambda i, j: (i, 0, 0)),
                  pl.BlockSpec((s, LANES), lambda i, j: (0, 0)),
                  pl.BlockSpec((1, 2 * VT_ROWS, s), lambda i, j: (i, VT_SLC_BLOCK, 0)),
                  pl.BlockSpec(bias.shape, lambda i, j: (0, 0, 0, 0))],
        out_specs=pl.BlockSpec((1, t, width), lambda i, j: (i, j, 0)),
        out_shape=jax.ShapeDtypeStruct((b, s, width), BF16),
        scratch_shapes=[pltpu.VMEM((NSA_HEADS, t, LANES), BF16),
                        pltpu.VMEM((2, NSA_HEADS, t, t), F32),
                        pltpu.VMEM((2, NSA_HEADS, SUBLANES, t), F32),
                        pltpu.VMEM((NSA_HEADS, SUBLANES, t), F32),
                        pltpu.VMEM((NSA_HEADS, VT_ROWS, t), F32)],
        compiler_params=_params(2),
    )(nsa_q, mask_bias, k2, e2, vt, bias)


def _block_onehot(s_len):
    blk = np.arange(s_len)[:, None] // SLC_BLOCK
    lane = np.arange(LANES)[None, :] % HEAD_DIM
    return jnp.asarray((blk == lane).astype(np.float32), dtype=BF16)


def _out_proj_kernel(x_ref, mod_ref, swa_ref, fox_ref, cmp_ref, slc_ref, win_ref, misc_ref, expand_ref, gn_ref,
                     w_ref, post_ref, o_ref):
    n_swa = SWA_HEADS * HEAD_DIM
    n_fox = FOX_HEADS * HEAD_DIM
    n_nsa = NSA_HEADS * HEAD_DIM
    gate = jax.nn.sigmoid(misc_ref[0])
    gate_hi = gate.astype(BF16)
    gate_lo = (gate - gate_hi.astype(F32)).astype(BF16)
    gates = (jnp.dot(gate_hi, expand_ref[...], preferred_element_type=F32)
             + jnp.dot(gate_lo, expand_ref[...], preferred_element_type=F32))
    o_nsa = (gates[:, 0:n_nsa] * cmp_ref[0].astype(F32) + gates[:, n_nsa:2 * n_nsa] * slc_ref[0].astype(F32)
             + gates[:, 2 * n_nsa:3 * n_nsa] * win_ref[0].astype(F32))
    a = _rms(swa_ref[0].astype(F32), gn_ref[:, 0:n_swa]).astype(BF16)
    b = _rms(fox_ref[0].astype(F32), gn_ref[:, n_swa:n_swa + n_fox]).astype(BF16)
    c = _rms(o_nsa, gn_ref[:, n_swa + n_fox:]).astype(BF16)
    y = (jnp.dot(a, w_ref[0:n_swa, :], preferred_element_type=F32)
         + jnp.dot(b, w_ref[n_swa:n_swa + n_fox, :], preferred_element_type=F32)
         + jnp.dot(c, w_ref[n_swa + n_fox:, :], preferred_element_type=F32))
    o_ref[0] = x_ref[0] + mod_ref[0, 2:3, :] * _rms(y, post_ref[...])


def _gate_expansion():
    expand = np.zeros((LANES, 3 * NSA_HEADS * HEAD_DIM), np.float32)
    for branch in range(3):
        for p in range(NSA_HEADS):
            col = (branch * NSA_HEADS + p) * HEAD_DIM
            expand[GATE_LANE + 8 * branch + p, col:col + HEAD_DIM] = 1.0
    return jnp.asarray(expand, dtype=BF16)


def _out_proj(x, mod, o_swa, o_fox, o_cmp, o_slc, o_win, misc, gn, w, post):
    b, s, d = x.shape
    expand = _gate_expansion()

    def rows(width):
        return pl.BlockSpec((1, ROW_TILE, width), lambda i, j: (i, j, 0))

    return pl.pallas_call(
        _out_proj_kernel,
        grid=(b, s // ROW_TILE),
        in_specs=[rows(d),
                  pl.BlockSpec((1, ADA_CHUNKS, d), lambda i, j: (i, 0, 0)),
                  rows(o_swa.shape[2]), rows(o_fox.shape[2]), rows(o_cmp.shape[2]), rows(o_slc.shape[2]),
                  rows(o_win.shape[2]), rows(LANES),
                  pl.BlockSpec(expand.shape, lambda i, j: (0, 0)),
                  pl.BlockSpec((1, d), lambda i, j: (0, 0)),
                  pl.BlockSpec((d, d), lambda i, j: (0, 0)),
                  pl.BlockSpec((1, d), lambda i, j: (0, 0))],
        out_specs=rows(d),
        out_shape=jax.ShapeDtypeStruct((b, s, d), F32),
        compiler_params=_params(2),
    )(x, mod, o_swa, o_fox, o_cmp, o_slc, o_win, misc, expand, gn, w, post)


def _ffn_kernel(x_ref, mod_ref, pre_ref, wg_ref, wu_ref, wd_ref, post_ref, o_ref):
    x = x_ref[0]
    h = (_rms(x, pre_ref[...]) * (1.0 + mod_ref[0, 4:5, :]) + mod_ref[0, 3:4, :]).astype(BF16)
    y = jnp.zeros(x.shape, F32)
    for c in range(wg_ref.shape[0]):
        gate = jnp.dot(h, wg_ref[c], preferred_element_type=F32)
        up = jnp.dot(h, wu_ref[c], preferred_element_type=F32)
        act = (gate * jax.nn.sigmoid(gate) * up).astype(BF16)
        y = y + jnp.dot(act, wd_ref[c], preferred_element_type=F32)
    o_ref[0] = x + mod_ref[0, 5:6, :] * _rms(y, post_ref[...])


def _ffn(x, mod, pre, wg, wu, wd, post):
    b, s, d = x.shape
    n_chunks = wg.shape[0]
    rows = pl.BlockSpec((1, ROW_TILE, d), lambda i, j: (i, j, 0))
    vec = pl.BlockSpec((1, d), lambda i, j: (0, 0))
    return pl.pallas_call(
        _ffn_kernel,
        grid=(b, s // ROW_TILE),
        in_specs=[rows,
                  pl.BlockSpec((1, ADA_CHUNKS, d), lambda i, j: (i, 0, 0)),
                  vec,
                  pl.BlockSpec((n_chunks, d, FFN_CHUNK), lambda i, j: (0, 0, 0)),
                  pl.BlockSpec((n_chunks, d, FFN_CHUNK), lambda i, j: (0, 0, 0)),
                  pl.BlockSpec((n_chunks, FFN_CHUNK, d), lambda i, j: (0, 0, 0)),
                  vec],
        out_specs=rows,
        out_shape=jax.ShapeDtypeStruct((b, s, d), F32),
        compiler_params=_params(2),
    )(x, mod, pre, wg, wu, wd, post)


def _forget_lanes():
    lanes, heads = [], []
    for h in range(FOX_HEADS):
        base = (h // 2) * LANES + (HEAD_DIM if h % 2 == 0 else 0)
        for j in range(KEY_BIAS_TERMS):
            lanes.append(base + j)
            heads.append(h)
    return np.array(lanes), np.array(heads)


def _in_proj_layout():
    d = HEAD_DIM
    o_qa, o_ka, o_va, o_qb, o_kb, o_vb, o_fb, o_qc = 0, 256, 384, 512, 768, 1024, 1280, 1284
    o_kc, o_vc, o_ksl, o_vsl, o_kw, o_vw, o_gc = 1796, 1924, 2052, 2180, 2308, 2436, 2564
    scale = LOG2E / math.sqrt(d)

    def head_cols(base, heads):
        return np.concatenate([np.arange(base + h * d, base + (h + 1) * d) for h in heads])

    def span(base, width):
        return np.arange(base, base + width)

    cols = [head_cols(o_qa, SWA_POS), span(o_ka, 128),
            span(o_qb, 256), span(o_kb, 256),
            head_cols(o_qc, NSA_POS),
            span(o_kc, 128), span(o_vc, 128),
            span(o_ksl, 128), span(o_kw, 128),
            span(o_vb, 256), span(o_va, 128), span(o_vsl, 128), span(o_vw, 128)]
    scales = [np.full(256, scale), np.ones(128), np.full(256, scale), np.ones(256), np.full(512, scale),
              np.ones(256), np.ones(256), np.ones(640)]
    lanes, heads = _forget_lanes()
    misc_cols = np.zeros(SEG_MISC[1] - SEG_MISC[0], np.int64)
    misc_scale = np.zeros(SEG_MISC[1] - SEG_MISC[0])
    misc_cols[lanes] = o_fb + heads
    misc_scale[lanes] = 1.0
    for branch in range(3):
        for p, h in enumerate(NSA_POS):
            misc_cols[GATE_LANE + 8 * branch + p] = o_gc + h * 3 + branch
            misc_scale[GATE_LANE + 8 * branch + p] = 1.0
    cols.append(misc_cols)
    scales.append(misc_scale)
    return np.concatenate(cols), np.concatenate(scales).astype(np.float32)


def _head_perm(pos):
    return np.concatenate([np.arange(h * HEAD_DIM, (h + 1) * HEAD_DIM) for h in pos])


def kernel(x, c, rel_bias, ada_w, ada_b, attn_pre_norm, attn_post_norm, ffn_pre_norm, ffn_post_norm, w_in,
           forget_bias, swa_sinks, cmp_pos, cmp_w1, cmp_w2, group_norm, w_out, ffn_w_gate, ffn_w_up, ffn_w_down):
    b, s, d = x.shape
    depth = w_in.shape[0]
    hidden = ffn_w_gate.shape[2]
    assert s % (2 * FLASH_T) == 0 and s // SLC_BLOCK <= HEAD_DIM and hidden % FFN_CHUNK == 0

    cols, scales = _in_proj_layout()
    w_all = (w_in[:, :, cols] * scales).astype(BF16)
    lanes, heads = _forget_lanes()
    fbias_all = jnp.zeros((depth, 1, SEG_MISC[1] - SEG_MISC[0]), F32).at[:, 0, lanes].set(
        forget_bias[:, heads].astype(F32))
    swa_perm = _head_perm(SWA_POS)
    nsa_perm = _head_perm(NSA_POS)
    n_swa, n_fox = SWA_HEADS * HEAD_DIM, FOX_HEADS * HEAD_DIM
    mix_perm = np.concatenate([swa_perm, n_swa + np.arange(n_fox), n_swa + n_fox + nsa_perm])
    gn_all = group_norm[:, mix_perm].astype(F32)
    w_out_all = w_out[:, mix_perm, :].astype(BF16)
    n_chunks = hidden // FFN_CHUNK
    wg_all = ffn_w_gate.reshape(depth, d, n_chunks, FFN_CHUNK).transpose(0, 2, 1, 3).astype(BF16)
    wu_all = ffn_w_up.reshape(depth, d, n_chunks, FFN_CHUNK).transpose(0, 2, 1, 3).astype(BF16)
    wd_all = ffn_w_down.reshape(depth, n_chunks, FFN_CHUNK, d).astype(BF16)

    tab_swa = rel_bias[:, np.array(SWA_POS)].astype(F32)
    tab_nsa = rel_bias[:, SWA_HEADS + np.array(NSA_POS)].astype(F32)
    bias_swa = _bias_table(tab_swa, _band_buckets_t(SWA_TILE, SWA_WINDOW))
    bias_win = _bias_table(tab_nsa, _band_buckets_t(WIN_TILE, NSA_WINDOW))
    bias_slc = _bias_table(tab_nsa, _toeplitz_buckets_t(FLASH_T, _near_tiles(FLASH_T)), subtract_last=True)
    n_rows = s // CMP_STRIDE
    bias_cmp = _bias_table(tab_nsa, _cmp_buckets_t(s, n_rows))
    overlap_t = _overlap_t(s)
    e2 = _block_onehot(s)

    mod_all = _adaln(c.astype(F32), ada_w.astype(F32), ada_b.astype(F32)).reshape(depth, b, ADA_CHUNKS, d)

    for layer in range(depth):
        mod = mod_all[layer]
        swa_qk, fox_qk, nsa_q, kc, vc, k2, misc, vt = _in_proj(
            x, mod, attn_pre_norm[layer].reshape(1, d).astype(F32), w_all[layer])
        o_swa = _banded_attention(swa_qk, swa_qk, 2, vt, VT_SWA_BLOCK, bias_swa,
                                  sinks=swa_sinks[layer][np.array(SWA_POS)].astype(F32))
        o_fox = _fox_attention(fox_qk, _fox_key_terms(misc, fbias_all[layer]), vt)
        cmp_kv = _compress(kc, vc, cmp_pos[layer], cmp_w1[layer], cmp_w2[layer])
        o_cmp, mask_bias = _select(nsa_q, cmp_kv, bias_cmp, overlap_t)
        o_slc = _slc_attention(nsa_q, mask_bias, k2, e2, vt, bias_slc)
        o_win = _banded_attention(nsa_q, k2, 1, vt, VT_WIN_BLOCK, bias_win)
        x = _out_proj(x, mod, o_swa, o_fox, o_cmp, o_slc, o_win, misc, gn_all[layer].reshape(1, d),
                      w_out_all[layer], attn_post_norm[layer].reshape(1, d).astype(F32))
        x = _ffn(x, mod, ffn_pre_norm[layer].reshape(1, d).astype(F32), wg_all[layer], wu_all[layer],
                 wd_all[layer], ffn_post_norm[layer].reshape(1, d).astype(F32))
    return x
```

```python
import functools
import math

import numpy as np
import jax
import jax.numpy as jnp
from jax import lax
from jax.experimental import pallas as pl
from jax.experimental.pallas import tpu as pltpu

F32 = jnp.float32
BF16 = jnp.bfloat16
HIGHEST = lax.Precision.HIGHEST

LANES = 128
SUBLANES = 8
VMEM_LIMIT = 56 * 1024 * 1024

HEAD_DIM = 64
SWA_HEADS = 4
SWA_WINDOW = 128
FOX_HEADS = 4
NSA_HEADS = 8
CMP_LEN = 32
CMP_STRIDE = 16
CMP_HIDDEN = 2 * HEAD_DIM
SLC_BLOCK = 64
TOPK = 16
NSA_WINDOW = 512
REL_BUCKETS = 32
REL_MAX_DISTANCE = 1024
ZERO_BUCKET = -2
RMS_EPS = 1e-6
NEG = -1e30
FORCE = 1e30
ADA_CHUNKS = 6
LOG2E = math.log2(math.e)

SWA_POS = (0, 2, 1, 3)
NSA_POS = (0, 4, 1, 5, 2, 6, 3, 7)

SWA_TILE = 256
WIN_TILE = 256
FLASH_T = 256
SEL_TQ = 256
ROW_TILE = 512
FFN_CHUNK = 256
VT_ROWS = HEAD_DIM + 16
KEY_BIAS_TERMS = 3

SEG_SWA = (0, 384)
SEG_FOX = (384, 896)
SEG_NSAQ = (896, 1408)
SEG_KC = (1408, 1536)
SEG_VC = (1536, 1664)
SEG_K2 = (1664, 1920)
SEG_V = (1920, 2560)
SEG_MISC = (2560, 2816)
GATE_LANE = 8
VT_FOX_BLOCK, VT_SWA_BLOCK, VT_SLC_BLOCK, VT_WIN_BLOCK = 0, 2, 3, 4


def _params(n_grid, vmem=VMEM_LIMIT):
    return pltpu.CompilerParams(dimension_semantics=("parallel",) * n_grid, vmem_limit_bytes=vmem)


def _dot_nt(a, b):
    return lax.dot_general(a, b, (((1,), (1,)), ((), ())), preferred_element_type=F32)


def _lane_lo(shape):
    return lax.broadcasted_iota(jnp.int32, shape, len(shape) - 1) < HEAD_DIM


def _adaln_kernel(c_ref, w_ref, b_ref, o_ref):
    c = c_ref[...]
    act = c * jax.nn.sigmoid(c)
    o_ref[0] = jnp.dot(act, w_ref[0], precision=HIGHEST, preferred_element_type=F32) + b_ref[0]


def _adaln(c, ada_w, ada_b):
    depth, d, n = ada_w.shape
    b = c.shape[0]
    return pl.pallas_call(
        _adaln_kernel,
        grid=(depth, n // d),
        in_specs=[pl.BlockSpec((b, d), lambda l, j: (0, 0)),
                  pl.BlockSpec((1, d, d), lambda l, j: (l, 0, j)),
                  pl.BlockSpec((1, 1, d), lambda l, j: (l, 0, j))],
        out_specs=pl.BlockSpec((1, b, d), lambda l, j: (l, 0, j)),
        out_shape=jax.ShapeDtypeStruct((depth, b, n), F32),
        compiler_params=_params(2),
    )(c, ada_w, ada_b.reshape(depth, 1, n))


def _t5_bucket(dist):
    n = jnp.maximum(dist, 0)
    max_exact = REL_BUCKETS // 2
    nf = jnp.maximum(n, 1).astype(jnp.float32)
    large = max_exact + (jnp.log(nf / max_exact) / math.log(REL_MAX_DISTANCE / max_exact)
                         * (REL_BUCKETS - max_exact)).astype(jnp.int32)
    large = jnp.minimum(large, REL_BUCKETS - 1)
    return jnp.where(n < max_exact, n, large)


def _bias_table_kernel(tab_ref, bucket_ref, o_ref, *, subtract_last):
    n_heads = o_ref.shape[0]
    values = [[(tab_ref[k, h] - (tab_ref[REL_BUCKETS - 1, h] if subtract_last else 0.0)) * LOG2E
               for h in range(n_heads)] for k in range(REL_BUCKETS)]

    def rows(chunk, carry):
        r0 = pl.multiple_of(chunk * SUBLANES, SUBLANES)
        bucket = bucket_ref[0, pl.ds(r0, SUBLANES), :]
        accs = [jnp.where(bucket == ZERO_BUCKET, 0.0, NEG) for _ in range(n_heads)]
        for k in range(REL_BUCKETS):
            hit = bucket == k
            for h in range(n_heads):
                accs[h] = jnp.where(hit, values[k][h], accs[h])
        for h in range(n_heads):
            o_ref[h, 0, pl.ds(r0, SUBLANES), :] = accs[h]
        return carry

    lax.fori_loop(0, bucket_ref.shape[1] // SUBLANES, rows, 0)


def _bias_table(table, bucket, subtract_last=False):
    n_heads = table.shape[1]
    n, r, c = bucket.shape
    return pl.pallas_call(
        functools.partial(_bias_table_kernel, subtract_last=subtract_last),
        grid=(n,),
        in_specs=[pl.BlockSpec(memory_space=pltpu.SMEM),
                  pl.BlockSpec((1, r, c), lambda i: (i, 0, 0))],
        out_specs=pl.BlockSpec((n_heads, 1, r, c), lambda i: (0, i, 0, 0)),
        out_shape=jax.ShapeDtypeStruct((n_heads, n, r, c), F32),
        compiler_params=_params(1),
    )(table, bucket)


def _band_buckets_t(tile, window):
    n_back = -(-(window - 1) // tile)
    t = jnp.arange(n_back + 1)[:, None, None]
    key = jnp.arange(tile)[None, :, None]
    query = jnp.arange(tile)[None, None, :]
    dist = query + (n_back - t) * tile - key
    return jnp.where((dist >= 0) & (dist < window), _t5_bucket(dist), -1).astype(jnp.int32)


def _toeplitz_buckets_t(tile, n_tiles):
    m = jnp.arange(n_tiles)[:, None, None]
    key = jnp.arange(tile)[None, :, None]
    query = jnp.arange(tile)[None, None, :]
    dist = m * tile + query - key
    near = jnp.where(dist >= 0, _t5_bucket(dist), -1).astype(jnp.int32)
    return jnp.concatenate([near, jnp.full((1, tile, tile), ZERO_BUCKET, jnp.int32)])


def _cmp_buckets_t(s_len, n_rows):
    n_c = n_rows - 1
    tile = jnp.arange(s_len // SEL_TQ)[:, None, None]
    n = jnp.arange(n_rows)[None, :, None]
    t = tile * SEL_TQ + jnp.arange(SEL_TQ)[None, None, :]
    dist = t - (n * CMP_STRIDE + CMP_LEN - 1)
    return jnp.where((dist >= 0) & (n < n_c), _t5_bucket(dist), -1).astype(jnp.int32)


def _near_tiles(tile):
    max_exact = REL_BUCKETS // 2
    first_const = math.ceil(max_exact * (REL_MAX_DISTANCE / max_exact) ** ((max_exact - 1) / max_exact)) + 1
    m = 1
    while m * tile - (tile - 1) < first_const:
        m += 1
    return m


def _rms(x, gain):
    return x * lax.rsqrt(jnp.mean(x * x, axis=-1, keepdims=True) + RMS_EPS) * gain


def _in_proj_kernel(x_ref, mod_ref, gain_ref, w_ref, swa_ref, fox_ref, nsaq_ref, kc_ref, vc_ref, k2_ref,
                    misc_ref, vt_ref):
    x = x_ref[0]
    h = _rms(x, gain_ref[...]) * (1.0 + mod_ref[0, 1:2, :]) + mod_ref[0, 0:1, :]
    hb = h.astype(BF16)

    def seg(bounds):
        return jnp.dot(hb, w_ref[:, bounds[0]:bounds[1]], preferred_element_type=F32)

    swa_ref[0] = seg(SEG_SWA).astype(BF16)
    fox_ref[0] = seg(SEG_FOX).astype(BF16)
    nsaq_ref[0] = seg(SEG_NSAQ).astype(BF16)
    kc_ref[0] = seg(SEG_KC).astype(BF16)
    vc_ref[0] = seg(SEG_VC).astype(BF16)
    k2_ref[0] = seg(SEG_K2).astype(BF16)
    misc_ref[0] = seg(SEG_MISC)

    rows = x.shape[0]
    extra_row = lax.broadcasted_iota(jnp.int32, (VT_ROWS - HEAD_DIM, rows), 0)
    extra = jnp.where(extra_row == 0, 1.0, 0.0).astype(BF16)
    values = seg(SEG_V)
    for c in range(values.shape[1] // LANES):
        vt = values[:, c * LANES:(c + 1) * LANES].T.astype(BF16)
        for half in range(2):
            base = (2 * c + half) * VT_ROWS
            vt_ref[0, base:base + HEAD_DIM, :] = vt[half * HEAD_DIM:(half + 1) * HEAD_DIM, :]
            vt_ref[0, base + HEAD_DIM:base + VT_ROWS, :] = extra


def _in_proj(x, mod, gain, w):
    b, s, d = x.shape
    n = w.shape[1]
    widths = [hi - lo for lo, hi in (SEG_SWA, SEG_FOX, SEG_NSAQ, SEG_KC, SEG_VC, SEG_K2, SEG_MISC)]
    dtypes = [BF16] * 6 + [F32]
    vt_rows = (SEG_V[1] - SEG_V[0]) // HEAD_DIM * VT_ROWS
    return pl.pallas_call(
        _in_proj_kernel,
        grid=(b, s // ROW_TILE),
        in_specs=[pl.BlockSpec((1, ROW_TILE, d), lambda i, j: (i, j, 0)),
                  pl.BlockSpec((1, ADA_CHUNKS, d), lambda i, j: (i, 0, 0)),
                  pl.BlockSpec((1, d), lambda i, j: (0, 0)),
                  pl.BlockSpec((d, n), lambda i, j: (0, 0))],
        out_specs=[pl.BlockSpec((1, ROW_TILE, wd), lambda i, j: (i, j, 0)) for wd in widths]
        + [pl.BlockSpec((1, vt_rows, ROW_TILE), lambda i, j: (i, 0, j))],
        out_shape=[jax.ShapeDtypeStruct((b, s, wd), dt) for wd, dt in zip(widths, dtypes)]
        + [jax.ShapeDtypeStruct((b, vt_rows, s), BF16)],
        compiler_params=_params(2),
    )(x, mod, gain, w)


def _banded_kernel(*refs, n_back, n_groups, has_sink, t):
    if has_sink:
        sink_ref, q_ref, k_ref, vt_ref, bias_ref, o_ref = refs
    else:
        q_ref, k_ref, vt_ref, bias_ref, o_ref = refs
    i = pl.program_id(1)
    lo = _lane_lo((t, LANES))
    n_tiles = n_back + 1

    def run(all_valid):
        starts = [pl.multiple_of(jnp.maximum(i - n_back + tt, 0) * t, t) for tt in range(n_tiles)]
        k_tiles = [k_ref[0, pl.ds(start, t), :] for start in starts]

        def scores(g):
            qg = q_ref[0, :, g * LANES:(g + 1) * LANES]
            zero = jnp.zeros_like(qg)
            qms = (jnp.where(lo, qg, zero), jnp.where(lo, zero, qg))
            return [[bias_ref[2 * g + half, tt] + _dot_nt(k_tiles[tt], qms[half]) for tt in range(n_tiles)]
                    for half in range(2)]

        def softmax_pv(g, sts):
            pair = []
            for half in range(2):
                tiles = sts[half]
                if not all_valid:
                    tiles = [jnp.where(i - n_back + tt >= 0, st, NEG) if tt < n_back else st
                             for tt, st in enumerate(tiles)]
                m = None
                for st in tiles:
                    part = st.reshape(t // SUBLANES, SUBLANES, t).max(axis=0)
                    m = part if m is None else jnp.maximum(m, part)
                m = _all_sublanes(m, jnp.maximum)
                if has_sink:
                    sink = sink_ref[2 * g + half] * LOG2E
                    m = jnp.maximum(m, sink)
                acc = None
                for tt, st in enumerate(tiles):
                    p = jnp.exp2((st.reshape(t // SUBLANES, SUBLANES, t) - m[None]).reshape(t, t).astype(BF16))
                    part = jnp.dot(vt_ref[0, half * VT_ROWS:(half + 1) * VT_ROWS, pl.ds(starts[tt], t)], p,
                                   preferred_element_type=F32)
                    acc = part if acc is None else acc + part
                denom = _all_sublanes(acc[HEAD_DIM:HEAD_DIM + SUBLANES, :], jnp.add)
                if has_sink:
                    denom = denom + jnp.exp2(sink - m)
                out = acc[0:HEAD_DIM, :].reshape(HEAD_DIM // SUBLANES, SUBLANES, t) / denom[None]
                pair.append(out.reshape(HEAD_DIM, t))
            o_ref[0, :, g * LANES:(g + 1) * LANES] = jnp.concatenate(pair, axis=0).T.astype(o_ref.dtype)

        pending = scores(0)
        for g in range(n_groups):
            current = pending
            if g + 1 < n_groups:
                pending = scores(g + 1)
            softmax_pv(g, current)

    @pl.when(i >= n_back)
    def _():
        run(True)

    @pl.when(i < n_back)
    def _():
        run(False)


def _banded_attention(q_arr, k_arr, k_blk, vt, vt_blk, bias, sinks=None):
    b, s, _ = q_arr.shape
    n_pos, n_tiles, t = bias.shape[0], bias.shape[1], bias.shape[2]
    width = n_pos * HEAD_DIM
    in_specs = [pl.BlockSpec((1, t, width), lambda i, j: (i, j, 0)),
                pl.BlockSpec((1, s, LANES), lambda i, j: (i, 0, k_blk)),
                pl.BlockSpec((1, 2 * VT_ROWS, s), lambda i, j: (i, vt_blk, 0)),
                pl.BlockSpec(bias.shape, lambda i, j: (0, 0, 0, 0))]
    args = [q_arr, k_arr, vt, bias]
    if sinks is not None:
        in_specs = [pl.BlockSpec(memory_space=pltpu.SMEM)] + in_specs
        args = [sinks] + args
    return pl.pallas_call(
        functools.partial(_banded_kernel, n_back=n_tiles - 1, n_groups=n_pos // 2, has_sink=sinks is not None, t=t),
        grid=(b, s // t),
        in_specs=in_specs,
        out_specs=pl.BlockSpec((1, t, width), lambda i, j: (i, j, 0)),
        out_shape=jax.ShapeDtypeStruct((b, s, width), BF16),
        compiler_params=_params(2),
    )(*args)


def _all_sublanes(x, op):
    for shift in (4, 2, 1):
        x = op(x, pltpu.roll(x, shift, 0))
    return x


def _flash_init(m_scr, acc_scr):
    m_scr[...] = jnp.full(m_scr.shape, NEG, F32)
    acc_scr[...] = jnp.zeros(acc_scr.shape, F32)


def _flash_update(h, st_ref, tile_max, vt_h, m_scr, acc_scr):
    tk, tq = st_ref.shape
    m_prev = m_scr[h]
    m_new = _all_sublanes(jnp.maximum(m_prev, tile_max), jnp.maximum)
    alpha = jnp.exp2(m_prev - m_new)
    p = jnp.exp2((st_ref[...].reshape(tk // SUBLANES, SUBLANES, tq) - m_new[None]).reshape(tk, tq).astype(BF16))
    acc = acc_scr[h].reshape(VT_ROWS // SUBLANES, SUBLANES, tq) * alpha[None]
    acc_scr[h] = acc.reshape(VT_ROWS, tq) + jnp.dot(vt_h, p, preferred_element_type=F32)
    m_scr[h] = m_new


def _flash_finish(o_ref, n_groups, acc_scr):
    tq = acc_scr.shape[2]
    for g in range(n_groups):
        pair = []
        for h in (2 * g, 2 * g + 1):
            denom = _all_sublanes(acc_scr[h, HEAD_DIM:HEAD_DIM + SUBLANES, :], jnp.add)
            out = acc_scr[h, 0:HEAD_DIM, :].reshape(HEAD_DIM // SUBLANES, SUBLANES, tq) / denom[None]
            pair.append(out.reshape(HEAD_DIM, tq))
        o_ref[0, :, g * LANES:(g + 1) * LANES] = jnp.concatenate(pair, axis=0).T.astype(o_ref.dtype)


def _flash_pipeline(i, n_heads, qk_scores, bias_tile, vt_slab, s_scr, tmax_scr, m_scr, acc_scr):
    def qk_head(thunk, h, j, slot):
        st = thunk() + bias_tile(h, j)
        s_scr[slot, h] = st
        tmax_scr[slot, h] = st.reshape(st.shape[0] // SUBLANES, SUBLANES, st.shape[1]).max(axis=0)

    def softmax_head(h, j, slot):
        _flash_update(h, s_scr.at[slot, h], tmax_scr[slot, h], vt_slab(h, j), m_scr, acc_scr)

    def softmax_all(j, slot):
        for h in range(n_heads):
            softmax_head(h, j, slot)

    def stage(j_qk, slot_qk, j_sm, slot_sm):
        thunks = qk_scores(j_qk)
        for h in range(n_heads):
            qk_head(thunks[h], h, j_qk, slot_qk)
            softmax_head(h, j_sm, slot_sm)

    for h, thunk in enumerate(qk_scores(0)):
        qk_head(thunk, h, 0, 0)

    def body(trip, carry):
        j = 2 * trip
        stage(j + 1, 1, j, 0)
        stage(j + 2, 0, j + 1, 1)
        return carry

    lax.fori_loop(0, i // 2, body, 0)
    last = 2 * (i // 2)

    @pl.when(i % 2 == 0)
    def _():
        softmax_all(last, 0)

    @pl.when(i % 2 == 1)
    def _():
        stage(last + 1, 1, last, 0)
        softmax_all(last + 1, 1)


def _fox_aug_kernel(misc_ref, fbias_ref, tri_ref, o_ref):
    s_len, width = misc_ref.shape[1], misc_ref.shape[2]
    term = lax.broadcasted_iota(jnp.int32, (LANES, width), 1) % HEAD_DIM
    carry = jnp.zeros((1, width), F32)
    for c in range(s_len // LANES):
        z = misc_ref[0, c * LANES:(c + 1) * LANES, :] + fbias_ref[...]
        log_f = jnp.minimum(z, 0.0) - jnp.log1p(jnp.exp(-jnp.abs(z)))
        cum = jnp.dot(tri_ref[...], log_f, precision=HIGHEST, preferred_element_type=F32) + carry
        carry = cum[LANES - 1:LANES, :]
        x = cum * (-LOG2E)
        hi = x.astype(BF16).astype(F32)
        rest = x - hi
        mid = rest.astype(BF16).astype(F32)
        low = rest - mid
        out = jnp.where(term == 0, hi, jnp.where(term == 1, mid, jnp.where(term == 2, low, 0.0)))
        o_ref[0, c * LANES:(c + 1) * LANES, :] = out.astype(BF16)


def _fox_key_terms(misc, fbias):
    b, s, width = misc.shape
    tri = jnp.asarray(np.tril(np.ones((LANES, LANES), np.float32)))
    return pl.pallas_call(
        _fox_aug_kernel,
        grid=(b,),
        in_specs=[pl.BlockSpec((1, s, width), lambda i: (i, 0, 0)),
                  pl.BlockSpec((1, width), lambda i: (0, 0)),
                  pl.BlockSpec((LANES, LANES), lambda i: (0, 0))],
        out_specs=pl.BlockSpec((1, s, width), lambda i: (i, 0, 0)),
        out_shape=jax.ShapeDtypeStruct((b, s, width), BF16),
        compiler_params=_params(1),
    )(misc, fbias, tri)


def _fox_kernel(q_ref, k_ref, aug_ref, vt_ref, mask_ref, o_ref, qs_scr, s_scr, tmax_scr, m_scr, acc_scr):
    t = FLASH_T
    i = pl.program_id(1)
    lo = _lane_lo((t, LANES))
    lane = lax.broadcasted_iota(jnp.int32, (t, LANES), 1)
    ones = jnp.where(lane % HEAD_DIM < KEY_BIAS_TERMS, 1.0, 0.0).astype(BF16)
    n_groups = FOX_HEADS // 2
    for g in range(n_groups):
        qg = q_ref[0, :, g * LANES:(g + 1) * LANES]
        qs_scr[2 * g] = jnp.where(lo, qg, ones)
        qs_scr[2 * g + 1] = jnp.where(lo, ones, qg)
    _flash_init(m_scr, acc_scr)

    def qk_scores(j):
        start = pl.multiple_of(j * t, t)
        scores = []
        for g in range(n_groups):
            k_tile = k_ref[0, pl.ds(start, t), g * LANES:(g + 1) * LANES]
            a_tile = aug_ref[0, pl.ds(start, t), g * LANES:(g + 1) * LANES]
            k_sel = (jnp.where(lo, k_tile, a_tile), jnp.where(lo, a_tile, k_tile))
            for half in range(2):
                scores.append(functools.partial(lambda k, h: _dot_nt(k, qs_scr[h]), k_sel[half], 2 * g + half))
        return scores

    def bias_tile(h, j):
        return mask_ref[jnp.minimum(i - j, 1)]

    def vt_slab(h, j):
        return vt_ref[0, h * VT_ROWS:(h + 1) * VT_ROWS, pl.ds(pl.multiple_of(j * t, t), t)]

    _flash_pipeline(i, FOX_HEADS, qk_scores, bias_tile, vt_slab, s_scr, tmax_scr, m_scr, acc_scr)
    _flash_finish(o_ref, n_groups, acc_scr)


def _fox_attention(fox_qk, key_terms, vt):
    b, s, _ = fox_qk.shape
    width = FOX_HEADS * HEAD_DIM
    t = FLASH_T
    idx = np.arange(t)
    diag = np.where(idx[:, None] <= idx[None, :], 0.0, NEG)
    masks = jnp.asarray(np.stack([diag, np.zeros((t, t))]).astype(np.float32))
    return pl.pallas_call(
        _fox_kernel,
        grid=(b, s // t),
        in_specs=[pl.BlockSpec((1, t, width), lambda i, j: (i, j, 0)),
                  pl.BlockSpec((1, s, width), lambda i, j: (i, 0, 1)),
                  pl.BlockSpec((1, s, width), lambda i, j: (i, 0, 0)),
                  pl.BlockSpec((1, FOX_HEADS * VT_ROWS, s), lambda i, j: (i, VT_FOX_BLOCK, 0)),
                  pl.BlockSpec(masks.shape, lambda i, j: (0, 0, 0))],
        out_specs=pl.BlockSpec((1, t, width), lambda i, j: (i, j, 0)),
        out_shape=jax.ShapeDtypeStruct((b, s, width), BF16),
        scratch_shapes=[pltpu.VMEM((FOX_HEADS, t, LANES), BF16),
                        pltpu.VMEM((2, FOX_HEADS, t, t), F32),
                        pltpu.VMEM((2, FOX_HEADS, SUBLANES, t), F32),
                        pltpu.VMEM((FOX_HEADS, SUBLANES, t), F32),
                        pltpu.VMEM((FOX_HEADS, VT_ROWS, t), F32)],
        compiler_params=_params(2),
    )(fox_qk, fox_qk, key_terms, vt, masks)


def _compress_kernel(x_ref, pe_ref, w1a_ref, w1b_ref, w2_ref, o_ref):
    x = x_ref[0].astype(F32)
    n_rows = x.shape[0]

    def mm(a, w):
        return jnp.dot(a, w, precision=HIGHEST, preferred_element_type=F32)

    first = mm(x, w1a_ref[0])
    second = mm(x, w1b_ref[0])
    pe_term = (mm(pe_ref[0, 0], w1a_ref[0]) + mm(pe_ref[0, 1], w1b_ref[0]))[0:1, :]
    pre = first + pltpu.roll(second, n_rows - 1, 0) + pe_term
    hid = 0.5 * pre * (1.0 + jnp.tanh(math.sqrt(2.0 / math.pi) * (pre + 0.044715 * (pre * pre * pre))))
    o_ref[0, 0] = mm(hid, w2_ref[0])


def _compress(kc, vc, cmp_pos, cmp_w1, cmp_w2):
    b, s, _ = kc.shape
    n_rows = s // CMP_STRIDE
    half = CMP_LEN // 2
    feat = CMP_STRIDE * LANES
    x = jnp.stack([kc.reshape(b, n_rows, feat), vc.reshape(b, n_rows, feat)])
    eye = jnp.eye(2, dtype=F32)
    w1 = cmp_w1.astype(F32).reshape(2, CMP_LEN, HEAD_DIM, CMP_HIDDEN)
    w1a = jnp.einsum('wldj,hg->wlhdgj', w1[:, :half], eye).reshape(2, feat, 2 * CMP_HIDDEN)
    w1b = jnp.einsum('wldj,hg->wlhdgj', w1[:, half:], eye).reshape(2, feat, 2 * CMP_HIDDEN)
    w2 = jnp.einsum('wjd,hg->whjgd', cmp_w2.astype(F32), eye).reshape(2, 2 * CMP_HIDDEN, LANES)
    w2 = jnp.concatenate([w2, jnp.roll(w2, HEAD_DIM, axis=2)], axis=2)
    pe = jnp.broadcast_to(cmp_pos.astype(F32).reshape(2, 2, half, 1, HEAD_DIM), (2, 2, half, 2, HEAD_DIM))
    pe = jnp.broadcast_to(pe.reshape(2, 2, 1, feat), (2, 2, 8, feat))
    return pl.pallas_call(
        _compress_kernel,
        grid=(2, b),
        in_specs=[pl.BlockSpec((None, 1, n_rows, feat), lambda w, i: (w, i, 0, 0)),
                  pl.BlockSpec((1, 2, 8, feat), lambda w, i: (w, 0, 0, 0)),
                  pl.BlockSpec((1, feat, 2 * CMP_HIDDEN), lambda w, i: (w, 0, 0)),
                  pl.BlockSpec((1, feat, 2 * CMP_HIDDEN), lambda w, i: (w, 0, 0)),
                  pl.BlockSpec((1, 2 * CMP_HIDDEN, 2 * LANES), lambda w, i: (w, 0, 0))],
        out_specs=pl.BlockSpec((1, 1, n_rows, 2 * LANES), lambda w, i: (w, i, 0, 0)),
        out_shape=jax.ShapeDtypeStruct((2, b, n_rows, 2 * LANES), F32),
        compiler_params=_params(2),
    )(x, pe, w1a, w1b, w2)


def _select_kernel(q_ref, kc_ref, vct_ref, bias_ref, ovl_ref, o_ref, mb_ref, count_scr):
    tq = SEL_TQ
    i = pl.program_id(0)
    lo = _lane_lo((tq, LANES))
    n_rows = kc_ref.shape[2]
    lo_k = _lane_lo((n_rows, LANES))
    n_grp = n_rows // SUBLANES

    k_own = kc_ref[0, 0, :, 0:LANES]
    k_swapped = kc_ref[0, 0, :, LANES:2 * LANES]
    hi = k_own.astype(BF16)
    low = (k_swapped - k_swapped.astype(BF16).astype(F32)).astype(BF16)
    k_sel = (jnp.where(lo_k, hi, low), jnp.where(lo_k, low, hi))

    def scores(g):
        qg = q_ref[0, :, g * LANES:(g + 1) * LANES]
        swapped = pltpu.roll(qg.astype(F32), HEAD_DIM, 1).astype(BF16)
        q_dup = (jnp.where(lo, qg, swapped), jnp.where(lo, swapped, qg))
        return [bias_ref[2 * g + half, 0] + _dot_nt(k_sel[half], q_dup[half]) for half in range(2)]

    query = i * tq + lax.broadcasted_iota(jnp.int32, (SUBLANES, tq), 1)
    has_keys = query >= CMP_LEN - 1
    p_sum = [None, None]

    def softmax_pv(g, sts):
        pair = []
        for half in range(2):
            s3 = sts[half].reshape(n_grp, SUBLANES, tq)
            m = _all_sublanes(s3.max(axis=0), jnp.maximum)
            e = jnp.exp2(s3 - m[None])
            inv = jnp.where(has_keys, 1.0 / _all_sublanes(e.sum(axis=0), jnp.add), 0.0)
            p = e * inv[None]
            p_sum[half] = p if p_sum[half] is None else p_sum[half] + p
            pair.append(jnp.dot(vct_ref[0, half * HEAD_DIM:(half + 1) * HEAD_DIM, :],
                                p.reshape(n_rows, tq).astype(BF16), preferred_element_type=F32))
        o_ref[0, :, g * LANES:(g + 1) * LANES] = jnp.concatenate(pair, axis=0).T.astype(o_ref.dtype)

    n_groups = NSA_HEADS // 2
    pending = scores(0)
    for g in range(n_groups):
        current = pending
        if g + 1 < n_groups:
            pending = scores(g + 1)
        softmax_pv(g, current)

    n_blk = ovl_ref.shape[0]
    blk_grp = n_blk // SUBLANES
    sub = lax.broadcasted_iota(jnp.int32, (SUBLANES, tq), 0)
    q_blk = query // SLC_BLOCK
    masks = []
    for half in (1, 0):
        imp = jnp.dot(ovl_ref[...], p_sum[half].reshape(n_rows, tq), precision=HIGHEST,
                      preferred_element_type=F32)
        rows = []
        for r in range(blk_grp):
            blk = sub + r * SUBLANES
            forced = jnp.where(blk == 0, 1, 0) + jnp.where(blk == q_blk, 1, 0) + jnp.where(blk == q_blk - 1, 1, 0)
            rows.append(jnp.where(forced > 0, FORCE,
                                  jnp.where(blk > q_blk, NEG, imp[r * SUBLANES:(r + 1) * SUBLANES, :])))
        count_scr[...] = jnp.zeros(count_scr.shape, jnp.int32)
        for r_other in range(blk_grp):
            @pl.when(r_other * SUBLANES * SLC_BLOCK < (i + 1) * tq)
            def _(r_other=r_other, rows=rows):
                counts = [None] * blk_grp
                for s_other in range(SUBLANES):
                    row = jnp.broadcast_to(rows[r_other][s_other:s_other + 1, :], (SUBLANES, tq))
                    for r in range(blk_grp):
                        if r > r_other:
                            beats = jnp.where(row >= rows[r], 1, 0)
                        elif r < r_other:
                            beats = jnp.where(row > rows[r], 1, 0)
                        else:
                            beats = jnp.where(sub > s_other, jnp.where(row >= rows[r], 1, 0),
                                              jnp.where(row > rows[r], 1, 0))
                        counts[r] = beats if counts[r] is None else counts[r] + beats
                for r in range(blk_grp):
                    count_scr[r] = count_scr[r] + counts[r]
        masks.extend(jnp.where(count_scr[r] < TOPK, 0.0, NEG) for r in range(blk_grp))
    for c in range(tq // LANES):
        mb_ref[0, c * LANES:(c + 1) * LANES, :] = jnp.concatenate(
            [mk[:, c * LANES:(c + 1) * LANES] for mk in masks], axis=0).T.astype(BF16)


def _select(nsa_q, cmp_kv, bias_c, overlap_t):
    b, s, width = nsa_q.shape
    n_rows = cmp_kv.shape[2]
    n_blk = overlap_t.shape[0]
    assert n_blk == HEAD_DIM
    tq = SEL_TQ
    vct = cmp_kv[1, :, :, 0:LANES].transpose(0, 2, 1).astype(BF16)
    return pl.pallas_call(
        _select_kernel,
        grid=(s // tq, b),
        in_specs=[pl.BlockSpec((1, tq, width), lambda j, i: (i, j, 0)),
                  pl.BlockSpec((1, 1, n_rows, 2 * LANES), lambda j, i: (0, i, 0, 0)),
                  pl.BlockSpec((1, LANES, n_rows), lambda j, i: (i, 0, 0)),
                  pl.BlockSpec((NSA_HEADS, 1, n_rows, tq), lambda j, i: (0, j, 0, 0)),
                  pl.BlockSpec((n_blk, n_rows), lambda j, i: (0, 0))],
        out_specs=[pl.BlockSpec((1, tq, width), lambda j, i: (i, j, 0)),
                   pl.BlockSpec((1, tq, LANES), lambda j, i: (i, j, 0))],
        out_shape=[jax.ShapeDtypeStruct((b, s, width), BF16),
                   jax.ShapeDtypeStruct((b, s, LANES), BF16)],
        scratch_shapes=[pltpu.VMEM((n_blk // SUBLANES, SUBLANES, tq), jnp.int32)],
        compiler_params=_params(2),
    )(nsa_q, cmp_kv, vct, bias_c, overlap_t)


def _overlap_t(s_len):
    n_rows = s_len // CMP_STRIDE
    c_start = np.arange(n_rows)[None, :] * CMP_STRIDE
    s_start = np.arange(HEAD_DIM)[:, None] * SLC_BLOCK
    ovl = np.clip(np.minimum(c_start + CMP_LEN, s_start + SLC_BLOCK) - np.maximum(c_start, s_start), 0, None)
    ovl = ovl.astype(np.float32) / CMP_LEN
    ovl[:, n_rows - 1] = 0.0
    ovl[s_len // SLC_BLOCK:, :] = 0.0
    return jnp.asarray(ovl)


def _slc_kernel(q_ref, mb_ref, k_ref, e2_ref, vt_ref, bias_ref, o_ref, qs_scr, s_scr, tmax_scr, m_scr, acc_scr, *,
                n_near):
    t = FLASH_T
    i = pl.program_id(1)
    lo = _lane_lo((t, LANES))
    n_groups = NSA_HEADS // 2
    mb = mb_ref[0]
    for g in range(n_groups):
        qg = q_ref[0, :, g * LANES:(g + 1) * LANES]
        qs_scr[2 * g] = jnp.where(lo, qg, mb)
        qs_scr[2 * g + 1] = jnp.where(lo, mb, qg)
    _flash_init(m_scr, acc_scr)

    def qk_scores(j):
        start = pl.multiple_of(j * t, t)
        k_tile = k_ref[0, pl.ds(start, t), :]
        e_tile = e2_ref[pl.ds(start, t), :]
        k_sel = (jnp.where(lo, k_tile, e_tile), jnp.where(lo, e_tile, k_tile))
        return [functools.partial(lambda k, pos: _dot_nt(k, qs_scr[pos]), k_sel[pos % 2], pos)
                for pos in range(NSA_HEADS)]

    def bias_tile(pos, j):
        return bias_ref[pos, jnp.minimum(i - j, n_near)]

    def vt_slab(pos, j):
        kv = pos % 2
        return vt_ref[0, kv * VT_ROWS:(kv + 1) * VT_ROWS, pl.ds(pl.multiple_of(j * t, t), t)]

    _flash_pipeline(i, NSA_HEADS, qk_scores, bias_tile, vt_slab, s_scr, tmax_scr, m_scr, acc_scr)
    _flash_finish(o_ref, n_groups, acc_scr)


def _slc_attention(nsa_q, mask_bias, k2, e2, vt, bias):
    b, s, width = nsa_q.shape
    t = FLASH_T
    n_near = bias.shape[1] - 1
    return pl.pallas_call(
        functools.partial(_slc_kernel, n_near=n_near),
        grid=(b, s // t),
        in_specs=[pl.BlockSpec((1, t, width), lambda i, j: (i, j, 0)),
                  pl.BlockSpec((1, t, LANES), lambda i, j: (i, j, 0)),
                  pl.BlockSpec((1, s, LANES), lambda i, j: (i, 0, 0)),
                  pl.BlockSpec((s, LANES), lambda i, j: (0, 0)),
                  pl.BlockSpec((1, 2 * VT_ROWS, s), lambda i, j: (i, VT_SLC_BLOCK, 0)),
                  pl.BlockSpec(bias.shape, lambda i, j: (0, 0, 0, 0))],
        out_specs=pl.BlockSpec((1, t, width), lambda i, j: (i, j, 0)),
        out_shape=jax.ShapeDtypeStruct((b, s, width), BF16),
        scratch_shapes=[pltpu.VMEM((NSA_HEADS, t, LANES), BF16),
                        pltpu.VMEM((2, NSA_HEADS, t, t), F32),
                        pltpu.VMEM((2, NSA_HEADS, SUBLANES, t), F32),
                        pltpu.VMEM((NSA_HEADS, SUBLANES, t), F32),
                        pltpu.VMEM((NSA_HEADS, VT_ROWS, t), F32)],
        compiler_params=_params(2),
    )(nsa_q, mask_bias, k2, e2, vt, bias)


def _block_onehot(s_len):
    blk = np.arange(s_len)[:, None] // SLC_BLOCK
    lane = np.arange(LANES)[None, :] % HEAD_DIM
    return jnp.asarray((blk == lane).astype(np.float32), dtype=BF16)


def _mix_ffn_kernel(x_ref, mod_ref, swa_ref, fox_ref, cmp_ref, slc_ref, win_ref, misc_ref, expand_ref, gn_ref,
                    w_ref, post_ref, pre_ref, wg_ref, wu_ref, wd_ref, fpost_ref, o_ref):
    n_swa = SWA_HEADS * HEAD_DIM
    n_fox = FOX_HEADS * HEAD_DIM
    n_nsa = NSA_HEADS * HEAD_DIM
    gate = jax.nn.sigmoid(misc_ref[0])
    gate_hi = gate.astype(BF16)
    gate_lo = (gate - gate_hi.astype(F32)).astype(BF16)
    gates = (jnp.dot(gate_hi, expand_ref[...], preferred_element_type=F32)
             + jnp.dot(gate_lo, expand_ref[...], preferred_element_type=F32))
    o_nsa = (gates[:, 0:n_nsa] * cmp_ref[0].astype(F32) + gates[:, n_nsa:2 * n_nsa] * slc_ref[0].astype(F32)
             + gates[:, 2 * n_nsa:3 * n_nsa] * win_ref[0].astype(F32))
    a = _rms(swa_ref[0].astype(F32), gn_ref[:, 0:n_swa]).astype(BF16)
    b = _rms(fox_ref[0].astype(F32), gn_ref[:, n_swa:n_swa + n_fox]).astype(BF16)
    c = _rms(o_nsa, gn_ref[:, n_swa + n_fox:]).astype(BF16)
    y = (jnp.dot(a, w_ref[0:n_swa, :], preferred_element_type=F32)
         + jnp.dot(b, w_ref[n_swa:n_swa + n_fox, :], preferred_element_type=F32)
         + jnp.dot(c, w_ref[n_swa + n_fox:, :], preferred_element_type=F32))
    x = x_ref[0] + mod_ref[0, 2:3, :] * _rms(y, post_ref[...])

    h = (_rms(x, pre_ref[...]) * (1.0 + mod_ref[0, 4:5, :]) + mod_ref[0, 3:4, :]).astype(BF16)
    y = jnp.zeros(x.shape, F32)
    for chunk in range(wg_ref.shape[0]):
        gate = jnp.dot(h, wg_ref[chunk], preferred_element_type=F32)
        up = jnp.dot(h, wu_ref[chunk], preferred_element_type=F32)
        act = (gate * jax.nn.sigmoid(gate) * up).astype(BF16)
        y = y + jnp.dot(act, wd_ref[chunk], preferred_element_type=F32)
    o_ref[0] = x + mod_ref[0, 5:6, :] * _rms(y, fpost_ref[...])


def _gate_expansion():
    expand = np.zeros((LANES, 3 * NSA_HEADS * HEAD_DIM), np.float32)
    for branch in range(3):
        for p in range(NSA_HEADS):
            col = (branch * NSA_HEADS + p) * HEAD_DIM
            expand[GATE_LANE + 8 * branch + p, col:col + HEAD_DIM] = 1.0
    return jnp.asarray(expand, dtype=BF16)


def _mix_ffn(x, mod, o_swa, o_fox, o_cmp, o_slc, o_win, misc, gn, w, post, pre, wg, wu, wd, fpost):
    b, s, d = x.shape
    expand = _gate_expansion()
    n_chunks = wg.shape[0]

    def rows(width):
        return pl.BlockSpec((1, ROW_TILE, width), lambda i, j: (i, j, 0))

    def whole(shape):
        return pl.BlockSpec(shape, lambda i, j: (0,) * len(shape), pipeline_mode=pl.Buffered(1))

    vec = pl.BlockSpec((1, d), lambda i, j: (0, 0))
    return pl.pallas_call(
        _mix_ffn_kernel,
        grid=(b, s // ROW_TILE),
        in_specs=[rows(d),
                  pl.BlockSpec((1, ADA_CHUNKS, d), lambda i, j: (i, 0, 0)),
                  rows(o_swa.shape[2]), rows(o_fox.shape[2]), rows(o_cmp.shape[2]), rows(o_slc.shape[2]),
                  rows(o_win.shape[2]), rows(LANES),
                  whole(expand.shape), vec, whole((d, d)), vec, vec,
                  whole((n_chunks, d, FFN_CHUNK)), whole((n_chunks, d, FFN_CHUNK)), whole((n_chunks, FFN_CHUNK, d)),
                  vec],
        out_specs=rows(d),
        out_shape=jax.ShapeDtypeStruct((b, s, d), F32),
        compiler_params=_params(2),
    )(x, mod, o_swa, o_fox, o_cmp, o_slc, o_win, misc, expand, gn, w, post, pre, wg, wu, wd, fpost)


def _forget_lanes():
    lanes, heads = [], []
    for h in range(FOX_HEADS):
        base = (h // 2) * LANES + (HEAD_DIM if h % 2 == 0 else 0)
        for j in range(KEY_BIAS_TERMS):
            lanes.append(base + j)
            heads.append(h)
    return np.array(lanes), np.array(heads)


def _in_proj_layout():
    d = HEAD_DIM
    o_qa, o_ka, o_va, o_qb, o_kb, o_vb, o_fb, o_qc = 0, 256, 384, 512, 768, 1024, 1280, 1284
    o_kc, o_vc, o_ksl, o_vsl, o_kw, o_vw, o_gc = 1796, 1924, 2052, 2180, 2308, 2436, 2564
    scale = LOG2E / math.sqrt(d)

    def head_cols(base, heads):
        return np.concatenate([np.arange(base + h * d, base + (h + 1) * d) for h in heads])

    def span(base, width):
        return np.arange(base, base + width)

    cols = [head_cols(o_qa, SWA_POS), span(o_ka, 128),
            span(o_qb, 256), span(o_kb, 256),
            head_cols(o_qc, NSA_POS),
            span(o_kc, 128), span(o_vc, 128),
            span(o_ksl, 128), span(o_kw, 128),
            span(o_vb, 256), span(o_va, 128), span(o_vsl, 128), span(o_vw, 128)]
    scales = [np.full(256, scale), np.ones(128), np.full(256, scale), np.ones(256), np.full(512, scale),
              np.ones(256), np.ones(256), np.ones(640)]
    lanes, heads = _forget_lanes()
    misc_cols = np.zeros(SEG_MISC[1] - SEG_MISC[0], np.int64)
    misc_scale = np.zeros(SEG_MISC[1] - SEG_MISC[0])
    misc_cols[lanes] = o_fb + heads
    misc_scale[lanes] = 1.0
    for branch in range(3):
        for p, h in enumerate(NSA_POS):
            misc_cols[GATE_LANE + 8 * branch + p] = o_gc + h * 3 + branch
            misc_scale[GATE_LANE + 8 * branch + p] = 1.0
    cols.append(misc_cols)
    scales.append(misc_scale)
    return np.concatenate(cols), np.concatenate(scales).astype(np.float32)


def _head_perm(pos):
    return np.concatenate([np.arange(h * HEAD_DIM, (h + 1) * HEAD_DIM) for h in pos])


def kernel(x, c, rel_bias, ada_w, ada_b, attn_pre_norm, attn_post_norm, ffn_pre_norm, ffn_post_norm, w_in,
           forget_bias, swa_sinks, cmp_pos, cmp_w1, cmp_w2, group_norm, w_out, ffn_w_gate, ffn_w_up, ffn_w_down):
    b, s, d = x.shape
    depth = w_in.shape[0]
    hidden = ffn_w_gate.shape[2]
    assert s % (2 * FLASH_T) == 0 and s // SLC_BLOCK <= HEAD_DIM and hidden % FFN_CHUNK == 0

    cols, scales = _in_proj_layout()
    w_all = (w_in[:, :, cols] * scales).astype(BF16)
    lanes, heads = _forget_lanes()
    fbias_all = jnp.zeros((depth, 1, SEG_MISC[1] - SEG_MISC[0]), F32).at[:, 0, lanes].set(
        forget_bias[:, heads].astype(F32))
    swa_perm = _head_perm(SWA_POS)
    nsa_perm = _head_perm(NSA_POS)
    n_swa, n_fox = SWA_HEADS * HEAD_DIM, FOX_HEADS * HEAD_DIM
    mix_perm = np.concatenate([swa_perm, n_swa + np.arange(n_fox), n_swa + n_fox + nsa_perm])
    gn_all = group_norm[:, mix_perm].astype(F32)
    w_out_all = w_out[:, mix_perm, :].astype(BF16)
    n_chunks = hidden // FFN_CHUNK
    wg_all = ffn_w_gate.reshape(depth, d, n_chunks, FFN_CHUNK).transpose(0, 2, 1, 3).astype(BF16)
    wu_all = ffn_w_up.reshape(depth, d, n_chunks, FFN_CHUNK).transpose(0, 2, 1, 3).astype(BF16)
    wd_all = ffn_w_down.reshape(depth, n_chunks, FFN_CHUNK, d).astype(BF16)

    tab_swa = rel_bias[:, np.array(SWA_POS)].astype(F32)
    tab_nsa = rel_bias[:, SWA_HEADS + np.array(NSA_POS)].astype(F32)
    bias_swa = _bias_table(tab_swa, _band_buckets_t(SWA_TILE, SWA_WINDOW))
    bias_win = _bias_table(tab_nsa, _band_buckets_t(WIN_TILE, NSA_WINDOW))
    bias_slc = _bias_table(tab_nsa, _toeplitz_buckets_t(FLASH_T, _near_tiles(FLASH_T)), subtract_last=True)
    n_rows = s // CMP_STRIDE
    bias_cmp = _bias_table(tab_nsa, _cmp_buckets_t(s, n_rows))
    overlap_t = _overlap_t(s)
    e2 = _block_onehot(s)

    mod_all = _adaln(c.astype(F32), ada_w.astype(F32), ada_b.astype(F32)).reshape(depth, b, ADA_CHUNKS, d)

    for layer in range(depth):
        mod = mod_all[layer]
        swa_qk, fox_qk, nsa_q, kc, vc, k2, misc, vt = _in_proj(
            x, mod, attn_pre_norm[layer].reshape(1, d).astype(F32), w_all[layer])
        o_swa = _banded_attention(swa_qk, swa_qk, 2, vt, VT_SWA_BLOCK, bias_swa,
                                  sinks=swa_sinks[layer][np.array(SWA_POS)].astype(F32))
        o_fox = _fox_attention(fox_qk, _fox_key_terms(misc, fbias_all[layer]), vt)
        cmp_kv = _compress(kc, vc, cmp_pos[layer], cmp_w1[layer], cmp_w2[layer])
        o_cmp, mask_bias = _select(nsa_q, cmp_kv, bias_cmp, overlap_t)
        o_slc = _slc_attention(nsa_q, mask_bias, k2, e2, vt, bias_slc)
        o_win = _banded_attention(nsa_q, k2, 1, vt, VT_WIN_BLOCK, bias_win)
        x = _mix_ffn(x, mod, o_swa, o_fox, o_cmp, o_slc, o_win, misc, gn_all[layer].reshape(1, d),
                     w_out_all[layer], attn_post_norm[layer].reshape(1, d).astype(F32),
                     ffn_pre_norm[layer].reshape(1, d).astype(F32), wg_all[layer], wu_all[layer], wd_all[layer],
                     ffn_post_norm[layer].reshape(1, d).astype(F32))
    return x
```

```python
import functools
import math

import numpy as np
import jax
import jax.numpy as jnp
from jax import lax
from jax.experimental import pallas as pl
from jax.experimental.pallas import tpu as pltpu

F32 = jnp.float32
BF16 = jnp.bfloat16
HIGHEST = lax.Precision.HIGHEST

LANES = 128
SUBLANES = 8
VMEM_LIMIT = 56 * 1024 * 1024

HEAD_DIM = 64
SWA_HEADS = 4
SWA_WINDOW = 128
FOX_HEADS = 4
NSA_HEADS = 8
CMP_LEN = 32
CMP_STRIDE = 16
CMP_HIDDEN = 2 * HEAD_DIM
SLC_BLOCK = 64
TOPK = 16
NSA_WINDOW = 512
REL_BUCKETS = 32
REL_MAX_DISTANCE = 1024
ZERO_BUCKET = -2
RMS_EPS = 1e-6
NEG = -1e30
FORCE = 1e30
ADA_CHUNKS = 6
LOG2E = math.log2(math.e)

SWA_POS = (0, 2, 1, 3)
NSA_POS = (0, 4, 1, 5, 2, 6, 3, 7)

SWA_TILE = 256
WIN_TILE = 256
FLASH_T = 256
SEL_TQ = 256
ROW_TILE = 512
FFN_CHUNK = 256
VT_ROWS = HEAD_DIM + 16
KEY_BIAS_TERMS = 3

SEG_SWA = (0, 384)
SEG_FOX = (384, 896)
SEG_NSAQ = (896, 1408)
SEG_KC = (1408, 1536)
SEG_VC = (1536, 1664)
SEG_K2 = (1664, 1920)
SEG_V = (1920, 2560)
SEG_MISC = (2560, 2816)
GATE_LANE = 8
VT_FOX_BLOCK, VT_SWA_BLOCK, VT_SLC_BLOCK, VT_WIN_BLOCK = 0, 2, 3, 4


def _params(n_grid, vmem=VMEM_LIMIT):
    return pltpu.CompilerParams(dimension_semantics=("parallel",) * n_grid, vmem_limit_bytes=vmem)


def _dot_nt(a, b):
    return lax.dot_general(a, b, (((1,), (1,)), ((), ())), preferred_element_type=F32)


def _lane_lo(shape):
    return lax.broadcasted_iota(jnp.int32, shape, len(shape) - 1) < HEAD_DIM


def _adaln_kernel(c_ref, w_ref, b_ref, o_ref):
    c = c_ref[...]
    act = c * jax.nn.sigmoid(c)
    o_ref[0] = jnp.dot(act, w_ref[0], precision=HIGHEST, preferred_element_type=F32) + b_ref[0]


def _adaln(c, ada_w, ada_b):
    depth, d, n = ada_w.shape
    b = c.shape[0]
    return pl.pallas_call(
        _adaln_kernel,
        grid=(depth, n // d),
        in_specs=[pl.BlockSpec((b, d), lambda l, j: (0, 0)),
                  pl.BlockSpec((1, d, d), lambda l, j: (l, 0, j)),
                  pl.BlockSpec((1, 1, d), lambda l, j: (l, 0, j))],
        out_specs=pl.BlockSpec((1, b, d), lambda l, j: (l, 0, j)),
        out_shape=jax.ShapeDtypeStruct((depth, b, n), F32),
        compiler_params=_params(2),
    )(c, ada_w, ada_b.reshape(depth, 1, n))


def _t5_bucket(dist):
    n = jnp.maximum(dist, 0)
    max_exact = REL_BUCKETS // 2
    nf = jnp.maximum(n, 1).astype(jnp.float32)
    large = max_exact + (jnp.log(nf / max_exact) / math.log(REL_MAX_DISTANCE / max_exact)
                         * (REL_BUCKETS - max_exact)).astype(jnp.int32)
    large = jnp.minimum(large, REL_BUCKETS - 1)
    return jnp.where(n < max_exact, n, large)


def _bias_table_kernel(tab_ref, bucket_ref, o_ref, *, subtract_last):
    n_heads = o_ref.shape[0]
    values = [[(tab_ref[k, h] - (tab_ref[REL_BUCKETS - 1, h] if subtract_last else 0.0)) * LOG2E
               for h in range(n_heads)] for k in range(REL_BUCKETS)]

    def rows(chunk, carry):
        r0 = pl.multiple_of(chunk * SUBLANES, SUBLANES)
        bucket = bucket_ref[0, pl.ds(r0, SUBLANES), :]
        accs = [jnp.where(bucket == ZERO_BUCKET, 0.0, NEG) for _ in range(n_heads)]
        for k in range(REL_BUCKETS):
            hit = bucket == k
            for h in range(n_heads):
                accs[h] = jnp.where(hit, values[k][h], accs[h])
        for h in range(n_heads):
            o_ref[h, 0, pl.ds(r0, SUBLANES), :] = accs[h]
        return carry

    lax.fori_loop(0, bucket_ref.shape[1] // SUBLANES, rows, 0)


def _bias_table(table, bucket, subtract_last=False):
    n_heads = table.shape[1]
    n, r, c = bucket.shape
    return pl.pallas_call(
        functools.partial(_bias_table_kernel, subtract_last=subtract_last),
        grid=(n,),
        in_specs=[pl.BlockSpec(memory_space=pltpu.SMEM),
                  pl.BlockSpec((1, r, c), lambda i: (i, 0, 0))],
        out_specs=pl.BlockSpec((n_heads, 1, r, c), lambda i: (0, i, 0, 0)),
        out_shape=jax.ShapeDtypeStruct((n_heads, n, r, c), F32),
        compiler_params=_params(1),
    )(table, bucket)


def _band_buckets_t(tile, window):
    n_back = -(-(window - 1) // tile)
    t = jnp.arange(n_back + 1)[:, None, None]
    key = jnp.arange(tile)[None, :, None]
    query = jnp.arange(tile)[None, None, :]
    dist = query + (n_back - t) * tile - key
    return jnp.where((dist >= 0) & (dist < window), _t5_bucket(dist), -1).astype(jnp.int32)


def _toeplitz_buckets_t(tile, n_tiles):
    m = jnp.arange(n_tiles)[:, None, None]
    key = jnp.arange(tile)[None, :, None]
    query = jnp.arange(tile)[None, None, :]
    dist = m * tile + query - key
    near = jnp.where(dist >= 0, _t5_bucket(dist), -1).astype(jnp.int32)
    return jnp.concatenate([near, jnp.full((1, tile, tile), ZERO_BUCKET, jnp.int32)])


def _cmp_buckets_t(s_len, n_rows):
    n_c = n_rows - 1
    tile = jnp.arange(s_len // SEL_TQ)[:, None, None]
    n = jnp.arange(n_rows)[None, :, None]
    t = tile * SEL_TQ + jnp.arange(SEL_TQ)[None, None, :]
    dist = t - (n * CMP_STRIDE + CMP_LEN - 1)
    return jnp.where((dist >= 0) & (n < n_c), _t5_bucket(dist), -1).astype(jnp.int32)


def _near_tiles(tile):
    max_exact = REL_BUCKETS // 2
    first_const = math.ceil(max_exact * (REL_MAX_DISTANCE / max_exact) ** ((max_exact - 1) / max_exact)) + 1
    m = 1
    while m * tile - (tile - 1) < first_const:
        m += 1
    return m


def _rms(x, gain):
    return x * lax.rsqrt(jnp.mean(x * x, axis=-1, keepdims=True) + RMS_EPS) * gain


def _in_proj_kernel(x_ref, mod_ref, gain_ref, w_ref, swa_ref, fox_ref, nsaq_ref, kc_ref, vc_ref, k2_ref,
                    misc_ref, vt_ref):
    x = x_ref[0]
    h = _rms(x, gain_ref[...]) * (1.0 + mod_ref[0, 1:2, :]) + mod_ref[0, 0:1, :]
    hb = h.astype(BF16)

    def seg(bounds):
        return jnp.dot(hb, w_ref[:, bounds[0]:bounds[1]], preferred_element_type=F32)

    swa_ref[0] = seg(SEG_SWA).astype(BF16)
    fox_ref[0] = seg(SEG_FOX).astype(BF16)
    nsaq_ref[0] = seg(SEG_NSAQ).astype(BF16)
    kc_ref[0] = seg(SEG_KC).astype(BF16)
    vc_ref[0] = seg(SEG_VC).astype(BF16)
    k2_ref[0] = seg(SEG_K2).astype(BF16)
    misc_ref[0] = seg(SEG_MISC)

    rows = x.shape[0]
    extra_row = lax.broadcasted_iota(jnp.int32, (VT_ROWS - HEAD_DIM, rows), 0)
    extra = jnp.where(extra_row == 0, 1.0, 0.0).astype(BF16)
    values = seg(SEG_V)
    for c in range(values.shape[1] // LANES):
        vt = values[:, c * LANES:(c + 1) * LANES].T.astype(BF16)
        for half in range(2):
            base = (2 * c + half) * VT_ROWS
            vt_ref[0, base:base + HEAD_DIM, :] = vt[half * HEAD_DIM:(half + 1) * HEAD_DIM, :]
            vt_ref[0, base + HEAD_DIM:base + VT_ROWS, :] = extra


def _in_proj(x, mod, gain, w):
    b, s, d = x.shape
    n = w.shape[1]
    widths = [hi - lo for lo, hi in (SEG_SWA, SEG_FOX, SEG_NSAQ, SEG_KC, SEG_VC, SEG_K2, SEG_MISC)]
    dtypes = [BF16] * 6 + [F32]
    vt_rows = (SEG_V[1] - SEG_V[0]) // HEAD_DIM * VT_ROWS
    return pl.pallas_call(
        _in_proj_kernel,
        grid=(b, s // ROW_TILE),
        in_specs=[pl.BlockSpec((1, ROW_TILE, d), lambda i, j: (i, j, 0)),
                  pl.BlockSpec((1, ADA_CHUNKS, d), lambda i, j: (i, 0, 0)),
                  pl.BlockSpec((1, d), lambda i, j: (0, 0)),
                  pl.BlockSpec((d, n), lambda i, j: (0, 0))],
        out_specs=[pl.BlockSpec((1, ROW_TILE, wd), lambda i, j: (i, j, 0)) for wd in widths]
        + [pl.BlockSpec((1, vt_rows, ROW_TILE), lambda i, j: (i, 0, j))],
        out_shape=[jax.ShapeDtypeStruct((b, s, wd), dt) for wd, dt in zip(widths, dtypes)]
        + [jax.ShapeDtypeStruct((b, vt_rows, s), BF16)],
        compiler_params=_params(2),
    )(x, mod, gain, w)


def _banded_kernel(*refs, n_back, n_groups, has_sink, t):
    if has_sink:
        sink_ref, q_ref, k_ref, vt_ref, bias_ref, o_ref = refs
    else:
        q_ref, k_ref, vt_ref, bias_ref, o_ref = refs
    i = pl.program_id(1)
    lo = _lane_lo((t, LANES))
    n_tiles = n_back + 1

    def run(all_valid):
        starts = [pl.multiple_of(jnp.maximum(i - n_back + tt, 0) * t, t) for tt in range(n_tiles)]
        k_tiles = [k_ref[0, pl.ds(start, t), :] for start in starts]

        def scores(g):
            qg = q_ref[0, :, g * LANES:(g + 1) * LANES]
            zero = jnp.zeros_like(qg)
            qms = (jnp.where(lo, qg, zero), jnp.where(lo, zero, qg))
            return [[bias_ref[2 * g + half, tt] + _dot_nt(k_tiles[tt], qms[half]) for tt in range(n_tiles)]
                    for half in range(2)]

        def softmax_pv(g, sts):
            pair = []
            for half in range(2):
                tiles = sts[half]
                if not all_valid:
                    tiles = [jnp.where(i - n_back + tt >= 0, st, NEG) if tt < n_back else st
                             for tt, st in enumerate(tiles)]
                m = None
                for st in tiles:
                    part = st.reshape(t // SUBLANES, SUBLANES, t).max(axis=0)
                    m = part if m is None else jnp.maximum(m, part)
                m = _all_sublanes(m, jnp.maximum)
                if has_sink:
                    sink = sink_ref[2 * g + half] * LOG2E
                    m = jnp.maximum(m, sink)
                acc = None
                for tt, st in enumerate(tiles):
                    p = jnp.exp2((st.reshape(t // SUBLANES, SUBLANES, t) - m[None]).reshape(t, t).astype(BF16))
                    part = jnp.dot(vt_ref[0, half * VT_ROWS:(half + 1) * VT_ROWS, pl.ds(starts[tt], t)], p,
                                   preferred_element_type=F32)
                    acc = part if acc is None else acc + part
                denom = _all_sublanes(acc[HEAD_DIM:HEAD_DIM + SUBLANES, :], jnp.add)
                if has_sink:
                    denom = denom + jnp.exp2(sink - m)
                out = acc[0:HEAD_DIM, :].reshape(HEAD_DIM // SUBLANES, SUBLANES, t) / denom[None]
                pair.append(out.reshape(HEAD_DIM, t))
            o_ref[0, :, g * LANES:(g + 1) * LANES] = jnp.concatenate(pair, axis=0).T.astype(o_ref.dtype)

        pending = scores(0)
        for g in range(n_groups):
            current = pending
            if g + 1 < n_groups:
                pending = scores(g + 1)
            softmax_pv(g, current)

    @pl.when(i >= n_back)
    def _():
        run(True)

    @pl.when(i < n_back)
    def _():
        run(False)


def _banded_attention(q_arr, k_arr, k_blk, vt, vt_blk, bias, sinks=None):
    b, s, _ = q_arr.shape
    n_pos, n_tiles, t = bias.shape[0], bias.shape[1], bias.shape[2]
    width = n_pos * HEAD_DIM
    in_specs = [pl.BlockSpec((1, t, width), lambda i, j: (i, j, 0)),
                pl.BlockSpec((1, s, LANES), lambda i, j: (i, 0, k_blk)),
                pl.BlockSpec((1, 2 * VT_ROWS, s), lambda i, j: (i, vt_blk, 0)),
                pl.BlockSpec(bias.shape, lambda i, j: (0, 0, 0, 0))]
    args = [q_arr, k_arr, vt, bias]
    if sinks is not None:
        in_specs = [pl.BlockSpec(memory_space=pltpu.SMEM)] + in_specs
        args = [sinks] + args
    return pl.pallas_call(
        functools.partial(_banded_kernel, n_back=n_tiles - 1, n_groups=n_pos // 2, has_sink=sinks is not None, t=t),
        grid=(b, s // t),
        in_specs=in_specs,
        out_specs=pl.BlockSpec((1, t, width), lambda i, j: (i, j, 0)),
        out_shape=jax.ShapeDtypeStruct((b, s, width), BF16),
        compiler_params=_params(2),
    )(*args)


def _all_sublanes(x, op):
    for shift in (4, 2, 1):
        x = op(x, pltpu.roll(x, shift, 0))
    return x


def _flash_init(m_scr, acc_scr):
    m_scr[...] = jnp.full(m_scr.shape, NEG, F32)
    acc_scr[...] = jnp.zeros(acc_scr.shape, F32)


def _flash_update(h, st_ref, tile_max, vt_h, m_scr, acc_scr):
    tk, tq = st_ref.shape
    m_prev = m_scr[h]
    m_new = _all_sublanes(jnp.maximum(m_prev, tile_max), jnp.maximum)
    alpha = jnp.exp2(m_prev - m_new)
    p = jnp.exp2((st_ref[...].reshape(tk // SUBLANES, SUBLANES, tq) - m_new[None]).reshape(tk, tq).astype(BF16))
    acc = acc_scr[h].reshape(VT_ROWS // SUBLANES, SUBLANES, tq) * alpha[None]
    acc_scr[h] = acc.reshape(VT_ROWS, tq) + jnp.dot(vt_h, p, preferred_element_type=F32)
    m_scr[h] = m_new


def _flash_finish(o_ref, n_groups, acc_scr):
    tq = acc_scr.shape[2]
    for g in range(n_groups):
        pair = []
        for h in (2 * g, 2 * g + 1):
            denom = _all_sublanes(acc_scr[h, HEAD_DIM:HEAD_DIM + SUBLANES, :], jnp.add)
            out = acc_scr[h, 0:HEAD_DIM, :].reshape(HEAD_DIM // SUBLANES, SUBLANES, tq) / denom[None]
            pair.append(out.reshape(HEAD_DIM, tq))
        o_ref[0, :, g * LANES:(g + 1) * LANES] = jnp.concatenate(pair, axis=0).T.astype(o_ref.dtype)


def _flash_pipeline(i, n_heads, qk_scores, bias_tile, vt_slab, s_scr, tmax_scr, m_scr, acc_scr):
    def qk_head(thunk, h, j, slot):
        st = thunk() + bias_tile(h, j)
        s_scr[slot, h] = st
        tmax_scr[slot, h] = st.reshape(st.shape[0] // SUBLANES, SUBLANES, st.shape[1]).max(axis=0)

    def softmax_head(h, j, slot):
        _flash_update(h, s_scr.at[slot, h], tmax_scr[slot, h], vt_slab(h, j), m_scr, acc_scr)

    def softmax_all(j, slot):
        for h in range(n_heads):
            softmax_head(h, j, slot)

    def stage(j_qk, slot_qk, j_sm, slot_sm):
        thunks = qk_scores(j_qk)
        for h in range(n_heads):
            qk_head(thunks[h], h, j_qk, slot_qk)
            softmax_head(h, j_sm, slot_sm)

    for h, thunk in enumerate(qk_scores(0)):
        qk_head(thunk, h, 0, 0)

    def body(trip, carry):
        j = 2 * trip
        stage(j + 1, 1, j, 0)
        stage(j + 2, 0, j + 1, 1)
        return carry

    lax.fori_loop(0, i // 2, body, 0)
    last = 2 * (i // 2)

    @pl.when(i % 2 == 0)
    def _():
        softmax_all(last, 0)

    @pl.when(i % 2 == 1)
    def _():
        stage(last + 1, 1, last, 0)
        softmax_all(last + 1, 1)


def _fox_aug_kernel(misc_ref, fbias_ref, tri_ref, o_ref):
    s_len, width = misc_ref.shape[1], misc_ref.shape[2]
    term = lax.broadcasted_iota(jnp.int32, (LANES, width), 1) % HEAD_DIM
    carry = jnp.zeros((1, width), F32)
    for c in range(s_len // LANES):
        z = misc_ref[0, c * LANES:(c + 1) * LANES, :] + fbias_ref[...]
        log_f = jnp.minimum(z, 0.0) - jnp.log1p(jnp.exp(-jnp.abs(z)))
        cum = jnp.dot(tri_ref[...], log_f, precision=HIGHEST, preferred_element_type=F32) + carry
        carry = cum[LANES - 1:LANES, :]
        x = cum * (-LOG2E)
        hi = x.astype(BF16).astype(F32)
        rest = x - hi
        mid = rest.astype(BF16).astype(F32)
        low = rest - mid
        out = jnp.where(term == 0, hi, jnp.where(term == 1, mid, jnp.where(term == 2, low, 0.0)))
        o_ref[0, c * LANES:(c + 1) * LANES, :] = out.astype(BF16)


def _fox_key_terms(misc, fbias):
    b, s, width = misc.shape
    tri = jnp.asarray(np.tril(np.ones((LANES, LANES), np.float32)))
    return pl.pallas_call(
        _fox_aug_kernel,
        grid=(b,),
        in_specs=[pl.BlockSpec((1, s, width), lambda i: (i, 0, 0)),
                  pl.BlockSpec((1, width), lambda i: (0, 0)),
                  pl.BlockSpec((LANES, LANES), lambda i: (0, 0))],
        out_specs=pl.BlockSpec((1, s, width), lambda i: (i, 0, 0)),
        out_shape=jax.ShapeDtypeStruct((b, s, width), BF16),
        compiler_params=_params(1),
    )(misc, fbias, tri)


def _fox_kernel(q_ref, k_ref, aug_ref, vt_ref, mask_ref, o_ref, qs_scr, s_scr, tmax_scr, m_scr, acc_scr):
    t = FLASH_T
    i = pl.program_id(1)
    lo = _lane_lo((t, LANES))
    lane = lax.broadcasted_iota(jnp.int32, (t, LANES), 1)
    ones = jnp.where(lane % HEAD_DIM < KEY_BIAS_TERMS, 1.0, 0.0).astype(BF16)
    n_groups = FOX_HEADS // 2
    for g in range(n_groups):
        qg = q_ref[0, :, g * LANES:(g + 1) * LANES]
        qs_scr[2 * g] = jnp.where(lo, qg, ones)
        qs_scr[2 * g + 1] = jnp.where(lo, ones, qg)
    _flash_init(m_scr, acc_scr)

    def qk_scores(j):
        start = pl.multiple_of(j * t, t)
        scores = []
        for g in range(n_groups):
            k_tile = k_ref[0, pl.ds(start, t), g * LANES:(g + 1) * LANES]
            a_tile = aug_ref[0, pl.ds(start, t), g * LANES:(g + 1) * LANES]
            k_sel = (jnp.where(lo, k_tile, a_tile), jnp.where(lo, a_tile, k_tile))
            for half in range(2):
                scores.append(functools.partial(lambda k, h: _dot_nt(k, qs_scr[h]), k_sel[half], 2 * g + half))
        return scores

    def bias_tile(h, j):
        return mask_ref[jnp.minimum(i - j, 1)]

    def vt_slab(h, j):
        return vt_ref[0, h * VT_ROWS:(h + 1) * VT_ROWS, pl.ds(pl.multiple_of(j * t, t), t)]

    _flash_pipeline(i, FOX_HEADS, qk_scores, bias_tile, vt_slab, s_scr, tmax_scr, m_scr, acc_scr)
    _flash_finish(o_ref, n_groups, acc_scr)


def _fox_attention(fox_qk, key_terms, vt):
    b, s, _ = fox_qk.shape
    width = FOX_HEADS * HEAD_DIM
    t = FLASH_T
    idx = np.arange(t)
    diag = np.where(idx[:, None] <= idx[None, :], 0.0, NEG)
    masks = jnp.asarray(np.stack([diag, np.zeros((t, t))]).astype(np.float32))
    return pl.pallas_call(
        _fox_kernel,
        grid=(b, s // t),
        in_specs=[pl.BlockSpec((1, t, width), lambda i, j: (i, j, 0)),
                  pl.BlockSpec((1, s, width), lambda i, j: (i, 0, 1)),
                  pl.BlockSpec((1, s, width), lambda i, j: (i, 0, 0)),
                  pl.BlockSpec((1, FOX_HEADS * VT_ROWS, s), lambda i, j: (i, VT_FOX_BLOCK, 0)),
                  pl.BlockSpec(masks.shape, lambda i, j: (0, 0, 0))],
        out_specs=pl.BlockSpec((1, t, width), lambda i, j: (i, j, 0)),
        out_shape=jax.ShapeDtypeStruct((b, s, width), BF16),
        scratch_shapes=[pltpu.VMEM((FOX_HEADS, t, LANES), BF16),
                        pltpu.VMEM((2, FOX_HEADS, t, t), F32),
                        pltpu.VMEM((2, FOX_HEADS, SUBLANES, t), F32),
                        pltpu.VMEM((FOX_HEADS, SUBLANES, t), F32),
                        pltpu.VMEM((FOX_HEADS, VT_ROWS, t), F32)],
        compiler_params=_params(2),
    )(fox_qk, fox_qk, key_terms, vt, masks)


def _compress_kernel(x_ref, pe_ref, w1a_ref, w1b_ref, w2_ref, o_ref):
    x = x_ref[0].astype(F32)
    n_rows = x.shape[0]

    def mm(a, w):
        return jnp.dot(a, w, precision=HIGHEST, preferred_element_type=F32)

    first = mm(x, w1a_ref[0])
    second = mm(x, w1b_ref[0])
    pe_term = (mm(pe_ref[0, 0], w1a_ref[0]) + mm(pe_ref[0, 1], w1b_ref[0]))[0:1, :]
    pre = first + pltpu.roll(second, n_rows - 1, 0) + pe_term
    hid = 0.5 * pre * (1.0 + jnp.tanh(math.sqrt(2.0 / math.pi) * (pre + 0.044715 * (pre * pre * pre))))
    o_ref[0, 0] = mm(hid, w2_ref[0])


def _compress(kc, vc, cmp_pos, cmp_w1, cmp_w2):
    b, s, _ = kc.shape
    n_rows = s // CMP_STRIDE
    half = CMP_LEN // 2
    feat = CMP_STRIDE * LANES
    x = jnp.stack([kc.reshape(b, n_rows, feat), vc.reshape(b, n_rows, feat)])
    eye = jnp.eye(2, dtype=F32)
    w1 = cmp_w1.astype(F32).reshape(2, CMP_LEN, HEAD_DIM, CMP_HIDDEN)
    w1a = jnp.einsum('wldj,hg->wlhdgj', w1[:, :half], eye).reshape(2, feat, 2 * CMP_HIDDEN)
    w1b = jnp.einsum('wldj,hg->wlhdgj', w1[:, half:], eye).reshape(2, feat, 2 * CMP_HIDDEN)
    w2 = jnp.einsum('wjd,hg->whjgd', cmp_w2.astype(F32), eye).reshape(2, 2 * CMP_HIDDEN, LANES)
    w2 = jnp.concatenate([w2, jnp.roll(w2, HEAD_DIM, axis=2)], axis=2)
    pe = jnp.broadcast_to(cmp_pos.astype(F32).reshape(2, 2, half, 1, HEAD_DIM), (2, 2, half, 2, HEAD_DIM))
    pe = jnp.broadcast_to(pe.reshape(2, 2, 1, feat), (2, 2, 8, feat))
    return pl.pallas_call(
        _compress_kernel,
        grid=(2, b),
        in_specs=[pl.BlockSpec((None, 1, n_rows, feat), lambda w, i: (w, i, 0, 0)),
                  pl.BlockSpec((1, 2, 8, feat), lambda w, i: (w, 0, 0, 0)),
                  pl.BlockSpec((1, feat, 2 * CMP_HIDDEN), lambda w, i: (w, 0, 0)),
                  pl.BlockSpec((1, feat, 2 * CMP_HIDDEN), lambda w, i: (w, 0, 0)),
                  pl.BlockSpec((1, 2 * CMP_HIDDEN, 2 * LANES), lambda w, i: (w, 0, 0))],
        out_specs=pl.BlockSpec((1, 1, n_rows, 2 * LANES), lambda w, i: (w, i, 0, 0)),
        out_shape=jax.ShapeDtypeStruct((2, b, n_rows, 2 * LANES), F32),
        compiler_params=_params(2),
    )(x, pe, w1a, w1b, w2)


def _select_kernel(q_ref, kc_ref, vct_ref, bias_ref, o_ref, mb_ref, count_scr, psum_scr):
    tq = SEL_TQ
    i = pl.program_id(0)
    lo = _lane_lo((tq, LANES))
    n_rows = kc_ref.shape[2]
    lo_k = _lane_lo((n_rows, LANES))
    n_grp = n_rows // SUBLANES

    k_own = kc_ref[0, 0, :, 0:LANES]
    k_swapped = kc_ref[0, 0, :, LANES:2 * LANES]
    hi = k_own.astype(BF16)
    low = (k_swapped - k_swapped.astype(BF16).astype(F32)).astype(BF16)
    k_sel = (jnp.where(lo_k, hi, low), jnp.where(lo_k, low, hi))

    def scores(g):
        qg = q_ref[0, :, g * LANES:(g + 1) * LANES]
        swapped = pltpu.roll(qg.astype(F32), HEAD_DIM, 1).astype(BF16)
        q_dup = (jnp.where(lo, qg, swapped), jnp.where(lo, swapped, qg))
        return [bias_ref[2 * g + half, 0] + _dot_nt(k_sel[half], q_dup[half]) for half in range(2)]

    query = i * tq + lax.broadcasted_iota(jnp.int32, (SUBLANES, tq), 1)
    has_keys = query >= CMP_LEN - 1
    p_sum = [None, None]

    def softmax_pv(g, sts):
        pair = []
        for half in range(2):
            s3 = sts[half].reshape(n_grp, SUBLANES, tq)
            m = _all_sublanes(s3.max(axis=0), jnp.maximum)
            e = jnp.exp2(s3 - m[None])
            inv = jnp.where(has_keys, 1.0 / _all_sublanes(e.sum(axis=0), jnp.add), 0.0)
            p = e * inv[None]
            p_sum[half] = p if p_sum[half] is None else p_sum[half] + p
            pair.append(jnp.dot(vct_ref[0, half * HEAD_DIM:(half + 1) * HEAD_DIM, :],
                                p.reshape(n_rows, tq).astype(BF16), preferred_element_type=F32))
        o_ref[0, :, g * LANES:(g + 1) * LANES] = jnp.concatenate(pair, axis=0).T.astype(o_ref.dtype)

    n_groups = NSA_HEADS // 2
    pending = scores(0)
    for g in range(n_groups):
        current = pending
        if g + 1 < n_groups:
            pending = scores(g + 1)
        softmax_pv(g, current)

    n_blk = HEAD_DIM
    blk_grp = n_blk // SUBLANES
    sub = lax.broadcasted_iota(jnp.int32, (SUBLANES, tq), 0)
    q_blk = query // SLC_BLOCK
    kind = []
    for r in range(blk_grp):
        blk = sub + r * SUBLANES
        behind = q_blk - blk
        forced = jnp.where(blk == 0, 1, 0) + jnp.where(behind == 0, 1, 0) + jnp.where(behind == 1, 1, 0)
        kind.append(jnp.where(behind < 0, 2, jnp.minimum(forced, 1)))
    masks = []
    per_blk = SLC_BLOCK // CMP_STRIDE
    n_real = n_rows // per_blk
    n_lane_chunks = tq // LANES
    psum_scr[:, 0:SUBLANES, :] = jnp.zeros((n_lane_chunks, SUBLANES, LANES), F32)
    for half in (1, 0):
        p_rows = p_sum[half].reshape(n_rows, tq)
        for c in range(n_lane_chunks):
            psum_scr[c, SUBLANES:SUBLANES + n_rows, :] = p_rows[:, c * LANES:(c + 1) * LANES]

        def every_fourth(offset):
            return jnp.concatenate([psum_scr[c, pl.ds(SUBLANES + offset, n_real, stride=per_blk), :]
                                    for c in range(n_lane_chunks)], axis=1)

        imp = (0.5 * (every_fourth(-1) + every_fourth(3))
               + (every_fourth(0) + every_fourth(1) + every_fourth(2)))
        if n_real < n_blk:
            imp = jnp.concatenate([imp, jnp.zeros((n_blk - n_real, tq), F32)], axis=0)
        rows = [jnp.where(kind[r] == 2, NEG, jnp.where(kind[r] == 1, FORCE, imp[r * SUBLANES:(r + 1) * SUBLANES, :]))
                for r in range(blk_grp)]
        count_scr[...] = jnp.zeros(count_scr.shape, jnp.int32)
        for r_other in range(blk_grp):
            @pl.when(r_other * SUBLANES * SLC_BLOCK < (i + 1) * tq)
            def _(r_other=r_other, rows=rows):
                counts = [None] * blk_grp
                for s_other in range(SUBLANES):
                    row = jnp.broadcast_to(rows[r_other][s_other:s_other + 1, :], (SUBLANES, tq))
                    for r in range(blk_grp):
                        if r > r_other:
                            beats = jnp.where(row >= rows[r], 1, 0)
                        elif r < r_other:
                            beats = jnp.where(row > rows[r], 1, 0)
                        else:
                            beats = jnp.where(sub > s_other, jnp.where(row >= rows[r], 1, 0),
                                              jnp.where(row > rows[r], 1, 0))
                        counts[r] = beats if counts[r] is None else counts[r] + beats
                for r in range(blk_grp):
                    count_scr[r] = count_scr[r] + counts[r]
        masks.extend(jnp.where(count_scr[r] < TOPK, 0.0, NEG) for r in range(blk_grp))
    for c in range(tq // LANES):
        mb_ref[0, c * LANES:(c + 1) * LANES, :] = jnp.concatenate(
            [mk[:, c * LANES:(c + 1) * LANES] for mk in masks], axis=0).T.astype(BF16)


def _select(nsa_q, cmp_kv, bias_c):
    b, s, width = nsa_q.shape
    n_rows = cmp_kv.shape[2]
    n_blk = HEAD_DIM
    assert CMP_LEN == 2 * CMP_STRIDE and SLC_BLOCK == 4 * CMP_STRIDE
    tq = SEL_TQ
    vct = cmp_kv[1, :, :, 0:LANES].transpose(0, 2, 1).astype(BF16)
    return pl.pallas_call(
        _select_kernel,
        grid=(s // tq, b),
        in_specs=[pl.BlockSpec((1, tq, width), lambda j, i: (i, j, 0)),
                  pl.BlockSpec((1, 1, n_rows, 2 * LANES), lambda j, i: (0, i, 0, 0)),
                  pl.BlockSpec((1, LANES, n_rows), lambda j, i: (i, 0, 0)),
                  pl.BlockSpec((NSA_HEADS, 1, n_rows, tq), lambda j, i: (0, j, 0, 0))],
        out_specs=[pl.BlockSpec((1, tq, width), lambda j, i: (i, j, 0)),
                   pl.BlockSpec((1, tq, LANES), lambda j, i: (i, j, 0))],
        out_shape=[jax.ShapeDtypeStruct((b, s, width), BF16),
                   jax.ShapeDtypeStruct((b, s, LANES), BF16)],
        scratch_shapes=[pltpu.VMEM((n_blk // SUBLANES, SUBLANES, tq), jnp.int32),
                        pltpu.VMEM((tq // LANES, SUBLANES + n_rows, LANES), F32)],
        compiler_params=_params(2),
    )(nsa_q, cmp_kv, vct, bias_c)


def _slc_kernel(q_ref, mb_ref, k_ref, e2_ref, vt_ref, bias_ref, o_ref, qs_scr, s_scr, tmax_scr, m_scr, acc_scr, *,
                n_near):
    t = FLASH_T
    i = pl.program_id(1)
    lo = _lane_lo((t, LANES))
    n_groups = NSA_HEADS // 2
    mb = mb_ref[0]
    for g in range(n_groups):
        qg = q_ref[0, :, g * LANES:(g + 1) * LANES]
        qs_scr[2 * g] = jnp.where(lo, qg, mb)
        qs_scr[2 * g + 1] = jnp.where(lo, mb, qg)
    _flash_init(m_scr, acc_scr)

    def qk_scores(j):
        start = pl.multiple_of(j * t, t)
        k_tile = k_ref[0, pl.ds(start, t), :]
        e_tile = e2_ref[pl.ds(start, t), :]
        k_sel = (jnp.where(lo, k_tile, e_tile), jnp.where(lo, e_tile, k_tile))
        return [functools.partial(lambda k, pos: _dot_nt(k, qs_scr[pos]), k_sel[pos % 2], pos)
                for pos in range(NSA_HEADS)]

    def bias_tile(pos, j):
        return bias_ref[pos, jnp.minimum(i - j, n_near)]

    def vt_slab(pos, j):
        kv = pos % 2
        return vt_ref[0, kv * VT_ROWS:(kv + 1) * VT_ROWS, pl.ds(pl.multiple_of(j * t, t), t)]

    _flash_pipeline(i, NSA_HEADS, qk_scores, bias_tile, vt_slab, s_scr, tmax_scr, m_scr, acc_scr)
    _flash_finish(o_ref, n_groups, acc_scr)


def _slc_attention(nsa_q, mask_bias, k2, e2, vt, bias):
    b, s, width = nsa_q.shape
    t = FLASH_T
    n_near = bias.shape[1] - 1
    return pl.pallas_call(
        functools.partial(_slc_kernel, n_near=n_near),
        grid=(b, s // t),
        in_specs=[pl.BlockSpec((1, t, width), lambda i, j: (i, j, 0)),
                  pl.BlockSpec((1, t, LANES), lambda i, j: (i, j, 0)),
                  pl.BlockSpec((1, s, LANES), lambda i, j: (i, 0, 0)),
                  pl.BlockSpec((s, LANES), lambda i, j: (0, 0)),
                  pl.BlockSpec((1, 2 * VT_ROWS, s), lambda i, j: (i, VT_SLC_BLOCK, 0)),
                  pl.BlockSpec(bias.shape, lambda i, j: (0, 0, 0, 0))],
        out_specs=pl.BlockSpec((1, t, width), lambda i, j: (i, j, 0)),
        out_shape=jax.ShapeDtypeStruct((b, s, width), BF16),
        scratch_shapes=[pltpu.VMEM((NSA_HEADS, t, LANES), BF16),
                        pltpu.VMEM((2, NSA_HEADS, t, t), F32),
                        pltpu.VMEM((2, NSA_HEADS, SUBLANES, t), F32),
                        pltpu.VMEM((NSA_HEADS, SUBLANES, t), F32),
                        pltpu.VMEM((NSA_HEADS, VT_ROWS, t), F32)],
        compiler_params=_params(2),
    )(nsa_q, mask_bias, k2, e2, vt, bias)


def _block_onehot(s_len):
    blk = np.arange(s_len)[:, None] // SLC_BLOCK
    lane = np.arange(LANES)[None, :] % HEAD_DIM
    return jnp.asarray((blk == lane).astype(np.float32), dtype=BF16)


def _mix_ffn_kernel(x_ref, mod_ref, swa_ref, fox_ref, cmp_ref, slc_ref, win_ref, misc_ref, expand_ref, gn_ref,
                    w_ref, post_ref, pre_ref, wg_ref, wu_ref, wd_ref, fpost_ref, o_ref):
    n_swa = SWA_HEADS * HEAD_DIM
    n_fox = FOX_HEADS * HEAD_DIM
    n_nsa = NSA_HEADS * HEAD_DIM
    gate = jax.nn.sigmoid(misc_ref[0])
    gate_hi = gate.astype(BF16)
    gate_lo = (gate - gate_hi.astype(F32)).astype(BF16)
    gates = (jnp.dot(gate_hi, expand_ref[...], preferred_element_type=F32)
             + jnp.dot(gate_lo, expand_ref[...], preferred_element_type=F32))
    o_nsa = (gates[:, 0:n_nsa] * cmp_ref[0].astype(F32) + gates[:, n_nsa:2 * n_nsa] * slc_ref[0].astype(F32)
             + gates[:, 2 * n_nsa:3 * n_nsa] * win_ref[0].astype(F32))
    a = _rms(swa_ref[0].astype(F32), gn_ref[:, 0:n_swa]).astype(BF16)
    b = _rms(fox_ref[0].astype(F32), gn_ref[:, n_swa:n_swa + n_fox]).astype(BF16)
    c = _rms(o_nsa, gn_ref[:, n_swa + n_fox:]).astype(BF16)
    y = (jnp.dot(a, w_ref[0:n_swa, :], preferred_element_type=F32)
         + jnp.dot(b, w_ref[n_swa:n_swa + n_fox, :], preferred_element_type=F32)
         + jnp.dot(c, w_ref[n_swa + n_fox:, :], preferred_element_type=F32))
    x = x_ref[0] + mod_ref[0, 2:3, :] * _rms(y, post_ref[...])

    h = (_rms(x, pre_ref[...]) * (1.0 + mod_ref[0, 4:5, :]) + mod_ref[0, 3:4, :]).astype(BF16)
    y = jnp.zeros(x.shape, F32)
    for chunk in range(wg_ref.shape[0]):
        gate = jnp.dot(h, wg_ref[chunk], preferred_element_type=F32)
        up = jnp.dot(h, wu_ref[chunk], preferred_element_type=F32)
        act = (gate * jax.nn.sigmoid(gate) * up).astype(BF16)
        y = y + jnp.dot(act, wd_ref[chunk], preferred_element_type=F32)
    o_ref[0] = x + mod_ref[0, 5:6, :] * _rms(y, fpost_ref[...])


def _gate_expansion():
    expand = np.zeros((LANES, 3 * NSA_HEADS * HEAD_DIM), np.float32)
    for branch in range(3):
        for p in range(NSA_HEADS):
            col = (branch * NSA_HEADS + p) * HEAD_DIM
            expand[GATE_LANE + 8 * branch + p, col:col + HEAD_DIM] = 1.0
    return jnp.asarray(expand, dtype=BF16)


def _mix_ffn(x, mod, o_swa, o_fox, o_cmp, o_slc, o_win, misc, gn, w, post, pre, wg, wu, wd, fpost):
    b, s, d = x.shape
    expand = _gate_expansion()
    n_chunks = wg.shape[0]

    def rows(width):
        return pl.BlockSpec((1, ROW_TILE, width), lambda i, j: (i, j, 0))

    def whole(shape):
        return pl.BlockSpec(shape, lambda i, j: (0,) * len(shape), pipeline_mode=pl.Buffered(1))

    vec = pl.BlockSpec((1, d), lambda i, j: (0, 0))
    return pl.pallas_call(
        _mix_ffn_kernel,
        grid=(b, s // ROW_TILE),
        in_specs=[rows(d),
                  pl.BlockSpec((1, ADA_CHUNKS, d), lambda i, j: (i, 0, 0)),
                  rows(o_swa.shape[2]), rows(o_fox.shape[2]), rows(o_cmp.shape[2]), rows(o_slc.shape[2]),
                  rows(o_win.shape[2]), rows(LANES),
                  whole(expand.shape), vec, whole((d, d)), vec, vec,
                  whole((n_chunks, d, FFN_CHUNK)), whole((n_chunks, d, FFN_CHUNK)), whole((n_chunks, FFN_CHUNK, d)),
                  vec],
        out_specs=rows(d),
        out_shape=jax.ShapeDtypeStruct((b, s, d), F32),
        compiler_params=_params(2),
    )(x, mod, o_swa, o_fox, o_cmp, o_slc, o_win, misc, expand, gn, w, post, pre, wg, wu, wd, fpost)


def _forget_lanes():
    lanes, heads = [], []
    for h in range(FOX_HEADS):
        base = (h // 2) * LANES + (HEAD_DIM if h % 2 == 0 else 0)
        for j in range(KEY_BIAS_TERMS):
            lanes.append(base + j)
            heads.append(h)
    return np.array(lanes), np.array(heads)


def _in_proj_layout():
    d = HEAD_DIM
    o_qa, o_ka, o_va, o_qb, o_kb, o_vb, o_fb, o_qc = 0, 256, 384, 512, 768, 1024, 1280, 1284
    o_kc, o_vc, o_ksl, o_vsl, o_kw, o_vw, o_gc = 1796, 1924, 2052, 2180, 2308, 2436, 2564
    scale = LOG2E / math.sqrt(d)

    def head_cols(base, heads):
        return np.concatenate([np.arange(base + h * d, base + (h + 1) * d) for h in heads])

    def span(base, width):
        return np.arange(base, base + width)

    cols = [head_cols(o_qa, SWA_POS), span(o_ka, 128),
            span(o_qb, 256), span(o_kb, 256),
            head_cols(o_qc, NSA_POS),
            span(o_kc, 128), span(o_vc, 128),
            span(o_ksl, 128), span(o_kw, 128),
            span(o_vb, 256), span(o_va, 128), span(o_vsl, 128), span(o_vw, 128)]
    scales = [np.full(256, scale), np.ones(128), np.full(256, scale), np.ones(256), np.full(512, scale),
              np.ones(256), np.ones(256), np.ones(640)]
    lanes, heads = _forget_lanes()
    misc_cols = np.zeros(SEG_MISC[1] - SEG_MISC[0], np.int64)
    misc_scale = np.zeros(SEG_MISC[1] - SEG_MISC[0])
    misc_cols[lanes] = o_fb + heads
    misc_scale[lanes] = 1.0
    for branch in range(3):
        for p, h in enumerate(NSA_POS):
            misc_cols[GATE_LANE + 8 * branch + p] = o_gc + h * 3 + branch
            misc_scale[GATE_LANE + 8 * branch + p] = 1.0
    cols.append(misc_cols)
    scales.append(misc_scale)
    return np.concatenate(cols), np.concatenate(scales).astype(np.float32)


def _head_perm(pos):
    return np.concatenate([np.arange(h * HEAD_DIM, (h + 1) * HEAD_DIM) for h in pos])


def kernel(x, c, rel_bias, ada_w, ada_b, attn_pre_norm, attn_post_norm, ffn_pre_norm, ffn_post_norm, w_in,
           forget_bias, swa_sinks, cmp_pos, cmp_w1, cmp_w2, group_norm, w_out, ffn_w_gate, ffn_w_up, ffn_w_down):
    b, s, d = x.shape
    depth = w_in.shape[0]
    hidden = ffn_w_gate.shape[2]
    assert s % (2 * FLASH_T) == 0 and s // SLC_BLOCK <= HEAD_DIM and hidden % FFN_CHUNK == 0

    cols, scales = _in_proj_layout()
    w_all = (w_in[:, :, cols] * scales).astype(BF16)
    lanes, heads = _forget_lanes()
    fbias_all = jnp.zeros((depth, 1, SEG_MISC[1] - SEG_MISC[0]), F32).at[:, 0, lanes].set(
        forget_bias[:, heads].astype(F32))
    swa_perm = _head_perm(SWA_POS)
    nsa_perm = _head_perm(NSA_POS)
    n_swa, n_fox = SWA_HEADS * HEAD_DIM, FOX_HEADS * HEAD_DIM
    mix_perm = np.concatenate([swa_perm, n_swa + np.arange(n_fox), n_swa + n_fox + nsa_perm])
    gn_all = group_norm[:, mix_perm].astype(F32)
    w_out_all = w_out[:, mix_perm, :].astype(BF16)
    n_chunks = hidden // FFN_CHUNK
    wg_all = ffn_w_gate.reshape(depth, d, n_chunks, FFN_CHUNK).transpose(0, 2, 1, 3).astype(BF16)
    wu_all = ffn_w_up.reshape(depth, d, n_chunks, FFN_CHUNK).transpose(0, 2, 1, 3).astype(BF16)
    wd_all = ffn_w_down.reshape(depth, n_chunks, FFN_CHUNK, d).astype(BF16)

    tab_swa = rel_bias[:, np.array(SWA_POS)].astype(F32)
    tab_nsa = rel_bias[:, SWA_HEADS + np.array(NSA_POS)].astype(F32)
    bias_swa = _bias_table(tab_swa, _band_buckets_t(SWA_TILE, SWA_WINDOW))
    bias_win = _bias_table(tab_nsa, _band_buckets_t(WIN_TILE, NSA_WINDOW))
    bias_slc = _bias_table(tab_nsa, _toeplitz_buckets_t(FLASH_T, _near_tiles(FLASH_T)), subtract_last=True)
    n_rows = s // CMP_STRIDE
    bias_cmp = _bias_table(tab_nsa, _cmp_buckets_t(s, n_rows))
    e2 = _block_onehot(s)

    mod_all = _adaln(c.astype(F32), ada_w.astype(F32), ada_b.astype(F32)).reshape(depth, b, ADA_CHUNKS, d)

    for layer in range(depth):
        mod = mod_all[layer]
        swa_qk, fox_qk, nsa_q, kc, vc, k2, misc, vt = _in_proj(
            x, mod, attn_pre_norm[layer].reshape(1, d).astype(F32), w_all[layer])
        o_swa = _banded_attention(swa_qk, swa_qk, 2, vt, VT_SWA_BLOCK, bias_swa,
                                  sinks=swa_sinks[layer][np.array(SWA_POS)].astype(F32))
        o_fox = _fox_attention(fox_qk, _fox_key_terms(misc, fbias_all[layer]), vt)
        cmp_kv = _compress(kc, vc, cmp_pos[layer], cmp_w1[layer], cmp_w2[layer])
        o_cmp, mask_bias = _select(nsa_q, cmp_kv, bias_cmp)
        o_slc = _slc_attention(nsa_q, mask_bias, k2, e2, vt, bias_slc)
        o_win = _banded_attention(nsa_q, k2, 1, vt, VT_WIN_BLOCK, bias_win)
        x = _mix_ffn(x, mod, o_swa, o_fox, o_cmp, o_slc, o_win, misc, gn_all[layer].reshape(1, d),
                     w_out_all[layer], attn_post_norm[layer].reshape(1, d).astype(F32),
                     ffn_pre_norm[layer].reshape(1, d).astype(F32), wg_all[layer], wu_all[layer], wd_all[layer],
                     ffn_post_norm[layer].reshape(1, d).astype(F32))
    return x
```

```python
import functools
import math

import numpy as np
import jax
import jax.numpy as jnp
from jax import lax
from jax.experimental import pallas as pl
from jax.experimental.pallas import tpu as pltpu

F32 = jnp.float32
BF16 = jnp.bfloat16
HIGHEST = lax.Precision.HIGHEST

LANES = 128
SUBLANES = 8
VMEM_LIMIT = 56 * 1024 * 1024

HEAD_DIM = 64
SWA_HEADS = 4
SWA_WINDOW = 128
FOX_HEADS = 4
NSA_HEADS = 8
CMP_LEN = 32
CMP_STRIDE = 16
CMP_HIDDEN = 2 * HEAD_DIM
SLC_BLOCK = 64
TOPK = 16
NSA_WINDOW = 512
REL_BUCKETS = 32
REL_MAX_DISTANCE = 1024
ZERO_BUCKET = -2
RMS_EPS = 1e-6
NEG = -1e30
FORCE = 1e30
ADA_CHUNKS = 6
LOG2E = math.log2(math.e)

SWA_POS = (0, 2, 1, 3)
NSA_POS = (0, 4, 1, 5, 2, 6, 3, 7)

SWA_TILE = 256
WIN_TILE = 256
FLASH_T = 256
SEL_TQ = 256
ROW_TILE = 512
FFN_CHUNK = 256
VT_ROWS = HEAD_DIM + 16
KEY_BIAS_TERMS = 3

SEG_SWA = (0, 384)
SEG_FOX = (384, 896)
SEG_NSAQ = (896, 1408)
SEG_KC = (1408, 1536)
SEG_VC = (1536, 1664)
SEG_K2 = (1664, 1920)
SEG_V = (1920, 2560)
SEG_MISC = (2560, 2816)
GATE_LANE = 8
VT_FOX_BLOCK, VT_SWA_BLOCK, VT_SLC_BLOCK, VT_WIN_BLOCK = 0, 2, 3, 4


def _params(n_grid, vmem=VMEM_LIMIT):
    return pltpu.CompilerParams(dimension_semantics=("parallel",) * n_grid, vmem_limit_bytes=vmem)


def _dot_nt(a, b):
    return lax.dot_general(a, b, (((1,), (1,)), ((), ())), preferred_element_type=F32)


def _lane_lo(shape):
    return lax.broadcasted_iota(jnp.int32, shape, len(shape) - 1) < HEAD_DIM


def _adaln_kernel(c_ref, w_ref, b_ref, o_ref):
    c = c_ref[...]
    act = c * jax.nn.sigmoid(c)
    o_ref[0] = jnp.dot(act, w_ref[0], precision=HIGHEST, preferred_element_type=F32) + b_ref[0]


def _adaln(c, ada_w, ada_b):
    depth, d, n = ada_w.shape
    b = c.shape[0]
    return pl.pallas_call(
        _adaln_kernel,
        grid=(depth, n // d),
        in_specs=[pl.BlockSpec((b, d), lambda l, j: (0, 0)),
                  pl.BlockSpec((1, d, d), lambda l, j: (l, 0, j)),
                  pl.BlockSpec((1, 1, d), lambda l, j: (l, 0, j))],
        out_specs=pl.BlockSpec((1, b, d), lambda l, j: (l, 0, j)),
        out_shape=jax.ShapeDtypeStruct((depth, b, n), F32),
        compiler_params=_params(2),
    )(c, ada_w, ada_b.reshape(depth, 1, n))


def _t5_bucket(dist):
    n = jnp.maximum(dist, 0)
    max_exact = REL_BUCKETS // 2
    nf = jnp.maximum(n, 1).astype(jnp.float32)
    large = max_exact + (jnp.log(nf / max_exact) / math.log(REL_MAX_DISTANCE / max_exact)
                         * (REL_BUCKETS - max_exact)).astype(jnp.int32)
    large = jnp.minimum(large, REL_BUCKETS - 1)
    return jnp.where(n < max_exact, n, large)


def _bias_table_kernel(tab_ref, bucket_ref, o_ref, *, subtract_last):
    n_heads = o_ref.shape[0]
    values = [[(tab_ref[k, h] - (tab_ref[REL_BUCKETS - 1, h] if subtract_last else 0.0)) * LOG2E
               for h in range(n_heads)] for k in range(REL_BUCKETS)]

    def rows(chunk, carry):
        r0 = pl.multiple_of(chunk * SUBLANES, SUBLANES)
        bucket = bucket_ref[0, pl.ds(r0, SUBLANES), :]
        accs = [jnp.where(bucket == ZERO_BUCKET, 0.0, NEG) for _ in range(n_heads)]
        for k in range(REL_BUCKETS):
            hit = bucket == k
            for h in range(n_heads):
                accs[h] = jnp.where(hit, values[k][h], accs[h])
        for h in range(n_heads):
            o_ref[h, 0, pl.ds(r0, SUBLANES), :] = accs[h]
        return carry

    lax.fori_loop(0, bucket_ref.shape[1] // SUBLANES, rows, 0)


def _bias_table(table, bucket, subtract_last=False):
    n_heads = table.shape[1]
    n, r, c = bucket.shape
    return pl.pallas_call(
        functools.partial(_bias_table_kernel, subtract_last=subtract_last),
        grid=(n,),
        in_specs=[pl.BlockSpec(memory_space=pltpu.SMEM),
                  pl.BlockSpec((1, r, c), lambda i: (i, 0, 0))],
        out_specs=pl.BlockSpec((n_heads, 1, r, c), lambda i: (0, i, 0, 0)),
        out_shape=jax.ShapeDtypeStruct((n_heads, n, r, c), F32),
        compiler_params=_params(1),
    )(table, bucket)


def _band_buckets_t(tile, window):
    n_back = -(-(window - 1) // tile)
    t = jnp.arange(n_back + 1)[:, None, None]
    key = jnp.arange(tile)[None, :, None]
    query = jnp.arange(tile)[None, None, :]
    dist = query + (n_back - t) * tile - key
    return jnp.where((dist >= 0) & (dist < window), _t5_bucket(dist), -1).astype(jnp.int32)


def _toeplitz_buckets_t(tile, n_tiles):
    m = jnp.arange(n_tiles)[:, None, None]
    key = jnp.arange(tile)[None, :, None]
    query = jnp.arange(tile)[None, None, :]
    dist = m * tile + query - key
    near = jnp.where(dist >= 0, _t5_bucket(dist), -1).astype(jnp.int32)
    return jnp.concatenate([near, jnp.full((1, tile, tile), ZERO_BUCKET, jnp.int32)])


def _cmp_buckets_t(s_len, n_rows):
    n_c = n_rows - 1
    tile = jnp.arange(s_len // SEL_TQ)[:, None, None]
    n = jnp.arange(n_rows)[None, :, None]
    t = tile * SEL_TQ + jnp.arange(SEL_TQ)[None, None, :]
    dist = t - (n * CMP_STRIDE + CMP_LEN - 1)
    return jnp.where((dist >= 0) & (n < n_c), _t5_bucket(dist), -1).astype(jnp.int32)


def _near_tiles(tile):
    max_exact = REL_BUCKETS // 2
    first_const = math.ceil(max_exact * (REL_MAX_DISTANCE / max_exact) ** ((max_exact - 1) / max_exact)) + 1
    m = 1
    while m * tile - (tile - 1) < first_const:
        m += 1
    return m


def _rms(x, gain):
    return x * lax.rsqrt(jnp.mean(x * x, axis=-1, keepdims=True) + RMS_EPS) * gain


def _in_proj_kernel(x_ref, mod_ref, gain_ref, w_ref, swa_ref, fox_ref, nsaq_ref, kc_ref, vc_ref, k2_ref,
                    misc_ref, vt_ref, pack_scr):
    x = x_ref[0]
    h = _rms(x, gain_ref[...]) * (1.0 + mod_ref[0, 1:2, :]) + mod_ref[0, 0:1, :]
    hb = h.astype(BF16)

    def seg(bounds):
        return jnp.dot(hb, w_ref[:, bounds[0]:bounds[1]], preferred_element_type=F32)

    swa_ref[0] = seg(SEG_SWA).astype(BF16)
    fox_ref[0] = seg(SEG_FOX).astype(BF16)
    nsaq_ref[0] = seg(SEG_NSAQ).astype(BF16)
    for slot, (bounds, out_ref) in enumerate(((SEG_KC, kc_ref), (SEG_VC, vc_ref))):
        pack_scr[slot] = seg(bounds)
        for tok in range(CMP_STRIDE):
            out_ref[0, :, tok * LANES:(tok + 1) * LANES] = pack_scr[
                slot, pl.ds(tok, x.shape[0] // CMP_STRIDE, stride=CMP_STRIDE), :].astype(BF16)
    k2_ref[0] = seg(SEG_K2).astype(BF16)
    misc_ref[0] = seg(SEG_MISC)

    rows = x.shape[0]
    extra_row = lax.broadcasted_iota(jnp.int32, (VT_ROWS - HEAD_DIM, rows), 0)
    extra = jnp.where(extra_row == 0, 1.0, 0.0).astype(BF16)
    values = seg(SEG_V)
    for c in range(values.shape[1] // LANES):
        vt = values[:, c * LANES:(c + 1) * LANES].T.astype(BF16)
        for half in range(2):
            base = (2 * c + half) * VT_ROWS
            vt_ref[0, base:base + HEAD_DIM, :] = vt[half * HEAD_DIM:(half + 1) * HEAD_DIM, :]
            vt_ref[0, base + HEAD_DIM:base + VT_ROWS, :] = extra


def _in_proj(x, mod, gain, w):
    b, s, d = x.shape
    n = w.shape[1]

    def rows(width, dtype):
        return (pl.BlockSpec((1, ROW_TILE, width), lambda i, j: (i, j, 0)), jax.ShapeDtypeStruct((b, s, width), dtype))

    packed = (pl.BlockSpec((1, ROW_TILE // CMP_STRIDE, CMP_STRIDE * LANES), lambda i, j: (i, j, 0)),
              jax.ShapeDtypeStruct((b, s // CMP_STRIDE, CMP_STRIDE * LANES), BF16))
    vt_rows = (SEG_V[1] - SEG_V[0]) // HEAD_DIM * VT_ROWS
    outs = [rows(SEG_SWA[1] - SEG_SWA[0], BF16), rows(SEG_FOX[1] - SEG_FOX[0], BF16),
            rows(SEG_NSAQ[1] - SEG_NSAQ[0], BF16), packed, packed, rows(SEG_K2[1] - SEG_K2[0], BF16),
            rows(SEG_MISC[1] - SEG_MISC[0], F32),
            (pl.BlockSpec((1, vt_rows, ROW_TILE), lambda i, j: (i, 0, j)),
             jax.ShapeDtypeStruct((b, vt_rows, s), BF16))]
    return pl.pallas_call(
        _in_proj_kernel,
        grid=(b, s // ROW_TILE),
        in_specs=[pl.BlockSpec((1, ROW_TILE, d), lambda i, j: (i, j, 0)),
                  pl.BlockSpec((1, ADA_CHUNKS, d), lambda i, j: (i, 0, 0)),
                  pl.BlockSpec((1, d), lambda i, j: (0, 0)),
                  pl.BlockSpec((d, n), lambda i, j: (0, 0))],
        out_specs=[spec for spec, _ in outs],
        out_shape=[shape for _, shape in outs],
        scratch_shapes=[pltpu.VMEM((2, ROW_TILE, LANES), F32)],
        compiler_params=_params(2),
    )(x, mod, gain, w)


def _banded_kernel(*refs, n_back, n_groups, has_sink, t):
    if has_sink:
        sink_ref, q_ref, k_ref, vt_ref, bias_ref, o_ref = refs
    else:
        q_ref, k_ref, vt_ref, bias_ref, o_ref = refs
    i = pl.program_id(1)
    lo = _lane_lo((t, LANES))
    n_tiles = n_back + 1

    def run(all_valid):
        starts = [pl.multiple_of(jnp.maximum(i - n_back + tt, 0) * t, t) for tt in range(n_tiles)]
        k_tiles = [k_ref[0, pl.ds(start, t), :] for start in starts]

        def scores(g):
            qg = q_ref[0, :, g * LANES:(g + 1) * LANES]
            zero = jnp.zeros_like(qg)
            qms = (jnp.where(lo, qg, zero), jnp.where(lo, zero, qg))
            return [[bias_ref[2 * g + half, tt] + _dot_nt(k_tiles[tt], qms[half]) for tt in range(n_tiles)]
                    for half in range(2)]

        def softmax_pv(g, sts):
            pair = []
            for half in range(2):
                tiles = sts[half]
                if not all_valid:
                    tiles = [jnp.where(i - n_back + tt >= 0, st, NEG) if tt < n_back else st
                             for tt, st in enumerate(tiles)]
                m = None
                for st in tiles:
                    part = st.reshape(t // SUBLANES, SUBLANES, t).max(axis=0)
                    m = part if m is None else jnp.maximum(m, part)
                m = _all_sublanes(m, jnp.maximum)
                if has_sink:
                    sink = sink_ref[2 * g + half] * LOG2E
                    m = jnp.maximum(m, sink)
                acc = None
                for tt, st in enumerate(tiles):
                    p = jnp.exp2((st.reshape(t // SUBLANES, SUBLANES, t) - m[None]).reshape(t, t).astype(BF16))
                    part = jnp.dot(vt_ref[0, half * VT_ROWS:(half + 1) * VT_ROWS, pl.ds(starts[tt], t)], p,
                                   preferred_element_type=F32)
                    acc = part if acc is None else acc + part
                denom = _all_sublanes(acc[HEAD_DIM:HEAD_DIM + SUBLANES, :], jnp.add)
                if has_sink:
                    denom = denom + jnp.exp2(sink - m)
                out = acc[0:HEAD_DIM, :].reshape(HEAD_DIM // SUBLANES, SUBLANES, t) / denom[None]
                pair.append(out.reshape(HEAD_DIM, t))
            o_ref[0, :, g * LANES:(g + 1) * LANES] = jnp.concatenate(pair, axis=0).T.astype(o_ref.dtype)

        pending = scores(0)
        for g in range(n_groups):
            current = pending
            if g + 1 < n_groups:
                pending = scores(g + 1)
            softmax_pv(g, current)

    @pl.when(i >= n_back)
    def _():
        run(True)

    @pl.when(i < n_back)
    def _():
        run(False)


def _banded_attention(q_arr, k_arr, k_blk, vt, vt_blk, bias, sinks=None):
    b, s, _ = q_arr.shape
    n_pos, n_tiles, t = bias.shape[0], bias.shape[1], bias.shape[2]
    width = n_pos * HEAD_DIM
    in_specs = [pl.BlockSpec((1, t, width), lambda i, j: (i, j, 0)),
                pl.BlockSpec((1, s, LANES), lambda i, j: (i, 0, k_blk)),
                pl.BlockSpec((1, 2 * VT_ROWS, s), lambda i, j: (i, vt_blk, 0)),
                pl.BlockSpec(bias.shape, lambda i, j: (0, 0, 0, 0))]
    args = [q_arr, k_arr, vt, bias]
    if sinks is not None:
        in_specs = [pl.BlockSpec(memory_space=pltpu.SMEM)] + in_specs
        args = [sinks] + args
    return pl.pallas_call(
        functools.partial(_banded_kernel, n_back=n_tiles - 1, n_groups=n_pos // 2, has_sink=sinks is not None, t=t),
        grid=(b, s // t),
        in_specs=in_specs,
        out_specs=pl.BlockSpec((1, t, width), lambda i, j: (i, j, 0)),
        out_shape=jax.ShapeDtypeStruct((b, s, width), BF16),
        compiler_params=_params(2),
    )(*args)


def _all_sublanes(x, op):
    for shift in (4, 2, 1):
        x = op(x, pltpu.roll(x, shift, 0))
    return x


def _flash_init(m_scr, acc_scr):
    m_scr[...] = jnp.full(m_scr.shape, NEG, F32)
    acc_scr[...] = jnp.zeros(acc_scr.shape, F32)


def _flash_update(h, st_ref, tile_max, vt_h, m_scr, acc_scr):
    tk, tq = st_ref.shape
    m_prev = m_scr[h]
    m_new = _all_sublanes(jnp.maximum(m_prev, tile_max), jnp.maximum)
    alpha = jnp.exp2(m_prev - m_new)
    p = jnp.exp2((st_ref[...].reshape(tk // SUBLANES, SUBLANES, tq) - m_new[None]).reshape(tk, tq).astype(BF16))
    acc = acc_scr[h].reshape(VT_ROWS // SUBLANES, SUBLANES, tq) * alpha[None]
    acc_scr[h] = acc.reshape(VT_ROWS, tq) + jnp.dot(vt_h, p, preferred_element_type=F32)
    m_scr[h] = m_new


def _flash_finish(o_ref, n_groups, acc_scr):
    tq = acc_scr.shape[2]
    for g in range(n_groups):
        pair = []
        for h in (2 * g, 2 * g + 1):
            denom = _all_sublanes(acc_scr[h, HEAD_DIM:HEAD_DIM + SUBLANES, :], jnp.add)
            out = acc_scr[h, 0:HEAD_DIM, :].reshape(HEAD_DIM // SUBLANES, SUBLANES, tq) / denom[None]
            pair.append(out.reshape(HEAD_DIM, tq))
        o_ref[0, :, g * LANES:(g + 1) * LANES] = jnp.concatenate(pair, axis=0).T.astype(o_ref.dtype)


def _flash_pipeline(i, n_heads, qk_scores, bias_tile, vt_slab, s_scr, tmax_scr, m_scr, acc_scr):
    def qk_head(thunk, h, j, slot):
        st = thunk() + bias_tile(h, j)
        s_scr[slot, h] = st
        tmax_scr[slot, h] = st.reshape(st.shape[0] // SUBLANES, SUBLANES, st.shape[1]).max(axis=0)

    def softmax_head(h, j, slot):
        _flash_update(h, s_scr.at[slot, h], tmax_scr[slot, h], vt_slab(h, j), m_scr, acc_scr)

    def softmax_all(j, slot):
        for h in range(n_heads):
            softmax_head(h, j, slot)

    def stage(j_qk, slot_qk, j_sm, slot_sm):
        thunks = qk_scores(j_qk)
        for h in range(n_heads):
            qk_head(thunks[h], h, j_qk, slot_qk)
            softmax_head(h, j_sm, slot_sm)

    for h, thunk in enumerate(qk_scores(0)):
        qk_head(thunk, h, 0, 0)

    def body(trip, carry):
        j = 2 * trip
        stage(j + 1, 1, j, 0)
        stage(j + 2, 0, j + 1, 1)
        return carry

    lax.fori_loop(0, i // 2, body, 0)
    last = 2 * (i // 2)

    @pl.when(i % 2 == 0)
    def _():
        softmax_all(last, 0)

    @pl.when(i % 2 == 1)
    def _():
        stage(last + 1, 1, last, 0)
        softmax_all(last + 1, 1)


def _fox_aug_kernel(misc_ref, fbias_ref, tri_ref, o_ref):
    s_len, width = misc_ref.shape[1], misc_ref.shape[2]
    term = lax.broadcasted_iota(jnp.int32, (LANES, width), 1) % HEAD_DIM
    carry = jnp.zeros((1, width), F32)
    for c in range(s_len // LANES):
        z = misc_ref[0, c * LANES:(c + 1) * LANES, :] + fbias_ref[...]
        log_f = jnp.minimum(z, 0.0) - jnp.log1p(jnp.exp(-jnp.abs(z)))
        cum = jnp.dot(tri_ref[...], log_f, precision=HIGHEST, preferred_element_type=F32) + carry
        carry = cum[LANES - 1:LANES, :]
        x = cum * (-LOG2E)
        hi = x.astype(BF16).astype(F32)
        rest = x - hi
        mid = rest.astype(BF16).astype(F32)
        low = rest - mid
        out = jnp.where(term == 0, hi, jnp.where(term == 1, mid, jnp.where(term == 2, low, 0.0)))
        o_ref[0, c * LANES:(c + 1) * LANES, :] = out.astype(BF16)


def _fox_key_terms(misc, fbias):
    b, s, width = misc.shape
    tri = jnp.asarray(np.tril(np.ones((LANES, LANES), np.float32)))
    return pl.pallas_call(
        _fox_aug_kernel,
        grid=(b,),
        in_specs=[pl.BlockSpec((1, s, width), lambda i: (i, 0, 0)),
                  pl.BlockSpec((1, width), lambda i: (0, 0)),
                  pl.BlockSpec((LANES, LANES), lambda i: (0, 0))],
        out_specs=pl.BlockSpec((1, s, width), lambda i: (i, 0, 0)),
        out_shape=jax.ShapeDtypeStruct((b, s, width), BF16),
        compiler_params=_params(1),
    )(misc, fbias, tri)


def _fox_kernel(q_ref, k_ref, aug_ref, vt_ref, mask_ref, o_ref, qs_scr, s_scr, tmax_scr, m_scr, acc_scr):
    t = FLASH_T
    i = pl.program_id(1)
    lo = _lane_lo((t, LANES))
    lane = lax.broadcasted_iota(jnp.int32, (t, LANES), 1)
    ones = jnp.where(lane % HEAD_DIM < KEY_BIAS_TERMS, 1.0, 0.0).astype(BF16)
    n_groups = FOX_HEADS // 2
    for g in range(n_groups):
        qg = q_ref[0, :, g * LANES:(g + 1) * LANES]
        qs_scr[2 * g] = jnp.where(lo, qg, ones)
        qs_scr[2 * g + 1] = jnp.where(lo, ones, qg)
    _flash_init(m_scr, acc_scr)

    def qk_scores(j):
        start = pl.multiple_of(j * t, t)
        scores = []
        for g in range(n_groups):
            k_tile = k_ref[0, pl.ds(start, t), g * LANES:(g + 1) * LANES]
            a_tile = aug_ref[0, pl.ds(start, t), g * LANES:(g + 1) * LANES]
            k_sel = (jnp.where(lo, k_tile, a_tile), jnp.where(lo, a_tile, k_tile))
            for half in range(2):
                scores.append(functools.partial(lambda k, h: _dot_nt(k, qs_scr[h]), k_sel[half], 2 * g + half))
        return scores

    def bias_tile(h, j):
        return mask_ref[jnp.minimum(i - j, 1)]

    def vt_slab(h, j):
        return vt_ref[0, h * VT_ROWS:(h + 1) * VT_ROWS, pl.ds(pl.multiple_of(j * t, t), t)]

    _flash_pipeline(i, FOX_HEADS, qk_scores, bias_tile, vt_slab, s_scr, tmax_scr, m_scr, acc_scr)
    _flash_finish(o_ref, n_groups, acc_scr)


def _fox_attention(fox_qk, key_terms, vt):
    b, s, _ = fox_qk.shape
    width = FOX_HEADS * HEAD_DIM
    t = FLASH_T
    idx = np.arange(t)
    diag = np.where(idx[:, None] <= idx[None, :], 0.0, NEG)
    masks = jnp.asarray(np.stack([diag, np.zeros((t, t))]).astype(np.float32))
    return pl.pallas_call(
        _fox_kernel,
        grid=(b, s // t),
        in_specs=[pl.BlockSpec((1, t, width), lambda i, j: (i, j, 0)),
                  pl.BlockSpec((1, s, width), lambda i, j: (i, 0, 1)),
                  pl.BlockSpec((1, s, width), lambda i, j: (i, 0, 0)),
                  pl.BlockSpec((1, FOX_HEADS * VT_ROWS, s), lambda i, j: (i, VT_FOX_BLOCK, 0)),
                  pl.BlockSpec(masks.shape, lambda i, j: (0, 0, 0))],
        out_specs=pl.BlockSpec((1, t, width), lambda i, j: (i, j, 0)),
        out_shape=jax.ShapeDtypeStruct((b, s, width), BF16),
        scratch_shapes=[pltpu.VMEM((FOX_HEADS, t, LANES), BF16),
                        pltpu.VMEM((2, FOX_HEADS, t, t), F32),
                        pltpu.VMEM((2, FOX_HEADS, SUBLANES, t), F32),
                        pltpu.VMEM((FOX_HEADS, SUBLANES, t), F32),
                        pltpu.VMEM((FOX_HEADS, VT_ROWS, t), F32)],
        compiler_params=_params(2),
    )(fox_qk, fox_qk, key_terms, vt, masks)


def _split3(x):
    hi = x.astype(BF16)
    rest = x - hi.astype(F32)
    mid = rest.astype(BF16)
    low = (rest - mid.astype(F32)).astype(BF16)
    return hi, mid, low


def _compress_kernel(x_ref, pe_ref, w1a_ref, w1b_ref, w2_ref, o_ref):
    x = x_ref[0]
    n_rows = x.shape[0]

    def mm3(lhs, w_ref):
        return sum(jnp.dot(lhs, w_ref[piece], preferred_element_type=F32) for piece in range(3))

    first = mm3(x, w1a_ref)
    second = mm3(x, w1b_ref)
    pe_term = sum(mm3(piece, w1a_ref) for piece in _split3(pe_ref[0])) \
        + sum(mm3(piece, w1b_ref) for piece in _split3(pe_ref[1]))
    pre = first + pltpu.roll(second, n_rows - 1, 0) + pe_term[0:1, :]
    hid = 0.5 * pre * (1.0 + jnp.tanh(math.sqrt(2.0 / math.pi) * (pre + 0.044715 * (pre * pre * pre))))
    o_ref[0] = jnp.dot(hid, w2_ref[...], precision=HIGHEST, preferred_element_type=F32)


def _compress(x, pos, w1, w2):
    b, n_rows, feat = x.shape
    half = CMP_LEN // 2
    eye = jnp.eye(2, dtype=F32)
    w1 = w1.astype(F32).reshape(CMP_LEN, HEAD_DIM, CMP_HIDDEN)
    w1a = jnp.stack(_split3(jnp.einsum('ldj,hg->lhdgj', w1[:half], eye).reshape(feat, 2 * CMP_HIDDEN)))
    w1b = jnp.stack(_split3(jnp.einsum('ldj,hg->lhdgj', w1[half:], eye).reshape(feat, 2 * CMP_HIDDEN)))
    w2 = jnp.einsum('jd,hg->hjgd', w2.astype(F32), eye).reshape(2 * CMP_HIDDEN, LANES)
    w2 = jnp.concatenate([w2, jnp.roll(w2, HEAD_DIM, axis=1)], axis=1)
    pe = jnp.broadcast_to(pos.astype(F32).reshape(2, half, 1, HEAD_DIM), (2, half, 2, HEAD_DIM))
    pe = jnp.broadcast_to(pe.reshape(2, 1, feat), (2, 8, feat))
    return pl.pallas_call(
        _compress_kernel,
        grid=(b,),
        in_specs=[pl.BlockSpec((1, n_rows, feat), lambda i: (i, 0, 0)),
                  pl.BlockSpec((2, 8, feat), lambda i: (0, 0, 0)),
                  pl.BlockSpec((3, feat, 2 * CMP_HIDDEN), lambda i: (0, 0, 0)),
                  pl.BlockSpec((3, feat, 2 * CMP_HIDDEN), lambda i: (0, 0, 0)),
                  pl.BlockSpec((2 * CMP_HIDDEN, 2 * LANES), lambda i: (0, 0))],
        out_specs=pl.BlockSpec((1, n_rows, 2 * LANES), lambda i: (i, 0, 0)),
        out_shape=jax.ShapeDtypeStruct((b, n_rows, 2 * LANES), F32),
        compiler_params=_params(1),
    )(x, pe, w1a, w1b, w2)


def _select_kernel(q_ref, kc_ref, vct_ref, bias_ref, o_ref, mb_ref, count_scr, psum_scr):
    tq = SEL_TQ
    i = pl.program_id(0)
    lo = _lane_lo((tq, LANES))
    n_rows = kc_ref.shape[1]
    lo_k = _lane_lo((n_rows, LANES))
    n_grp = n_rows // SUBLANES

    k_own = kc_ref[0, :, 0:LANES]
    k_swapped = kc_ref[0, :, LANES:2 * LANES]
    hi = k_own.astype(BF16)
    low = (k_swapped - k_swapped.astype(BF16).astype(F32)).astype(BF16)
    k_sel = (jnp.where(lo_k, hi, low), jnp.where(lo_k, low, hi))

    def scores(g):
        qg = q_ref[0, :, g * LANES:(g + 1) * LANES]
        swapped = pltpu.roll(qg.astype(F32), HEAD_DIM, 1).astype(BF16)
        q_dup = (jnp.where(lo, qg, swapped), jnp.where(lo, swapped, qg))
        return [bias_ref[2 * g + half, 0] + _dot_nt(k_sel[half], q_dup[half]) for half in range(2)]

    query = i * tq + lax.broadcasted_iota(jnp.int32, (SUBLANES, tq), 1)
    has_keys = query >= CMP_LEN - 1
    p_sum = [None, None]

    def softmax_pv(g, sts):
        pair = []
        for half in range(2):
            s3 = sts[half].reshape(n_grp, SUBLANES, tq)
            m = _all_sublanes(s3.max(axis=0), jnp.maximum)
            e = jnp.exp2(s3 - m[None])
            inv = jnp.where(has_keys, 1.0 / _all_sublanes(e.sum(axis=0), jnp.add), 0.0)
            p = e * inv[None]
            p_sum[half] = p if p_sum[half] is None else p_sum[half] + p
            pair.append(jnp.dot(vct_ref[0, half * HEAD_DIM:(half + 1) * HEAD_DIM, :],
                                p.reshape(n_rows, tq).astype(BF16), preferred_element_type=F32))
        o_ref[0, :, g * LANES:(g + 1) * LANES] = jnp.concatenate(pair, axis=0).T.astype(o_ref.dtype)

    n_groups = NSA_HEADS // 2
    pending = scores(0)
    for g in range(n_groups):
        current = pending
        if g + 1 < n_groups:
            pending = scores(g + 1)
        softmax_pv(g, current)

    n_blk = HEAD_DIM
    blk_grp = n_blk // SUBLANES
    sub = lax.broadcasted_iota(jnp.int32, (SUBLANES, tq), 0)
    q_blk = query // SLC_BLOCK
    kind = []
    for r in range(blk_grp):
        blk = sub + r * SUBLANES
        behind = q_blk - blk
        forced = jnp.where(blk == 0, 1, 0) + jnp.where(behind == 0, 1, 0) + jnp.where(behind == 1, 1, 0)
        kind.append(jnp.where(behind < 0, 2, jnp.minimum(forced, 1)))
    masks = []
    per_blk = SLC_BLOCK // CMP_STRIDE
    n_real = n_rows // per_blk
    n_lane_chunks = tq // LANES
    psum_scr[:, 0:SUBLANES, :] = jnp.zeros((n_lane_chunks, SUBLANES, LANES), F32)
    for half in (1, 0):
        p_rows = p_sum[half].reshape(n_rows, tq)
        for c in range(n_lane_chunks):
            psum_scr[c, SUBLANES:SUBLANES + n_rows, :] = p_rows[:, c * LANES:(c + 1) * LANES]

        def every_fourth(offset):
            return jnp.concatenate([psum_scr[c, pl.ds(SUBLANES + offset, n_real, stride=per_blk), :]
                                    for c in range(n_lane_chunks)], axis=1)

        imp = (0.5 * (every_fourth(-1) + every_fourth(3))
               + (every_fourth(0) + every_fourth(1) + every_fourth(2)))
        if n_real < n_blk:
            imp = jnp.concatenate([imp, jnp.zeros((n_blk - n_real, tq), F32)], axis=0)
        rows = [jnp.where(kind[r] == 2, NEG, jnp.where(kind[r] == 1, FORCE, imp[r * SUBLANES:(r + 1) * SUBLANES, :]))
                for r in range(blk_grp)]
        count_scr[...] = jnp.zeros(count_scr.shape, jnp.int32)
        for r_other in range(blk_grp):
            @pl.when(r_other * SUBLANES * SLC_BLOCK < (i + 1) * tq)
            def _(r_other=r_other, rows=rows):
                counts = [None] * blk_grp
                for s_other in range(SUBLANES):
                    row = jnp.broadcast_to(rows[r_other][s_other:s_other + 1, :], (SUBLANES, tq))
                    for r in range(blk_grp):
                        if r > r_other:
                            beats = jnp.where(row >= rows[r], 1, 0)
                        elif r < r_other:
                            beats = jnp.where(row > rows[r], 1, 0)
                        else:
                            beats = jnp.where(sub > s_other, jnp.where(row >= rows[r], 1, 0),
                                              jnp.where(row > rows[r], 1, 0))
                        counts[r] = beats if counts[r] is None else counts[r] + beats
                for r in range(blk_grp):
                    count_scr[r] = count_scr[r] + counts[r]
        masks.extend(jnp.where(count_scr[r] < TOPK, 0.0, NEG) for r in range(blk_grp))
    for c in range(tq // LANES):
        mb_ref[0, c * LANES:(c + 1) * LANES, :] = jnp.concatenate(
            [mk[:, c * LANES:(c + 1) * LANES] for mk in masks], axis=0).T.astype(BF16)


def _select(nsa_q, k_cmp, v_cmp, bias_c):
    b, s, width = nsa_q.shape
    n_rows = k_cmp.shape[1]
    n_blk = HEAD_DIM
    assert CMP_LEN == 2 * CMP_STRIDE and SLC_BLOCK == 4 * CMP_STRIDE
    tq = SEL_TQ
    vct = v_cmp[:, :, 0:LANES].transpose(0, 2, 1).astype(BF16)
    return pl.pallas_call(
        _select_kernel,
        grid=(s // tq, b),
        in_specs=[pl.BlockSpec((1, tq, width), lambda j, i: (i, j, 0)),
                  pl.BlockSpec((1, n_rows, 2 * LANES), lambda j, i: (i, 0, 0)),
                  pl.BlockSpec((1, LANES, n_rows), lambda j, i: (i, 0, 0)),
                  pl.BlockSpec((NSA_HEADS, 1, n_rows, tq), lambda j, i: (0, j, 0, 0))],
        out_specs=[pl.BlockSpec((1, tq, width), lambda j, i: (i, j, 0)),
                   pl.BlockSpec((1, tq, LANES), lambda j, i: (i, j, 0))],
        out_shape=[jax.ShapeDtypeStruct((b, s, width), BF16),
                   jax.ShapeDtypeStruct((b, s, LANES), BF16)],
        scratch_shapes=[pltpu.VMEM((n_blk // SUBLANES, SUBLANES, tq), jnp.int32),
                        pltpu.VMEM((tq // LANES, SUBLANES + n_rows, LANES), F32)],
        compiler_params=_params(2),
    )(nsa_q, k_cmp, vct, bias_c)


def _slc_kernel(q_ref, mb_ref, k_ref, e2_ref, vt_ref, bias_ref, o_ref, qs_scr, s_scr, tmax_scr, m_scr, acc_scr, *,
                n_near):
    t = FLASH_T
    i = pl.program_id(1)
    lo = _lane_lo((t, LANES))
    n_groups = NSA_HEADS // 2
    mb = mb_ref[0]
    for g in range(n_groups):
        qg = q_ref[0, :, g * LANES:(g + 1) * LANES]
        qs_scr[2 * g] = jnp.where(lo, qg, mb)
        qs_scr[2 * g + 1] = jnp.where(lo, mb, qg)
    _flash_init(m_scr, acc_scr)

    def qk_scores(j):
        start = pl.multiple_of(j * t, t)
        k_tile = k_ref[0, pl.ds(start, t), :]
        e_tile = e2_ref[pl.ds(start, t), :]
        k_sel = (jnp.where(lo, k_tile, e_tile), jnp.where(lo, e_tile, k_tile))
        return [functools.partial(lambda k, pos: _dot_nt(k, qs_scr[pos]), k_sel[pos % 2], pos)
                for pos in range(NSA_HEADS)]

    def bias_tile(pos, j):
        return bias_ref[pos, jnp.minimum(i - j, n_near)]

    def vt_slab(pos, j):
        kv = pos % 2
        return vt_ref[0, kv * VT_ROWS:(kv + 1) * VT_ROWS, pl.ds(pl.multiple_of(j * t, t), t)]

    _flash_pipeline(i, NSA_HEADS, qk_scores, bias_tile, vt_slab, s_scr, tmax_scr, m_scr, acc_scr)
    _flash_finish(o_ref, n_groups, acc_scr)


def _slc_attention(nsa_q, mask_bias, k2, e2, vt, bias):
    b, s, width = nsa_q.shape
    t = FLASH_T
    n_near = bias.shape[1] - 1
    return pl.pallas_call(
        functools.partial(_slc_kernel, n_near=n_near),
        grid=(b, s // t),
        in_specs=[pl.BlockSpec((1, t, width), lambda i, j: (i, j, 0)),
                  pl.BlockSpec((1, t, LANES), lambda i, j: (i, j, 0)),
                  pl.BlockSpec((1, s, LANES), lambda i, j: (i, 0, 0)),
                  pl.BlockSpec((s, LANES), lambda i, j: (0, 0)),
                  pl.BlockSpec((1, 2 * VT_ROWS, s), lambda i, j: (i, VT_SLC_BLOCK, 0)),
                  pl.BlockSpec(bias.shape, lambda i, j: (0, 0, 0, 0))],
        out_specs=pl.BlockSpec((1, t, width), lambda i, j: (i, j, 0)),
        out_shape=jax.ShapeDtypeStruct((b, s, width), BF16),
        scratch_shapes=[pltpu.VMEM((NSA_HEADS, t, LANES), BF16),
                        pltpu.VMEM((2, NSA_HEADS, t, t), F32),
                        pltpu.VMEM((2, NSA_HEADS, SUBLANES, t), F32),
                        pltpu.VMEM((NSA_HEADS, SUBLANES, t), F32),
                        pltpu.VMEM((NSA_HEADS, VT_ROWS, t), F32)],
        compiler_params=_params(2),
    )(nsa_q, mask_bias, k2, e2, vt, bias)


def _block_onehot(s_len):
    blk = np.arange(s_len)[:, None] // SLC_BLOCK
    lane = np.arange(LANES)[None, :] % HEAD_DIM
    return jnp.asarray((blk == lane).astype(np.float32), dtype=BF16)


def _mix_ffn_kernel(x_ref, mod_ref, swa_ref, fox_ref, cmp_ref, slc_ref, win_ref, misc_ref, expand_ref, gn_ref,
                    w_ref, post_ref, pre_ref, wg_ref, wu_ref, wd_ref, fpost_ref, o_ref):
    n_swa = SWA_HEADS * HEAD_DIM
    n_fox = FOX_HEADS * HEAD_DIM
    n_nsa = NSA_HEADS * HEAD_DIM
    gate = jax.nn.sigmoid(misc_ref[0])
    gate_hi = gate.astype(BF16)
    gate_lo = (gate - gate_hi.astype(F32)).astype(BF16)
    gates = (jnp.dot(gate_hi, expand_ref[...], preferred_element_type=F32)
             + jnp.dot(gate_lo, expand_ref[...], preferred_element_type=F32))
    o_nsa = (gates[:, 0:n_nsa] * cmp_ref[0].astype(F32) + gates[:, n_nsa:2 * n_nsa] * slc_ref[0].astype(F32)
             + gates[:, 2 * n_nsa:3 * n_nsa] * win_ref[0].astype(F32))
    a = _rms(swa_ref[0].astype(F32), gn_ref[:, 0:n_swa]).astype(BF16)
    b = _rms(fox_ref[0].astype(F32), gn_ref[:, n_swa:n_swa + n_fox]).astype(BF16)
    c = _rms(o_nsa, gn_ref[:, n_swa + n_fox:]).astype(BF16)
    y = (jnp.dot(a, w_ref[0:n_swa, :], preferred_element_type=F32)
         + jnp.dot(b, w_ref[n_swa:n_swa + n_fox, :], preferred_element_type=F32)
         + jnp.dot(c, w_ref[n_swa + n_fox:, :], preferred_element_type=F32))
    x = x_ref[0] + mod_ref[0, 2:3, :] * _rms(y, post_ref[...])

    h = (_rms(x, pre_ref[...]) * (1.0 + mod_ref[0, 4:5, :]) + mod_ref[0, 3:4, :]).astype(BF16)
    y = jnp.zeros(x.shape, F32)
    for chunk in range(wg_ref.shape[0]):
        gate = jnp.dot(h, wg_ref[chunk], preferred_element_type=F32)
        up = jnp.dot(h, wu_ref[chunk], preferred_element_type=F32)
        act = (gate * jax.nn.sigmoid(gate) * up).astype(BF16)
        y = y + jnp.dot(act, wd_ref[chunk], preferred_element_type=F32)
    o_ref[0] = x + mod_ref[0, 5:6, :] * _rms(y, fpost_ref[...])


def _gate_expansion():
    expand = np.zeros((LANES, 3 * NSA_HEADS * HEAD_DIM), np.float32)
    for branch in range(3):
        for p in range(NSA_HEADS):
            col = (branch * NSA_HEADS + p) * HEAD_DIM
            expand[GATE_LANE + 8 * branch + p, col:col + HEAD_DIM] = 1.0
    return jnp.asarray(expand, dtype=BF16)


def _mix_ffn(x, mod, o_swa, o_fox, o_cmp, o_slc, o_win, misc, gn, w, post, pre, wg, wu, wd, fpost):
    b, s, d = x.shape
    expand = _gate_expansion()
    n_chunks = wg.shape[0]

    def rows(width):
        return pl.BlockSpec((1, ROW_TILE, width), lambda i, j: (i, j, 0))

    def whole(shape):
        return pl.BlockSpec(shape, lambda i, j: (0,) * len(shape), pipeline_mode=pl.Buffered(1))

    vec = pl.BlockSpec((1, d), lambda i, j: (0, 0))
    return pl.pallas_call(
        _mix_ffn_kernel,
        grid=(b, s // ROW_TILE),
        in_specs=[rows(d),
                  pl.BlockSpec((1, ADA_CHUNKS, d), lambda i, j: (i, 0, 0)),
                  rows(o_swa.shape[2]), rows(o_fox.shape[2]), rows(o_cmp.shape[2]), rows(o_slc.shape[2]),
                  rows(o_win.shape[2]), rows(LANES),
                  whole(expand.shape), vec, whole((d, d)), vec, vec,
                  whole((n_chunks, d, FFN_CHUNK)), whole((n_chunks, d, FFN_CHUNK)), whole((n_chunks, FFN_CHUNK, d)),
                  vec],
        out_specs=rows(d),
        out_shape=jax.ShapeDtypeStruct((b, s, d), F32),
        compiler_params=_params(2),
    )(x, mod, o_swa, o_fox, o_cmp, o_slc, o_win, misc, expand, gn, w, post, pre, wg, wu, wd, fpost)


def _forget_lanes():
    lanes, heads = [], []
    for h in range(FOX_HEADS):
        base = (h // 2) * LANES + (HEAD_DIM if h % 2 == 0 else 0)
        for j in range(KEY_BIAS_TERMS):
            lanes.append(base + j)
            heads.append(h)
    return np.array(lanes), np.array(heads)


def _in_proj_layout():
    d = HEAD_DIM
    o_qa, o_ka, o_va, o_qb, o_kb, o_vb, o_fb, o_qc = 0, 256, 384, 512, 768, 1024, 1280, 1284
    o_kc, o_vc, o_ksl, o_vsl, o_kw, o_vw, o_gc = 1796, 1924, 2052, 2180, 2308, 2436, 2564
    scale = LOG2E / math.sqrt(d)

    def head_cols(base, heads):
        return np.concatenate([np.arange(base + h * d, base + (h + 1) * d) for h in heads])

    def span(base, width):
        return np.arange(base, base + width)

    cols = [head_cols(o_qa, SWA_POS), span(o_ka, 128),
            span(o_qb, 256), span(o_kb, 256),
            head_cols(o_qc, NSA_POS),
            span(o_kc, 128), span(o_vc, 128),
            span(o_ksl, 128), span(o_kw, 128),
            span(o_vb, 256), span(o_va, 128), span(o_vsl, 128), span(o_vw, 128)]
    scales = [np.full(256, scale), np.ones(128), np.full(256, scale), np.ones(256), np.full(512, scale),
              np.ones(256), np.ones(256), np.ones(640)]
    lanes, heads = _forget_lanes()
    misc_cols = np.zeros(SEG_MISC[1] - SEG_MISC[0], np.int64)
    misc_scale = np.zeros(SEG_MISC[1] - SEG_MISC[0])
    misc_cols[lanes] = o_fb + heads
    misc_scale[lanes] = 1.0
    for branch in range(3):
        for p, h in enumerate(NSA_POS):
            misc_cols[GATE_LANE + 8 * branch + p] = o_gc + h * 3 + branch
            misc_scale[GATE_LANE + 8 * branch + p] = 1.0
    cols.append(misc_cols)
    scales.append(misc_scale)
    return np.concatenate(cols), np.concatenate(scales).astype(np.float32)


def _head_perm(pos):
    return np.concatenate([np.arange(h * HEAD_DIM, (h + 1) * HEAD_DIM) for h in pos])


def kernel(x, c, rel_bias, ada_w, ada_b, attn_pre_norm, attn_post_norm, ffn_pre_norm, ffn_post_norm, w_in,
           forget_bias, swa_sinks, cmp_pos, cmp_w1, cmp_w2, group_norm, w_out, ffn_w_gate, ffn_w_up, ffn_w_down):
    b, s, d = x.shape
    depth = w_in.shape[0]
    hidden = ffn_w_gate.shape[2]
    assert s % (2 * FLASH_T) == 0 and s // SLC_BLOCK <= HEAD_DIM and hidden % FFN_CHUNK == 0

    cols, scales = _in_proj_layout()
    n_main = SEG_MISC[0]
    breaks = np.flatnonzero(np.diff(cols[:n_main]) != 1) + 1
    runs = np.split(cols[:n_main], breaks)
    w_main = jnp.concatenate([w_in[:, :, int(r[0]):int(r[-1]) + 1] for r in runs], axis=2) * scales[:n_main]
    used = np.flatnonzero(scales[n_main:] != 0)
    w_misc = jnp.zeros((depth, d, SEG_MISC[1] - SEG_MISC[0]), w_in.dtype).at[:, :, used].set(
        w_in[:, :, cols[n_main:][used]])
    w_all = jnp.concatenate([w_main, w_misc], axis=2).astype(BF16)
    lanes, heads = _forget_lanes()
    fbias_all = jnp.zeros((depth, 1, SEG_MISC[1] - SEG_MISC[0]), F32).at[:, 0, lanes].set(
        forget_bias[:, heads].astype(F32))
    swa_perm = _head_perm(SWA_POS)
    nsa_perm = _head_perm(NSA_POS)
    n_swa, n_fox = SWA_HEADS * HEAD_DIM, FOX_HEADS * HEAD_DIM
    mix_perm = np.concatenate([swa_perm, n_swa + np.arange(n_fox), n_swa + n_fox + nsa_perm])
    gn_all = group_norm[:, mix_perm].astype(F32)
    w_out_all = w_out[:, mix_perm, :].astype(BF16)
    n_chunks = hidden // FFN_CHUNK
    wg_all = ffn_w_gate.reshape(depth, d, n_chunks, FFN_CHUNK).transpose(0, 2, 1, 3).astype(BF16)
    wu_all = ffn_w_up.reshape(depth, d, n_chunks, FFN_CHUNK).transpose(0, 2, 1, 3).astype(BF16)
    wd_all = ffn_w_down.reshape(depth, n_chunks, FFN_CHUNK, d).astype(BF16)

    tab_swa = rel_bias[:, np.array(SWA_POS)].astype(F32)
    tab_nsa = rel_bias[:, SWA_HEADS + np.array(NSA_POS)].astype(F32)
    bias_swa = _bias_table(tab_swa, _band_buckets_t(SWA_TILE, SWA_WINDOW))
    bias_win = _bias_table(tab_nsa, _band_buckets_t(WIN_TILE, NSA_WINDOW))
    bias_slc = _bias_table(tab_nsa, _toeplitz_buckets_t(FLASH_T, _near_tiles(FLASH_T)), subtract_last=True)
    n_rows = s // CMP_STRIDE
    bias_cmp = _bias_table(tab_nsa, _cmp_buckets_t(s, n_rows))
    e2 = _block_onehot(s)

    mod_all = _adaln(c.astype(F32), ada_w.astype(F32), ada_b.astype(F32)).reshape(depth, b, ADA_CHUNKS, d)

    for layer in range(depth):
        mod = mod_all[layer]
        swa_qk, fox_qk, nsa_q, kc, vc, k2, misc, vt = _in_proj(
            x, mod, attn_pre_norm[layer].reshape(1, d).astype(F32), w_all[layer])
        o_swa = _banded_attention(swa_qk, swa_qk, 2, vt, VT_SWA_BLOCK, bias_swa,
                                  sinks=swa_sinks[layer][np.array(SWA_POS)].astype(F32))
        o_fox = _fox_attention(fox_qk, _fox_key_terms(misc, fbias_all[layer]), vt)
        k_cmp = _compress(kc, cmp_pos[layer, 0], cmp_w1[layer, 0], cmp_w2[layer, 0])
        v_cmp = _compress(vc, cmp_pos[layer, 1], cmp_w1[layer, 1], cmp_w2[layer, 1])
        o_cmp, mask_bias = _select(nsa_q, k_cmp, v_cmp, bias_cmp)
        o_slc = _slc_attention(nsa_q, mask_bias, k2, e2, vt, bias_slc)
        o_win = _banded_attention(nsa_q, k2, 1, vt, VT_WIN_BLOCK, bias_win)
        x = _mix_ffn(x, mod, o_swa, o_fox, o_cmp, o_slc, o_win, misc, gn_all[layer].reshape(1, d),
                     w_out_all[layer], attn_post_norm[layer].reshape(1, d).astype(F32),
                     ffn_pre_norm[layer].reshape(1, d).astype(F32), wg_all[layer], wu_all[layer], wd_all[layer],
                     ffn_post_norm[layer].reshape(1, d).astype(F32))
    return x
```

```python
import functools
import math

import numpy as np
import jax
import jax.numpy as jnp
from jax import lax
from jax.experimental import pallas as pl
from jax.experimental.pallas import tpu as pltpu

F32 = jnp.float32
BF16 = jnp.bfloat16
HIGHEST = lax.Precision.HIGHEST

LANES = 128
SUBLANES = 8
VMEM_LIMIT = 56 * 1024 * 1024

HEAD_DIM = 64
SWA_HEADS = 4
SWA_WINDOW = 128
FOX_HEADS = 4
NSA_HEADS = 8
CMP_LEN = 32
CMP_STRIDE = 16
CMP_HIDDEN = 2 * HEAD_DIM
SLC_BLOCK = 64
TOPK = 16
NSA_WINDOW = 512
REL_BUCKETS = 32
REL_MAX_DISTANCE = 1024
ZERO_BUCKET = -2
RMS_EPS = 1e-6
NEG = -1e30
FORCE = 1e30
ADA_CHUNKS = 6
LOG2E = math.log2(math.e)

SWA_POS = (0, 2, 1, 3)
NSA_POS = (0, 4, 1, 5, 2, 6, 3, 7)

SWA_TILE = 256
WIN_TILE = 256
FLASH_T = 256
SEL_TQ = 256
ROW_TILE = 512
FFN_CHUNK = 256
VT_ROWS = HEAD_DIM + 16
KEY_BIAS_TERMS = 3

SEG_SWA = (0, 384)
SEG_FOX = (384, 896)
SEG_NSAQ = (896, 1408)
SEG_KC = (1408, 1536)
SEG_VC = (1536, 1664)
SEG_K2 = (1664, 1920)
SEG_V = (1920, 2560)
SEG_MISC = (2560, 2816)
GATE_LANE = 8
VT_FOX_BLOCK, VT_SWA_BLOCK, VT_SLC_BLOCK, VT_WIN_BLOCK = 0, 2, 3, 4


def _params(n_grid, vmem=VMEM_LIMIT):
    return pltpu.CompilerParams(dimension_semantics=("parallel",) * n_grid, vmem_limit_bytes=vmem)


def _dot_nt(a, b):
    return lax.dot_general(a, b, (((1,), (1,)), ((), ())), preferred_element_type=F32)


def _lane_lo(shape):
    return lax.broadcasted_iota(jnp.int32, shape, len(shape) - 1) < HEAD_DIM


def _adaln_kernel(c_ref, w_ref, b_ref, o_ref):
    c = c_ref[...]
    act = c * jax.nn.sigmoid(c)
    o_ref[0] = jnp.dot(act, w_ref[0], precision=HIGHEST, preferred_element_type=F32) + b_ref[0]


def _adaln(c, ada_w, ada_b):
    depth, d, n = ada_w.shape
    b = c.shape[0]
    return pl.pallas_call(
        _adaln_kernel,
        grid=(depth, n // d),
        in_specs=[pl.BlockSpec((b, d), lambda l, j: (0, 0)),
                  pl.BlockSpec((1, d, d), lambda l, j: (l, 0, j)),
                  pl.BlockSpec((1, 1, d), lambda l, j: (l, 0, j))],
        out_specs=pl.BlockSpec((1, b, d), lambda l, j: (l, 0, j)),
        out_shape=jax.ShapeDtypeStruct((depth, b, n), F32),
        compiler_params=_params(2),
    )(c, ada_w, ada_b.reshape(depth, 1, n))


def _t5_bucket(dist):
    n = jnp.maximum(dist, 0)
    max_exact = REL_BUCKETS // 2
    nf = jnp.maximum(n, 1).astype(jnp.float32)
    large = max_exact + (jnp.log(nf / max_exact) / math.log(REL_MAX_DISTANCE / max_exact)
                         * (REL_BUCKETS - max_exact)).astype(jnp.int32)
    large = jnp.minimum(large, REL_BUCKETS - 1)
    return jnp.where(n < max_exact, n, large)


def _bias_table_kernel(tab_ref, bucket_ref, o_ref, *, subtract_last):
    n_heads = o_ref.shape[0]
    values = [[(tab_ref[k, h] - (tab_ref[REL_BUCKETS - 1, h] if subtract_last else 0.0)) * LOG2E
               for h in range(n_heads)] for k in range(REL_BUCKETS)]

    def rows(chunk, carry):
        r0 = pl.multiple_of(chunk * SUBLANES, SUBLANES)
        bucket = bucket_ref[0, pl.ds(r0, SUBLANES), :]
        accs = [jnp.where(bucket == ZERO_BUCKET, 0.0, NEG) for _ in range(n_heads)]
        for k in range(REL_BUCKETS):
            hit = bucket == k
            for h in range(n_heads):
                accs[h] = jnp.where(hit, values[k][h], accs[h])
        for h in range(n_heads):
            o_ref[h, 0, pl.ds(r0, SUBLANES), :] = accs[h]
        return carry

    lax.fori_loop(0, bucket_ref.shape[1] // SUBLANES, rows, 0)


def _bias_table(table, bucket, subtract_last=False):
    n_heads = table.shape[1]
    n, r, c = bucket.shape
    return pl.pallas_call(
        functools.partial(_bias_table_kernel, subtract_last=subtract_last),
        grid=(n,),
        in_specs=[pl.BlockSpec(memory_space=pltpu.SMEM),
                  pl.BlockSpec((1, r, c), lambda i: (i, 0, 0))],
        out_specs=pl.BlockSpec((n_heads, 1, r, c), lambda i: (0, i, 0, 0)),
        out_shape=jax.ShapeDtypeStruct((n_heads, n, r, c), F32),
        compiler_params=_params(1),
    )(table, bucket)


def _band_buckets_t(tile, window):
    n_back = -(-(window - 1) // tile)
    t = jnp.arange(n_back + 1)[:, None, None]
    key = jnp.arange(tile)[None, :, None]
    query = jnp.arange(tile)[None, None, :]
    dist = query + (n_back - t) * tile - key
    return jnp.where((dist >= 0) & (dist < window), _t5_bucket(dist), -1).astype(jnp.int32)


def _toeplitz_buckets_t(tile, n_tiles):
    m = jnp.arange(n_tiles)[:, None, None]
    key = jnp.arange(tile)[None, :, None]
    query = jnp.arange(tile)[None, None, :]
    dist = m * tile + query - key
    near = jnp.where(dist >= 0, _t5_bucket(dist), -1).astype(jnp.int32)
    return jnp.concatenate([near, jnp.full((1, tile, tile), ZERO_BUCKET, jnp.int32)])


def _cmp_buckets_t(s_len, n_rows):
    n_c = n_rows - 1
    tile = jnp.arange(s_len // SEL_TQ)[:, None, None]
    n = jnp.arange(n_rows)[None, :, None]
    t = tile * SEL_TQ + jnp.arange(SEL_TQ)[None, None, :]
    dist = t - (n * CMP_STRIDE + CMP_LEN - 1)
    return jnp.where((dist >= 0) & (n < n_c), _t5_bucket(dist), -1).astype(jnp.int32)


def _near_tiles(tile):
    max_exact = REL_BUCKETS // 2
    first_const = math.ceil(max_exact * (REL_MAX_DISTANCE / max_exact) ** ((max_exact - 1) / max_exact)) + 1
    m = 1
    while m * tile - (tile - 1) < first_const:
        m += 1
    return m


def _rms(x, gain):
    return x * lax.rsqrt(jnp.mean(x * x, axis=-1, keepdims=True) + RMS_EPS) * gain


def _in_proj_kernel(x_ref, mod_ref, gain_ref, w_ref, swa_ref, fox_ref, nsaq_ref, kc_ref, vc_ref, k2_ref,
                    misc_ref, vt_ref, pack_scr):
    x = x_ref[0]
    h = _rms(x, gain_ref[...]) * (1.0 + mod_ref[0, 1:2, :]) + mod_ref[0, 0:1, :]
    hb = h.astype(BF16)

    def seg(bounds):
        return jnp.dot(hb, w_ref[:, bounds[0]:bounds[1]], preferred_element_type=F32)

    swa_ref[0] = seg(SEG_SWA).astype(BF16)
    fox_ref[0] = seg(SEG_FOX).astype(BF16)
    nsaq_ref[0] = seg(SEG_NSAQ).astype(BF16)
    for slot, (bounds, out_ref) in enumerate(((SEG_KC, kc_ref), (SEG_VC, vc_ref))):
        pack_scr[slot] = seg(bounds)
        for tok in range(CMP_STRIDE):
            out_ref[0, :, tok * LANES:(tok + 1) * LANES] = pack_scr[
                slot, pl.ds(tok, x.shape[0] // CMP_STRIDE, stride=CMP_STRIDE), :].astype(BF16)
    k2_ref[0] = seg(SEG_K2).astype(BF16)
    misc_ref[0] = seg(SEG_MISC)

    rows = x.shape[0]
    extra_row = lax.broadcasted_iota(jnp.int32, (VT_ROWS - HEAD_DIM, rows), 0)
    extra = jnp.where(extra_row == 0, 1.0, 0.0).astype(BF16)
    values = seg(SEG_V)
    for c in range(values.shape[1] // LANES):
        vt = values[:, c * LANES:(c + 1) * LANES].T.astype(BF16)
        for half in range(2):
            base = (2 * c + half) * VT_ROWS
            vt_ref[0, base:base + HEAD_DIM, :] = vt[half * HEAD_DIM:(half + 1) * HEAD_DIM, :]
            vt_ref[0, base + HEAD_DIM:base + VT_ROWS, :] = extra


def _in_proj(x, mod, gain, w, layer):
    b, s, d = x.shape
    n = w.shape[2]

    def rows(width, dtype):
        return (pl.BlockSpec((1, ROW_TILE, width), lambda i, j: (i, j, 0)), jax.ShapeDtypeStruct((b, s, width), dtype))

    packed = (pl.BlockSpec((1, ROW_TILE // CMP_STRIDE, CMP_STRIDE * LANES), lambda i, j: (i, j, 0)),
              jax.ShapeDtypeStruct((b, s // CMP_STRIDE, CMP_STRIDE * LANES), BF16))
    vt_rows = (SEG_V[1] - SEG_V[0]) // HEAD_DIM * VT_ROWS
    outs = [rows(SEG_SWA[1] - SEG_SWA[0], BF16), rows(SEG_FOX[1] - SEG_FOX[0], BF16),
            rows(SEG_NSAQ[1] - SEG_NSAQ[0], BF16), packed, packed, rows(SEG_K2[1] - SEG_K2[0], BF16),
            rows(SEG_MISC[1] - SEG_MISC[0], F32),
            (pl.BlockSpec((1, vt_rows, ROW_TILE), lambda i, j: (i, 0, j)),
             jax.ShapeDtypeStruct((b, vt_rows, s), BF16))]
    return pl.pallas_call(
        _in_proj_kernel,
        grid=(b, s // ROW_TILE),
        in_specs=[pl.BlockSpec((1, ROW_TILE, d), lambda i, j: (i, j, 0)),
                  pl.BlockSpec((1, ADA_CHUNKS, d), lambda i, j: (i, 0, 0)),
                  pl.BlockSpec((1, d), lambda i, j: (0, 0)),
                  pl.BlockSpec((None, d, n), lambda i, j: (layer, 0, 0))],
        out_specs=[spec for spec, _ in outs],
        out_shape=[shape for _, shape in outs],
        scratch_shapes=[pltpu.VMEM((2, ROW_TILE, LANES), F32)],
        compiler_params=_params(2),
    )(x, mod, gain, w)


def _banded_kernel(*refs, n_back, n_groups, has_sink, t):
    if has_sink:
        sink_ref, q_ref, k_ref, vt_ref, bias_ref, o_ref = refs
    else:
        q_ref, k_ref, vt_ref, bias_ref, o_ref = refs
    i = pl.program_id(1)
    lo = _lane_lo((t, LANES))
    n_tiles = n_back + 1

    def run(all_valid):
        starts = [pl.multiple_of(jnp.maximum(i - n_back + tt, 0) * t, t) for tt in range(n_tiles)]
        k_tiles = [k_ref[0, pl.ds(start, t), :] for start in starts]

        def scores(g):
            qg = q_ref[0, :, g * LANES:(g + 1) * LANES]
            zero = jnp.zeros_like(qg)
            qms = (jnp.where(lo, qg, zero), jnp.where(lo, zero, qg))
            return [[bias_ref[2 * g + half, tt] + _dot_nt(k_tiles[tt], qms[half]) for tt in range(n_tiles)]
                    for half in range(2)]

        def softmax_pv(g, sts):
            pair = []
            for half in range(2):
                tiles = sts[half]
                if not all_valid:
                    tiles = [jnp.where(i - n_back + tt >= 0, st, NEG) if tt < n_back else st
                             for tt, st in enumerate(tiles)]
                m = None
                for st in tiles:
                    part = st.reshape(t // SUBLANES, SUBLANES, t).max(axis=0)
                    m = part if m is None else jnp.maximum(m, part)
                m = _all_sublanes(m, jnp.maximum)
                if has_sink:
                    sink = sink_ref[2 * g + half] * LOG2E
                    m = jnp.maximum(m, sink)
                acc = None
                for tt, st in enumerate(tiles):
                    p = jnp.exp2((st.reshape(t // SUBLANES, SUBLANES, t) - m[None]).reshape(t, t).astype(BF16))
                    part = jnp.dot(vt_ref[0, half * VT_ROWS:(half + 1) * VT_ROWS, pl.ds(starts[tt], t)], p,
                                   preferred_element_type=F32)
                    acc = part if acc is None else acc + part
                denom = _all_sublanes(acc[HEAD_DIM:HEAD_DIM + SUBLANES, :], jnp.add)
                if has_sink:
                    denom = denom + jnp.exp2(sink - m)
                out = acc[0:HEAD_DIM, :].reshape(HEAD_DIM // SUBLANES, SUBLANES, t) / denom[None]
                pair.append(out.reshape(HEAD_DIM, t))
            o_ref[0, :, g * LANES:(g + 1) * LANES] = jnp.concatenate(pair, axis=0).T.astype(o_ref.dtype)

        pending = scores(0)
        for g in range(n_groups):
            current = pending
            if g + 1 < n_groups:
                pending = scores(g + 1)
            softmax_pv(g, current)

    @pl.when(i >= n_back)
    def _():
        run(True)

    @pl.when(i < n_back)
    def _():
        run(False)


def _banded_attention(q_arr, k_arr, k_blk, vt, vt_blk, bias, sinks=None):
    b, s, _ = q_arr.shape
    n_pos, n_tiles, t = bias.shape[0], bias.shape[1], bias.shape[2]
    width = n_pos * HEAD_DIM
    in_specs = [pl.BlockSpec((1, t, width), lambda i, j: (i, j, 0)),
                pl.BlockSpec((1, s, LANES), lambda i, j: (i, 0, k_blk)),
                pl.BlockSpec((1, 2 * VT_ROWS, s), lambda i, j: (i, vt_blk, 0)),
                pl.BlockSpec(bias.shape, lambda i, j: (0, 0, 0, 0))]
    args = [q_arr, k_arr, vt, bias]
    if sinks is not None:
        in_specs = [pl.BlockSpec(memory_space=pltpu.SMEM)] + in_specs
        args = [sinks] + args
    return pl.pallas_call(
        functools.partial(_banded_kernel, n_back=n_tiles - 1, n_groups=n_pos // 2, has_sink=sinks is not None, t=t),
        grid=(b, s // t),
        in_specs=in_specs,
        out_specs=pl.BlockSpec((1, t, width), lambda i, j: (i, j, 0)),
        out_shape=jax.ShapeDtypeStruct((b, s, width), BF16),
        compiler_params=_params(2),
    )(*args)


def _all_sublanes(x, op):
    for shift in (4, 2, 1):
        x = op(x, pltpu.roll(x, shift, 0))
    return x


def _flash_init(m_scr, acc_scr):
    m_scr[...] = jnp.full(m_scr.shape, NEG, F32)
    acc_scr[...] = jnp.zeros(acc_scr.shape, F32)


def _flash_update(h, st_ref, tile_max, vt_h, m_scr, acc_scr):
    tk, tq = st_ref.shape
    m_prev = m_scr[h]
    m_new = _all_sublanes(jnp.maximum(m_prev, tile_max), jnp.maximum)
    alpha = jnp.exp2(m_prev - m_new)
    p = jnp.exp2((st_ref[...].reshape(tk // SUBLANES, SUBLANES, tq) - m_new[None]).reshape(tk, tq).astype(BF16))
    acc = acc_scr[h].reshape(VT_ROWS // SUBLANES, SUBLANES, tq) * alpha[None]
    acc_scr[h] = acc.reshape(VT_ROWS, tq) + jnp.dot(vt_h, p, preferred_element_type=F32)
    m_scr[h] = m_new


def _flash_finish(o_ref, n_groups, acc_scr):
    tq = acc_scr.shape[2]
    for g in range(n_groups):
        pair = []
        for h in (2 * g, 2 * g + 1):
            denom = _all_sublanes(acc_scr[h, HEAD_DIM:HEAD_DIM + SUBLANES, :], jnp.add)
            out = acc_scr[h, 0:HEAD_DIM, :].reshape(HEAD_DIM // SUBLANES, SUBLANES, tq) / denom[None]
            pair.append(out.reshape(HEAD_DIM, tq))
        o_ref[0, :, g * LANES:(g + 1) * LANES] = jnp.concatenate(pair, axis=0).T.astype(o_ref.dtype)


def _flash_pipeline(i, n_heads, qk_scores, bias_tile, vt_slab, s_scr, tmax_scr, m_scr, acc_scr):
    def qk_head(thunk, h, j, slot):
        st = thunk() + bias_tile(h, j)
        s_scr[slot, h] = st
        tmax_scr[slot, h] = st.reshape(st.shape[0] // SUBLANES, SUBLANES, st.shape[1]).max(axis=0)

    def softmax_head(h, j, slot):
        _flash_update(h, s_scr.at[slot, h], tmax_scr[slot, h], vt_slab(h, j), m_scr, acc_scr)

    def softmax_all(j, slot):
        for h in range(n_heads):
            softmax_head(h, j, slot)

    def stage(j_qk, slot_qk, j_sm, slot_sm):
        thunks = qk_scores(j_qk)
        for h in range(n_heads):
            qk_head(thunks[h], h, j_qk, slot_qk)
            softmax_head(h, j_sm, slot_sm)

    for h, thunk in enumerate(qk_scores(0)):
        qk_head(thunk, h, 0, 0)

    def body(trip, carry):
        j = 2 * trip
        stage(j + 1, 1, j, 0)
        stage(j + 2, 0, j + 1, 1)
        return carry

    lax.fori_loop(0, i // 2, body, 0)
    last = 2 * (i // 2)

    @pl.when(i % 2 == 0)
    def _():
        softmax_all(last, 0)

    @pl.when(i % 2 == 1)
    def _():
        stage(last + 1, 1, last, 0)
        softmax_all(last + 1, 1)


def _fox_aug_kernel(misc_ref, fbias_ref, tri_ref, o_ref):
    s_len, width = misc_ref.shape[1], misc_ref.shape[2]
    term = lax.broadcasted_iota(jnp.int32, (LANES, width), 1) % HEAD_DIM
    carry = jnp.zeros((1, width), F32)
    for c in range(s_len // LANES):
        z = misc_ref[0, c * LANES:(c + 1) * LANES, :] + fbias_ref[...]
        log_f = jnp.minimum(z, 0.0) - jnp.log1p(jnp.exp(-jnp.abs(z)))
        cum = jnp.dot(tri_ref[...], log_f, precision=HIGHEST, preferred_element_type=F32) + carry
        carry = cum[LANES - 1:LANES, :]
        x = cum * (-LOG2E)
        hi = x.astype(BF16).astype(F32)
        rest = x - hi
        mid = rest.astype(BF16).astype(F32)
        low = rest - mid
        out = jnp.where(term == 0, hi, jnp.where(term == 1, mid, jnp.where(term == 2, low, 0.0)))
        o_ref[0, c * LANES:(c + 1) * LANES, :] = out.astype(BF16)


def _fox_key_terms(misc, fbias):
    b, s, width = misc.shape
    tri = jnp.asarray(np.tril(np.ones((LANES, LANES), np.float32)))
    return pl.pallas_call(
        _fox_aug_kernel,
        grid=(b,),
        in_specs=[pl.BlockSpec((1, s, width), lambda i: (i, 0, 0)),
                  pl.BlockSpec((1, width), lambda i: (0, 0)),
                  pl.BlockSpec((LANES, LANES), lambda i: (0, 0))],
        out_specs=pl.BlockSpec((1, s, width), lambda i: (i, 0, 0)),
        out_shape=jax.ShapeDtypeStruct((b, s, width), BF16),
        compiler_params=_params(1),
    )(misc, fbias, tri)


def _fox_kernel(q_ref, k_ref, aug_ref, vt_ref, mask_ref, o_ref, qs_scr, s_scr, tmax_scr, m_scr, acc_scr):
    t = FLASH_T
    i = pl.program_id(1)
    lo = _lane_lo((t, LANES))
    lane = lax.broadcasted_iota(jnp.int32, (t, LANES), 1)
    ones = jnp.where(lane % HEAD_DIM < KEY_BIAS_TERMS, 1.0, 0.0).astype(BF16)
    n_groups = FOX_HEADS // 2
    for g in range(n_groups):
        qg = q_ref[0, :, g * LANES:(g + 1) * LANES]
        qs_scr[2 * g] = jnp.where(lo, qg, ones)
        qs_scr[2 * g + 1] = jnp.where(lo, ones, qg)
    _flash_init(m_scr, acc_scr)

    def qk_scores(j):
        start = pl.multiple_of(j * t, t)
        scores = []
        for g in range(n_groups):
            k_tile = k_ref[0, pl.ds(start, t), g * LANES:(g + 1) * LANES]
            a_tile = aug_ref[0, pl.ds(start, t), g * LANES:(g + 1) * LANES]
            k_sel = (jnp.where(lo, k_tile, a_tile), jnp.where(lo, a_tile, k_tile))
            for half in range(2):
                scores.append(functools.partial(lambda k, h: _dot_nt(k, qs_scr[h]), k_sel[half], 2 * g + half))
        return scores

    def bias_tile(h, j):
        return mask_ref[jnp.minimum(i - j, 1)]

    def vt_slab(h, j):
        return vt_ref[0, h * VT_ROWS:(h + 1) * VT_ROWS, pl.ds(pl.multiple_of(j * t, t), t)]

    _flash_pipeline(i, FOX_HEADS, qk_scores, bias_tile, vt_slab, s_scr, tmax_scr, m_scr, acc_scr)
    _flash_finish(o_ref, n_groups, acc_scr)


def _fox_attention(fox_qk, key_terms, vt):
    b, s, _ = fox_qk.shape
    width = FOX_HEADS * HEAD_DIM
    t = FLASH_T
    idx = np.arange(t)
    diag = np.where(idx[:, None] <= idx[None, :], 0.0, NEG)
    masks = jnp.asarray(np.stack([diag, np.zeros((t, t))]).astype(np.float32))
    return pl.pallas_call(
        _fox_kernel,
        grid=(b, s // t),
        in_specs=[pl.BlockSpec((1, t, width), lambda i, j: (i, j, 0)),
                  pl.BlockSpec((1, s, width), lambda i, j: (i, 0, 1)),
                  pl.BlockSpec((1, s, width), lambda i, j: (i, 0, 0)),
                  pl.BlockSpec((1, FOX_HEADS * VT_ROWS, s), lambda i, j: (i, VT_FOX_BLOCK, 0)),
                  pl.BlockSpec(masks.shape, lambda i, j: (0, 0, 0))],
        out_specs=pl.BlockSpec((1, t, width), lambda i, j: (i, j, 0)),
        out_shape=jax.ShapeDtypeStruct((b, s, width), BF16),
        scratch_shapes=[pltpu.VMEM((FOX_HEADS, t, LANES), BF16),
                        pltpu.VMEM((2, FOX_HEADS, t, t), F32),
                        pltpu.VMEM((2, FOX_HEADS, SUBLANES, t), F32),
                        pltpu.VMEM((FOX_HEADS, SUBLANES, t), F32),
                        pltpu.VMEM((FOX_HEADS, VT_ROWS, t), F32)],
        compiler_params=_params(2),
    )(fox_qk, fox_qk, key_terms, vt, masks)


def _split3(x):
    hi = x.astype(BF16)
    rest = x - hi.astype(F32)
    mid = rest.astype(BF16)
    low = (rest - mid.astype(F32)).astype(BF16)
    return hi, mid, low


def _compress_kernel(x_ref, pe_ref, w1a_ref, w1b_ref, w2_ref, o_ref):
    x = x_ref[0]
    n_rows = x.shape[0]

    def mm3(lhs, w_ref):
        return sum(jnp.dot(lhs, w_ref[piece], preferred_element_type=F32) for piece in range(3))

    first = mm3(x, w1a_ref)
    second = mm3(x, w1b_ref)
    pe_term = sum(mm3(piece, w1a_ref) for piece in _split3(pe_ref[0])) \
        + sum(mm3(piece, w1b_ref) for piece in _split3(pe_ref[1]))
    pre = first + pltpu.roll(second, n_rows - 1, 0) + pe_term[0:1, :]
    hid = 0.5 * pre * (1.0 + jnp.tanh(math.sqrt(2.0 / math.pi) * (pre + 0.044715 * (pre * pre * pre))))
    o_ref[0] = jnp.dot(hid, w2_ref[...], precision=HIGHEST, preferred_element_type=F32)


def _compress(x, pos, w1, w2):
    b, n_rows, feat = x.shape
    half = CMP_LEN // 2
    eye = jnp.eye(2, dtype=F32)
    w1 = w1.astype(F32).reshape(CMP_LEN, HEAD_DIM, CMP_HIDDEN)
    w1a = jnp.stack(_split3(jnp.einsum('ldj,hg->lhdgj', w1[:half], eye).reshape(feat, 2 * CMP_HIDDEN)))
    w1b = jnp.stack(_split3(jnp.einsum('ldj,hg->lhdgj', w1[half:], eye).reshape(feat, 2 * CMP_HIDDEN)))
    w2 = jnp.einsum('jd,hg->hjgd', w2.astype(F32), eye).reshape(2 * CMP_HIDDEN, LANES)
    w2 = jnp.concatenate([w2, jnp.roll(w2, HEAD_DIM, axis=1)], axis=1)
    pe = jnp.broadcast_to(pos.astype(F32).reshape(2, half, 1, HEAD_DIM), (2, half, 2, HEAD_DIM))
    pe = jnp.broadcast_to(pe.reshape(2, 1, feat), (2, 8, feat))
    return pl.pallas_call(
        _compress_kernel,
        grid=(b,),
        in_specs=[pl.BlockSpec((1, n_rows, feat), lambda i: (i, 0, 0)),
                  pl.BlockSpec((2, 8, feat), lambda i: (0, 0, 0)),
                  pl.BlockSpec((3, feat, 2 * CMP_HIDDEN), lambda i: (0, 0, 0)),
                  pl.BlockSpec((3, feat, 2 * CMP_HIDDEN), lambda i: (0, 0, 0)),
                  pl.BlockSpec((2 * CMP_HIDDEN, 2 * LANES), lambda i: (0, 0))],
        out_specs=pl.BlockSpec((1, n_rows, 2 * LANES), lambda i: (i, 0, 0)),
        out_shape=jax.ShapeDtypeStruct((b, n_rows, 2 * LANES), F32),
        compiler_params=_params(1),
    )(x, pe, w1a, w1b, w2)


def _select_kernel(q_ref, kc_ref, vct_ref, bias_ref, o_ref, mb_ref, count_scr, psum_scr):
    tq = SEL_TQ
    i = pl.program_id(0)
    lo = _lane_lo((tq, LANES))
    n_rows = kc_ref.shape[1]
    lo_k = _lane_lo((n_rows, LANES))
    n_grp = n_rows // SUBLANES

    k_own = kc_ref[0, :, 0:LANES]
    k_swapped = kc_ref[0, :, LANES:2 * LANES]
    hi = k_own.astype(BF16)
    low = (k_swapped - k_swapped.astype(BF16).astype(F32)).astype(BF16)
    k_sel = (jnp.where(lo_k, hi, low), jnp.where(lo_k, low, hi))

    def scores(g):
        qg = q_ref[0, :, g * LANES:(g + 1) * LANES]
        swapped = pltpu.roll(qg.astype(F32), HEAD_DIM, 1).astype(BF16)
        q_dup = (jnp.where(lo, qg, swapped), jnp.where(lo, swapped, qg))
        return [bias_ref[2 * g + half, 0] + _dot_nt(k_sel[half], q_dup[half]) for half in range(2)]

    query = i * tq + lax.broadcasted_iota(jnp.int32, (SUBLANES, tq), 1)
    has_keys = query >= CMP_LEN - 1
    p_sum = [None, None]

    def softmax_pv(g, sts):
        pair = []
        for half in range(2):
            s3 = sts[half].reshape(n_grp, SUBLANES, tq)
            m = _all_sublanes(s3.max(axis=0), jnp.maximum)
            e = jnp.exp2(s3 - m[None])
            inv = jnp.where(has_keys, 1.0 / _all_sublanes(e.sum(axis=0), jnp.add), 0.0)
            p = e * inv[None]
            p_sum[half] = p if p_sum[half] is None else p_sum[half] + p
            pair.append(jnp.dot(vct_ref[0, half * HEAD_DIM:(half + 1) * HEAD_DIM, :],
                                p.reshape(n_rows, tq).astype(BF16), preferred_element_type=F32))
        o_ref[0, :, g * LANES:(g + 1) * LANES] = jnp.concatenate(pair, axis=0).T.astype(o_ref.dtype)

    n_groups = NSA_HEADS // 2
    pending = scores(0)
    for g in range(n_groups):
        current = pending
        if g + 1 < n_groups:
            pending = scores(g + 1)
        softmax_pv(g, current)

    n_blk = HEAD_DIM
    blk_grp = n_blk // SUBLANES
    sub = lax.broadcasted_iota(jnp.int32, (SUBLANES, tq), 0)
    q_blk = query // SLC_BLOCK
    kind = []
    for r in range(blk_grp):
        blk = sub + r * SUBLANES
        behind = q_blk - blk
        forced = jnp.where(blk == 0, 1, 0) + jnp.where(behind == 0, 1, 0) + jnp.where(behind == 1, 1, 0)
        kind.append(jnp.where(behind < 0, 2, jnp.minimum(forced, 1)))
    masks = []
    per_blk = SLC_BLOCK // CMP_STRIDE
    n_real = n_rows // per_blk
    n_lane_chunks = tq // LANES
    psum_scr[:, 0:SUBLANES, :] = jnp.zeros((n_lane_chunks, SUBLANES, LANES), F32)
    for half in (1, 0):
        p_rows = p_sum[half].reshape(n_rows, tq)
        for c in range(n_lane_chunks):
            psum_scr[c, SUBLANES:SUBLANES + n_rows, :] = p_rows[:, c * LANES:(c + 1) * LANES]

        def every_fourth(offset):
            return jnp.concatenate([psum_scr[c, pl.ds(SUBLANES + offset, n_real, stride=per_blk), :]
                                    for c in range(n_lane_chunks)], axis=1)

        imp = (0.5 * (every_fourth(-1) + every_fourth(3))
               + (every_fourth(0) + every_fourth(1) + every_fourth(2)))
        if n_real < n_blk:
            imp = jnp.concatenate([imp, jnp.zeros((n_blk - n_real, tq), F32)], axis=0)
        rows = [jnp.where(kind[r] == 2, NEG, jnp.where(kind[r] == 1, FORCE, imp[r * SUBLANES:(r + 1) * SUBLANES, :]))
                for r in range(blk_grp)]
        count_scr[...] = jnp.zeros(count_scr.shape, jnp.int32)
        for r_other in range(blk_grp):
            @pl.when(r_other * SUBLANES * SLC_BLOCK < (i + 1) * tq)
            def _(r_other=r_other, rows=rows):
                counts = [None] * blk_grp
                for s_other in range(SUBLANES):
                    row = jnp.broadcast_to(rows[r_other][s_other:s_other + 1, :], (SUBLANES, tq))
                    for r in range(blk_grp):
                        if r > r_other:
                            beats = jnp.where(row >= rows[r], 1, 0)
                        elif r < r_other:
                            beats = jnp.where(row > rows[r], 1, 0)
                        else:
                            beats = jnp.where(sub > s_other, jnp.where(row >= rows[r], 1, 0),
                                              jnp.where(row > rows[r], 1, 0))
                        counts[r] = beats if counts[r] is None else counts[r] + beats
                for r in range(blk_grp):
                    count_scr[r] = count_scr[r] + counts[r]
        masks.extend(jnp.where(count_scr[r] < TOPK, 0.0, NEG) for r in range(blk_grp))
    for c in range(tq // LANES):
        mb_ref[0, c * LANES:(c + 1) * LANES, :] = jnp.concatenate(
            [mk[:, c * LANES:(c + 1) * LANES] for mk in masks], axis=0).T.astype(BF16)


def _select(nsa_q, k_cmp, v_cmp, bias_c):
    b, s, width = nsa_q.shape
    n_rows = k_cmp.shape[1]
    n_blk = HEAD_DIM
    assert CMP_LEN == 2 * CMP_STRIDE and SLC_BLOCK == 4 * CMP_STRIDE
    tq = SEL_TQ
    vct = v_cmp[:, :, 0:LANES].transpose(0, 2, 1).astype(BF16)
    return pl.pallas_call(
        _select_kernel,
        grid=(s // tq, b),
        in_specs=[pl.BlockSpec((1, tq, width), lambda j, i: (i, j, 0)),
                  pl.BlockSpec((1, n_rows, 2 * LANES), lambda j, i: (i, 0, 0)),
                  pl.BlockSpec((1, LANES, n_rows), lambda j, i: (i, 0, 0)),
                  pl.BlockSpec((NSA_HEADS, 1, n_rows, tq), lambda j, i: (0, j, 0, 0))],
        out_specs=[pl.BlockSpec((1, tq, width), lambda j, i: (i, j, 0)),
                   pl.BlockSpec((1, tq, LANES), lambda j, i: (i, j, 0))],
        out_shape=[jax.ShapeDtypeStruct((b, s, width), BF16),
                   jax.ShapeDtypeStruct((b, s, LANES), BF16)],
        scratch_shapes=[pltpu.VMEM((n_blk // SUBLANES, SUBLANES, tq), jnp.int32),
                        pltpu.VMEM((tq // LANES, SUBLANES + n_rows, LANES), F32)],
        compiler_params=_params(2),
    )(nsa_q, k_cmp, vct, bias_c)


def _slc_kernel(q_ref, mb_ref, k_ref, e2_ref, vt_ref, bias_ref, o_ref, qs_scr, s_scr, tmax_scr, m_scr, acc_scr, *,
                n_near):
    t = FLASH_T
    i = pl.program_id(1)
    lo = _lane_lo((t, LANES))
    n_groups = NSA_HEADS // 2
    mb = mb_ref[0]
    for g in range(n_groups):
        qg = q_ref[0, :, g * LANES:(g + 1) * LANES]
        qs_scr[2 * g] = jnp.where(lo, qg, mb)
        qs_scr[2 * g + 1] = jnp.where(lo, mb, qg)
    _flash_init(m_scr, acc_scr)

    def qk_scores(j):
        start = pl.multiple_of(j * t, t)
        k_tile = k_ref[0, pl.ds(start, t), :]
        e_tile = e2_ref[pl.ds(start, t), :]
        k_sel = (jnp.where(lo, k_tile, e_tile), jnp.where(lo, e_tile, k_tile))
        return [functools.partial(lambda k, pos: _dot_nt(k, qs_scr[pos]), k_sel[pos % 2], pos)
                for pos in range(NSA_HEADS)]

    def bias_tile(pos, j):
        return bias_ref[pos, jnp.minimum(i - j, n_near)]

    def vt_slab(pos, j):
        kv = pos % 2
        return vt_ref[0, kv * VT_ROWS:(kv + 1) * VT_ROWS, pl.ds(pl.multiple_of(j * t, t), t)]

    _flash_pipeline(i, NSA_HEADS, qk_scores, bias_tile, vt_slab, s_scr, tmax_scr, m_scr, acc_scr)
    _flash_finish(o_ref, n_groups, acc_scr)


def _slc_attention(nsa_q, mask_bias, k2, e2, vt, bias):
    b, s, width = nsa_q.shape
    t = FLASH_T
    n_near = bias.shape[1] - 1
    return pl.pallas_call(
        functools.partial(_slc_kernel, n_near=n_near),
        grid=(b, s // t),
        in_specs=[pl.BlockSpec((1, t, width), lambda i, j: (i, j, 0)),
                  pl.BlockSpec((1, t, LANES), lambda i, j: (i, j, 0)),
                  pl.BlockSpec((1, s, LANES), lambda i, j: (i, 0, 0)),
                  pl.BlockSpec((s, LANES), lambda i, j: (0, 0)),
                  pl.BlockSpec((1, 2 * VT_ROWS, s), lambda i, j: (i, VT_SLC_BLOCK, 0)),
                  pl.BlockSpec(bias.shape, lambda i, j: (0, 0, 0, 0))],
        out_specs=pl.BlockSpec((1, t, width), lambda i, j: (i, j, 0)),
        out_shape=jax.ShapeDtypeStruct((b, s, width), BF16),
        scratch_shapes=[pltpu.VMEM((NSA_HEADS, t, LANES), BF16),
                        pltpu.VMEM((2, NSA_HEADS, t, t), F32),
                        pltpu.VMEM((2, NSA_HEADS, SUBLANES, t), F32),
                        pltpu.VMEM((NSA_HEADS, SUBLANES, t), F32),
                        pltpu.VMEM((NSA_HEADS, VT_ROWS, t), F32)],
        compiler_params=_params(2),
    )(nsa_q, mask_bias, k2, e2, vt, bias)


def _block_onehot(s_len):
    blk = np.arange(s_len)[:, None] // SLC_BLOCK
    lane = np.arange(LANES)[None, :] % HEAD_DIM
    return jnp.asarray((blk == lane).astype(np.float32), dtype=BF16)


def _mix_ffn_kernel(x_ref, mod_ref, swa_ref, fox_ref, cmp_ref, slc_ref, win_ref, misc_ref, expand_ref, gn_ref,
                    w_ref, post_ref, pre_ref, wg_ref, wu_ref, wd_ref, fpost_ref, o_ref):
    n_swa = SWA_HEADS * HEAD_DIM
    n_fox = FOX_HEADS * HEAD_DIM
    n_nsa = NSA_HEADS * HEAD_DIM
    gate = jax.nn.sigmoid(misc_ref[0])
    gate_hi = gate.astype(BF16)
    gate_lo = (gate - gate_hi.astype(F32)).astype(BF16)
    gates = (jnp.dot(gate_hi, expand_ref[...], preferred_element_type=F32)
             + jnp.dot(gate_lo, expand_ref[...], preferred_element_type=F32))
    o_nsa = (gates[:, 0:n_nsa] * cmp_ref[0].astype(F32) + gates[:, n_nsa:2 * n_nsa] * slc_ref[0].astype(F32)
             + gates[:, 2 * n_nsa:3 * n_nsa] * win_ref[0].astype(F32))
    a = _rms(swa_ref[0].astype(F32), gn_ref[:, 0:n_swa]).astype(BF16)
    b = _rms(fox_ref[0].astype(F32), gn_ref[:, n_swa:n_swa + n_fox]).astype(BF16)
    c = _rms(o_nsa, gn_ref[:, n_swa + n_fox:]).astype(BF16)
    y = (jnp.dot(a, w_ref[0:n_swa, :], preferred_element_type=F32)
         + jnp.dot(b, w_ref[n_swa:n_swa + n_fox, :], preferred_element_type=F32)
         + jnp.dot(c, w_ref[n_swa + n_fox:, :], preferred_element_type=F32))
    x = x_ref[0] + mod_ref[0, 2:3, :] * _rms(y, post_ref[...])

    h = (_rms(x, pre_ref[...]) * (1.0 + mod_ref[0, 4:5, :]) + mod_ref[0, 3:4, :]).astype(BF16)
    y = jnp.zeros(x.shape, F32)
    for chunk in range(wg_ref.shape[0]):
        gate = jnp.dot(h, wg_ref[chunk], preferred_element_type=F32)
        up = jnp.dot(h, wu_ref[chunk], preferred_element_type=F32)
        act = (gate * jax.nn.sigmoid(gate) * up).astype(BF16)
        y = y + jnp.dot(act, wd_ref[chunk], preferred_element_type=F32)
    o_ref[0] = x + mod_ref[0, 5:6, :] * _rms(y, fpost_ref[...])


def _gate_expansion():
    expand = np.zeros((LANES, 3 * NSA_HEADS * HEAD_DIM), np.float32)
    for branch in range(3):
        for p in range(NSA_HEADS):
            col = (branch * NSA_HEADS + p) * HEAD_DIM
            expand[GATE_LANE + 8 * branch + p, col:col + HEAD_DIM] = 1.0
    return jnp.asarray(expand, dtype=BF16)


def _mix_ffn(x, mod, o_swa, o_fox, o_cmp, o_slc, o_win, misc, gn, w, post, pre, wg, wu, wd, fpost, layer):
    b, s, d = x.shape
    expand = _gate_expansion()
    n_chunks = wg.shape[1]

    def rows(width):
        return pl.BlockSpec((1, ROW_TILE, width), lambda i, j: (i, j, 0))

    def whole(shape):
        return pl.BlockSpec(shape, lambda i, j: (0,) * len(shape), pipeline_mode=pl.Buffered(1))

    def of_layer(shape):
        return pl.BlockSpec((None,) + shape, lambda i, j: (layer,) + (0,) * len(shape),
                            pipeline_mode=pl.Buffered(1))

    vec = pl.BlockSpec((1, d), lambda i, j: (0, 0))
    return pl.pallas_call(
        _mix_ffn_kernel,
        grid=(b, s // ROW_TILE),
        in_specs=[rows(d),
                  pl.BlockSpec((1, ADA_CHUNKS, d), lambda i, j: (i, 0, 0)),
                  rows(o_swa.shape[2]), rows(o_fox.shape[2]), rows(o_cmp.shape[2]), rows(o_slc.shape[2]),
                  rows(o_win.shape[2]), rows(LANES),
                  whole(expand.shape), vec, of_layer((d, d)), vec, vec,
                  of_layer((n_chunks, d, FFN_CHUNK)), of_layer((n_chunks, d, FFN_CHUNK)),
                  of_layer((n_chunks, FFN_CHUNK, d)),
                  vec],
        out_specs=rows(d),
        out_shape=jax.ShapeDtypeStruct((b, s, d), F32),
        compiler_params=_params(2),
    )(x, mod, o_swa, o_fox, o_cmp, o_slc, o_win, misc, expand, gn, w, post, pre, wg, wu, wd, fpost)


def _forget_lanes():
    lanes, heads = [], []
    for h in range(FOX_HEADS):
        base = (h // 2) * LANES + (HEAD_DIM if h % 2 == 0 else 0)
        for j in range(KEY_BIAS_TERMS):
            lanes.append(base + j)
            heads.append(h)
    return np.array(lanes), np.array(heads)


def _in_proj_layout():
    d = HEAD_DIM
    o_qa, o_ka, o_va, o_qb, o_kb, o_vb, o_fb, o_qc = 0, 256, 384, 512, 768, 1024, 1280, 1284
    o_kc, o_vc, o_ksl, o_vsl, o_kw, o_vw, o_gc = 1796, 1924, 2052, 2180, 2308, 2436, 2564
    scale = LOG2E / math.sqrt(d)

    def head_cols(base, heads):
        return np.concatenate([np.arange(base + h * d, base + (h + 1) * d) for h in heads])

    def span(base, width):
        return np.arange(base, base + width)

    cols = [head_cols(o_qa, SWA_POS), span(o_ka, 128),
            span(o_qb, 256), span(o_kb, 256),
            head_cols(o_qc, NSA_POS),
            span(o_kc, 128), span(o_vc, 128),
            span(o_ksl, 128), span(o_kw, 128),
            span(o_vb, 256), span(o_va, 128), span(o_vsl, 128), span(o_vw, 128)]
    scales = [np.full(256, scale), np.ones(128), np.full(256, scale), np.ones(256), np.full(512, scale),
              np.ones(256), np.ones(256), np.ones(640)]
    lanes, heads = _forget_lanes()
    misc_cols = np.zeros(SEG_MISC[1] - SEG_MISC[0], np.int64)
    misc_scale = np.zeros(SEG_MISC[1] - SEG_MISC[0])
    misc_cols[lanes] = o_fb + heads
    misc_scale[lanes] = 1.0
    for branch in range(3):
        for p, h in enumerate(NSA_POS):
            misc_cols[GATE_LANE + 8 * branch + p] = o_gc + h * 3 + branch
            misc_scale[GATE_LANE + 8 * branch + p] = 1.0
    cols.append(misc_cols)
    scales.append(misc_scale)
    return np.concatenate(cols), np.concatenate(scales).astype(np.float32)


def _head_perm(pos):
    return np.concatenate([np.arange(h * HEAD_DIM, (h + 1) * HEAD_DIM) for h in pos])


def kernel(x, c, rel_bias, ada_w, ada_b, attn_pre_norm, attn_post_norm, ffn_pre_norm, ffn_post_norm, w_in,
           forget_bias, swa_sinks, cmp_pos, cmp_w1, cmp_w2, group_norm, w_out, ffn_w_gate, ffn_w_up, ffn_w_down):
    b, s, d = x.shape
    depth = w_in.shape[0]
    hidden = ffn_w_gate.shape[2]
    assert s % (2 * FLASH_T) == 0 and s // SLC_BLOCK <= HEAD_DIM and hidden % FFN_CHUNK == 0

    cols, scales = _in_proj_layout()
    n_main = SEG_MISC[0]
    breaks = np.flatnonzero(np.diff(cols[:n_main]) != 1) + 1
    runs = np.split(cols[:n_main], breaks)
    w_main = jnp.concatenate([w_in[:, :, int(r[0]):int(r[-1]) + 1] for r in runs], axis=2) * scales[:n_main]
    used = np.flatnonzero(scales[n_main:] != 0)
    w_misc = jnp.zeros((depth, d, SEG_MISC[1] - SEG_MISC[0]), w_in.dtype).at[:, :, used].set(
        w_in[:, :, cols[n_main:][used]])
    w_all = jnp.concatenate([w_main, w_misc], axis=2).astype(BF16)
    lanes, heads = _forget_lanes()
    fbias_all = jnp.zeros((depth, 1, SEG_MISC[1] - SEG_MISC[0]), F32).at[:, 0, lanes].set(
        forget_bias[:, heads].astype(F32))
    swa_perm = _head_perm(SWA_POS)
    nsa_perm = _head_perm(NSA_POS)
    n_swa, n_fox = SWA_HEADS * HEAD_DIM, FOX_HEADS * HEAD_DIM
    mix_perm = np.concatenate([swa_perm, n_swa + np.arange(n_fox), n_swa + n_fox + nsa_perm])
    gn_all = group_norm[:, mix_perm].astype(F32)
    w_out_all = w_out[:, mix_perm, :].astype(BF16)
    n_chunks = hidden // FFN_CHUNK
    wg_all = ffn_w_gate.reshape(depth, d, n_chunks, FFN_CHUNK).transpose(0, 2, 1, 3).astype(BF16)
    wu_all = ffn_w_up.reshape(depth, d, n_chunks, FFN_CHUNK).transpose(0, 2, 1, 3).astype(BF16)
    wd_all = ffn_w_down.reshape(depth, n_chunks, FFN_CHUNK, d).astype(BF16)

    tab_swa = rel_bias[:, np.array(SWA_POS)].astype(F32)
    tab_nsa = rel_bias[:, SWA_HEADS + np.array(NSA_POS)].astype(F32)
    bias_swa = _bias_table(tab_swa, _band_buckets_t(SWA_TILE, SWA_WINDOW))
    bias_win = _bias_table(tab_nsa, _band_buckets_t(WIN_TILE, NSA_WINDOW))
    bias_slc = _bias_table(tab_nsa, _toeplitz_buckets_t(FLASH_T, _near_tiles(FLASH_T)), subtract_last=True)
    n_rows = s // CMP_STRIDE
    bias_cmp = _bias_table(tab_nsa, _cmp_buckets_t(s, n_rows))
    e2 = _block_onehot(s)

    mod_all = _adaln(c.astype(F32), ada_w.astype(F32), ada_b.astype(F32)).reshape(depth, b, ADA_CHUNKS, d)

    for layer in range(depth):
        mod = mod_all[layer]
        swa_qk, fox_qk, nsa_q, kc, vc, k2, misc, vt = _in_proj(
            x, mod, attn_pre_norm[layer].reshape(1, d).astype(F32), w_all, layer)
        o_swa = _banded_attention(swa_qk, swa_qk, 2, vt, VT_SWA_BLOCK, bias_swa,
                                  sinks=swa_sinks[layer][np.array(SWA_POS)].astype(F32))
        o_fox = _fox_attention(fox_qk, _fox_key_terms(misc, fbias_all[layer]), vt)
        k_cmp = _compress(kc, cmp_pos[layer, 0], cmp_w1[layer, 0], cmp_w2[layer, 0])
        v_cmp = _compress(vc, cmp_pos[layer, 1], cmp_w1[layer, 1], cmp_w2[layer, 1])
        o_cmp, mask_bias = _select(nsa_q, k_cmp, v_cmp, bias_cmp)
        o_slc = _slc_attention(nsa_q, mask_bias, k2, e2, vt, bias_slc)
        o_win = _banded_attention(nsa_q, k2, 1, vt, VT_WIN_BLOCK, bias_win)
        x = _mix_ffn(x, mod, o_swa, o_fox, o_cmp, o_slc, o_win, misc, gn_all[layer].reshape(1, d),
                     w_out_all, attn_post_norm[layer].reshape(1, d).astype(F32),
                     ffn_pre_norm[layer].reshape(1, d).astype(F32), wg_all, wu_all, wd_all,
                     ffn_post_norm[layer].reshape(1, d).astype(F32), layer)
    return x
```

```python
import functools
import math

import numpy as np
import jax
import jax.numpy as jnp
from jax import lax
from jax.experimental import pallas as pl
from jax.experimental.pallas import tpu as pltpu

F32 = jnp.float32
BF16 = jnp.bfloat16
HIGHEST = lax.Precision.HIGHEST

LANES = 128
SUBLANES = 8
VMEM_LIMIT = 56 * 1024 * 1024

HEAD_DIM = 64
SWA_HEADS = 4
SWA_WINDOW = 128
FOX_HEADS = 4
NSA_HEADS = 8
CMP_LEN = 32
CMP_STRIDE = 16
CMP_HIDDEN = 2 * HEAD_DIM
SLC_BLOCK = 64
TOPK = 16
NSA_WINDOW = 512
REL_BUCKETS = 32
REL_MAX_DISTANCE = 1024
ZERO_BUCKET = -2
RMS_EPS = 1e-6
NEG = -1e30
FORCE = 1e30
ADA_CHUNKS = 6
LOG2E = math.log2(math.e)

SWA_POS = (0, 2, 1, 3)
NSA_POS = (0, 4, 1, 5, 2, 6, 3, 7)

SWA_TILE = 256
WIN_TILE = 256
FLASH_T = 256
SEL_TQ = 256
ROW_TILE = 512
FFN_CHUNK = 256
VT_ROWS = HEAD_DIM + 16
KEY_BIAS_TERMS = 3

SEG_SWA = (0, 384)
SEG_FOX = (384, 896)
SEG_NSAQ = (896, 1408)
SEG_KC = (1408, 1536)
SEG_VC = (1536, 1664)
SEG_K2 = (1664, 1920)
SEG_V = (1920, 2560)
SEG_MISC = (2560, 2816)
GATE_LANE = 8
VT_FOX_BLOCK, VT_SWA_BLOCK, VT_SLC_BLOCK, VT_WIN_BLOCK = 0, 2, 3, 4


def _params(n_grid, vmem=VMEM_LIMIT):
    return pltpu.CompilerParams(dimension_semantics=("parallel",) * n_grid, vmem_limit_bytes=vmem)


def _dot_nt(a, b):
    return lax.dot_general(a, b, (((1,), (1,)), ((), ())), preferred_element_type=F32)


def _lane_lo(shape):
    return lax.broadcasted_iota(jnp.int32, shape, len(shape) - 1) < HEAD_DIM


def _adaln_kernel(c_ref, w_ref, b_ref, o_ref):
    c = c_ref[...]
    act = c * jax.nn.sigmoid(c)
    o_ref[0] = jnp.dot(act, w_ref[0], precision=HIGHEST, preferred_element_type=F32) + b_ref[0]


def _adaln(c, ada_w, ada_b):
    depth, d, n = ada_w.shape
    b = c.shape[0]
    return pl.pallas_call(
        _adaln_kernel,
        grid=(depth, n // d),
        in_specs=[pl.BlockSpec((b, d), lambda l, j: (0, 0)),
                  pl.BlockSpec((1, d, d), lambda l, j: (l, 0, j)),
                  pl.BlockSpec((1, 1, d), lambda l, j: (l, 0, j))],
        out_specs=pl.BlockSpec((1, b, d), lambda l, j: (l, 0, j)),
        out_shape=jax.ShapeDtypeStruct((depth, b, n), F32),
        compiler_params=_params(2),
    )(c, ada_w, ada_b.reshape(depth, 1, n))


def _t5_bucket(dist):
    n = jnp.maximum(dist, 0)
    max_exact = REL_BUCKETS // 2
    nf = jnp.maximum(n, 1).astype(jnp.float32)
    large = max_exact + (jnp.log(nf / max_exact) / math.log(REL_MAX_DISTANCE / max_exact)
                         * (REL_BUCKETS - max_exact)).astype(jnp.int32)
    large = jnp.minimum(large, REL_BUCKETS - 1)
    return jnp.where(n < max_exact, n, large)


def _bias_table_kernel(tab_ref, bucket_ref, o_ref, *, subtract_last):
    n_heads = o_ref.shape[0]
    values = [[(tab_ref[k, h] - (tab_ref[REL_BUCKETS - 1, h] if subtract_last else 0.0)) * LOG2E
               for h in range(n_heads)] for k in range(REL_BUCKETS)]

    def rows(chunk, carry):
        r0 = pl.multiple_of(chunk * SUBLANES, SUBLANES)
        bucket = bucket_ref[0, pl.ds(r0, SUBLANES), :]
        accs = [jnp.where(bucket == ZERO_BUCKET, 0.0, NEG) for _ in range(n_heads)]
        for k in range(REL_BUCKETS):
            hit = bucket == k
            for h in range(n_heads):
                accs[h] = jnp.where(hit, values[k][h], accs[h])
        for h in range(n_heads):
            o_ref[h, 0, pl.ds(r0, SUBLANES), :] = accs[h]
        return carry

    lax.fori_loop(0, bucket_ref.shape[1] // SUBLANES, rows, 0)


def _bias_table(table, bucket, subtract_last=False):
    n_heads = table.shape[1]
    n, r, c = bucket.shape
    return pl.pallas_call(
        functools.partial(_bias_table_kernel, subtract_last=subtract_last),
        grid=(n,),
        in_specs=[pl.BlockSpec(memory_space=pltpu.SMEM),
                  pl.BlockSpec((1, r, c), lambda i: (i, 0, 0))],
        out_specs=pl.BlockSpec((n_heads, 1, r, c), lambda i: (0, i, 0, 0)),
        out_shape=jax.ShapeDtypeStruct((n_heads, n, r, c), F32),
        compiler_params=_params(1),
    )(table, bucket)


def _band_buckets_t(tile, window):
    n_back = -(-(window - 1) // tile)
    t = jnp.arange(n_back + 1)[:, None, None]
    key = jnp.arange(tile)[None, :, None]
    query = jnp.arange(tile)[None, None, :]
    dist = query + (n_back - t) * tile - key
    return jnp.where((dist >= 0) & (dist < window), _t5_bucket(dist), -1).astype(jnp.int32)


def _toeplitz_buckets_t(tile, n_tiles):
    m = jnp.arange(n_tiles)[:, None, None]
    key = jnp.arange(tile)[None, :, None]
    query = jnp.arange(tile)[None, None, :]
    dist = m * tile + query - key
    near = jnp.where(dist >= 0, _t5_bucket(dist), -1).astype(jnp.int32)
    return jnp.concatenate([near, jnp.full((1, tile, tile), ZERO_BUCKET, jnp.int32)])


def _cmp_buckets_t(s_len, n_rows):
    n_c = n_rows - 1
    tile = jnp.arange(s_len // SEL_TQ)[:, None, None]
    n = jnp.arange(n_rows)[None, :, None]
    t = tile * SEL_TQ + jnp.arange(SEL_TQ)[None, None, :]
    dist = t - (n * CMP_STRIDE + CMP_LEN - 1)
    return jnp.where((dist >= 0) & (n < n_c), _t5_bucket(dist), -1).astype(jnp.int32)


def _near_tiles(tile):
    max_exact = REL_BUCKETS // 2
    first_const = math.ceil(max_exact * (REL_MAX_DISTANCE / max_exact) ** ((max_exact - 1) / max_exact)) + 1
    m = 1
    while m * tile - (tile - 1) < first_const:
        m += 1
    return m


def _rms(x, gain):
    return x * lax.rsqrt(jnp.mean(x * x, axis=-1, keepdims=True) + RMS_EPS) * gain


def _in_proj_kernel(x_ref, mod_ref, gain_ref, w_ref, wm_ref, swa_ref, fox_ref, nsaq_ref, kc_ref, vc_ref, k2_ref,
                    misc_ref, vt_ref, pack_scr):
    x = x_ref[0]
    h = _rms(x, gain_ref[...]) * (1.0 + mod_ref[0, 1:2, :]) + mod_ref[0, 0:1, :]
    hb = h.astype(BF16)

    def seg(bounds):
        if bounds == SEG_MISC:
            return jnp.dot(hb, wm_ref[...], preferred_element_type=F32)
        return jnp.dot(hb, w_ref[:, bounds[0]:bounds[1]], preferred_element_type=F32)

    swa_ref[0] = seg(SEG_SWA).astype(BF16)
    fox_ref[0] = seg(SEG_FOX).astype(BF16)
    nsaq_ref[0] = seg(SEG_NSAQ).astype(BF16)
    for slot, (bounds, out_ref) in enumerate(((SEG_KC, kc_ref), (SEG_VC, vc_ref))):
        pack_scr[slot] = seg(bounds)
        for tok in range(CMP_STRIDE):
            out_ref[0, :, tok * LANES:(tok + 1) * LANES] = pack_scr[
                slot, pl.ds(tok, x.shape[0] // CMP_STRIDE, stride=CMP_STRIDE), :].astype(BF16)
    k2_ref[0] = seg(SEG_K2).astype(BF16)
    misc_ref[0] = seg(SEG_MISC)

    rows = x.shape[0]
    extra_row = lax.broadcasted_iota(jnp.int32, (VT_ROWS - HEAD_DIM, rows), 0)
    extra = jnp.where(extra_row == 0, 1.0, 0.0).astype(BF16)
    values = seg(SEG_V)
    for c in range(values.shape[1] // LANES):
        vt = values[:, c * LANES:(c + 1) * LANES].T.astype(BF16)
        for half in range(2):
            base = (2 * c + half) * VT_ROWS
            vt_ref[0, base:base + HEAD_DIM, :] = vt[half * HEAD_DIM:(half + 1) * HEAD_DIM, :]
            vt_ref[0, base + HEAD_DIM:base + VT_ROWS, :] = extra


def _in_proj(x, mod, gain, w, w_misc, layer):
    b, s, d = x.shape
    n = w.shape[2]

    def rows(width, dtype):
        return (pl.BlockSpec((1, ROW_TILE, width), lambda i, j: (i, j, 0)), jax.ShapeDtypeStruct((b, s, width), dtype))

    packed = (pl.BlockSpec((1, ROW_TILE // CMP_STRIDE, CMP_STRIDE * LANES), lambda i, j: (i, j, 0)),
              jax.ShapeDtypeStruct((b, s // CMP_STRIDE, CMP_STRIDE * LANES), BF16))
    vt_rows = (SEG_V[1] - SEG_V[0]) // HEAD_DIM * VT_ROWS
    outs = [rows(SEG_SWA[1] - SEG_SWA[0], BF16), rows(SEG_FOX[1] - SEG_FOX[0], BF16),
            rows(SEG_NSAQ[1] - SEG_NSAQ[0], BF16), packed, packed, rows(SEG_K2[1] - SEG_K2[0], BF16),
            rows(SEG_MISC[1] - SEG_MISC[0], F32),
            (pl.BlockSpec((1, vt_rows, ROW_TILE), lambda i, j: (i, 0, j)),
             jax.ShapeDtypeStruct((b, vt_rows, s), BF16))]
    return pl.pallas_call(
        _in_proj_kernel,
        grid=(b, s // ROW_TILE),
        in_specs=[pl.BlockSpec((1, ROW_TILE, d), lambda i, j: (i, j, 0)),
                  pl.BlockSpec((None, 1, ADA_CHUNKS, d), lambda i, j: (layer, i, 0, 0)),
                  pl.BlockSpec((None, 1, d), lambda i, j: (layer, 0, 0)),
                  pl.BlockSpec((None, d, n), lambda i, j: (layer, 0, 0)),
                  pl.BlockSpec((None, d, w_misc.shape[2]), lambda i, j: (layer, 0, 0))],
        out_specs=[spec for spec, _ in outs],
        out_shape=[shape for _, shape in outs],
        scratch_shapes=[pltpu.VMEM((2, ROW_TILE, LANES), F32)],
        compiler_params=_params(2),
    )(x, mod, gain, w, w_misc)


def _banded_kernel(*refs, n_back, n_groups, has_sink, t, layer):
    if has_sink:
        sink_ref, q_ref, k_ref, vt_ref, bias_ref, o_ref = refs
    else:
        q_ref, k_ref, vt_ref, bias_ref, o_ref = refs
    i = pl.program_id(1)
    lo = _lane_lo((t, LANES))
    n_tiles = n_back + 1

    def run(all_valid):
        starts = [pl.multiple_of(jnp.maximum(i - n_back + tt, 0) * t, t) for tt in range(n_tiles)]
        k_tiles = [k_ref[0, pl.ds(start, t), :] for start in starts]

        def scores(g):
            qg = q_ref[0, :, g * LANES:(g + 1) * LANES]
            zero = jnp.zeros_like(qg)
            qms = (jnp.where(lo, qg, zero), jnp.where(lo, zero, qg))
            return [[bias_ref[2 * g + half, tt] + _dot_nt(k_tiles[tt], qms[half]) for tt in range(n_tiles)]
                    for half in range(2)]

        def softmax_pv(g, sts):
            pair = []
            for half in range(2):
                tiles = sts[half]
                if not all_valid:
                    tiles = [jnp.where(i - n_back + tt >= 0, st, NEG) if tt < n_back else st
                             for tt, st in enumerate(tiles)]
                m = None
                for st in tiles:
                    part = st.reshape(t // SUBLANES, SUBLANES, t).max(axis=0)
                    m = part if m is None else jnp.maximum(m, part)
                m = _all_sublanes(m, jnp.maximum)
                if has_sink:
                    sink = sink_ref[layer, 2 * g + half] * LOG2E
                    m = jnp.maximum(m, sink)
                acc = None
                for tt, st in enumerate(tiles):
                    p = jnp.exp2((st.reshape(t // SUBLANES, SUBLANES, t) - m[None]).reshape(t, t).astype(BF16))
                    part = jnp.dot(vt_ref[0, half * VT_ROWS:(half + 1) * VT_ROWS, pl.ds(starts[tt], t)], p,
                                   preferred_element_type=F32)
                    acc = part if acc is None else acc + part
                denom = _all_sublanes(acc[HEAD_DIM:HEAD_DIM + SUBLANES, :], jnp.add)
                if has_sink:
                    denom = denom + jnp.exp2(sink - m)
                out = acc[0:HEAD_DIM, :].reshape(HEAD_DIM // SUBLANES, SUBLANES, t) / denom[None]
                pair.append(out.reshape(HEAD_DIM, t))
            o_ref[0, :, g * LANES:(g + 1) * LANES] = jnp.concatenate(pair, axis=0).T.astype(o_ref.dtype)

        pending = scores(0)
        for g in range(n_groups):
            current = pending
            if g + 1 < n_groups:
                pending = scores(g + 1)
            softmax_pv(g, current)

    @pl.when(i >= n_back)
    def _():
        run(True)

    @pl.when(i < n_back)
    def _():
        run(False)


def _banded_attention(q_arr, k_arr, k_blk, vt, vt_blk, bias, sinks=None, layer=0):
    b, s, _ = q_arr.shape
    n_pos, n_tiles, t = bias.shape[0], bias.shape[1], bias.shape[2]
    width = n_pos * HEAD_DIM
    in_specs = [pl.BlockSpec((1, t, width), lambda i, j: (i, j, 0)),
                pl.BlockSpec((1, s, LANES), lambda i, j: (i, 0, k_blk)),
                pl.BlockSpec((1, 2 * VT_ROWS, s), lambda i, j: (i, vt_blk, 0)),
                pl.BlockSpec(bias.shape, lambda i, j: (0, 0, 0, 0))]
    args = [q_arr, k_arr, vt, bias]
    if sinks is not None:
        in_specs = [pl.BlockSpec(memory_space=pltpu.SMEM)] + in_specs
        args = [sinks] + args
    return pl.pallas_call(
        functools.partial(_banded_kernel, n_back=n_tiles - 1, n_groups=n_pos // 2, has_sink=sinks is not None, t=t,
                          layer=layer),
        grid=(b, s // t),
        in_specs=in_specs,
        out_specs=pl.BlockSpec((1, t, width), lambda i, j: (i, j, 0)),
        out_shape=jax.ShapeDtypeStruct((b, s, width), BF16),
        compiler_params=_params(2),
    )(*args)


def _all_sublanes(x, op):
    for shift in (4, 2, 1):
        x = op(x, pltpu.roll(x, shift, 0))
    return x


def _flash_init(m_scr, acc_scr):
    m_scr[...] = jnp.full(m_scr.shape, NEG, F32)
    acc_scr[...] = jnp.zeros(acc_scr.shape, F32)


def _flash_update(h, st_ref, tile_max, vt_h, m_scr, acc_scr):
    tk, tq = st_ref.shape
    m_prev = m_scr[h]
    m_new = _all_sublanes(jnp.maximum(m_prev, tile_max), jnp.maximum)
    alpha = jnp.exp2(m_prev - m_new)
    p = jnp.exp2((st_ref[...].reshape(tk // SUBLANES, SUBLANES, tq) - m_new[None]).reshape(tk, tq).astype(BF16))
    acc = acc_scr[h].reshape(VT_ROWS // SUBLANES, SUBLANES, tq) * alpha[None]
    acc_scr[h] = acc.reshape(VT_ROWS, tq) + jnp.dot(vt_h, p, preferred_element_type=F32)
    m_scr[h] = m_new


def _flash_finish(o_ref, n_groups, acc_scr):
    tq = acc_scr.shape[2]
    for g in range(n_groups):
        pair = []
        for h in (2 * g, 2 * g + 1):
            denom = _all_sublanes(acc_scr[h, HEAD_DIM:HEAD_DIM + SUBLANES, :], jnp.add)
            out = acc_scr[h, 0:HEAD_DIM, :].reshape(HEAD_DIM // SUBLANES, SUBLANES, tq) / denom[None]
            pair.append(out.reshape(HEAD_DIM, tq))
        o_ref[0, :, g * LANES:(g + 1) * LANES] = jnp.concatenate(pair, axis=0).T.astype(o_ref.dtype)


def _flash_pipeline(i, n_heads, qk_scores, bias_tile, vt_slab, s_scr, tmax_scr, m_scr, acc_scr):
    def qk_head(thunk, h, j, slot):
        st = thunk() + bias_tile(h, j)
        s_scr[slot, h] = st
        tmax_scr[slot, h] = st.reshape(st.shape[0] // SUBLANES, SUBLANES, st.shape[1]).max(axis=0)

    def softmax_head(h, j, slot):
        _flash_update(h, s_scr.at[slot, h], tmax_scr[slot, h], vt_slab(h, j), m_scr, acc_scr)

    def softmax_all(j, slot):
        for h in range(n_heads):
            softmax_head(h, j, slot)

    def stage(j_qk, slot_qk, j_sm, slot_sm):
        thunks = qk_scores(j_qk)
        for h in range(n_heads):
            qk_head(thunks[h], h, j_qk, slot_qk)
            softmax_head(h, j_sm, slot_sm)

    for h, thunk in enumerate(qk_scores(0)):
        qk_head(thunk, h, 0, 0)

    def body(trip, carry):
        j = 2 * trip
        stage(j + 1, 1, j, 0)
        stage(j + 2, 0, j + 1, 1)
        return carry

    lax.fori_loop(0, i // 2, body, 0)
    last = 2 * (i // 2)

    @pl.when(i % 2 == 0)
    def _():
        softmax_all(last, 0)

    @pl.when(i % 2 == 1)
    def _():
        stage(last + 1, 1, last, 0)
        softmax_all(last + 1, 1)


def _fox_aug_kernel(misc_ref, fbias_ref, tri_ref, o_ref):
    s_len, width = misc_ref.shape[1], misc_ref.shape[2]
    term = lax.broadcasted_iota(jnp.int32, (LANES, width), 1) % HEAD_DIM
    carry = jnp.zeros((1, width), F32)
    for c in range(s_len // LANES):
        z = misc_ref[0, c * LANES:(c + 1) * LANES, :] + fbias_ref[...]
        log_f = jnp.minimum(z, 0.0) - jnp.log1p(jnp.exp(-jnp.abs(z)))
        cum = jnp.dot(tri_ref[...], log_f, precision=HIGHEST, preferred_element_type=F32) + carry
        carry = cum[LANES - 1:LANES, :]
        x = cum * (-LOG2E)
        hi = x.astype(BF16).astype(F32)
        rest = x - hi
        mid = rest.astype(BF16).astype(F32)
        low = rest - mid
        out = jnp.where(term == 0, hi, jnp.where(term == 1, mid, jnp.where(term == 2, low, 0.0)))
        o_ref[0, c * LANES:(c + 1) * LANES, :] = out.astype(BF16)


def _fox_key_terms(misc, fbias, layer):
    b, s, width = misc.shape
    tri = jnp.asarray(np.tril(np.ones((LANES, LANES), np.float32)))
    return pl.pallas_call(
        _fox_aug_kernel,
        grid=(b,),
        in_specs=[pl.BlockSpec((1, s, width), lambda i: (i, 0, 0)),
                  pl.BlockSpec((None, 1, width), lambda i: (layer, 0, 0)),
                  pl.BlockSpec((LANES, LANES), lambda i: (0, 0))],
        out_specs=pl.BlockSpec((1, s, width), lambda i: (i, 0, 0)),
        out_shape=jax.ShapeDtypeStruct((b, s, width), BF16),
        compiler_params=_params(1),
    )(misc, fbias, tri)


def _fox_kernel(q_ref, k_ref, aug_ref, vt_ref, mask_ref, o_ref, qs_scr, s_scr, tmax_scr, m_scr, acc_scr):
    t = FLASH_T
    i = pl.program_id(1)
    lo = _lane_lo((t, LANES))
    lane = lax.broadcasted_iota(jnp.int32, (t, LANES), 1)
    ones = jnp.where(lane % HEAD_DIM < KEY_BIAS_TERMS, 1.0, 0.0).astype(BF16)
    n_groups = FOX_HEADS // 2
    for g in range(n_groups):
        qg = q_ref[0, :, g * LANES:(g + 1) * LANES]
        qs_scr[2 * g] = jnp.where(lo, qg, ones)
        qs_scr[2 * g + 1] = jnp.where(lo, ones, qg)
    _flash_init(m_scr, acc_scr)

    def qk_scores(j):
        start = pl.multiple_of(j * t, t)
        scores = []
        for g in range(n_groups):
            k_tile = k_ref[0, pl.ds(start, t), g * LANES:(g + 1) * LANES]
            a_tile = aug_ref[0, pl.ds(start, t), g * LANES:(g + 1) * LANES]
            k_sel = (jnp.where(lo, k_tile, a_tile), jnp.where(lo, a_tile, k_tile))
            for half in range(2):
                scores.append(functools.partial(lambda k, h: _dot_nt(k, qs_scr[h]), k_sel[half], 2 * g + half))
        return scores

    def bias_tile(h, j):
        return mask_ref[jnp.minimum(i - j, 1)]

    def vt_slab(h, j):
        return vt_ref[0, h * VT_ROWS:(h + 1) * VT_ROWS, pl.ds(pl.multiple_of(j * t, t), t)]

    _flash_pipeline(i, FOX_HEADS, qk_scores, bias_tile, vt_slab, s_scr, tmax_scr, m_scr, acc_scr)
    _flash_finish(o_ref, n_groups, acc_scr)


def _fox_attention(fox_qk, key_terms, vt):
    b, s, _ = fox_qk.shape
    width = FOX_HEADS * HEAD_DIM
    t = FLASH_T
    idx = np.arange(t)
    diag = np.where(idx[:, None] <= idx[None, :], 0.0, NEG)
    masks = jnp.asarray(np.stack([diag, np.zeros((t, t))]).astype(np.float32))
    return pl.pallas_call(
        _fox_kernel,
        grid=(b, s // t),
        in_specs=[pl.BlockSpec((1, t, width), lambda i, j: (i, j, 0)),
                  pl.BlockSpec((1, s, width), lambda i, j: (i, 0, 1)),
                  pl.BlockSpec((1, s, width), lambda i, j: (i, 0, 0)),
                  pl.BlockSpec((1, FOX_HEADS * VT_ROWS, s), lambda i, j: (i, VT_FOX_BLOCK, 0)),
                  pl.BlockSpec(masks.shape, lambda i, j: (0, 0, 0))],
        out_specs=pl.BlockSpec((1, t, width), lambda i, j: (i, j, 0)),
        out_shape=jax.ShapeDtypeStruct((b, s, width), BF16),
        scratch_shapes=[pltpu.VMEM((FOX_HEADS, t, LANES), BF16),
                        pltpu.VMEM((2, FOX_HEADS, t, t), F32),
                        pltpu.VMEM((2, FOX_HEADS, SUBLANES, t), F32),
                        pltpu.VMEM((FOX_HEADS, SUBLANES, t), F32),
                        pltpu.VMEM((FOX_HEADS, VT_ROWS, t), F32)],
        compiler_params=_params(2),
    )(fox_qk, fox_qk, key_terms, vt, masks)


def _split3(x):
    hi = x.astype(BF16)
    rest = x - hi.astype(F32)
    mid = rest.astype(BF16)
    low = (rest - mid.astype(F32)).astype(BF16)
    return hi, mid, low


def _compress_kernel(x_ref, pe_ref, w1a_ref, w1b_ref, w2_ref, o_ref):
    x = x_ref[0]
    n_rows = x.shape[0]

    def mm3(lhs, w_ref):
        return sum(jnp.dot(lhs, w_ref[piece], preferred_element_type=F32) for piece in range(3))

    first = mm3(x, w1a_ref)
    second = mm3(x, w1b_ref)
    pe_term = sum(mm3(piece, w1a_ref) for piece in _split3(pe_ref[0])) \
        + sum(mm3(piece, w1b_ref) for piece in _split3(pe_ref[1]))
    pre = first + pltpu.roll(second, n_rows - 1, 0) + pe_term[0:1, :]
    hid = 0.5 * pre * (1.0 + jnp.tanh(math.sqrt(2.0 / math.pi) * (pre + 0.044715 * (pre * pre * pre))))
    o_ref[0] = jnp.dot(hid, w2_ref[...], precision=HIGHEST, preferred_element_type=F32)


def _compress_weights(cmp_pos, cmp_w1, cmp_w2):
    depth = cmp_w1.shape[0]
    half = CMP_LEN // 2
    feat = CMP_STRIDE * LANES
    eye = jnp.eye(2, dtype=F32)
    w1 = cmp_w1.astype(F32).reshape(depth, 2, CMP_LEN, HEAD_DIM, CMP_HIDDEN)

    def pieces(w):
        big = jnp.einsum('nwldj,hg->nwlhdgj', w, eye).reshape(depth, 2, feat, 2 * CMP_HIDDEN)
        return jnp.stack(_split3(big), axis=2)

    w2 = jnp.einsum('nwjd,hg->nwhjgd', cmp_w2.astype(F32), eye).reshape(depth, 2, 2 * CMP_HIDDEN, LANES)
    w2 = jnp.concatenate([w2, jnp.roll(w2, HEAD_DIM, axis=3)], axis=3)
    pe = jnp.broadcast_to(cmp_pos.astype(F32).reshape(depth, 2, 2, half, 1, HEAD_DIM),
                          (depth, 2, 2, half, 2, HEAD_DIM))
    pe = jnp.broadcast_to(pe.reshape(depth, 2, 2, 1, feat), (depth, 2, 2, 8, feat))
    return pe, pieces(w1[:, :, :half]), pieces(w1[:, :, half:]), w2


def _compress(x, weights, layer, branch):
    b, n_rows, feat = x.shape
    pe, w1a, w1b, w2 = weights

    def picked(shape):
        return pl.BlockSpec((None, None) + shape, lambda i: (layer, branch) + (0,) * len(shape))

    return pl.pallas_call(
        _compress_kernel,
        grid=(b,),
        in_specs=[pl.BlockSpec((1, n_rows, feat), lambda i: (i, 0, 0)),
                  picked((2, 8, feat)),
                  picked((3, feat, 2 * CMP_HIDDEN)),
                  picked((3, feat, 2 * CMP_HIDDEN)),
                  picked((2 * CMP_HIDDEN, 2 * LANES))],
        out_specs=pl.BlockSpec((1, n_rows, 2 * LANES), lambda i: (i, 0, 0)),
        out_shape=jax.ShapeDtypeStruct((b, n_rows, 2 * LANES), F32),
        compiler_params=_params(1),
    )(x, pe, w1a, w1b, w2)


def _select_kernel(q_ref, kc_ref, vct_ref, bias_ref, o_ref, mb_ref, count_scr, psum_scr):
    tq = SEL_TQ
    i = pl.program_id(0)
    lo = _lane_lo((tq, LANES))
    n_rows = kc_ref.shape[1]
    lo_k = _lane_lo((n_rows, LANES))
    n_grp = n_rows // SUBLANES

    k_own = kc_ref[0, :, 0:LANES]
    k_swapped = kc_ref[0, :, LANES:2 * LANES]
    hi = k_own.astype(BF16)
    low = (k_swapped - k_swapped.astype(BF16).astype(F32)).astype(BF16)
    k_sel = (jnp.where(lo_k, hi, low), jnp.where(lo_k, low, hi))

    def scores(g):
        qg = q_ref[0, :, g * LANES:(g + 1) * LANES]
        swapped = pltpu.roll(qg.astype(F32), HEAD_DIM, 1).astype(BF16)
        q_dup = (jnp.where(lo, qg, swapped), jnp.where(lo, swapped, qg))
        return [bias_ref[2 * g + half, 0] + _dot_nt(k_sel[half], q_dup[half]) for half in range(2)]

    query = i * tq + lax.broadcasted_iota(jnp.int32, (SUBLANES, tq), 1)
    has_keys = query >= CMP_LEN - 1
    p_sum = [None, None]

    def softmax_pv(g, sts):
        pair = []
        for half in range(2):
            s3 = sts[half].reshape(n_grp, SUBLANES, tq)
            m = _all_sublanes(s3.max(axis=0), jnp.maximum)
            e = jnp.exp2(s3 - m[None])
            inv = jnp.where(has_keys, 1.0 / _all_sublanes(e.sum(axis=0), jnp.add), 0.0)
            p = e * inv[None]
            p_sum[half] = p if p_sum[half] is None else p_sum[half] + p
            pair.append(jnp.dot(vct_ref[0, half * HEAD_DIM:(half + 1) * HEAD_DIM, :],
                                p.reshape(n_rows, tq).astype(BF16), preferred_element_type=F32))
        o_ref[0, :, g * LANES:(g + 1) * LANES] = jnp.concatenate(pair, axis=0).T.astype(o_ref.dtype)

    n_groups = NSA_HEADS // 2
    pending = scores(0)
    for g in range(n_groups):
        current = pending
        if g + 1 < n_groups:
            pending = scores(g + 1)
        softmax_pv(g, current)

    n_blk = HEAD_DIM
    blk_grp = n_blk // SUBLANES
    sub = lax.broadcasted_iota(jnp.int32, (SUBLANES, tq), 0)
    q_blk = query // SLC_BLOCK
    kind = []
    for r in range(blk_grp):
        blk = sub + r * SUBLANES
        behind = q_blk - blk
        forced = jnp.where(blk == 0, 1, 0) + jnp.where(behind == 0, 1, 0) + jnp.where(behind == 1, 1, 0)
        kind.append(jnp.where(behind < 0, 2, jnp.minimum(forced, 1)))
    masks = []
    per_blk = SLC_BLOCK // CMP_STRIDE
    n_real = n_rows // per_blk
    n_lane_chunks = tq // LANES
    psum_scr[:, 0:SUBLANES, :] = jnp.zeros((n_lane_chunks, SUBLANES, LANES), F32)
    for half in (1, 0):
        p_rows = p_sum[half].reshape(n_rows, tq)
        for c in range(n_lane_chunks):
            psum_scr[c, SUBLANES:SUBLANES + n_rows, :] = p_rows[:, c * LANES:(c + 1) * LANES]

        def every_fourth(offset):
            return jnp.concatenate([psum_scr[c, pl.ds(SUBLANES + offset, n_real, stride=per_blk), :]
                                    for c in range(n_lane_chunks)], axis=1)

        imp = (0.5 * (every_fourth(-1) + every_fourth(3))
               + (every_fourth(0) + every_fourth(1) + every_fourth(2)))
        if n_real < n_blk:
            imp = jnp.concatenate([imp, jnp.zeros((n_blk - n_real, tq), F32)], axis=0)
        rows = [jnp.where(kind[r] == 2, NEG, jnp.where(kind[r] == 1, FORCE, imp[r * SUBLANES:(r + 1) * SUBLANES, :]))
                for r in range(blk_grp)]
        count_scr[...] = jnp.zeros(count_scr.shape, jnp.int32)
        for r_other in range(blk_grp):
            @pl.when(r_other * SUBLANES * SLC_BLOCK < (i + 1) * tq)
            def _(r_other=r_other, rows=rows):
                counts = [None] * blk_grp
                for s_other in range(SUBLANES):
                    row = jnp.broadcast_to(rows[r_other][s_other:s_other + 1, :], (SUBLANES, tq))
                    for r in range(blk_grp):
                        if r > r_other:
                            beats = jnp.where(row >= rows[r], 1, 0)
                        elif r < r_other:
                            beats = jnp.where(row > rows[r], 1, 0)
                        else:
                            beats = jnp.where(sub > s_other, jnp.where(row >= rows[r], 1, 0),
                                              jnp.where(row > rows[r], 1, 0))
                        counts[r] = beats if counts[r] is None else counts[r] + beats
                for r in range(blk_grp):
                    count_scr[r] = count_scr[r] + counts[r]
        masks.extend(jnp.where(count_scr[r] < TOPK, 0.0, NEG) for r in range(blk_grp))
    for c in range(tq // LANES):
        mb_ref[0, c * LANES:(c + 1) * LANES, :] = jnp.concatenate(
            [mk[:, c * LANES:(c + 1) * LANES] for mk in masks], axis=0).T.astype(BF16)


def _select(nsa_q, k_cmp, v_cmp, bias_c):
    b, s, width = nsa_q.shape
    n_rows = k_cmp.shape[1]
    n_blk = HEAD_DIM
    assert CMP_LEN == 2 * CMP_STRIDE and SLC_BLOCK == 4 * CMP_STRIDE
    tq = SEL_TQ
    vct = v_cmp[:, :, 0:LANES].transpose(0, 2, 1).astype(BF16)
    return pl.pallas_call(
        _select_kernel,
        grid=(s // tq, b),
        in_specs=[pl.BlockSpec((1, tq, width), lambda j, i: (i, j, 0)),
                  pl.BlockSpec((1, n_rows, 2 * LANES), lambda j, i: (i, 0, 0)),
                  pl.BlockSpec((1, LANES, n_rows), lambda j, i: (i, 0, 0)),
                  pl.BlockSpec((NSA_HEADS, 1, n_rows, tq), lambda j, i: (0, j, 0, 0))],
        out_specs=[pl.BlockSpec((1, tq, width), lambda j, i: (i, j, 0)),
                   pl.BlockSpec((1, tq, LANES), lambda j, i: (i, j, 0))],
        out_shape=[jax.ShapeDtypeStruct((b, s, width), BF16),
                   jax.ShapeDtypeStruct((b, s, LANES), BF16)],
        scratch_shapes=[pltpu.VMEM((n_blk // SUBLANES, SUBLANES, tq), jnp.int32),
                        pltpu.VMEM((tq // LANES, SUBLANES + n_rows, LANES), F32)],
        compiler_params=_params(2),
    )(nsa_q, k_cmp, vct, bias_c)


def _slc_kernel(q_ref, mb_ref, k_ref, e2_ref, vt_ref, bias_ref, o_ref, qs_scr, s_scr, tmax_scr, m_scr, acc_scr, *,
                n_near):
    t = FLASH_T
    i = pl.program_id(1)
    lo = _lane_lo((t, LANES))
    n_groups = NSA_HEADS // 2
    mb = mb_ref[0]
    for g in range(n_groups):
        qg = q_ref[0, :, g * LANES:(g + 1) * LANES]
        qs_scr[2 * g] = jnp.where(lo, qg, mb)
        qs_scr[2 * g + 1] = jnp.where(lo, mb, qg)
    _flash_init(m_scr, acc_scr)

    def qk_scores(j):
        start = pl.multiple_of(j * t, t)
        k_tile = k_ref[0, pl.ds(start, t), :]
        e_tile = e2_ref[pl.ds(start, t), :]
        k_sel = (jnp.where(lo, k_tile, e_tile), jnp.where(lo, e_tile, k_tile))
        return [functools.partial(lambda k, pos: _dot_nt(k, qs_scr[pos]), k_sel[pos % 2], pos)
                for pos in range(NSA_HEADS)]

    def bias_tile(pos, j):
        return bias_ref[pos, jnp.minimum(i - j, n_near)]

    def vt_slab(pos, j):
        kv = pos % 2
        return vt_ref[0, kv * VT_ROWS:(kv + 1) * VT_ROWS, pl.ds(pl.multiple_of(j * t, t), t)]

    _flash_pipeline(i, NSA_HEADS, qk_scores, bias_tile, vt_slab, s_scr, tmax_scr, m_scr, acc_scr)
    _flash_finish(o_ref, n_groups, acc_scr)


def _slc_attention(nsa_q, mask_bias, k2, e2, vt, bias):
    b, s, width = nsa_q.shape
    t = FLASH_T
    n_near = bias.shape[1] - 1
    return pl.pallas_call(
        functools.partial(_slc_kernel, n_near=n_near),
        grid=(b, s // t),
        in_specs=[pl.BlockSpec((1, t, width), lambda i, j: (i, j, 0)),
                  pl.BlockSpec((1, t, LANES), lambda i, j: (i, j, 0)),
                  pl.BlockSpec((1, s, LANES), lambda i, j: (i, 0, 0)),
                  pl.BlockSpec((s, LANES), lambda i, j: (0, 0)),
                  pl.BlockSpec((1, 2 * VT_ROWS, s), lambda i, j: (i, VT_SLC_BLOCK, 0)),
                  pl.BlockSpec(bias.shape, lambda i, j: (0, 0, 0, 0))],
        out_specs=pl.BlockSpec((1, t, width), lambda i, j: (i, j, 0)),
        out_shape=jax.ShapeDtypeStruct((b, s, width), BF16),
        scratch_shapes=[pltpu.VMEM((NSA_HEADS, t, LANES), BF16),
                        pltpu.VMEM((2, NSA_HEADS, t, t), F32),
                        pltpu.VMEM((2, NSA_HEADS, SUBLANES, t), F32),
                        pltpu.VMEM((NSA_HEADS, SUBLANES, t), F32),
                        pltpu.VMEM((NSA_HEADS, VT_ROWS, t), F32)],
        compiler_params=_params(2),
    )(nsa_q, mask_bias, k2, e2, vt, bias)


def _block_onehot(s_len):
    blk = np.arange(s_len)[:, None] // SLC_BLOCK
    lane = np.arange(LANES)[None, :] % HEAD_DIM
    return jnp.asarray((blk == lane).astype(np.float32), dtype=BF16)


def _mix_ffn_kernel(x_ref, mod_ref, swa_ref, fox_ref, cmp_ref, slc_ref, win_ref, misc_ref, expand_ref, gn_ref,
                    w_ref, post_ref, pre_ref, wg_ref, wu_ref, wd_ref, fpost_ref, o_ref):
    n_swa = SWA_HEADS * HEAD_DIM
    n_fox = FOX_HEADS * HEAD_DIM
    n_nsa = NSA_HEADS * HEAD_DIM
    gate = jax.nn.sigmoid(misc_ref[0])
    gate_hi = gate.astype(BF16)
    gate_lo = (gate - gate_hi.astype(F32)).astype(BF16)
    gates = (jnp.dot(gate_hi, expand_ref[...], preferred_element_type=F32)
             + jnp.dot(gate_lo, expand_ref[...], preferred_element_type=F32))
    o_nsa = (gates[:, 0:n_nsa] * cmp_ref[0].astype(F32) + gates[:, n_nsa:2 * n_nsa] * slc_ref[0].astype(F32)
             + gates[:, 2 * n_nsa:3 * n_nsa] * win_ref[0].astype(F32))
    a = _rms(swa_ref[0].astype(F32), gn_ref[:, 0:n_swa]).astype(BF16)
    b = _rms(fox_ref[0].astype(F32), gn_ref[:, n_swa:n_swa + n_fox]).astype(BF16)
    c = _rms(o_nsa, gn_ref[:, n_swa + n_fox:]).astype(BF16)
    y = (jnp.dot(a, w_ref[0:n_swa, :], preferred_element_type=F32)
         + jnp.dot(b, w_ref[n_swa:n_swa + n_fox, :], preferred_element_type=F32)
         + jnp.dot(c, w_ref[n_swa + n_fox:, :], preferred_element_type=F32))
    x = x_ref[0] + mod_ref[0, 2:3, :] * _rms(y, post_ref[...])

    h = (_rms(x, pre_ref[...]) * (1.0 + mod_ref[0, 4:5, :]) + mod_ref[0, 3:4, :]).astype(BF16)
    y = jnp.zeros(x.shape, F32)
    for chunk in range(wg_ref.shape[1] // FFN_CHUNK):
        cols = slice(chunk * FFN_CHUNK, (chunk + 1) * FFN_CHUNK)
        gate = jnp.dot(h, wg_ref[:, cols], preferred_element_type=F32)
        up = jnp.dot(h, wu_ref[:, cols], preferred_element_type=F32)
        act = (gate * jax.nn.sigmoid(gate) * up).astype(BF16)
        y = y + jnp.dot(act, wd_ref[cols, :], preferred_element_type=F32)
    o_ref[0] = x + mod_ref[0, 5:6, :] * _rms(y, fpost_ref[...])


def _gate_expansion():
    expand = np.zeros((LANES, 3 * NSA_HEADS * HEAD_DIM), np.float32)
    for branch in range(3):
        for p in range(NSA_HEADS):
            col = (branch * NSA_HEADS + p) * HEAD_DIM
            expand[GATE_LANE + 8 * branch + p, col:col + HEAD_DIM] = 1.0
    return jnp.asarray(expand, dtype=BF16)


def _mix_ffn(x, mod, o_swa, o_fox, o_cmp, o_slc, o_win, misc, gn, w, post, pre, wg, wu, wd, fpost, layer):
    b, s, d = x.shape
    expand = _gate_expansion()
    hidden = wg.shape[2]

    def rows(width):
        return pl.BlockSpec((1, ROW_TILE, width), lambda i, j: (i, j, 0))

    def whole(shape):
        return pl.BlockSpec(shape, lambda i, j: (0,) * len(shape), pipeline_mode=pl.Buffered(1))

    def of_layer(shape):
        return pl.BlockSpec((None,) + shape, lambda i, j: (layer,) + (0,) * len(shape),
                            pipeline_mode=pl.Buffered(1))

    vec = pl.BlockSpec((None, 1, d), lambda i, j: (layer, 0, 0))
    return pl.pallas_call(
        _mix_ffn_kernel,
        grid=(b, s // ROW_TILE),
        in_specs=[rows(d),
                  pl.BlockSpec((None, 1, ADA_CHUNKS, d), lambda i, j: (layer, i, 0, 0)),
                  rows(o_swa.shape[2]), rows(o_fox.shape[2]), rows(o_cmp.shape[2]), rows(o_slc.shape[2]),
                  rows(o_win.shape[2]), rows(LANES),
                  whole(expand.shape), vec, of_layer((d, d)), vec, vec,
                  of_layer((d, hidden)), of_layer((d, hidden)), of_layer((hidden, d)),
                  vec],
        out_specs=rows(d),
        out_shape=jax.ShapeDtypeStruct((b, s, d), F32),
        compiler_params=_params(2),
    )(x, mod, o_swa, o_fox, o_cmp, o_slc, o_win, misc, expand, gn, w, post, pre, wg, wu, wd, fpost)


def _forget_lanes():
    lanes, heads = [], []
    for h in range(FOX_HEADS):
        base = (h // 2) * LANES + (HEAD_DIM if h % 2 == 0 else 0)
        for j in range(KEY_BIAS_TERMS):
            lanes.append(base + j)
            heads.append(h)
    return np.array(lanes), np.array(heads)


def _in_proj_layout():
    d = HEAD_DIM
    o_qa, o_ka, o_va, o_qb, o_kb, o_vb, o_fb, o_qc = 0, 256, 384, 512, 768, 1024, 1280, 1284
    o_kc, o_vc, o_ksl, o_vsl, o_kw, o_vw, o_gc = 1796, 1924, 2052, 2180, 2308, 2436, 2564
    scale = LOG2E / math.sqrt(d)

    def head_cols(base, heads):
        return np.concatenate([np.arange(base + h * d, base + (h + 1) * d) for h in heads])

    def span(base, width):
        return np.arange(base, base + width)

    cols = [head_cols(o_qa, SWA_POS), span(o_ka, 128),
            span(o_qb, 256), span(o_kb, 256),
            head_cols(o_qc, NSA_POS),
            span(o_kc, 128), span(o_vc, 128),
            span(o_ksl, 128), span(o_kw, 128),
            span(o_vb, 256), span(o_va, 128), span(o_vsl, 128), span(o_vw, 128)]
    scales = [np.full(256, scale), np.ones(128), np.full(256, scale), np.ones(256), np.full(512, scale),
              np.ones(256), np.ones(256), np.ones(640)]
    lanes, heads = _forget_lanes()
    misc_cols = np.zeros(SEG_MISC[1] - SEG_MISC[0], np.int64)
    misc_scale = np.zeros(SEG_MISC[1] - SEG_MISC[0])
    misc_cols[lanes] = o_fb + heads
    misc_scale[lanes] = 1.0
    for branch in range(3):
        for p, h in enumerate(NSA_POS):
            misc_cols[GATE_LANE + 8 * branch + p] = o_gc + h * 3 + branch
            misc_scale[GATE_LANE + 8 * branch + p] = 1.0
    cols.append(misc_cols)
    scales.append(misc_scale)
    return np.concatenate(cols), np.concatenate(scales).astype(np.float32)


def _head_perm(pos):
    return np.concatenate([np.arange(h * HEAD_DIM, (h + 1) * HEAD_DIM) for h in pos])


def kernel(x, c, rel_bias, ada_w, ada_b, attn_pre_norm, attn_post_norm, ffn_pre_norm, ffn_post_norm, w_in,
           forget_bias, swa_sinks, cmp_pos, cmp_w1, cmp_w2, group_norm, w_out, ffn_w_gate, ffn_w_up, ffn_w_down):
    b, s, d = x.shape
    depth = w_in.shape[0]
    hidden = ffn_w_gate.shape[2]
    assert s % (2 * FLASH_T) == 0 and s // SLC_BLOCK <= HEAD_DIM and hidden % FFN_CHUNK == 0

    cols, scales = _in_proj_layout()
    n_main = SEG_MISC[0]
    breaks = np.flatnonzero(np.diff(cols[:n_main]) != 1) + 1
    runs = np.split(cols[:n_main], breaks)
    w_main = (jnp.concatenate([w_in[:, :, int(r[0]):int(r[-1]) + 1] for r in runs], axis=2)
              * scales[:n_main]).astype(BF16)
    used = np.flatnonzero(scales[n_main:] != 0)
    w_misc = jnp.zeros((depth, d, SEG_MISC[1] - SEG_MISC[0]), w_in.dtype).at[:, :, used].set(
        w_in[:, :, cols[n_main:][used]]).astype(BF16)
    lanes, heads = _forget_lanes()
    fbias_all = jnp.zeros((depth, 1, SEG_MISC[1] - SEG_MISC[0]), F32).at[:, 0, lanes].set(
        forget_bias[:, heads].astype(F32))
    swa_perm = _head_perm(SWA_POS)
    nsa_perm = _head_perm(NSA_POS)
    n_swa, n_fox = SWA_HEADS * HEAD_DIM, FOX_HEADS * HEAD_DIM
    mix_perm = np.concatenate([swa_perm, n_swa + np.arange(n_fox), n_swa + n_fox + nsa_perm])
    w_out_all = w_out[:, mix_perm, :].astype(BF16)
    wg_all = ffn_w_gate.astype(BF16)
    wu_all = ffn_w_up.astype(BF16)
    wd_all = ffn_w_down.astype(BF16)
    cmp_weights = _compress_weights(cmp_pos, cmp_w1, cmp_w2)

    def stacked(v):
        return v.astype(F32).reshape(depth, 1, v.shape[1])

    gn_all = stacked(group_norm[:, mix_perm])
    attn_pre, attn_post = stacked(attn_pre_norm), stacked(attn_post_norm)
    ffn_pre, ffn_post = stacked(ffn_pre_norm), stacked(ffn_post_norm)
    sinks_all = swa_sinks[:, np.array(SWA_POS)].astype(F32)

    tab_swa = rel_bias[:, np.array(SWA_POS)].astype(F32)
    tab_nsa = rel_bias[:, SWA_HEADS + np.array(NSA_POS)].astype(F32)
    bias_swa = _bias_table(tab_swa, _band_buckets_t(SWA_TILE, SWA_WINDOW))
    bias_win = _bias_table(tab_nsa, _band_buckets_t(WIN_TILE, NSA_WINDOW))
    bias_slc = _bias_table(tab_nsa, _toeplitz_buckets_t(FLASH_T, _near_tiles(FLASH_T)), subtract_last=True)
    n_rows = s // CMP_STRIDE
    bias_cmp = _bias_table(tab_nsa, _cmp_buckets_t(s, n_rows))
    e2 = _block_onehot(s)

    mod_all = _adaln(c.astype(F32), ada_w.astype(F32), ada_b.astype(F32)).reshape(depth, b, ADA_CHUNKS, d)

    for layer in range(depth):
        swa_qk, fox_qk, nsa_q, kc, vc, k2, misc, vt = _in_proj(x, mod_all, attn_pre, w_main, w_misc, layer)
        o_swa = _banded_attention(swa_qk, swa_qk, 2, vt, VT_SWA_BLOCK, bias_swa, sinks=sinks_all, layer=layer)
        o_fox = _fox_attention(fox_qk, _fox_key_terms(misc, fbias_all, layer), vt)
        k_cmp = _compress(kc, cmp_weights, layer, 0)
        v_cmp = _compress(vc, cmp_weights, layer, 1)
        o_cmp, mask_bias = _select(nsa_q, k_cmp, v_cmp, bias_cmp)
        o_slc = _slc_attention(nsa_q, mask_bias, k2, e2, vt, bias_slc)
        o_win = _banded_attention(nsa_q, k2, 1, vt, VT_WIN_BLOCK, bias_win)
        x = _mix_ffn(x, mod_all, o_swa, o_fox, o_cmp, o_slc, o_win, misc, gn_all, w_out_all, attn_post, ffn_pre,
                     wg_all, wu_all, wd_all, ffn_post, layer)
    return x
```

```python
import functools
import math

import numpy as np
import jax
import jax.numpy as jnp
from jax import lax
from jax.experimental import pallas as pl
from jax.experimental.pallas import tpu as pltpu

F32 = jnp.float32
BF16 = jnp.bfloat16
HIGHEST = lax.Precision.HIGHEST

LANES = 128
SUBLANES = 8
VMEM_LIMIT = 56 * 1024 * 1024

HEAD_DIM = 64
SWA_HEADS = 4
SWA_WINDOW = 128
FOX_HEADS = 4
NSA_HEADS = 8
CMP_LEN = 32
CMP_STRIDE = 16
CMP_HIDDEN = 2 * HEAD_DIM
SLC_BLOCK = 64
TOPK = 16
NSA_WINDOW = 512
REL_BUCKETS = 32
REL_MAX_DISTANCE = 1024
ZERO_BUCKET = -2
RMS_EPS = 1e-6
NEG = -1e30
FORCE = 1e30
ADA_CHUNKS = 6
LOG2E = math.log2(math.e)

SWA_POS = (0, 2, 1, 3)
NSA_POS = (0, 4, 1, 5, 2, 6, 3, 7)

SWA_TILE = 256
WIN_TILE = 256
FLASH_T = 256
SEL_TQ = 256
IN_ROW_TILE = 1024
ROW_TILE = 512
FFN_CHUNK = 256
VT_ROWS = HEAD_DIM + 16
KEY_BIAS_TERMS = 3

SEG_SWA = (0, 384)
SEG_FOX = (384, 896)
SEG_NSAQ = (896, 1408)
SEG_KC = (1408, 1536)
SEG_VC = (1536, 1664)
SEG_K2 = (1664, 1920)
SEG_V = (1920, 2560)
SEG_MISC = (2560, 2816)
GATE_LANE = 8
VT_FOX_BLOCK, VT_SWA_BLOCK, VT_SLC_BLOCK, VT_WIN_BLOCK = 0, 2, 3, 4


def _params(n_grid, vmem=VMEM_LIMIT):
    return pltpu.CompilerParams(dimension_semantics=("parallel",) * n_grid, vmem_limit_bytes=vmem)


def _dot_nt(a, b):
    return lax.dot_general(a, b, (((1,), (1,)), ((), ())), preferred_element_type=F32)


def _lane_lo(shape):
    return lax.broadcasted_iota(jnp.int32, shape, len(shape) - 1) < HEAD_DIM


def _adaln_kernel(c_ref, w_ref, b_ref, o_ref):
    c = c_ref[...]
    act = c * jax.nn.sigmoid(c)
    o_ref[0] = jnp.dot(act, w_ref[0], precision=HIGHEST, preferred_element_type=F32) + b_ref[0]


def _adaln(c, ada_w, ada_b):
    depth, d, n = ada_w.shape
    b = c.shape[0]
    return pl.pallas_call(
        _adaln_kernel,
        grid=(depth, n // d),
        in_specs=[pl.BlockSpec((b, d), lambda l, j: (0, 0)),
                  pl.BlockSpec((1, d, d), lambda l, j: (l, 0, j)),
                  pl.BlockSpec((1, 1, d), lambda l, j: (l, 0, j))],
        out_specs=pl.BlockSpec((1, b, d), lambda l, j: (l, 0, j)),
        out_shape=jax.ShapeDtypeStruct((depth, b, n), F32),
        compiler_params=_params(2),
    )(c, ada_w, ada_b.reshape(depth, 1, n))


def _t5_bucket(dist):
    n = jnp.maximum(dist, 0)
    max_exact = REL_BUCKETS // 2
    nf = jnp.maximum(n, 1).astype(jnp.float32)
    large = max_exact + (jnp.log(nf / max_exact) / math.log(REL_MAX_DISTANCE / max_exact)
                         * (REL_BUCKETS - max_exact)).astype(jnp.int32)
    large = jnp.minimum(large, REL_BUCKETS - 1)
    return jnp.where(n < max_exact, n, large)


def _bias_table_kernel(tab_ref, bucket_ref, o_ref, *, subtract_last):
    n_heads = o_ref.shape[0]
    values = [[(tab_ref[k, h] - (tab_ref[REL_BUCKETS - 1, h] if subtract_last else 0.0)) * LOG2E
               for h in range(n_heads)] for k in range(REL_BUCKETS)]

    def rows(chunk, carry):
        r0 = pl.multiple_of(chunk * SUBLANES, SUBLANES)
        bucket = bucket_ref[0, pl.ds(r0, SUBLANES), :]
        accs = [jnp.where(bucket == ZERO_BUCKET, 0.0, NEG) for _ in range(n_heads)]
        for k in range(REL_BUCKETS):
            hit = bucket == k
            for h in range(n_heads):
                accs[h] = jnp.where(hit, values[k][h], accs[h])
        for h in range(n_heads):
            o_ref[h, 0, pl.ds(r0, SUBLANES), :] = accs[h]
        return carry

    lax.fori_loop(0, bucket_ref.shape[1] // SUBLANES, rows, 0)


def _bias_table(table, bucket, subtract_last=False):
    n_heads = table.shape[1]
    n, r, c = bucket.shape
    return pl.pallas_call(
        functools.partial(_bias_table_kernel, subtract_last=subtract_last),
        grid=(n,),
        in_specs=[pl.BlockSpec(memory_space=pltpu.SMEM),
                  pl.BlockSpec((1, r, c), lambda i: (i, 0, 0))],
        out_specs=pl.BlockSpec((n_heads, 1, r, c), lambda i: (0, i, 0, 0)),
        out_shape=jax.ShapeDtypeStruct((n_heads, n, r, c), F32),
        compiler_params=_params(1),
    )(table, bucket)


def _band_buckets_t(tile, window):
    n_back = -(-(window - 1) // tile)
    t = jnp.arange(n_back + 1)[:, None, None]
    key = jnp.arange(tile)[None, :, None]
    query = jnp.arange(tile)[None, None, :]
    dist = query + (n_back - t) * tile - key
    return jnp.where((dist >= 0) & (dist < window), _t5_bucket(dist), -1).astype(jnp.int32)


def _toeplitz_buckets_t(tile, n_tiles):
    m = jnp.arange(n_tiles)[:, None, None]
    key = jnp.arange(tile)[None, :, None]
    query = jnp.arange(tile)[None, None, :]
    dist = m * tile + query - key
    near = jnp.where(dist >= 0, _t5_bucket(dist), -1).astype(jnp.int32)
    return jnp.concatenate([near, jnp.full((1, tile, tile), ZERO_BUCKET, jnp.int32)])


def _cmp_buckets_t(s_len, n_rows):
    n_c = n_rows - 1
    tile = jnp.arange(s_len // SEL_TQ)[:, None, None]
    n = jnp.arange(n_rows)[None, :, None]
    t = tile * SEL_TQ + jnp.arange(SEL_TQ)[None, None, :]
    dist = t - (n * CMP_STRIDE + CMP_LEN - 1)
    return jnp.where((dist >= 0) & (n < n_c), _t5_bucket(dist), -1).astype(jnp.int32)


def _near_tiles(tile):
    max_exact = REL_BUCKETS // 2
    first_const = math.ceil(max_exact * (REL_MAX_DISTANCE / max_exact) ** ((max_exact - 1) / max_exact)) + 1
    m = 1
    while m * tile - (tile - 1) < first_const:
        m += 1
    return m


def _rms(x, gain):
    return x * lax.rsqrt(jnp.mean(x * x, axis=-1, keepdims=True) + RMS_EPS) * gain


def _in_proj_kernel(x_ref, mod_ref, gain_ref, w_ref, wm_ref, swa_ref, fox_ref, nsaq_ref, kc_ref, vc_ref, k2_ref,
                    misc_ref, vt_ref, pack_scr):
    x = x_ref[0]
    h = _rms(x, gain_ref[...]) * (1.0 + mod_ref[0, 1:2, :]) + mod_ref[0, 0:1, :]
    hb = h.astype(BF16)

    def seg(bounds):
        if bounds == SEG_MISC:
            return jnp.dot(hb, wm_ref[...], preferred_element_type=F32)
        return jnp.dot(hb, w_ref[:, bounds[0]:bounds[1]], preferred_element_type=F32)

    swa_ref[0] = seg(SEG_SWA).astype(BF16)
    fox_ref[0] = seg(SEG_FOX).astype(BF16)
    nsaq_ref[0] = seg(SEG_NSAQ).astype(BF16)
    for slot, (bounds, out_ref) in enumerate(((SEG_KC, kc_ref), (SEG_VC, vc_ref))):
        pack_scr[slot] = seg(bounds)
        for tok in range(CMP_STRIDE):
            out_ref[0, :, tok * LANES:(tok + 1) * LANES] = pack_scr[
                slot, pl.ds(tok, x.shape[0] // CMP_STRIDE, stride=CMP_STRIDE), :].astype(BF16)
    k2_ref[0] = seg(SEG_K2).astype(BF16)
    misc_ref[0] = seg(SEG_MISC)

    rows = x.shape[0]
    extra_row = lax.broadcasted_iota(jnp.int32, (VT_ROWS - HEAD_DIM, rows), 0)
    extra = jnp.where(extra_row == 0, 1.0, 0.0).astype(BF16)
    values = seg(SEG_V)
    for c in range(values.shape[1] // LANES):
        vt = values[:, c * LANES:(c + 1) * LANES].T.astype(BF16)
        for half in range(2):
            base = (2 * c + half) * VT_ROWS
            vt_ref[0, base:base + HEAD_DIM, :] = vt[half * HEAD_DIM:(half + 1) * HEAD_DIM, :]
            vt_ref[0, base + HEAD_DIM:base + VT_ROWS, :] = extra


def _in_proj(x, mod, gain, w, w_misc, layer):
    b, s, d = x.shape
    n = w.shape[2]

    tile = IN_ROW_TILE

    def rows(width, dtype):
        return (pl.BlockSpec((1, tile, width), lambda i, j: (i, j, 0)), jax.ShapeDtypeStruct((b, s, width), dtype))

    packed = (pl.BlockSpec((1, tile // CMP_STRIDE, CMP_STRIDE * LANES), lambda i, j: (i, j, 0)),
              jax.ShapeDtypeStruct((b, s // CMP_STRIDE, CMP_STRIDE * LANES), BF16))
    vt_rows = (SEG_V[1] - SEG_V[0]) // HEAD_DIM * VT_ROWS
    outs = [rows(SEG_SWA[1] - SEG_SWA[0], BF16), rows(SEG_FOX[1] - SEG_FOX[0], BF16),
            rows(SEG_NSAQ[1] - SEG_NSAQ[0], BF16), packed, packed, rows(SEG_K2[1] - SEG_K2[0], BF16),
            rows(SEG_MISC[1] - SEG_MISC[0], F32),
            (pl.BlockSpec((1, vt_rows, tile), lambda i, j: (i, 0, j)),
             jax.ShapeDtypeStruct((b, vt_rows, s), BF16))]
    return pl.pallas_call(
        _in_proj_kernel,
        grid=(b, s // tile),
        in_specs=[pl.BlockSpec((1, tile, d), lambda i, j: (i, j, 0)),
                  pl.BlockSpec((None, 1, ADA_CHUNKS, d), lambda i, j: (layer, i, 0, 0)),
                  pl.BlockSpec((None, 1, d), lambda i, j: (layer, 0, 0)),
                  pl.BlockSpec((None, d, n), lambda i, j: (layer, 0, 0)),
                  pl.BlockSpec((None, d, w_misc.shape[2]), lambda i, j: (layer, 0, 0))],
        out_specs=[spec for spec, _ in outs],
        out_shape=[shape for _, shape in outs],
        scratch_shapes=[pltpu.VMEM((2, tile, LANES), F32)],
        compiler_params=_params(2),
    )(x, mod, gain, w, w_misc)


def _banded_kernel(*refs, n_back, n_groups, has_sink, t, layer):
    if has_sink:
        sink_ref, q_ref, k_ref, vt_ref, bias_ref, o_ref = refs
    else:
        q_ref, k_ref, vt_ref, bias_ref, o_ref = refs
    i = pl.program_id(1)
    lo = _lane_lo((t, LANES))
    n_tiles = n_back + 1

    def run(all_valid):
        starts = [pl.multiple_of(jnp.maximum(i - n_back + tt, 0) * t, t) for tt in range(n_tiles)]
        k_tiles = [k_ref[0, pl.ds(start, t), :] for start in starts]

        def scores(g):
            qg = q_ref[0, :, g * LANES:(g + 1) * LANES]
            zero = jnp.zeros_like(qg)
            qms = (jnp.where(lo, qg, zero), jnp.where(lo, zero, qg))
            return [[bias_ref[2 * g + half, tt] + _dot_nt(k_tiles[tt], qms[half]) for tt in range(n_tiles)]
                    for half in range(2)]

        def softmax_pv(g, sts):
            pair = []
            for half in range(2):
                tiles = sts[half]
                if not all_valid:
                    tiles = [jnp.where(i - n_back + tt >= 0, st, NEG) if tt < n_back else st
                             for tt, st in enumerate(tiles)]
                m = None
                for st in tiles:
                    part = st.reshape(t // SUBLANES, SUBLANES, t).max(axis=0)
                    m = part if m is None else jnp.maximum(m, part)
                m = _all_sublanes(m, jnp.maximum)
                if has_sink:
                    sink = sink_ref[layer, 2 * g + half] * LOG2E
                    m = jnp.maximum(m, sink)
                acc = None
                for tt, st in enumerate(tiles):
                    p = jnp.exp2((st.reshape(t // SUBLANES, SUBLANES, t) - m[None]).reshape(t, t).astype(BF16))
                    part = jnp.dot(vt_ref[0, half * VT_ROWS:(half + 1) * VT_ROWS, pl.ds(starts[tt], t)], p,
                                   preferred_element_type=F32)
                    acc = part if acc is None else acc + part
                denom = _all_sublanes(acc[HEAD_DIM:HEAD_DIM + SUBLANES, :], jnp.add)
                if has_sink:
                    denom = denom + jnp.exp2(sink - m)
                out = acc[0:HEAD_DIM, :].reshape(HEAD_DIM // SUBLANES, SUBLANES, t) / denom[None]
                pair.append(out.reshape(HEAD_DIM, t))
            o_ref[0, :, g * LANES:(g + 1) * LANES] = jnp.concatenate(pair, axis=0).T.astype(o_ref.dtype)

        pending = scores(0)
        for g in range(n_groups):
            current = pending
            if g + 1 < n_groups:
                pending = scores(g + 1)
            softmax_pv(g, current)

    @pl.when(i >= n_back)
    def _():
        run(True)

    @pl.when(i < n_back)
    def _():
        run(False)


def _banded_attention(q_arr, k_arr, k_blk, vt, vt_blk, bias, sinks=None, layer=0):
    b, s, _ = q_arr.shape
    n_pos, n_tiles, t = bias.shape[0], bias.shape[1], bias.shape[2]
    width = n_pos * HEAD_DIM
    in_specs = [pl.BlockSpec((1, t, width), lambda i, j: (i, j, 0)),
                pl.BlockSpec((1, s, LANES), lambda i, j: (i, 0, k_blk)),
                pl.BlockSpec((1, 2 * VT_ROWS, s), lambda i, j: (i, vt_blk, 0)),
                pl.BlockSpec(bias.shape, lambda i, j: (0, 0, 0, 0))]
    args = [q_arr, k_arr, vt, bias]
    if sinks is not None:
        in_specs = [pl.BlockSpec(memory_space=pltpu.SMEM)] + in_specs
        args = [sinks] + args
    return pl.pallas_call(
        functools.partial(_banded_kernel, n_back=n_tiles - 1, n_groups=n_pos // 2, has_sink=sinks is not None, t=t,
                          layer=layer),
        grid=(b, s // t),
        in_specs=in_specs,
        out_specs=pl.BlockSpec((1, t, width), lambda i, j: (i, j, 0)),
        out_shape=jax.ShapeDtypeStruct((b, s, width), BF16),
        compiler_params=_params(2),
    )(*args)


def _all_sublanes(x, op):
    for shift in (4, 2, 1):
        x = op(x, pltpu.roll(x, shift, 0))
    return x


def _flash_init(m_scr, acc_scr):
    m_scr[...] = jnp.full(m_scr.shape, NEG, F32)
    acc_scr[...] = jnp.zeros(acc_scr.shape, F32)


def _flash_update(h, st_ref, tile_max, vt_h, m_scr, acc_scr):
    tk, tq = st_ref.shape
    m_prev = m_scr[h]
    m_new = _all_sublanes(jnp.maximum(m_prev, tile_max), jnp.maximum)
    alpha = jnp.exp2(m_prev - m_new)
    p = jnp.exp2((st_ref[...].reshape(tk // SUBLANES, SUBLANES, tq) - m_new[None]).reshape(tk, tq).astype(BF16))
    acc = acc_scr[h].reshape(VT_ROWS // SUBLANES, SUBLANES, tq) * alpha[None]
    acc_scr[h] = acc.reshape(VT_ROWS, tq) + jnp.dot(vt_h, p, preferred_element_type=F32)
    m_scr[h] = m_new


def _flash_finish(o_ref, n_groups, acc_scr):
    tq = acc_scr.shape[2]
    for g in range(n_groups):
        pair = []
        for h in (2 * g, 2 * g + 1):
            denom = _all_sublanes(acc_scr[h, HEAD_DIM:HEAD_DIM + SUBLANES, :], jnp.add)
            out = acc_scr[h, 0:HEAD_DIM, :].reshape(HEAD_DIM // SUBLANES, SUBLANES, tq) / denom[None]
            pair.append(out.reshape(HEAD_DIM, tq))
        o_ref[0, :, g * LANES:(g + 1) * LANES] = jnp.concatenate(pair, axis=0).T.astype(o_ref.dtype)


def _flash_pipeline(i, n_heads, qk_scores, bias_tile, vt_slab, s_scr, tmax_scr, m_scr, acc_scr):
    def qk_head(thunk, h, j, slot):
        st = thunk() + bias_tile(h, j)
        s_scr[slot, h] = st
        tmax_scr[slot, h] = st.reshape(st.shape[0] // SUBLANES, SUBLANES, st.shape[1]).max(axis=0)

    def softmax_head(h, j, slot):
        _flash_update(h, s_scr.at[slot, h], tmax_scr[slot, h], vt_slab(h, j), m_scr, acc_scr)

    def softmax_all(j, slot):
        for h in range(n_heads):
            softmax_head(h, j, slot)

    def stage(j_qk, slot_qk, j_sm, slot_sm):
        thunks = qk_scores(j_qk)
        for h in range(n_heads):
            qk_head(thunks[h], h, j_qk, slot_qk)
            softmax_head(h, j_sm, slot_sm)

    for h, thunk in enumerate(qk_scores(0)):
        qk_head(thunk, h, 0, 0)

    def body(trip, carry):
        j = 2 * trip
        stage(j + 1, 1, j, 0)
        stage(j + 2, 0, j + 1, 1)
        return carry

    lax.fori_loop(0, i // 2, body, 0)
    last = 2 * (i // 2)

    @pl.when(i % 2 == 0)
    def _():
        softmax_all(last, 0)

    @pl.when(i % 2 == 1)
    def _():
        stage(last + 1, 1, last, 0)
        softmax_all(last + 1, 1)


def _fox_aug_kernel(misc_ref, fbias_ref, tri_ref, o_ref):
    s_len, width = misc_ref.shape[1], misc_ref.shape[2]
    term = lax.broadcasted_iota(jnp.int32, (LANES, width), 1) % HEAD_DIM
    carry = jnp.zeros((1, width), F32)
    for c in range(s_len // LANES):
        z = misc_ref[0, c * LANES:(c + 1) * LANES, :] + fbias_ref[...]
        log_f = jnp.minimum(z, 0.0) - jnp.log1p(jnp.exp(-jnp.abs(z)))
        cum = jnp.dot(tri_ref[...], log_f, precision=HIGHEST, preferred_element_type=F32) + carry
        carry = cum[LANES - 1:LANES, :]
        x = cum * (-LOG2E)
        hi = x.astype(BF16).astype(F32)
        rest = x - hi
        mid = rest.astype(BF16).astype(F32)
        low = rest - mid
        out = jnp.where(term == 0, hi, jnp.where(term == 1, mid, jnp.where(term == 2, low, 0.0)))
        o_ref[0, c * LANES:(c + 1) * LANES, :] = out.astype(BF16)


def _fox_key_terms(misc, fbias, layer):
    b, s, width = misc.shape
    tri = jnp.asarray(np.tril(np.ones((LANES, LANES), np.float32)))
    return pl.pallas_call(
        _fox_aug_kernel,
        grid=(b,),
        in_specs=[pl.BlockSpec((1, s, width), lambda i: (i, 0, 0)),
                  pl.BlockSpec((None, 1, width), lambda i: (layer, 0, 0)),
                  pl.BlockSpec((LANES, LANES), lambda i: (0, 0))],
        out_specs=pl.BlockSpec((1, s, width), lambda i: (i, 0, 0)),
        out_shape=jax.ShapeDtypeStruct((b, s, width), BF16),
        compiler_params=_params(1),
    )(misc, fbias, tri)


def _fox_kernel(q_ref, k_ref, aug_ref, vt_ref, mask_ref, o_ref, qs_scr, s_scr, tmax_scr, m_scr, acc_scr):
    t = FLASH_T
    i = pl.program_id(1)
    lo = _lane_lo((t, LANES))
    lane = lax.broadcasted_iota(jnp.int32, (t, LANES), 1)
    ones = jnp.where(lane % HEAD_DIM < KEY_BIAS_TERMS, 1.0, 0.0).astype(BF16)
    n_groups = FOX_HEADS // 2
    for g in range(n_groups):
        qg = q_ref[0, :, g * LANES:(g + 1) * LANES]
        qs_scr[2 * g] = jnp.where(lo, qg, ones)
        qs_scr[2 * g + 1] = jnp.where(lo, ones, qg)
    _flash_init(m_scr, acc_scr)

    def qk_scores(j):
        start = pl.multiple_of(j * t, t)
        scores = []
        for g in range(n_groups):
            k_tile = k_ref[0, pl.ds(start, t), g * LANES:(g + 1) * LANES]
            a_tile = aug_ref[0, pl.ds(start, t), g * LANES:(g + 1) * LANES]
            k_sel = (jnp.where(lo, k_tile, a_tile), jnp.where(lo, a_tile, k_tile))
            for half in range(2):
                scores.append(functools.partial(lambda k, h: _dot_nt(k, qs_scr[h]), k_sel[half], 2 * g + half))
        return scores

    def bias_tile(h, j):
        return mask_ref[jnp.minimum(i - j, 1)]

    def vt_slab(h, j):
        return vt_ref[0, h * VT_ROWS:(h + 1) * VT_ROWS, pl.ds(pl.multiple_of(j * t, t), t)]

    _flash_pipeline(i, FOX_HEADS, qk_scores, bias_tile, vt_slab, s_scr, tmax_scr, m_scr, acc_scr)
    _flash_finish(o_ref, n_groups, acc_scr)


def _fox_attention(fox_qk, key_terms, vt):
    b, s, _ = fox_qk.shape
    width = FOX_HEADS * HEAD_DIM
    t = FLASH_T
    idx = np.arange(t)
    diag = np.where(idx[:, None] <= idx[None, :], 0.0, NEG)
    masks = jnp.asarray(np.stack([diag, np.zeros((t, t))]).astype(np.float32))
    return pl.pallas_call(
        _fox_kernel,
        grid=(b, s // t),
        in_specs=[pl.BlockSpec((1, t, width), lambda i, j: (i, j, 0)),
                  pl.BlockSpec((1, s, width), lambda i, j: (i, 0, 1)),
                  pl.BlockSpec((1, s, width), lambda i, j: (i, 0, 0)),
                  pl.BlockSpec((1, FOX_HEADS * VT_ROWS, s), lambda i, j: (i, VT_FOX_BLOCK, 0)),
                  pl.BlockSpec(masks.shape, lambda i, j: (0, 0, 0))],
        out_specs=pl.BlockSpec((1, t, width), lambda i, j: (i, j, 0)),
        out_shape=jax.ShapeDtypeStruct((b, s, width), BF16),
        scratch_shapes=[pltpu.VMEM((FOX_HEADS, t, LANES), BF16),
                        pltpu.VMEM((2, FOX_HEADS, t, t), F32),
                        pltpu.VMEM((2, FOX_HEADS, SUBLANES, t), F32),
                        pltpu.VMEM((FOX_HEADS, SUBLANES, t), F32),
                        pltpu.VMEM((FOX_HEADS, VT_ROWS, t), F32)],
        compiler_params=_params(2),
    )(fox_qk, fox_qk, key_terms, vt, masks)


def _split3(x):
    hi = x.astype(BF16)
    rest = x - hi.astype(F32)
    mid = rest.astype(BF16)
    low = (rest - mid.astype(F32)).astype(BF16)
    return hi, mid, low


def _compress_kernel(x_ref, pe_ref, w1a_ref, w1b_ref, w2_ref, o_ref):
    x = x_ref[0]
    n_rows = x.shape[0]

    def mm3(lhs, w_ref):
        return sum(jnp.dot(lhs, w_ref[piece], preferred_element_type=F32) for piece in range(3))

    first = mm3(x, w1a_ref)
    second = mm3(x, w1b_ref)
    pe_term = sum(mm3(piece, w1a_ref) for piece in _split3(pe_ref[0])) \
        + sum(mm3(piece, w1b_ref) for piece in _split3(pe_ref[1]))
    pre = first + pltpu.roll(second, n_rows - 1, 0) + pe_term[0:1, :]
    hid = 0.5 * pre * (1.0 + jnp.tanh(math.sqrt(2.0 / math.pi) * (pre + 0.044715 * (pre * pre * pre))))
    o_ref[0] = jnp.dot(hid, w2_ref[...], precision=HIGHEST, preferred_element_type=F32)


def _compress_weights(cmp_pos, cmp_w1, cmp_w2):
    depth = cmp_w1.shape[0]
    half = CMP_LEN // 2
    feat = CMP_STRIDE * LANES
    eye = jnp.eye(2, dtype=F32)
    w1 = cmp_w1.astype(F32).reshape(depth, 2, CMP_LEN, HEAD_DIM, CMP_HIDDEN)

    def pieces(w):
        big = jnp.einsum('nwldj,hg->nwlhdgj', w, eye).reshape(depth, 2, feat, 2 * CMP_HIDDEN)
        return jnp.stack(_split3(big), axis=2)

    w2 = jnp.einsum('nwjd,hg->nwhjgd', cmp_w2.astype(F32), eye).reshape(depth, 2, 2 * CMP_HIDDEN, LANES)
    w2 = jnp.concatenate([w2, jnp.roll(w2, HEAD_DIM, axis=3)], axis=3)
    pe = jnp.broadcast_to(cmp_pos.astype(F32).reshape(depth, 2, 2, half, 1, HEAD_DIM),
                          (depth, 2, 2, half, 2, HEAD_DIM))
    pe = jnp.broadcast_to(pe.reshape(depth, 2, 2, 1, feat), (depth, 2, 2, 8, feat))
    return pe, pieces(w1[:, :, :half]), pieces(w1[:, :, half:]), w2


def _compress(x, weights, layer, branch):
    b, n_rows, feat = x.shape
    pe, w1a, w1b, w2 = weights

    def picked(shape):
        return pl.BlockSpec((None, None) + shape, lambda i: (layer, branch) + (0,) * len(shape))

    return pl.pallas_call(
        _compress_kernel,
        grid=(b,),
        in_specs=[pl.BlockSpec((1, n_rows, feat), lambda i: (i, 0, 0)),
                  picked((2, 8, feat)),
                  picked((3, feat, 2 * CMP_HIDDEN)),
                  picked((3, feat, 2 * CMP_HIDDEN)),
                  picked((2 * CMP_HIDDEN, 2 * LANES))],
        out_specs=pl.BlockSpec((1, n_rows, 2 * LANES), lambda i: (i, 0, 0)),
        out_shape=jax.ShapeDtypeStruct((b, n_rows, 2 * LANES), F32),
        compiler_params=_params(1),
    )(x, pe, w1a, w1b, w2)


def _select_kernel(q_ref, kc_ref, vct_ref, bias_ref, o_ref, mb_ref, count_scr, psum_scr):
    tq = SEL_TQ
    i = pl.program_id(0)
    lo = _lane_lo((tq, LANES))
    n_rows = kc_ref.shape[1]
    lo_k = _lane_lo((n_rows, LANES))
    n_grp = n_rows // SUBLANES

    k_own = kc_ref[0, :, 0:LANES]
    k_swapped = kc_ref[0, :, LANES:2 * LANES]
    hi = k_own.astype(BF16)
    low = (k_swapped - k_swapped.astype(BF16).astype(F32)).astype(BF16)
    k_sel = (jnp.where(lo_k, hi, low), jnp.where(lo_k, low, hi))

    def scores(g):
        qg = q_ref[0, :, g * LANES:(g + 1) * LANES]
        swapped = pltpu.roll(qg.astype(F32), HEAD_DIM, 1).astype(BF16)
        q_dup = (jnp.where(lo, qg, swapped), jnp.where(lo, swapped, qg))
        return [bias_ref[2 * g + half, 0] + _dot_nt(k_sel[half], q_dup[half]) for half in range(2)]

    query = i * tq + lax.broadcasted_iota(jnp.int32, (SUBLANES, tq), 1)
    has_keys = query >= CMP_LEN - 1
    p_sum = [None, None]

    def softmax_pv(g, sts):
        pair = []
        for half in range(2):
            s3 = sts[half].reshape(n_grp, SUBLANES, tq)
            m = _all_sublanes(s3.max(axis=0), jnp.maximum)
            e = jnp.exp2(s3 - m[None])
            inv = jnp.where(has_keys, 1.0 / _all_sublanes(e.sum(axis=0), jnp.add), 0.0)
            p = e * inv[None]
            p_sum[half] = p if p_sum[half] is None else p_sum[half] + p
            pair.append(jnp.dot(vct_ref[0, half * HEAD_DIM:(half + 1) * HEAD_DIM, :],
                                p.reshape(n_rows, tq).astype(BF16), preferred_element_type=F32))
        o_ref[0, :, g * LANES:(g + 1) * LANES] = jnp.concatenate(pair, axis=0).T.astype(o_ref.dtype)

    n_groups = NSA_HEADS // 2
    pending = scores(0)
    for g in range(n_groups):
        current = pending
        if g + 1 < n_groups:
            pending = scores(g + 1)
        softmax_pv(g, current)

    n_blk = HEAD_DIM
    blk_grp = n_blk // SUBLANES
    sub = lax.broadcasted_iota(jnp.int32, (SUBLANES, tq), 0)
    q_blk = query // SLC_BLOCK
    kind = []
    for r in range(blk_grp):
        blk = sub + r * SUBLANES
        behind = q_blk - blk
        forced = jnp.where(blk == 0, 1, 0) + jnp.where(behind == 0, 1, 0) + jnp.where(behind == 1, 1, 0)
        kind.append(jnp.where(behind < 0, 2, jnp.minimum(forced, 1)))
    masks = []
    per_blk = SLC_BLOCK // CMP_STRIDE
    n_real = n_rows // per_blk
    n_lane_chunks = tq // LANES
    psum_scr[:, 0:SUBLANES, :] = jnp.zeros((n_lane_chunks, SUBLANES, LANES), F32)
    for half in (1, 0):
        p_rows = p_sum[half].reshape(n_rows, tq)
        for c in range(n_lane_chunks):
            psum_scr[c, SUBLANES:SUBLANES + n_rows, :] = p_rows[:, c * LANES:(c + 1) * LANES]

        def every_fourth(offset):
            return jnp.concatenate([psum_scr[c, pl.ds(SUBLANES + offset, n_real, stride=per_blk), :]
                                    for c in range(n_lane_chunks)], axis=1)

        imp = (0.5 * (every_fourth(-1) + every_fourth(3))
               + (every_fourth(0) + every_fourth(1) + every_fourth(2)))
        if n_real < n_blk:
            imp = jnp.concatenate([imp, jnp.zeros((n_blk - n_real, tq), F32)], axis=0)
        rows = [jnp.where(kind[r] == 2, NEG, jnp.where(kind[r] == 1, FORCE, imp[r * SUBLANES:(r + 1) * SUBLANES, :]))
                for r in range(blk_grp)]
        count_scr[...] = jnp.zeros(count_scr.shape, jnp.int32)
        for r_other in range(blk_grp):
            @pl.when(r_other * SUBLANES * SLC_BLOCK < (i + 1) * tq)
            def _(r_other=r_other, rows=rows):
                counts = [None] * blk_grp
                for s_other in range(SUBLANES):
                    row = jnp.broadcast_to(rows[r_other][s_other:s_other + 1, :], (SUBLANES, tq))
                    for r in range(blk_grp):
                        if r > r_other:
                            beats = jnp.where(row >= rows[r], 1, 0)
                        elif r < r_other:
                            beats = jnp.where(row > rows[r], 1, 0)
                        else:
                            beats = jnp.where(sub > s_other, jnp.where(row >= rows[r], 1, 0),
                                              jnp.where(row > rows[r], 1, 0))
                        counts[r] = beats if counts[r] is None else counts[r] + beats
                for r in range(blk_grp):
                    count_scr[r] = count_scr[r] + counts[r]
        masks.extend(jnp.where(count_scr[r] < TOPK, 0.0, NEG) for r in range(blk_grp))
    for c in range(tq // LANES):
        mb_ref[0, c * LANES:(c + 1) * LANES, :] = jnp.concatenate(
            [mk[:, c * LANES:(c + 1) * LANES] for mk in masks], axis=0).T.astype(BF16)


def _select(nsa_q, k_cmp, v_cmp, bias_c):
    b, s, width = nsa_q.shape
    n_rows = k_cmp.shape[1]
    n_blk = HEAD_DIM
    assert CMP_LEN == 2 * CMP_STRIDE and SLC_BLOCK == 4 * CMP_STRIDE
    tq = SEL_TQ
    vct = v_cmp[:, :, 0:LANES].transpose(0, 2, 1).astype(BF16)
    return pl.pallas_call(
        _select_kernel,
        grid=(s // tq, b),
        in_specs=[pl.BlockSpec((1, tq, width), lambda j, i: (i, j, 0)),
                  pl.BlockSpec((1, n_rows, 2 * LANES), lambda j, i: (i, 0, 0)),
                  pl.BlockSpec((1, LANES, n_rows), lambda j, i: (i, 0, 0)),
                  pl.BlockSpec((NSA_HEADS, 1, n_rows, tq), lambda j, i: (0, j, 0, 0))],
        out_specs=[pl.BlockSpec((1, tq, width), lambda j, i: (i, j, 0)),
                   pl.BlockSpec((1, tq, LANES), lambda j, i: (i, j, 0))],
        out_shape=[jax.ShapeDtypeStruct((b, s, width), BF16),
                   jax.ShapeDtypeStruct((b, s, LANES), BF16)],
        scratch_shapes=[pltpu.VMEM((n_blk // SUBLANES, SUBLANES, tq), jnp.int32),
                        pltpu.VMEM((tq // LANES, SUBLANES + n_rows, LANES), F32)],
        compiler_params=_params(2),
    )(nsa_q, k_cmp, vct, bias_c)


def _slc_kernel(q_ref, mb_ref, k_ref, e2_ref, vt_ref, bias_ref, o_ref, qs_scr, s_scr, tmax_scr, m_scr, acc_scr, *,
                n_near):
    t = FLASH_T
    i = pl.program_id(1)
    lo = _lane_lo((t, LANES))
    n_groups = NSA_HEADS // 2
    mb = mb_ref[0]
    for g in range(n_groups):
        qg = q_ref[0, :, g * LANES:(g + 1) * LANES]
        qs_scr[2 * g] = jnp.where(lo, qg, mb)
        qs_scr[2 * g + 1] = jnp.where(lo, mb, qg)
    _flash_init(m_scr, acc_scr)

    def qk_scores(j):
        start = pl.multiple_of(j * t, t)
        k_tile = k_ref[0, pl.ds(start, t), :]
        e_tile = e2_ref[pl.ds(start, t), :]
        k_sel = (jnp.where(lo, k_tile, e_tile), jnp.where(lo, e_tile, k_tile))
        return [functools.partial(lambda k, pos: _dot_nt(k, qs_scr[pos]), k_sel[pos % 2], pos)
                for pos in range(NSA_HEADS)]

    def bias_tile(pos, j):
        return bias_ref[pos, jnp.minimum(i - j, n_near)]

    def vt_slab(pos, j):
        kv = pos % 2
        return vt_ref[0, kv * VT_ROWS:(kv + 1) * VT_ROWS, pl.ds(pl.multiple_of(j * t, t), t)]

    _flash_pipeline(i, NSA_HEADS, qk_scores, bias_tile, vt_slab, s_scr, tmax_scr, m_scr, acc_scr)
    _flash_finish(o_ref, n_groups, acc_scr)


def _slc_attention(nsa_q, mask_bias, k2, e2, vt, bias):
    b, s, width = nsa_q.shape
    t = FLASH_T
    n_near = bias.shape[1] - 1
    return pl.pallas_call(
        functools.partial(_slc_kernel, n_near=n_near),
        grid=(b, s // t),
        in_specs=[pl.BlockSpec((1, t, width), lambda i, j: (i, j, 0)),
                  pl.BlockSpec((1, t, LANES), lambda i, j: (i, j, 0)),
                  pl.BlockSpec((1, s, LANES), lambda i, j: (i, 0, 0)),
                  pl.BlockSpec((s, LANES), lambda i, j: (0, 0)),
                  pl.BlockSpec((1, 2 * VT_ROWS, s), lambda i, j: (i, VT_SLC_BLOCK, 0)),
                  pl.BlockSpec(bias.shape, lambda i, j: (0, 0, 0, 0))],
        out_specs=pl.BlockSpec((1, t, width), lambda i, j: (i, j, 0)),
        out_shape=jax.ShapeDtypeStruct((b, s, width), BF16),
        scratch_shapes=[pltpu.VMEM((NSA_HEADS, t, LANES), BF16),
                        pltpu.VMEM((2, NSA_HEADS, t, t), F32),
                        pltpu.VMEM((2, NSA_HEADS, SUBLANES, t), F32),
                        pltpu.VMEM((NSA_HEADS, SUBLANES, t), F32),
                        pltpu.VMEM((NSA_HEADS, VT_ROWS, t), F32)],
        compiler_params=_params(2),
    )(nsa_q, mask_bias, k2, e2, vt, bias)


def _block_onehot(s_len):
    blk = np.arange(s_len)[:, None] // SLC_BLOCK
    lane = np.arange(LANES)[None, :] % HEAD_DIM
    return jnp.asarray((blk == lane).astype(np.float32), dtype=BF16)


def _mix_ffn_kernel(x_ref, mod_ref, swa_ref, fox_ref, cmp_ref, slc_ref, win_ref, misc_ref, expand_ref, gn_ref,
                    w_ref, post_ref, pre_ref, wg_ref, wu_ref, wd_ref, fpost_ref, o_ref):
    n_swa = SWA_HEADS * HEAD_DIM
    n_fox = FOX_HEADS * HEAD_DIM
    n_nsa = NSA_HEADS * HEAD_DIM
    gate = jax.nn.sigmoid(misc_ref[0])
    gate_hi = gate.astype(BF16)
    gate_lo = (gate - gate_hi.astype(F32)).astype(BF16)
    gates = (jnp.dot(gate_hi, expand_ref[...], preferred_element_type=F32)
             + jnp.dot(gate_lo, expand_ref[...], preferred_element_type=F32))
    o_nsa = (gates[:, 0:n_nsa] * cmp_ref[0].astype(F32) + gates[:, n_nsa:2 * n_nsa] * slc_ref[0].astype(F32)
             + gates[:, 2 * n_nsa:3 * n_nsa] * win_ref[0].astype(F32))
    a = _rms(swa_ref[0].astype(F32), gn_ref[:, 0:n_swa]).astype(BF16)
    b = _rms(fox_ref[0].astype(F32), gn_ref[:, n_swa:n_swa + n_fox]).astype(BF16)
    c = _rms(o_nsa, gn_ref[:, n_swa + n_fox:]).astype(BF16)
    y = (jnp.dot(a, w_ref[0:n_swa, :], preferred_element_type=F32)
         + jnp.dot(b, w_ref[n_swa:n_swa + n_fox, :], preferred_element_type=F32)
         + jnp.dot(c, w_ref[n_swa + n_fox:, :], preferred_element_type=F32))
    x = x_ref[0] + mod_ref[0, 2:3, :] * _rms(y, post_ref[...])

    h = (_rms(x, pre_ref[...]) * (1.0 + mod_ref[0, 4:5, :]) + mod_ref[0, 3:4, :]).astype(BF16)
    y = jnp.zeros(x.shape, F32)
    for chunk in range(wg_ref.shape[1] // FFN_CHUNK):
        cols = slice(chunk * FFN_CHUNK, (chunk + 1) * FFN_CHUNK)
        gate = jnp.dot(h, wg_ref[:, cols], preferred_element_type=F32)
        up = jnp.dot(h, wu_ref[:, cols], preferred_element_type=F32)
        act = (gate * jax.nn.sigmoid(gate) * up).astype(BF16)
        y = y + jnp.dot(act, wd_ref[cols, :], preferred_element_type=F32)
    o_ref[0] = x + mod_ref[0, 5:6, :] * _rms(y, fpost_ref[...])


def _gate_expansion():
    expand = np.zeros((LANES, 3 * NSA_HEADS * HEAD_DIM), np.float32)
    for branch in range(3):
        for p in range(NSA_HEADS):
            col = (branch * NSA_HEADS + p) * HEAD_DIM
            expand[GATE_LANE + 8 * branch + p, col:col + HEAD_DIM] = 1.0
    return jnp.asarray(expand, dtype=BF16)


def _mix_ffn(x, mod, o_swa, o_fox, o_cmp, o_slc, o_win, misc, gn, w, post, pre, wg, wu, wd, fpost, layer):
    b, s, d = x.shape
    expand = _gate_expansion()
    hidden = wg.shape[2]

    def rows(width):
        return pl.BlockSpec((1, ROW_TILE, width), lambda i, j: (i, j, 0))

    def whole(shape):
        return pl.BlockSpec(shape, lambda i, j: (0,) * len(shape), pipeline_mode=pl.Buffered(1))

    def of_layer(shape):
        return pl.BlockSpec((None,) + shape, lambda i, j: (layer,) + (0,) * len(shape),
                            pipeline_mode=pl.Buffered(1))

    vec = pl.BlockSpec((None, 1, d), lambda i, j: (layer, 0, 0))
    return pl.pallas_call(
        _mix_ffn_kernel,
        grid=(b, s // ROW_TILE),
        in_specs=[rows(d),
                  pl.BlockSpec((None, 1, ADA_CHUNKS, d), lambda i, j: (layer, i, 0, 0)),
                  rows(o_swa.shape[2]), rows(o_fox.shape[2]), rows(o_cmp.shape[2]), rows(o_slc.shape[2]),
                  rows(o_win.shape[2]), rows(LANES),
                  whole(expand.shape), vec, of_layer((d, d)), vec, vec,
                  of_layer((d, hidden)), of_layer((d, hidden)), of_layer((hidden, d)),
                  vec],
        out_specs=rows(d),
        out_shape=jax.ShapeDtypeStruct((b, s, d), F32),
        compiler_params=_params(2),
    )(x, mod, o_swa, o_fox, o_cmp, o_slc, o_win, misc, expand, gn, w, post, pre, wg, wu, wd, fpost)


def _forget_lanes():
    lanes, heads = [], []
    for h in range(FOX_HEADS):
        base = (h // 2) * LANES + (HEAD_DIM if h % 2 == 0 else 0)
        for j in range(KEY_BIAS_TERMS):
            lanes.append(base + j)
            heads.append(h)
    return np.array(lanes), np.array(heads)


def _in_proj_layout():
    d = HEAD_DIM
    o_qa, o_ka, o_va, o_qb, o_kb, o_vb, o_fb, o_qc = 0, 256, 384, 512, 768, 1024, 1280, 1284
    o_kc, o_vc, o_ksl, o_vsl, o_kw, o_vw, o_gc = 1796, 1924, 2052, 2180, 2308, 2436, 2564
    scale = LOG2E / math.sqrt(d)

    def head_cols(base, heads):
        return np.concatenate([np.arange(base + h * d, base + (h + 1) * d) for h in heads])

    def span(base, width):
        return np.arange(base, base + width)

    cols = [head_cols(o_qa, SWA_POS), span(o_ka, 128),
            span(o_qb, 256), span(o_kb, 256),
            head_cols(o_qc, NSA_POS),
            span(o_kc, 128), span(o_vc, 128),
            span(o_ksl, 128), span(o_kw, 128),
            span(o_vb, 256), span(o_va, 128), span(o_vsl, 128), span(o_vw, 128)]
    scales = [np.full(256, scale), np.ones(128), np.full(256, scale), np.ones(256), np.full(512, scale),
              np.ones(256), np.ones(256), np.ones(640)]
    lanes, heads = _forget_lanes()
    misc_cols = np.zeros(SEG_MISC[1] - SEG_MISC[0], np.int64)
    misc_scale = np.zeros(SEG_MISC[1] - SEG_MISC[0])
    misc_cols[lanes] = o_fb + heads
    misc_scale[lanes] = 1.0
    for branch in range(3):
        for p, h in enumerate(NSA_POS):
            misc_cols[GATE_LANE + 8 * branch + p] = o_gc + h * 3 + branch
            misc_scale[GATE_LANE + 8 * branch + p] = 1.0
    cols.append(misc_cols)
    scales.append(misc_scale)
    return np.concatenate(cols), np.concatenate(scales).astype(np.float32)


def _head_perm(pos):
    return np.concatenate([np.arange(h * HEAD_DIM, (h + 1) * HEAD_DIM) for h in pos])


def kernel(x, c, rel_bias, ada_w, ada_b, attn_pre_norm, attn_post_norm, ffn_pre_norm, ffn_post_norm, w_in,
           forget_bias, swa_sinks, cmp_pos, cmp_w1, cmp_w2, group_norm, w_out, ffn_w_gate, ffn_w_up, ffn_w_down):
    b, s, d = x.shape
    depth = w_in.shape[0]
    hidden = ffn_w_gate.shape[2]
    assert s % (2 * FLASH_T) == 0 and s // SLC_BLOCK <= HEAD_DIM and hidden % FFN_CHUNK == 0

    cols, scales = _in_proj_layout()
    n_main = SEG_MISC[0]
    breaks = np.flatnonzero(np.diff(cols[:n_main]) != 1) + 1
    runs = np.split(cols[:n_main], breaks)
    w_main = (jnp.concatenate([w_in[:, :, int(r[0]):int(r[-1]) + 1] for r in runs], axis=2)
              * scales[:n_main]).astype(BF16)
    used = np.flatnonzero(scales[n_main:] != 0)
    w_misc = jnp.zeros((depth, d, SEG_MISC[1] - SEG_MISC[0]), w_in.dtype).at[:, :, used].set(
        w_in[:, :, cols[n_main:][used]]).astype(BF16)
    lanes, heads = _forget_lanes()
    fbias_all = jnp.zeros((depth, 1, SEG_MISC[1] - SEG_MISC[0]), F32).at[:, 0, lanes].set(
        forget_bias[:, heads].astype(F32))
    swa_perm = _head_perm(SWA_POS)
    nsa_perm = _head_perm(NSA_POS)
    n_swa, n_fox = SWA_HEADS * HEAD_DIM, FOX_HEADS * HEAD_DIM
    mix_perm = np.concatenate([swa_perm, n_swa + np.arange(n_fox), n_swa + n_fox + nsa_perm])
    w_out_all = w_out[:, mix_perm, :].astype(BF16)
    wg_all = ffn_w_gate.astype(BF16)
    wu_all = ffn_w_up.astype(BF16)
    wd_all = ffn_w_down.astype(BF16)
    cmp_weights = _compress_weights(cmp_pos, cmp_w1, cmp_w2)

    def stacked(v):
        return v.astype(F32).reshape(depth, 1, v.shape[1])

    gn_all = stacked(group_norm[:, mix_perm])
    attn_pre, attn_post = stacked(attn_pre_norm), stacked(attn_post_norm)
    ffn_pre, ffn_post = stacked(ffn_pre_norm), stacked(ffn_post_norm)
    sinks_all = swa_sinks[:, np.array(SWA_POS)].astype(F32)

    tab_swa = rel_bias[:, np.array(SWA_POS)].astype(F32)
    tab_nsa = rel_bias[:, SWA_HEADS + np.array(NSA_POS)].astype(F32)
    bias_swa = _bias_table(tab_swa, _band_buckets_t(SWA_TILE, SWA_WINDOW))
    bias_win = _bias_table(tab_nsa, _band_buckets_t(WIN_TILE, NSA_WINDOW))
    bias_slc = _bias_table(tab_nsa, _toeplitz_buckets_t(FLASH_T, _near_tiles(FLASH_T)), subtract_last=True)
    n_rows = s // CMP_STRIDE
    bias_cmp = _bias_table(tab_nsa, _cmp_buckets_t(s, n_rows))
    e2 = _block_onehot(s)

    mod_all = _adaln(c.astype(F32), ada_w.astype(F32), ada_b.astype(F32)).reshape(depth, b, ADA_CHUNKS, d)

    for layer in range(depth):
        swa_qk, fox_qk, nsa_q, kc, vc, k2, misc, vt = _in_proj(x, mod_all, attn_pre, w_main, w_misc, layer)
        o_swa = _banded_attention(swa_qk, swa_qk, 2, vt, VT_SWA_BLOCK, bias_swa, sinks=sinks_all, layer=layer)
        o_fox = _fox_attention(fox_qk, _fox_key_terms(misc, fbias_all, layer), vt)
        k_cmp = _compress(kc, cmp_weights, layer, 0)
        v_cmp = _compress(vc, cmp_weights, layer, 1)
        o_cmp, mask_bias = _select(nsa_q, k_cmp, v_cmp, bias_cmp)
        o_slc = _slc_attention(nsa_q, mask_bias, k2, e2, vt, bias_slc)
        o_win = _banded_attention(nsa_q, k2, 1, vt, VT_WIN_BLOCK, bias_win)
        x = _mix_ffn(x, mod_all, o_swa, o_fox, o_cmp, o_slc, o_win, misc, gn_all, w_out_all, attn_post, ffn_pre,
                     wg_all, wu_all, wd_all, ffn_post, layer)
    return x
```

```python
import functools
import math

import numpy as np
import jax
import jax.numpy as jnp
from jax import lax
from jax.experimental import pallas as pl
from jax.experimental.pallas import tpu as pltpu

F32 = jnp.float32
BF16 = jnp.bfloat16
HIGHEST = lax.Precision.HIGHEST

LANES = 128
SUBLANES = 8
VMEM_LIMIT = 56 * 1024 * 1024

HEAD_DIM = 64
SWA_HEADS = 4
SWA_WINDOW = 128
FOX_HEADS = 4
NSA_HEADS = 8
CMP_LEN = 32
CMP_STRIDE = 16
CMP_HIDDEN = 2 * HEAD_DIM
SLC_BLOCK = 64
TOPK = 16
NSA_WINDOW = 512
REL_BUCKETS = 32
REL_MAX_DISTANCE = 1024
ZERO_BUCKET = -2
RMS_EPS = 1e-6
NEG = -1e30
FORCE = 1e30
ADA_CHUNKS = 6
LOG2E = math.log2(math.e)

SWA_POS = (0, 2, 1, 3)
NSA_POS = (0, 4, 1, 5, 2, 6, 3, 7)

SWA_TILE = 256
WIN_TILE = 256
FLASH_T = 256
FOX_TQ = 512
SEL_TQ = 256
IN_ROW_TILE = 1024
ROW_TILE = 512
FFN_CHUNK = 256
VT_ROWS = HEAD_DIM + 16
KEY_BIAS_TERMS = 3

SEG_SWA = (0, 384)
SEG_FOX = (384, 896)
SEG_NSAQ = (896, 1408)
SEG_KC = (1408, 1536)
SEG_VC = (1536, 1664)
SEG_K2 = (1664, 1920)
SEG_V = (1920, 2560)
SEG_MISC = (2560, 2816)
GATE_LANE = 8
VT_FOX_BLOCK, VT_SWA_BLOCK, VT_SLC_BLOCK, VT_WIN_BLOCK = 0, 2, 3, 4


def _params(n_grid, vmem=VMEM_LIMIT):
    return pltpu.CompilerParams(dimension_semantics=("parallel",) * n_grid, vmem_limit_bytes=vmem)


def _dot_nt(a, b):
    return lax.dot_general(a, b, (((1,), (1,)), ((), ())), preferred_element_type=F32)


def _lane_lo(shape):
    return lax.broadcasted_iota(jnp.int32, shape, len(shape) - 1) < HEAD_DIM


def _adaln_kernel(c_ref, w_ref, b_ref, o_ref):
    c = c_ref[...]
    act = c * jax.nn.sigmoid(c)
    o_ref[0] = jnp.dot(act, w_ref[0], precision=HIGHEST, preferred_element_type=F32) + b_ref[0]


def _adaln(c, ada_w, ada_b):
    depth, d, n = ada_w.shape
    b = c.shape[0]
    return pl.pallas_call(
        _adaln_kernel,
        grid=(depth, n // d),
        in_specs=[pl.BlockSpec((b, d), lambda l, j: (0, 0)),
                  pl.BlockSpec((1, d, d), lambda l, j: (l, 0, j)),
                  pl.BlockSpec((1, 1, d), lambda l, j: (l, 0, j))],
        out_specs=pl.BlockSpec((1, b, d), lambda l, j: (l, 0, j)),
        out_shape=jax.ShapeDtypeStruct((depth, b, n), F32),
        compiler_params=_params(2),
    )(c, ada_w, ada_b.reshape(depth, 1, n))


def _t5_bucket(dist):
    n = jnp.maximum(dist, 0)
    max_exact = REL_BUCKETS // 2
    nf = jnp.maximum(n, 1).astype(jnp.float32)
    large = max_exact + (jnp.log(nf / max_exact) / math.log(REL_MAX_DISTANCE / max_exact)
                         * (REL_BUCKETS - max_exact)).astype(jnp.int32)
    large = jnp.minimum(large, REL_BUCKETS - 1)
    return jnp.where(n < max_exact, n, large)


def _bias_table_kernel(tab_ref, bucket_ref, o_ref, *, subtract_last):
    n_heads = o_ref.shape[0]
    values = [[(tab_ref[k, h] - (tab_ref[REL_BUCKETS - 1, h] if subtract_last else 0.0)) * LOG2E
               for h in range(n_heads)] for k in range(REL_BUCKETS)]

    def rows(chunk, carry):
        r0 = pl.multiple_of(chunk * SUBLANES, SUBLANES)
        bucket = bucket_ref[0, pl.ds(r0, SUBLANES), :]
        accs = [jnp.where(bucket == ZERO_BUCKET, 0.0, NEG) for _ in range(n_heads)]
        for k in range(REL_BUCKETS):
            hit = bucket == k
            for h in range(n_heads):
                accs[h] = jnp.where(hit, values[k][h], accs[h])
        for h in range(n_heads):
            o_ref[h, 0, pl.ds(r0, SUBLANES), :] = accs[h]
        return carry

    lax.fori_loop(0, bucket_ref.shape[1] // SUBLANES, rows, 0)


def _bias_table(table, bucket, subtract_last=False):
    n_heads = table.shape[1]
    n, r, c = bucket.shape
    return pl.pallas_call(
        functools.partial(_bias_table_kernel, subtract_last=subtract_last),
        grid=(n,),
        in_specs=[pl.BlockSpec(memory_space=pltpu.SMEM),
                  pl.BlockSpec((1, r, c), lambda i: (i, 0, 0))],
        out_specs=pl.BlockSpec((n_heads, 1, r, c), lambda i: (0, i, 0, 0)),
        out_shape=jax.ShapeDtypeStruct((n_heads, n, r, c), F32),
        compiler_params=_params(1),
    )(table, bucket)


def _band_buckets_t(tile, window):
    n_back = -(-(window - 1) // tile)
    t = jnp.arange(n_back + 1)[:, None, None]
    key = jnp.arange(tile)[None, :, None]
    query = jnp.arange(tile)[None, None, :]
    dist = query + (n_back - t) * tile - key
    return jnp.where((dist >= 0) & (dist < window), _t5_bucket(dist), -1).astype(jnp.int32)


def _toeplitz_buckets_t(tile, n_tiles):
    m = jnp.arange(n_tiles)[:, None, None]
    key = jnp.arange(tile)[None, :, None]
    query = jnp.arange(tile)[None, None, :]
    dist = m * tile + query - key
    near = jnp.where(dist >= 0, _t5_bucket(dist), -1).astype(jnp.int32)
    return jnp.concatenate([near, jnp.full((1, tile, tile), ZERO_BUCKET, jnp.int32)])


def _cmp_buckets_t(s_len, n_rows):
    n_c = n_rows - 1
    tile = jnp.arange(s_len // SEL_TQ)[:, None, None]
    n = jnp.arange(n_rows)[None, :, None]
    t = tile * SEL_TQ + jnp.arange(SEL_TQ)[None, None, :]
    dist = t - (n * CMP_STRIDE + CMP_LEN - 1)
    return jnp.where((dist >= 0) & (n < n_c), _t5_bucket(dist), -1).astype(jnp.int32)


def _near_tiles(tile):
    max_exact = REL_BUCKETS // 2
    first_const = math.ceil(max_exact * (REL_MAX_DISTANCE / max_exact) ** ((max_exact - 1) / max_exact)) + 1
    m = 1
    while m * tile - (tile - 1) < first_const:
        m += 1
    return m


def _rms(x, gain):
    return x * lax.rsqrt(jnp.mean(x * x, axis=-1, keepdims=True) + RMS_EPS) * gain


def _in_proj_kernel(x_ref, mod_ref, gain_ref, w_ref, wm_ref, swa_ref, fox_ref, nsaq_ref, kc_ref, vc_ref, k2_ref,
                    misc_ref, vt_ref, pack_scr):
    x = x_ref[0]
    h = _rms(x, gain_ref[...]) * (1.0 + mod_ref[0, 1:2, :]) + mod_ref[0, 0:1, :]
    hb = h.astype(BF16)

    def seg(bounds):
        if bounds == SEG_MISC:
            return jnp.dot(hb, wm_ref[...], preferred_element_type=F32)
        return jnp.dot(hb, w_ref[:, bounds[0]:bounds[1]], preferred_element_type=F32)

    swa_ref[0] = seg(SEG_SWA).astype(BF16)
    fox_ref[0] = seg(SEG_FOX).astype(BF16)
    nsaq_ref[0] = seg(SEG_NSAQ).astype(BF16)
    for slot, (bounds, out_ref) in enumerate(((SEG_KC, kc_ref), (SEG_VC, vc_ref))):
        pack_scr[slot] = seg(bounds)
        for tok in range(CMP_STRIDE):
            out_ref[0, :, tok * LANES:(tok + 1) * LANES] = pack_scr[
                slot, pl.ds(tok, x.shape[0] // CMP_STRIDE, stride=CMP_STRIDE), :].astype(BF16)
    k2_ref[0] = seg(SEG_K2).astype(BF16)
    misc_ref[0] = seg(SEG_MISC)

    rows = x.shape[0]
    extra_row = lax.broadcasted_iota(jnp.int32, (VT_ROWS - HEAD_DIM, rows), 0)
    extra = jnp.where(extra_row == 0, 1.0, 0.0).astype(BF16)
    values = seg(SEG_V)
    for c in range(values.shape[1] // LANES):
        vt = values[:, c * LANES:(c + 1) * LANES].T.astype(BF16)
        for half in range(2):
            base = (2 * c + half) * VT_ROWS
            vt_ref[0, base:base + HEAD_DIM, :] = vt[half * HEAD_DIM:(half + 1) * HEAD_DIM, :]
            vt_ref[0, base + HEAD_DIM:base + VT_ROWS, :] = extra


def _in_proj(x, mod, gain, w, w_misc, layer):
    b, s, d = x.shape
    n = w.shape[2]

    tile = IN_ROW_TILE

    def rows(width, dtype):
        return (pl.BlockSpec((1, tile, width), lambda i, j: (i, j, 0)), jax.ShapeDtypeStruct((b, s, width), dtype))

    packed = (pl.BlockSpec((1, tile // CMP_STRIDE, CMP_STRIDE * LANES), lambda i, j: (i, j, 0)),
              jax.ShapeDtypeStruct((b, s // CMP_STRIDE, CMP_STRIDE * LANES), BF16))
    vt_rows = (SEG_V[1] - SEG_V[0]) // HEAD_DIM * VT_ROWS
    outs = [rows(SEG_SWA[1] - SEG_SWA[0], BF16), rows(SEG_FOX[1] - SEG_FOX[0], BF16),
            rows(SEG_NSAQ[1] - SEG_NSAQ[0], BF16), packed, packed, rows(SEG_K2[1] - SEG_K2[0], BF16),
            rows(SEG_MISC[1] - SEG_MISC[0], F32),
            (pl.BlockSpec((1, vt_rows, tile), lambda i, j: (i, 0, j)),
             jax.ShapeDtypeStruct((b, vt_rows, s), BF16))]
    return pl.pallas_call(
        _in_proj_kernel,
        grid=(b, s // tile),
        in_specs=[pl.BlockSpec((1, tile, d), lambda i, j: (i, j, 0)),
                  pl.BlockSpec((None, 1, ADA_CHUNKS, d), lambda i, j: (layer, i, 0, 0)),
                  pl.BlockSpec((None, 1, d), lambda i, j: (layer, 0, 0)),
                  pl.BlockSpec((None, d, n), lambda i, j: (layer, 0, 0)),
                  pl.BlockSpec((None, d, w_misc.shape[2]), lambda i, j: (layer, 0, 0))],
        out_specs=[spec for spec, _ in outs],
        out_shape=[shape for _, shape in outs],
        scratch_shapes=[pltpu.VMEM((2, tile, LANES), F32)],
        compiler_params=_params(2),
    )(x, mod, gain, w, w_misc)


def _banded_kernel(*refs, n_back, n_groups, has_sink, t, layer):
    if has_sink:
        sink_ref, q_ref, k_ref, vt_ref, bias_ref, o_ref = refs
    else:
        q_ref, k_ref, vt_ref, bias_ref, o_ref = refs
    i = pl.program_id(1)
    lo = _lane_lo((t, LANES))
    n_tiles = n_back + 1

    def run(all_valid):
        starts = [pl.multiple_of(jnp.maximum(i - n_back + tt, 0) * t, t) for tt in range(n_tiles)]
        k_tiles = [k_ref[0, pl.ds(start, t), :] for start in starts]

        def scores(g):
            qg = q_ref[0, :, g * LANES:(g + 1) * LANES]
            zero = jnp.zeros_like(qg)
            qms = (jnp.where(lo, qg, zero), jnp.where(lo, zero, qg))
            return [[bias_ref[2 * g + half, tt] + _dot_nt(k_tiles[tt], qms[half]) for tt in range(n_tiles)]
                    for half in range(2)]

        def softmax_pv(g, sts):
            pair = []
            for half in range(2):
                tiles = sts[half]
                if not all_valid:
                    tiles = [jnp.where(i - n_back + tt >= 0, st, NEG) if tt < n_back else st
                             for tt, st in enumerate(tiles)]
                m = None
                for st in tiles:
                    part = st.reshape(t // SUBLANES, SUBLANES, t).max(axis=0)
                    m = part if m is None else jnp.maximum(m, part)
                m = _all_sublanes(m, jnp.maximum)
                if has_sink:
                    sink = sink_ref[layer, 2 * g + half] * LOG2E
                    m = jnp.maximum(m, sink)
                acc = None
                for tt, st in enumerate(tiles):
                    p = jnp.exp2((st.reshape(t // SUBLANES, SUBLANES, t) - m[None]).reshape(t, t).astype(BF16))
                    part = jnp.dot(vt_ref[0, half * VT_ROWS:(half + 1) * VT_ROWS, pl.ds(starts[tt], t)], p,
                                   preferred_element_type=F32)
                    acc = part if acc is None else acc + part
                denom = _all_sublanes(acc[HEAD_DIM:HEAD_DIM + SUBLANES, :], jnp.add)
                if has_sink:
                    denom = denom + jnp.exp2(sink - m)
                out = acc[0:HEAD_DIM, :].reshape(HEAD_DIM // SUBLANES, SUBLANES, t) / denom[None]
                pair.append(out.reshape(HEAD_DIM, t))
            o_ref[0, :, g * LANES:(g + 1) * LANES] = jnp.concatenate(pair, axis=0).T.astype(o_ref.dtype)

        pending = scores(0)
        for g in range(n_groups):
            current = pending
            if g + 1 < n_groups:
                pending = scores(g + 1)
            softmax_pv(g, current)

    @pl.when(i >= n_back)
    def _():
        run(True)

    @pl.when(i < n_back)
    def _():
        run(False)


def _banded_attention(q_arr, k_arr, k_blk, vt, vt_blk, bias, sinks=None, layer=0):
    b, s, _ = q_arr.shape
    n_pos, n_tiles, t = bias.shape[0], bias.shape[1], bias.shape[2]
    width = n_pos * HEAD_DIM
    in_specs = [pl.BlockSpec((1, t, width), lambda i, j: (i, j, 0)),
                pl.BlockSpec((1, s, LANES), lambda i, j: (i, 0, k_blk)),
                pl.BlockSpec((1, 2 * VT_ROWS, s), lambda i, j: (i, vt_blk, 0)),
                pl.BlockSpec(bias.shape, lambda i, j: (0, 0, 0, 0))]
    args = [q_arr, k_arr, vt, bias]
    if sinks is not None:
        in_specs = [pl.BlockSpec(memory_space=pltpu.SMEM)] + in_specs
        args = [sinks] + args
    return pl.pallas_call(
        functools.partial(_banded_kernel, n_back=n_tiles - 1, n_groups=n_pos // 2, has_sink=sinks is not None, t=t,
                          layer=layer),
        grid=(b, s // t),
        in_specs=in_specs,
        out_specs=pl.BlockSpec((1, t, width), lambda i, j: (i, j, 0)),
        out_shape=jax.ShapeDtypeStruct((b, s, width), BF16),
        compiler_params=_params(2),
    )(*args)


def _all_sublanes(x, op):
    for shift in (4, 2, 1):
        x = op(x, pltpu.roll(x, shift, 0))
    return x


def _flash_init(m_scr, acc_scr):
    m_scr[...] = jnp.full(m_scr.shape, NEG, F32)
    acc_scr[...] = jnp.zeros(acc_scr.shape, F32)


def _flash_update(h, st_ref, tile_max, vt_h, m_scr, acc_scr):
    tk, tq = st_ref.shape
    m_prev = m_scr[h]
    m_new = _all_sublanes(jnp.maximum(m_prev, tile_max), jnp.maximum)
    alpha = jnp.exp2(m_prev - m_new)
    p = jnp.exp2((st_ref[...].reshape(tk // SUBLANES, SUBLANES, tq) - m_new[None]).reshape(tk, tq).astype(BF16))
    acc = acc_scr[h].reshape(VT_ROWS // SUBLANES, SUBLANES, tq) * alpha[None]
    acc_scr[h] = acc.reshape(VT_ROWS, tq) + jnp.dot(vt_h, p, preferred_element_type=F32)
    m_scr[h] = m_new


def _flash_finish(o_ref, n_groups, acc_scr):
    tq = acc_scr.shape[2]
    for g in range(n_groups):
        pair = []
        for h in (2 * g, 2 * g + 1):
            denom = _all_sublanes(acc_scr[h, HEAD_DIM:HEAD_DIM + SUBLANES, :], jnp.add)
            out = acc_scr[h, 0:HEAD_DIM, :].reshape(HEAD_DIM // SUBLANES, SUBLANES, tq) / denom[None]
            pair.append(out.reshape(HEAD_DIM, tq))
        o_ref[0, :, g * LANES:(g + 1) * LANES] = jnp.concatenate(pair, axis=0).T.astype(o_ref.dtype)


def _flash_pipeline(i, n_heads, qk_scores, bias_tile, vt_slab, s_scr, tmax_scr, m_scr, acc_scr):
    def qk_head(thunk, h, j, slot):
        st = thunk() + bias_tile(h, j)
        s_scr[slot, h] = st
        tmax_scr[slot, h] = st.reshape(st.shape[0] // SUBLANES, SUBLANES, st.shape[1]).max(axis=0)

    def softmax_head(h, j, slot):
        _flash_update(h, s_scr.at[slot, h], tmax_scr[slot, h], vt_slab(h, j), m_scr, acc_scr)

    def softmax_all(j, slot):
        for h in range(n_heads):
            softmax_head(h, j, slot)

    def stage(j_qk, slot_qk, j_sm, slot_sm):
        thunks = qk_scores(j_qk)
        for h in range(n_heads):
            qk_head(thunks[h], h, j_qk, slot_qk)
            softmax_head(h, j_sm, slot_sm)

    for h, thunk in enumerate(qk_scores(0)):
        qk_head(thunk, h, 0, 0)

    def body(trip, carry):
        j = 2 * trip
        stage(j + 1, 1, j, 0)
        stage(j + 2, 0, j + 1, 1)
        return carry

    lax.fori_loop(0, i // 2, body, 0)
    last = 2 * (i // 2)

    @pl.when(i % 2 == 0)
    def _():
        softmax_all(last, 0)

    @pl.when(i % 2 == 1)
    def _():
        stage(last + 1, 1, last, 0)
        softmax_all(last + 1, 1)


def _fox_aug_kernel(misc_ref, fbias_ref, tri_ref, o_ref):
    s_len, width = misc_ref.shape[1], misc_ref.shape[2]
    term = lax.broadcasted_iota(jnp.int32, (LANES, width), 1) % HEAD_DIM
    carry = jnp.zeros((1, width), F32)
    for c in range(s_len // LANES):
        z = misc_ref[0, c * LANES:(c + 1) * LANES, :] + fbias_ref[...]
        log_f = jnp.minimum(z, 0.0) - jnp.log1p(jnp.exp(-jnp.abs(z)))
        cum = jnp.dot(tri_ref[...], log_f, precision=HIGHEST, preferred_element_type=F32) + carry
        carry = cum[LANES - 1:LANES, :]
        x = cum * (-LOG2E)
        hi = x.astype(BF16).astype(F32)
        rest = x - hi
        mid = rest.astype(BF16).astype(F32)
        low = rest - mid
        out = jnp.where(term == 0, hi, jnp.where(term == 1, mid, jnp.where(term == 2, low, 0.0)))
        o_ref[0, c * LANES:(c + 1) * LANES, :] = out.astype(BF16)


def _fox_key_terms(misc, fbias, layer):
    b, s, width = misc.shape
    tri = jnp.asarray(np.tril(np.ones((LANES, LANES), np.float32)))
    return pl.pallas_call(
        _fox_aug_kernel,
        grid=(b,),
        in_specs=[pl.BlockSpec((1, s, width), lambda i: (i, 0, 0)),
                  pl.BlockSpec((None, 1, width), lambda i: (layer, 0, 0)),
                  pl.BlockSpec((LANES, LANES), lambda i: (0, 0))],
        out_specs=pl.BlockSpec((1, s, width), lambda i: (i, 0, 0)),
        out_shape=jax.ShapeDtypeStruct((b, s, width), BF16),
        compiler_params=_params(1),
    )(misc, fbias, tri)


def _fox_kernel(q_ref, k_ref, aug_ref, vt_ref, mask_ref, o_ref, qs_scr, s_scr, tmax_scr, m_scr, acc_scr):
    t = FLASH_T
    tq = q_ref.shape[1]
    ratio = tq // t
    i = pl.program_id(1)
    lo = _lane_lo((t, LANES))
    lo_q = _lane_lo((tq, LANES))
    lane = lax.broadcasted_iota(jnp.int32, (tq, LANES), 1)
    ones = jnp.where(lane % HEAD_DIM < KEY_BIAS_TERMS, 1.0, 0.0).astype(BF16)
    n_groups = FOX_HEADS // 2
    for g in range(n_groups):
        qg = q_ref[0, :, g * LANES:(g + 1) * LANES]
        qs_scr[2 * g] = jnp.where(lo_q, qg, ones)
        qs_scr[2 * g + 1] = jnp.where(lo_q, ones, qg)
    _flash_init(m_scr, acc_scr)

    def qk_scores(j):
        start = pl.multiple_of(j * t, t)
        scores = []
        for g in range(n_groups):
            k_tile = k_ref[0, pl.ds(start, t), g * LANES:(g + 1) * LANES]
            a_tile = aug_ref[0, pl.ds(start, t), g * LANES:(g + 1) * LANES]
            k_sel = (jnp.where(lo, k_tile, a_tile), jnp.where(lo, a_tile, k_tile))
            for half in range(2):
                scores.append(functools.partial(lambda k, h: _dot_nt(k, qs_scr[h]), k_sel[half], 2 * g + half))
        return scores

    def bias_tile(h, j):
        return mask_ref[jnp.clip(j - ratio * i + 1, 0, ratio)]

    def vt_slab(h, j):
        return vt_ref[0, h * VT_ROWS:(h + 1) * VT_ROWS, pl.ds(pl.multiple_of(j * t, t), t)]

    _flash_pipeline(ratio * i + ratio - 1, FOX_HEADS, qk_scores, bias_tile, vt_slab, s_scr, tmax_scr, m_scr, acc_scr)
    _flash_finish(o_ref, n_groups, acc_scr)


def _fox_attention(fox_qk, key_terms, vt):
    b, s, _ = fox_qk.shape
    width = FOX_HEADS * HEAD_DIM
    t, tq = FLASH_T, FOX_TQ
    ratio = tq // t
    key = np.arange(t)[:, None]
    query = np.arange(tq)[None, :]
    masks = [np.zeros((t, tq))] + [np.where(r * t + key <= query, 0.0, NEG) for r in range(ratio)]
    masks = jnp.asarray(np.stack(masks).astype(np.float32))
    return pl.pallas_call(
        _fox_kernel,
        grid=(b, s // tq),
        in_specs=[pl.BlockSpec((1, tq, width), lambda i, j: (i, j, 0)),
                  pl.BlockSpec((1, s, width), lambda i, j: (i, 0, 1)),
                  pl.BlockSpec((1, s, width), lambda i, j: (i, 0, 0)),
                  pl.BlockSpec((1, FOX_HEADS * VT_ROWS, s), lambda i, j: (i, VT_FOX_BLOCK, 0)),
                  pl.BlockSpec(masks.shape, lambda i, j: (0, 0, 0))],
        out_specs=pl.BlockSpec((1, tq, width), lambda i, j: (i, j, 0)),
        out_shape=jax.ShapeDtypeStruct((b, s, width), BF16),
        scratch_shapes=[pltpu.VMEM((FOX_HEADS, tq, LANES), BF16),
                        pltpu.VMEM((2, FOX_HEADS, t, tq), F32),
                        pltpu.VMEM((2, FOX_HEADS, SUBLANES, tq), F32),
                        pltpu.VMEM((FOX_HEADS, SUBLANES, tq), F32),
                        pltpu.VMEM((FOX_HEADS, VT_ROWS, tq), F32)],
        compiler_params=_params(2),
    )(fox_qk, fox_qk, key_terms, vt, masks)


def _split3(x):
    hi = x.astype(BF16)
    rest = x - hi.astype(F32)
    mid = rest.astype(BF16)
    low = (rest - mid.astype(F32)).astype(BF16)
    return hi, mid, low


def _compress_kernel(x_ref, pe_ref, w1a_ref, w1b_ref, w2_ref, o_ref):
    x = x_ref[0]
    n_rows = x.shape[0]

    def mm3(lhs, w_ref):
        return sum(jnp.dot(lhs, w_ref[piece], preferred_element_type=F32) for piece in range(3))

    first = mm3(x, w1a_ref)
    second = mm3(x, w1b_ref)
    pe_term = sum(mm3(piece, w1a_ref) for piece in _split3(pe_ref[0])) \
        + sum(mm3(piece, w1b_ref) for piece in _split3(pe_ref[1]))
    pre = first + pltpu.roll(second, n_rows - 1, 0) + pe_term[0:1, :]
    hid = 0.5 * pre * (1.0 + jnp.tanh(math.sqrt(2.0 / math.pi) * (pre + 0.044715 * (pre * pre * pre))))
    o_ref[0] = jnp.dot(hid, w2_ref[...], precision=HIGHEST, preferred_element_type=F32)


def _compress_weights(cmp_pos, cmp_w1, cmp_w2):
    depth = cmp_w1.shape[0]
    half = CMP_LEN // 2
    feat = CMP_STRIDE * LANES
    eye = jnp.eye(2, dtype=F32)
    w1 = cmp_w1.astype(F32).reshape(depth, 2, CMP_LEN, HEAD_DIM, CMP_HIDDEN)

    def pieces(w):
        big = jnp.einsum('nwldj,hg->nwlhdgj', w, eye).reshape(depth, 2, feat, 2 * CMP_HIDDEN)
        return jnp.stack(_split3(big), axis=2)

    w2 = jnp.einsum('nwjd,hg->nwhjgd', cmp_w2.astype(F32), eye).reshape(depth, 2, 2 * CMP_HIDDEN, LANES)
    w2 = jnp.concatenate([w2, jnp.roll(w2, HEAD_DIM, axis=3)], axis=3)
    pe = jnp.broadcast_to(cmp_pos.astype(F32).reshape(depth, 2, 2, half, 1, HEAD_DIM),
                          (depth, 2, 2, half, 2, HEAD_DIM))
    pe = jnp.broadcast_to(pe.reshape(depth, 2, 2, 1, feat), (depth, 2, 2, 8, feat))
    return pe, pieces(w1[:, :, :half]), pieces(w1[:, :, half:]), w2


def _compress(x, weights, layer, branch):
    b, n_rows, feat = x.shape
    pe, w1a, w1b, w2 = weights

    def picked(shape):
        return pl.BlockSpec((None, None) + shape, lambda i: (layer, branch) + (0,) * len(shape))

    return pl.pallas_call(
        _compress_kernel,
        grid=(b,),
        in_specs=[pl.BlockSpec((1, n_rows, feat), lambda i: (i, 0, 0)),
                  picked((2, 8, feat)),
                  picked((3, feat, 2 * CMP_HIDDEN)),
                  picked((3, feat, 2 * CMP_HIDDEN)),
                  picked((2 * CMP_HIDDEN, 2 * LANES))],
        out_specs=pl.BlockSpec((1, n_rows, 2 * LANES), lambda i: (i, 0, 0)),
        out_shape=jax.ShapeDtypeStruct((b, n_rows, 2 * LANES), F32),
        compiler_params=_params(1),
    )(x, pe, w1a, w1b, w2)


def _select_kernel(q_ref, kc_ref, vct_ref, bias_ref, o_ref, mb_ref, count_scr, psum_scr):
    tq = SEL_TQ
    i = pl.program_id(0)
    lo = _lane_lo((tq, LANES))
    n_rows = kc_ref.shape[1]
    lo_k = _lane_lo((n_rows, LANES))
    n_grp = n_rows // SUBLANES

    k_own = kc_ref[0, :, 0:LANES]
    k_swapped = kc_ref[0, :, LANES:2 * LANES]
    hi = k_own.astype(BF16)
    low = (k_swapped - k_swapped.astype(BF16).astype(F32)).astype(BF16)
    k_sel = (jnp.where(lo_k, hi, low), jnp.where(lo_k, low, hi))

    def scores(g):
        qg = q_ref[0, :, g * LANES:(g + 1) * LANES]
        swapped = pltpu.roll(qg.astype(F32), HEAD_DIM, 1).astype(BF16)
        q_dup = (jnp.where(lo, qg, swapped), jnp.where(lo, swapped, qg))
        return [bias_ref[2 * g + half, 0] + _dot_nt(k_sel[half], q_dup[half]) for half in range(2)]

    query = i * tq + lax.broadcasted_iota(jnp.int32, (SUBLANES, tq), 1)
    has_keys = query >= CMP_LEN - 1
    p_sum = [None, None]

    def softmax_pv(g, sts):
        pair = []
        for half in range(2):
            s3 = sts[half].reshape(n_grp, SUBLANES, tq)
            m = _all_sublanes(s3.max(axis=0), jnp.maximum)
            e = jnp.exp2(s3 - m[None])
            inv = jnp.where(has_keys, 1.0 / _all_sublanes(e.sum(axis=0), jnp.add), 0.0)
            p = e * inv[None]
            p_sum[half] = p if p_sum[half] is None else p_sum[half] + p
            pair.append(jnp.dot(vct_ref[0, half * HEAD_DIM:(half + 1) * HEAD_DIM, :],
                                p.reshape(n_rows, tq).astype(BF16), preferred_element_type=F32))
        o_ref[0, :, g * LANES:(g + 1) * LANES] = jnp.concatenate(pair, axis=0).T.astype(o_ref.dtype)

    n_groups = NSA_HEADS // 2
    pending = scores(0)
    for g in range(n_groups):
        current = pending
        if g + 1 < n_groups:
            pending = scores(g + 1)
        softmax_pv(g, current)

    n_blk = HEAD_DIM
    blk_grp = n_blk // SUBLANES
    sub = lax.broadcasted_iota(jnp.int32, (SUBLANES, tq), 0)
    q_blk = query // SLC_BLOCK
    kind = []
    for r in range(blk_grp):
        blk = sub + r * SUBLANES
        behind = q_blk - blk
        forced = jnp.where(blk == 0, 1, 0) + jnp.where(behind == 0, 1, 0) + jnp.where(behind == 1, 1, 0)
        kind.append(jnp.where(behind < 0, 2, jnp.minimum(forced, 1)))
    masks = []
    per_blk = SLC_BLOCK // CMP_STRIDE
    n_real = n_rows // per_blk
    n_lane_chunks = tq // LANES
    psum_scr[:, 0:SUBLANES, :] = jnp.zeros((n_lane_chunks, SUBLANES, LANES), F32)
    for half in (1, 0):
        p_rows = p_sum[half].reshape(n_rows, tq)
        for c in range(n_lane_chunks):
            psum_scr[c, SUBLANES:SUBLANES + n_rows, :] = p_rows[:, c * LANES:(c + 1) * LANES]

        def every_fourth(offset):
            return jnp.concatenate([psum_scr[c, pl.ds(SUBLANES + offset, n_real, stride=per_blk), :]
                                    for c in range(n_lane_chunks)], axis=1)

        imp = (0.5 * (every_fourth(-1) + every_fourth(3))
               + (every_fourth(0) + every_fourth(1) + every_fourth(2)))
        if n_real < n_blk:
            imp = jnp.concatenate([imp, jnp.zeros((n_blk - n_real, tq), F32)], axis=0)
        rows = [jnp.where(kind[r] == 2, NEG, jnp.where(kind[r] == 1, FORCE, imp[r * SUBLANES:(r + 1) * SUBLANES, :]))
                for r in range(blk_grp)]
        count_scr[...] = jnp.zeros(count_scr.shape, jnp.int32)
        for r_other in range(blk_grp):
            @pl.when(r_other * SUBLANES * SLC_BLOCK < (i + 1) * tq)
            def _(r_other=r_other, rows=rows):
                counts = [None] * blk_grp
                for s_other in range(SUBLANES):
                    row = jnp.broadcast_to(rows[r_other][s_other:s_other + 1, :], (SUBLANES, tq))
                    for r in range(blk_grp):
                        if r > r_other:
                            beats = jnp.where(row >= rows[r], 1, 0)
                        elif r < r_other:
                            beats = jnp.where(row > rows[r], 1, 0)
                        else:
                            beats = jnp.where(sub > s_other, jnp.where(row >= rows[r], 1, 0),
                                              jnp.where(row > rows[r], 1, 0))
                        counts[r] = beats if counts[r] is None else counts[r] + beats
                for r in range(blk_grp):
                    count_scr[r] = count_scr[r] + counts[r]
        masks.extend(jnp.where(count_scr[r] < TOPK, 0.0, NEG) for r in range(blk_grp))
    for c in range(tq // LANES):
        mb_ref[0, c * LANES:(c + 1) * LANES, :] = jnp.concatenate(
            [mk[:, c * LANES:(c + 1) * LANES] for mk in masks], axis=0).T.astype(BF16)


def _select(nsa_q, k_cmp, v_cmp, bias_c):
    b, s, width = nsa_q.shape
    n_rows = k_cmp.shape[1]
    n_blk = HEAD_DIM
    assert CMP_LEN == 2 * CMP_STRIDE and SLC_BLOCK == 4 * CMP_STRIDE
    tq = SEL_TQ
    vct = v_cmp[:, :, 0:LANES].transpose(0, 2, 1).astype(BF16)
    return pl.pallas_call(
        _select_kernel,
        grid=(s // tq, b),
        in_specs=[pl.BlockSpec((1, tq, width), lambda j, i: (i, j, 0)),
                  pl.BlockSpec((1, n_rows, 2 * LANES), lambda j, i: (i, 0, 0)),
                  pl.BlockSpec((1, LANES, n_rows), lambda j, i: (i, 0, 0)),
                  pl.BlockSpec((NSA_HEADS, 1, n_rows, tq), lambda j, i: (0, j, 0, 0))],
        out_specs=[pl.BlockSpec((1, tq, width), lambda j, i: (i, j, 0)),
                   pl.BlockSpec((1, tq, LANES), lambda j, i: (i, j, 0))],
        out_shape=[jax.ShapeDtypeStruct((b, s, width), BF16),
                   jax.ShapeDtypeStruct((b, s, LANES), BF16)],
        scratch_shapes=[pltpu.VMEM((n_blk // SUBLANES, SUBLANES, tq), jnp.int32),
                        pltpu.VMEM((tq // LANES, SUBLANES + n_rows, LANES), F32)],
        compiler_params=_params(2),
    )(nsa_q, k_cmp, vct, bias_c)


def _slc_kernel(q_ref, mb_ref, k_ref, e2_ref, vt_ref, bias_ref, o_ref, qs_scr, s_scr, tmax_scr, m_scr, acc_scr, *,
                n_near):
    t = FLASH_T
    i = pl.program_id(1)
    lo = _lane_lo((t, LANES))
    n_groups = NSA_HEADS // 2
    mb = mb_ref[0]
    for g in range(n_groups):
        qg = q_ref[0, :, g * LANES:(g + 1) * LANES]
        qs_scr[2 * g] = jnp.where(lo, qg, mb)
        qs_scr[2 * g + 1] = jnp.where(lo, mb, qg)
    _flash_init(m_scr, acc_scr)

    def qk_scores(j):
        start = pl.multiple_of(j * t, t)
        k_tile = k_ref[0, pl.ds(start, t), :]
        e_tile = e2_ref[pl.ds(start, t), :]
        k_sel = (jnp.where(lo, k_tile, e_tile), jnp.where(lo, e_tile, k_tile))
        return [functools.partial(lambda k, pos: _dot_nt(k, qs_scr[pos]), k_sel[pos % 2], pos)
                for pos in range(NSA_HEADS)]

    def bias_tile(pos, j):
        return bias_ref[pos, jnp.minimum(i - j, n_near)]

    def vt_slab(pos, j):
        kv = pos % 2
        return vt_ref[0, kv * VT_ROWS:(kv + 1) * VT_ROWS, pl.ds(pl.multiple_of(j * t, t), t)]

    _flash_pipeline(i, NSA_HEADS, qk_scores, bias_tile, vt_slab, s_scr, tmax_scr, m_scr, acc_scr)
    _flash_finish(o_ref, n_groups, acc_scr)


def _slc_attention(nsa_q, mask_bias, k2, e2, vt, bias):
    b, s, width = nsa_q.shape
    t = FLASH_T
    n_near = bias.shape[1] - 1
    return pl.pallas_call(
        functools.partial(_slc_kernel, n_near=n_near),
        grid=(b, s // t),
        in_specs=[pl.BlockSpec((1, t, width), lambda i, j: (i, j, 0)),
                  pl.BlockSpec((1, t, LANES), lambda i, j: (i, j, 0)),
                  pl.BlockSpec((1, s, LANES), lambda i, j: (i, 0, 0)),
                  pl.BlockSpec((s, LANES), lambda i, j: (0, 0)),
                  pl.BlockSpec((1, 2 * VT_ROWS, s), lambda i, j: (i, VT_SLC_BLOCK, 0)),
                  pl.BlockSpec(bias.shape, lambda i, j: (0, 0, 0, 0))],
        out_specs=pl.BlockSpec((1, t, width), lambda i, j: (i, j, 0)),
        out_shape=jax.ShapeDtypeStruct((b, s, width), BF16),
        scratch_shapes=[pltpu.VMEM((NSA_HEADS, t, LANES), BF16),
                        pltpu.VMEM((2, NSA_HEADS, t, t), F32),
                        pltpu.VMEM((2, NSA_HEADS, SUBLANES, t), F32),
                        pltpu.VMEM((NSA_HEADS, SUBLANES, t), F32),
                        pltpu.VMEM((NSA_HEADS, VT_ROWS, t), F32)],
        compiler_params=_params(2),
    )(nsa_q, mask_bias, k2, e2, vt, bias)


def _block_onehot(s_len):
    blk = np.arange(s_len)[:, None] // SLC_BLOCK
    lane = np.arange(LANES)[None, :] % HEAD_DIM
    return jnp.asarray((blk == lane).astype(np.float32), dtype=BF16)


def _mix_ffn_kernel(x_ref, mod_ref, swa_ref, fox_ref, cmp_ref, slc_ref, win_ref, misc_ref, expand_ref, gn_ref,
                    w_ref, post_ref, pre_ref, wg_ref, wu_ref, wd_ref, fpost_ref, o_ref):
    n_swa = SWA_HEADS * HEAD_DIM
    n_fox = FOX_HEADS * HEAD_DIM
    n_nsa = NSA_HEADS * HEAD_DIM
    gate = jax.nn.sigmoid(misc_ref[0])
    gate_hi = gate.astype(BF16)
    gate_lo = (gate - gate_hi.astype(F32)).astype(BF16)
    gates = (jnp.dot(gate_hi, expand_ref[...], preferred_element_type=F32)
             + jnp.dot(gate_lo, expand_ref[...], preferred_element_type=F32))
    o_nsa = (gates[:, 0:n_nsa] * cmp_ref[0].astype(F32) + gates[:, n_nsa:2 * n_nsa] * slc_ref[0].astype(F32)
             + gates[:, 2 * n_nsa:3 * n_nsa] * win_ref[0].astype(F32))
    a = _rms(swa_ref[0].astype(F32), gn_ref[:, 0:n_swa]).astype(BF16)
    b = _rms(fox_ref[0].astype(F32), gn_ref[:, n_swa:n_swa + n_fox]).astype(BF16)
    c = _rms(o_nsa, gn_ref[:, n_swa + n_fox:]).astype(BF16)
    y = (jnp.dot(a, w_ref[0:n_swa, :], preferred_element_type=F32)
         + jnp.dot(b, w_ref[n_swa:n_swa + n_fox, :], preferred_element_type=F32)
         + jnp.dot(c, w_ref[n_swa + n_fox:, :], preferred_element_type=F32))
    x = x_ref[0] + mod_ref[0, 2:3, :] * _rms(y, post_ref[...])

    h = (_rms(x, pre_ref[...]) * (1.0 + mod_ref[0, 4:5, :]) + mod_ref[0, 3:4, :]).astype(BF16)
    y = jnp.zeros(x.shape, F32)
    for chunk in range(wg_ref.shape[1] // FFN_CHUNK):
        cols = slice(chunk * FFN_CHUNK, (chunk + 1) * FFN_CHUNK)
        gate = jnp.dot(h, wg_ref[:, cols], preferred_element_type=F32)
        up = jnp.dot(h, wu_ref[:, cols], preferred_element_type=F32)
        act = (gate * jax.nn.sigmoid(gate) * up).astype(BF16)
        y = y + jnp.dot(act, wd_ref[cols, :], preferred_element_type=F32)
    o_ref[0] = x + mod_ref[0, 5:6, :] * _rms(y, fpost_ref[...])


def _gate_expansion():
    expand = np.zeros((LANES, 3 * NSA_HEADS * HEAD_DIM), np.float32)
    for branch in range(3):
        for p in range(NSA_HEADS):
            col = (branch * NSA_HEADS + p) * HEAD_DIM
            expand[GATE_LANE + 8 * branch + p, col:col + HEAD_DIM] = 1.0
    return jnp.asarray(expand, dtype=BF16)


def _mix_ffn(x, mod, o_swa, o_fox, o_cmp, o_slc, o_win, misc, gn, w, post, pre, wg, wu, wd, fpost, layer):
    b, s, d = x.shape
    expand = _gate_expansion()
    hidden = wg.shape[2]

    def rows(width):
        return pl.BlockSpec((1, ROW_TILE, width), lambda i, j: (i, j, 0))

    def whole(shape):
        return pl.BlockSpec(shape, lambda i, j: (0,) * len(shape), pipeline_mode=pl.Buffered(1))

    def of_layer(shape):
        return pl.BlockSpec((None,) + shape, lambda i, j: (layer,) + (0,) * len(shape),
                            pipeline_mode=pl.Buffered(1))

    vec = pl.BlockSpec((None, 1, d), lambda i, j: (layer, 0, 0))
    return pl.pallas_call(
        _mix_ffn_kernel,
        grid=(b, s // ROW_TILE),
        in_specs=[rows(d),
                  pl.BlockSpec((None, 1, ADA_CHUNKS, d), lambda i, j: (layer, i, 0, 0)),
                  rows(o_swa.shape[2]), rows(o_fox.shape[2]), rows(o_cmp.shape[2]), rows(o_slc.shape[2]),
                  rows(o_win.shape[2]), rows(LANES),
                  whole(expand.shape), vec, of_layer((d, d)), vec, vec,
                  of_layer((d, hidden)), of_layer((d, hidden)), of_layer((hidden, d)),
                  vec],
        out_specs=rows(d),
        out_shape=jax.ShapeDtypeStruct((b, s, d), F32),
        compiler_params=_params(2),
    )(x, mod, o_swa, o_fox, o_cmp, o_slc, o_win, misc, expand, gn, w, post, pre, wg, wu, wd, fpost)


def _forget_lanes():
    lanes, heads = [], []
    for h in range(FOX_HEADS):
        base = (h // 2) * LANES + (HEAD_DIM if h % 2 == 0 else 0)
        for j in range(KEY_BIAS_TERMS):
            lanes.append(base + j)
            heads.append(h)
    return np.array(lanes), np.array(heads)


def _in_proj_layout():
    d = HEAD_DIM
    o_qa, o_ka, o_va, o_qb, o_kb, o_vb, o_fb, o_qc = 0, 256, 384, 512, 768, 1024, 1280, 1284
    o_kc, o_vc, o_ksl, o_vsl, o_kw, o_vw, o_gc = 1796, 1924, 2052, 2180, 2308, 2436, 2564
    scale = LOG2E / math.sqrt(d)

    def head_cols(base, heads):
        return np.concatenate([np.arange(base + h * d, base + (h + 1) * d) for h in heads])

    def span(base, width):
        return np.arange(base, base + width)

    cols = [head_cols(o_qa, SWA_POS), span(o_ka, 128),
            span(o_qb, 256), span(o_kb, 256),
            head_cols(o_qc, NSA_POS),
            span(o_kc, 128), span(o_vc, 128),
            span(o_ksl, 128), span(o_kw, 128),
            span(o_vb, 256), span(o_va, 128), span(o_vsl, 128), span(o_vw, 128)]
    scales = [np.full(256, scale), np.ones(128), np.full(256, scale), np.ones(256), np.full(512, scale),
              np.ones(256), np.ones(256), np.ones(640)]
    lanes, heads = _forget_lanes()
    misc_cols = np.zeros(SEG_MISC[1] - SEG_MISC[0], np.int64)
    misc_scale = np.zeros(SEG_MISC[1] - SEG_MISC[0])
    misc_cols[lanes] = o_fb + heads
    misc_scale[lanes] = 1.0
    for branch in range(3):
        for p, h in enumerate(NSA_POS):
            misc_cols[GATE_LANE + 8 * branch + p] = o_gc + h * 3 + branch
            misc_scale[GATE_LANE + 8 * branch + p] = 1.0
    cols.append(misc_cols)
    scales.append(misc_scale)
    return np.concatenate(cols), np.concatenate(scales).astype(np.float32)


def _head_perm(pos):
    return np.concatenate([np.arange(h * HEAD_DIM, (h + 1) * HEAD_DIM) for h in pos])


def kernel(x, c, rel_bias, ada_w, ada_b, attn_pre_norm, attn_post_norm, ffn_pre_norm, ffn_post_norm, w_in,
           forget_bias, swa_sinks, cmp_pos, cmp_w1, cmp_w2, group_norm, w_out, ffn_w_gate, ffn_w_up, ffn_w_down):
    b, s, d = x.shape
    depth = w_in.shape[0]
    hidden = ffn_w_gate.shape[2]
    assert s % (2 * FLASH_T) == 0 and s // SLC_BLOCK <= HEAD_DIM and hidden % FFN_CHUNK == 0

    cols, scales = _in_proj_layout()
    n_main = SEG_MISC[0]
    breaks = np.flatnonzero(np.diff(cols[:n_main]) != 1) + 1
    runs = np.split(cols[:n_main], breaks)
    w_main = (jnp.concatenate([w_in[:, :, int(r[0]):int(r[-1]) + 1] for r in runs], axis=2)
              * scales[:n_main]).astype(BF16)
    used = np.flatnonzero(scales[n_main:] != 0)
    w_misc = jnp.zeros((depth, d, SEG_MISC[1] - SEG_MISC[0]), w_in.dtype).at[:, :, used].set(
        w_in[:, :, cols[n_main:][used]]).astype(BF16)
    lanes, heads = _forget_lanes()
    fbias_all = jnp.zeros((depth, 1, SEG_MISC[1] - SEG_MISC[0]), F32).at[:, 0, lanes].set(
        forget_bias[:, heads].astype(F32))
    swa_perm = _head_perm(SWA_POS)
    nsa_perm = _head_perm(NSA_POS)
    n_swa, n_fox = SWA_HEADS * HEAD_DIM, FOX_HEADS * HEAD_DIM
    mix_perm = np.concatenate([swa_perm, n_swa + np.arange(n_fox), n_swa + n_fox + nsa_perm])
    w_out_all = w_out[:, mix_perm, :].astype(BF16)
    wg_all = ffn_w_gate.astype(BF16)
    wu_all = ffn_w_up.astype(BF16)
    wd_all = ffn_w_down.astype(BF16)
    cmp_weights = _compress_weights(cmp_pos, cmp_w1, cmp_w2)

    def stacked(v):
        return v.astype(F32).reshape(depth, 1, v.shape[1])

    gn_all = stacked(group_norm[:, mix_perm])
    attn_pre, attn_post = stacked(attn_pre_norm), stacked(attn_post_norm)
    ffn_pre, ffn_post = stacked(ffn_pre_norm), stacked(ffn_post_norm)
    sinks_all = swa_sinks[:, np.array(SWA_POS)].astype(F32)

    tab_swa = rel_bias[:, np.array(SWA_POS)].astype(F32)
    tab_nsa = rel_bias[:, SWA_HEADS + np.array(NSA_POS)].astype(F32)
    bias_swa = _bias_table(tab_swa, _band_buckets_t(SWA_TILE, SWA_WINDOW))
    bias_win = _bias_table(tab_nsa, _band_buckets_t(WIN_TILE, NSA_WINDOW))
    bias_slc = _bias_table(tab_nsa, _toeplitz_buckets_t(FLASH_T, _near_tiles(FLASH_T)), subtract_last=True)
    n_rows = s // CMP_STRIDE
    bias_cmp = _bias_table(tab_nsa, _cmp_buckets_t(s, n_rows))
    e2 = _block_onehot(s)

    mod_all = _adaln(c.astype(F32), ada_w.astype(F32), ada_b.astype(F32)).reshape(depth, b, ADA_CHUNKS, d)

    for layer in range(depth):
        swa_qk, fox_qk, nsa_q, kc, vc, k2, misc, vt = _in_proj(x, mod_all, attn_pre, w_main, w_misc, layer)
        o_swa = _banded_attention(swa_qk, swa_qk, 2, vt, VT_SWA_BLOCK, bias_swa, sinks=sinks_all, layer=layer)
        o_fox = _fox_attention(fox_qk, _fox_key_terms(misc, fbias_all, layer), vt)
        k_cmp = _compress(kc, cmp_weights, layer, 0)
        v_cmp = _compress(vc, cmp_weights, layer, 1)
        o_cmp, mask_bias = _select(nsa_q, k_cmp, v_cmp, bias_cmp)
        o_slc = _slc_attention(nsa_q, mask_bias, k2, e2, vt, bias_slc)
        o_win = _banded_attention(nsa_q, k2, 1, vt, VT_WIN_BLOCK, bias_win)
        x = _mix_ffn(x, mod_all, o_swa, o_fox, o_cmp, o_slc, o_win, misc, gn_all, w_out_all, attn_post, ffn_pre,
                     wg_all, wu_all, wd_all, ffn_post, layer)
    return x
```

```python
import functools
import math

import numpy as np
import jax
import jax.numpy as jnp
from jax import lax
from jax.experimental import pallas as pl
from jax.experimental.pallas import tpu as pltpu

F32 = jnp.float32
BF16 = jnp.bfloat16
HIGHEST = lax.Precision.HIGHEST

LANES = 128
SUBLANES = 8
VMEM_LIMIT = 56 * 1024 * 1024

HEAD_DIM = 64
SWA_HEADS = 4
SWA_WINDOW = 128
FOX_HEADS = 4
NSA_HEADS = 8
CMP_LEN = 32
CMP_STRIDE = 16
CMP_HIDDEN = 2 * HEAD_DIM
SLC_BLOCK = 64
TOPK = 16
NSA_WINDOW = 512
REL_BUCKETS = 32
REL_MAX_DISTANCE = 1024
ZERO_BUCKET = -2
RMS_EPS = 1e-6
NEG = -1e30
FORCE = 1e30
ADA_CHUNKS = 6
LOG2E = math.log2(math.e)

SWA_POS = (0, 2, 1, 3)
NSA_POS = (0, 4, 1, 5, 2, 6, 3, 7)

SWA_TILE = 256
WIN_TILE = 256
FLASH_T = 256
FOX_TQ = 512
SLC_TQ = 512
SEL_TQ = 256
IN_ROW_TILE = 1024
ROW_TILE = 512
FFN_CHUNK = 256
VT_ROWS = HEAD_DIM + 16
KEY_BIAS_TERMS = 3

SEG_SWA = (0, 384)
SEG_FOX = (384, 896)
SEG_NSAQ = (896, 1408)
SEG_KC = (1408, 1536)
SEG_VC = (1536, 1664)
SEG_K2 = (1664, 1920)
SEG_V = (1920, 2560)
SEG_MISC = (2560, 2816)
GATE_LANE = 8
VT_FOX_BLOCK, VT_SWA_BLOCK, VT_SLC_BLOCK, VT_WIN_BLOCK = 0, 2, 3, 4


def _params(n_grid, vmem=VMEM_LIMIT):
    return pltpu.CompilerParams(dimension_semantics=("parallel",) * n_grid, vmem_limit_bytes=vmem)


def _dot_nt(a, b):
    return lax.dot_general(a, b, (((1,), (1,)), ((), ())), preferred_element_type=F32)


def _lane_lo(shape):
    return lax.broadcasted_iota(jnp.int32, shape, len(shape) - 1) < HEAD_DIM


def _adaln_kernel(c_ref, w_ref, b_ref, o_ref):
    c = c_ref[...]
    act = c * jax.nn.sigmoid(c)
    o_ref[0] = jnp.dot(act, w_ref[0], precision=HIGHEST, preferred_element_type=F32) + b_ref[0]


def _adaln(c, ada_w, ada_b):
    depth, d, n = ada_w.shape
    b = c.shape[0]
    return pl.pallas_call(
        _adaln_kernel,
        grid=(depth, n // d),
        in_specs=[pl.BlockSpec((b, d), lambda l, j: (0, 0)),
                  pl.BlockSpec((1, d, d), lambda l, j: (l, 0, j)),
                  pl.BlockSpec((1, 1, d), lambda l, j: (l, 0, j))],
        out_specs=pl.BlockSpec((1, b, d), lambda l, j: (l, 0, j)),
        out_shape=jax.ShapeDtypeStruct((depth, b, n), F32),
        compiler_params=_params(2),
    )(c, ada_w, ada_b.reshape(depth, 1, n))


def _t5_bucket(dist):
    n = jnp.maximum(dist, 0)
    max_exact = REL_BUCKETS // 2
    nf = jnp.maximum(n, 1).astype(jnp.float32)
    large = max_exact + (jnp.log(nf / max_exact) / math.log(REL_MAX_DISTANCE / max_exact)
                         * (REL_BUCKETS - max_exact)).astype(jnp.int32)
    large = jnp.minimum(large, REL_BUCKETS - 1)
    return jnp.where(n < max_exact, n, large)


def _bias_table_kernel(tab_ref, bucket_ref, o_ref, *, subtract_last):
    n_heads = o_ref.shape[0]
    values = [[(tab_ref[k, h] - (tab_ref[REL_BUCKETS - 1, h] if subtract_last else 0.0)) * LOG2E
               for h in range(n_heads)] for k in range(REL_BUCKETS)]

    def rows(chunk, carry):
        r0 = pl.multiple_of(chunk * SUBLANES, SUBLANES)
        bucket = bucket_ref[0, pl.ds(r0, SUBLANES), :]
        accs = [jnp.where(bucket == ZERO_BUCKET, 0.0, NEG) for _ in range(n_heads)]
        for k in range(REL_BUCKETS):
            hit = bucket == k
            for h in range(n_heads):
                accs[h] = jnp.where(hit, values[k][h], accs[h])
        for h in range(n_heads):
            o_ref[h, 0, pl.ds(r0, SUBLANES), :] = accs[h]
        return carry

    lax.fori_loop(0, bucket_ref.shape[1] // SUBLANES, rows, 0)


def _bias_table(table, bucket, subtract_last=False):
    n_heads = table.shape[1]
    n, r, c = bucket.shape
    return pl.pallas_call(
        functools.partial(_bias_table_kernel, subtract_last=subtract_last),
        grid=(n,),
        in_specs=[pl.BlockSpec(memory_space=pltpu.SMEM),
                  pl.BlockSpec((1, r, c), lambda i: (i, 0, 0))],
        out_specs=pl.BlockSpec((n_heads, 1, r, c), lambda i: (0, i, 0, 0)),
        out_shape=jax.ShapeDtypeStruct((n_heads, n, r, c), F32),
        compiler_params=_params(1),
    )(table, bucket)


def _band_buckets_t(tile, window):
    n_back = -(-(window - 1) // tile)
    t = jnp.arange(n_back + 1)[:, None, None]
    key = jnp.arange(tile)[None, :, None]
    query = jnp.arange(tile)[None, None, :]
    dist = query + (n_back - t) * tile - key
    return jnp.where((dist >= 0) & (dist < window), _t5_bucket(dist), -1).astype(jnp.int32)


def _toeplitz_buckets_t(tile, n_tiles):
    m = jnp.arange(n_tiles)[:, None, None]
    key = jnp.arange(tile)[None, :, None]
    query = jnp.arange(tile)[None, None, :]
    dist = m * tile + query - key
    near = jnp.where(dist >= 0, _t5_bucket(dist), -1).astype(jnp.int32)
    return jnp.concatenate([near, jnp.full((1, tile, tile), ZERO_BUCKET, jnp.int32),
                            jnp.full((1, tile, tile), -1, jnp.int32)])


def _cmp_buckets_t(s_len, n_rows):
    n_c = n_rows - 1
    tile = jnp.arange(s_len // SEL_TQ)[:, None, None]
    n = jnp.arange(n_rows)[None, :, None]
    t = tile * SEL_TQ + jnp.arange(SEL_TQ)[None, None, :]
    dist = t - (n * CMP_STRIDE + CMP_LEN - 1)
    return jnp.where((dist >= 0) & (n < n_c), _t5_bucket(dist), -1).astype(jnp.int32)


def _near_tiles(tile):
    max_exact = REL_BUCKETS // 2
    first_const = math.ceil(max_exact * (REL_MAX_DISTANCE / max_exact) ** ((max_exact - 1) / max_exact)) + 1
    m = 1
    while m * tile - (tile - 1) < first_const:
        m += 1
    return m


def _rms(x, gain):
    return x * lax.rsqrt(jnp.mean(x * x, axis=-1, keepdims=True) + RMS_EPS) * gain


def _in_proj_kernel(x_ref, mod_ref, gain_ref, w_ref, wm_ref, swa_ref, fox_ref, nsaq_ref, kc_ref, vc_ref, k2_ref,
                    misc_ref, vt_ref, pack_scr):
    x = x_ref[0]
    h = _rms(x, gain_ref[...]) * (1.0 + mod_ref[0, 1:2, :]) + mod_ref[0, 0:1, :]
    hb = h.astype(BF16)

    def seg(bounds):
        if bounds == SEG_MISC:
            return jnp.dot(hb, wm_ref[...], preferred_element_type=F32)
        return jnp.dot(hb, w_ref[:, bounds[0]:bounds[1]], preferred_element_type=F32)

    swa_ref[0] = seg(SEG_SWA).astype(BF16)
    fox_ref[0] = seg(SEG_FOX).astype(BF16)
    nsaq_ref[0] = seg(SEG_NSAQ).astype(BF16)
    for slot, (bounds, out_ref) in enumerate(((SEG_KC, kc_ref), (SEG_VC, vc_ref))):
        pack_scr[slot] = seg(bounds)
        for tok in range(CMP_STRIDE):
            out_ref[0, :, tok * LANES:(tok + 1) * LANES] = pack_scr[
                slot, pl.ds(tok, x.shape[0] // CMP_STRIDE, stride=CMP_STRIDE), :].astype(BF16)
    k2_ref[0] = seg(SEG_K2).astype(BF16)
    misc_ref[0] = seg(SEG_MISC)

    rows = x.shape[0]
    extra_row = lax.broadcasted_iota(jnp.int32, (VT_ROWS - HEAD_DIM, rows), 0)
    extra = jnp.where(extra_row == 0, 1.0, 0.0).astype(BF16)
    values = seg(SEG_V)
    for c in range(values.shape[1] // LANES):
        vt = values[:, c * LANES:(c + 1) * LANES].T.astype(BF16)
        for half in range(2):
            base = (2 * c + half) * VT_ROWS
            vt_ref[0, base:base + HEAD_DIM, :] = vt[half * HEAD_DIM:(half + 1) * HEAD_DIM, :]
            vt_ref[0, base + HEAD_DIM:base + VT_ROWS, :] = extra


def _in_proj(x, mod, gain, w, w_misc, layer):
    b, s, d = x.shape
    n = w.shape[2]

    tile = IN_ROW_TILE

    def rows(width, dtype):
        return (pl.BlockSpec((1, tile, width), lambda i, j: (i, j, 0)), jax.ShapeDtypeStruct((b, s, width), dtype))

    packed = (pl.BlockSpec((1, tile // CMP_STRIDE, CMP_STRIDE * LANES), lambda i, j: (i, j, 0)),
              jax.ShapeDtypeStruct((b, s // CMP_STRIDE, CMP_STRIDE * LANES), BF16))
    vt_rows = (SEG_V[1] - SEG_V[0]) // HEAD_DIM * VT_ROWS
    outs = [rows(SEG_SWA[1] - SEG_SWA[0], BF16), rows(SEG_FOX[1] - SEG_FOX[0], BF16),
            rows(SEG_NSAQ[1] - SEG_NSAQ[0], BF16), packed, packed, rows(SEG_K2[1] - SEG_K2[0], BF16),
            rows(SEG_MISC[1] - SEG_MISC[0], F32),
            (pl.BlockSpec((1, vt_rows, tile), lambda i, j: (i, 0, j)),
             jax.ShapeDtypeStruct((b, vt_rows, s), BF16))]
    return pl.pallas_call(
        _in_proj_kernel,
        grid=(b, s // tile),
        in_specs=[pl.BlockSpec((1, tile, d), lambda i, j: (i, j, 0)),
                  pl.BlockSpec((None, 1, ADA_CHUNKS, d), lambda i, j: (layer, i, 0, 0)),
                  pl.BlockSpec((None, 1, d), lambda i, j: (layer, 0, 0)),
                  pl.BlockSpec((None, d, n), lambda i, j: (layer, 0, 0)),
                  pl.BlockSpec((None, d, w_misc.shape[2]), lambda i, j: (layer, 0, 0))],
        out_specs=[spec for spec, _ in outs],
        out_shape=[shape for _, shape in outs],
        scratch_shapes=[pltpu.VMEM((2, tile, LANES), F32)],
        compiler_params=_params(2),
    )(x, mod, gain, w, w_misc)


def _banded_kernel(*refs, n_back, n_groups, has_sink, t, layer):
    if has_sink:
        sink_ref, q_ref, k_ref, vt_ref, bias_ref, o_ref = refs
    else:
        q_ref, k_ref, vt_ref, bias_ref, o_ref = refs
    i = pl.program_id(1)
    lo = _lane_lo((t, LANES))
    n_tiles = n_back + 1

    def run(all_valid):
        starts = [pl.multiple_of(jnp.maximum(i - n_back + tt, 0) * t, t) for tt in range(n_tiles)]
        k_tiles = [k_ref[0, pl.ds(start, t), :] for start in starts]

        def scores(g):
            qg = q_ref[0, :, g * LANES:(g + 1) * LANES]
            zero = jnp.zeros_like(qg)
            qms = (jnp.where(lo, qg, zero), jnp.where(lo, zero, qg))
            return [[bias_ref[2 * g + half, tt] + _dot_nt(k_tiles[tt], qms[half]) for tt in range(n_tiles)]
                    for half in range(2)]

        def softmax_pv(g, sts):
            pair = []
            for half in range(2):
                tiles = sts[half]
                if not all_valid:
                    tiles = [jnp.where(i - n_back + tt >= 0, st, NEG) if tt < n_back else st
                             for tt, st in enumerate(tiles)]
                m = None
                for st in tiles:
                    part = st.reshape(t // SUBLANES, SUBLANES, t).max(axis=0)
                    m = part if m is None else jnp.maximum(m, part)
                m = _all_sublanes(m, jnp.maximum)
                if has_sink:
                    sink = sink_ref[layer, 2 * g + half] * LOG2E
                    m = jnp.maximum(m, sink)
                acc = None
                for tt, st in enumerate(tiles):
                    p = jnp.exp2((st.reshape(t // SUBLANES, SUBLANES, t) - m[None]).reshape(t, t).astype(BF16))
                    part = jnp.dot(vt_ref[0, half * VT_ROWS:(half + 1) * VT_ROWS, pl.ds(starts[tt], t)], p,
                                   preferred_element_type=F32)
                    acc = part if acc is None else acc + part
                denom = _all_sublanes(acc[HEAD_DIM:HEAD_DIM + SUBLANES, :], jnp.add)
                if has_sink:
                    denom = denom + jnp.exp2(sink - m)
                out = acc[0:HEAD_DIM, :].reshape(HEAD_DIM // SUBLANES, SUBLANES, t) / denom[None]
                pair.append(out.reshape(HEAD_DIM, t))
            o_ref[0, :, g * LANES:(g + 1) * LANES] = jnp.concatenate(pair, axis=0).T.astype(o_ref.dtype)

        pending = scores(0)
        for g in range(n_groups):
            current = pending
            if g + 1 < n_groups:
                pending = scores(g + 1)
            softmax_pv(g, current)

    @pl.when(i >= n_back)
    def _():
        run(True)

    @pl.when(i < n_back)
    def _():
        run(False)


def _banded_attention(q_arr, k_arr, k_blk, vt, vt_blk, bias, sinks=None, layer=0):
    b, s, _ = q_arr.shape
    n_pos, n_tiles, t = bias.shape[0], bias.shape[1], bias.shape[2]
    width = n_pos * HEAD_DIM
    in_specs = [pl.BlockSpec((1, t, width), lambda i, j: (i, j, 0)),
                pl.BlockSpec((1, s, LANES), lambda i, j: (i, 0, k_blk)),
                pl.BlockSpec((1, 2 * VT_ROWS, s), lambda i, j: (i, vt_blk, 0)),
                pl.BlockSpec(bias.shape, lambda i, j: (0, 0, 0, 0))]
    args = [q_arr, k_arr, vt, bias]
    if sinks is not None:
        in_specs = [pl.BlockSpec(memory_space=pltpu.SMEM)] + in_specs
        args = [sinks] + args
    return pl.pallas_call(
        functools.partial(_banded_kernel, n_back=n_tiles - 1, n_groups=n_pos // 2, has_sink=sinks is not None, t=t,
                          layer=layer),
        grid=(b, s // t),
        in_specs=in_specs,
        out_specs=pl.BlockSpec((1, t, width), lambda i, j: (i, j, 0)),
        out_shape=jax.ShapeDtypeStruct((b, s, width), BF16),
        compiler_params=_params(2),
    )(*args)


def _all_sublanes(x, op):
    for shift in (4, 2, 1):
        x = op(x, pltpu.roll(x, shift, 0))
    return x


def _flash_init(m_scr, acc_scr):
    m_scr[...] = jnp.full(m_scr.shape, NEG, F32)
    acc_scr[...] = jnp.zeros(acc_scr.shape, F32)


def _flash_update(h, st_ref, tile_max, vt_h, m_scr, acc_scr):
    tk, tq = st_ref.shape
    m_prev = m_scr[h]
    m_new = _all_sublanes(jnp.maximum(m_prev, tile_max), jnp.maximum)
    alpha = jnp.exp2(m_prev - m_new)
    p = jnp.exp2((st_ref[...].reshape(tk // SUBLANES, SUBLANES, tq) - m_new[None]).reshape(tk, tq).astype(BF16))
    acc = acc_scr[h].reshape(VT_ROWS // SUBLANES, SUBLANES, tq) * alpha[None]
    acc_scr[h] = acc.reshape(VT_ROWS, tq) + jnp.dot(vt_h, p, preferred_element_type=F32)
    m_scr[h] = m_new


def _flash_finish(o_ref, n_groups, acc_scr):
    tq = acc_scr.shape[2]
    for g in range(n_groups):
        pair = []
        for h in (2 * g, 2 * g + 1):
            denom = _all_sublanes(acc_scr[h, HEAD_DIM:HEAD_DIM + SUBLANES, :], jnp.add)
            out = acc_scr[h, 0:HEAD_DIM, :].reshape(HEAD_DIM // SUBLANES, SUBLANES, tq) / denom[None]
            pair.append(out.reshape(HEAD_DIM, tq))
        o_ref[0, :, g * LANES:(g + 1) * LANES] = jnp.concatenate(pair, axis=0).T.astype(o_ref.dtype)


def _flash_pipeline(i, n_heads, qk_scores, bias_tile, vt_slab, s_scr, tmax_scr, m_scr, acc_scr):
    def qk_head(thunk, h, j, slot):
        st = thunk()
        bias = bias_tile(h, j)
        if isinstance(bias, tuple):
            width = st.shape[1] // len(bias)
            st = jnp.concatenate([st[:, c * width:(c + 1) * width] + part for c, part in enumerate(bias)], axis=1)
        else:
            st = st + bias
        s_scr[slot, h] = st
        tmax_scr[slot, h] = st.reshape(st.shape[0] // SUBLANES, SUBLANES, st.shape[1]).max(axis=0)

    def softmax_head(h, j, slot):
        _flash_update(h, s_scr.at[slot, h], tmax_scr[slot, h], vt_slab(h, j), m_scr, acc_scr)

    def softmax_all(j, slot):
        for h in range(n_heads):
            softmax_head(h, j, slot)

    def stage(j_qk, slot_qk, j_sm, slot_sm):
        thunks = qk_scores(j_qk)
        for h in range(n_heads):
            qk_head(thunks[h], h, j_qk, slot_qk)
            softmax_head(h, j_sm, slot_sm)

    for h, thunk in enumerate(qk_scores(0)):
        qk_head(thunk, h, 0, 0)

    def body(trip, carry):
        j = 2 * trip
        stage(j + 1, 1, j, 0)
        stage(j + 2, 0, j + 1, 1)
        return carry

    lax.fori_loop(0, i // 2, body, 0)
    last = 2 * (i // 2)

    @pl.when(i % 2 == 0)
    def _():
        softmax_all(last, 0)

    @pl.when(i % 2 == 1)
    def _():
        stage(last + 1, 1, last, 0)
        softmax_all(last + 1, 1)


def _fox_aug_kernel(misc_ref, fbias_ref, tri_ref, o_ref):
    s_len, width = misc_ref.shape[1], misc_ref.shape[2]
    term = lax.broadcasted_iota(jnp.int32, (LANES, width), 1) % HEAD_DIM
    carry = jnp.zeros((1, width), F32)
    for c in range(s_len // LANES):
        z = misc_ref[0, c * LANES:(c + 1) * LANES, :] + fbias_ref[...]
        log_f = jnp.minimum(z, 0.0) - jnp.log1p(jnp.exp(-jnp.abs(z)))
        cum = jnp.dot(tri_ref[...], log_f, precision=HIGHEST, preferred_element_type=F32) + carry
        carry = cum[LANES - 1:LANES, :]
        x = cum * (-LOG2E)
        hi = x.astype(BF16).astype(F32)
        rest = x - hi
        mid = rest.astype(BF16).astype(F32)
        low = rest - mid
        out = jnp.where(term == 0, hi, jnp.where(term == 1, mid, jnp.where(term == 2, low, 0.0)))
        o_ref[0, c * LANES:(c + 1) * LANES, :] = out.astype(BF16)


def _fox_key_terms(misc, fbias, layer):
    b, s, width = misc.shape
    tri = jnp.asarray(np.tril(np.ones((LANES, LANES), np.float32)))
    return pl.pallas_call(
        _fox_aug_kernel,
        grid=(b,),
        in_specs=[pl.BlockSpec((1, s, width), lambda i: (i, 0, 0)),
                  pl.BlockSpec((None, 1, width), lambda i: (layer, 0, 0)),
                  pl.BlockSpec((LANES, LANES), lambda i: (0, 0))],
        out_specs=pl.BlockSpec((1, s, width), lambda i: (i, 0, 0)),
        out_shape=jax.ShapeDtypeStruct((b, s, width), BF16),
        compiler_params=_params(1),
    )(misc, fbias, tri)


def _fox_kernel(q_ref, k_ref, aug_ref, vt_ref, mask_ref, o_ref, qs_scr, s_scr, tmax_scr, m_scr, acc_scr):
    t = FLASH_T
    tq = q_ref.shape[1]
    ratio = tq // t
    i = pl.program_id(1)
    lo = _lane_lo((t, LANES))
    lo_q = _lane_lo((tq, LANES))
    lane = lax.broadcasted_iota(jnp.int32, (tq, LANES), 1)
    ones = jnp.where(lane % HEAD_DIM < KEY_BIAS_TERMS, 1.0, 0.0).astype(BF16)
    n_groups = FOX_HEADS // 2
    for g in range(n_groups):
        qg = q_ref[0, :, g * LANES:(g + 1) * LANES]
        qs_scr[2 * g] = jnp.where(lo_q, qg, ones)
        qs_scr[2 * g + 1] = jnp.where(lo_q, ones, qg)
    _flash_init(m_scr, acc_scr)

    def qk_scores(j):
        start = pl.multiple_of(j * t, t)
        scores = []
        for g in range(n_groups):
            k_tile = k_ref[0, pl.ds(start, t), g * LANES:(g + 1) * LANES]
            a_tile = aug_ref[0, pl.ds(start, t), g * LANES:(g + 1) * LANES]
            k_sel = (jnp.where(lo, k_tile, a_tile), jnp.where(lo, a_tile, k_tile))
            for half in range(2):
                scores.append(functools.partial(lambda k, h: _dot_nt(k, qs_scr[h]), k_sel[half], 2 * g + half))
        return scores

    def bias_tile(h, j):
        return mask_ref[jnp.clip(j - ratio * i + 1, 0, ratio)]

    def vt_slab(h, j):
        return vt_ref[0, h * VT_ROWS:(h + 1) * VT_ROWS, pl.ds(pl.multiple_of(j * t, t), t)]

    _flash_pipeline(ratio * i + ratio - 1, FOX_HEADS, qk_scores, bias_tile, vt_slab, s_scr, tmax_scr, m_scr, acc_scr)
    _flash_finish(o_ref, n_groups, acc_scr)


def _fox_attention(fox_qk, key_terms, vt):
    b, s, _ = fox_qk.shape
    width = FOX_HEADS * HEAD_DIM
    t, tq = FLASH_T, FOX_TQ
    ratio = tq // t
    key = np.arange(t)[:, None]
    query = np.arange(tq)[None, :]
    masks = [np.zeros((t, tq))] + [np.where(r * t + key <= query, 0.0, NEG) for r in range(ratio)]
    masks = jnp.asarray(np.stack(masks).astype(np.float32))
    return pl.pallas_call(
        _fox_kernel,
        grid=(b, s // tq),
        in_specs=[pl.BlockSpec((1, tq, width), lambda i, j: (i, j, 0)),
                  pl.BlockSpec((1, s, width), lambda i, j: (i, 0, 1)),
                  pl.BlockSpec((1, s, width), lambda i, j: (i, 0, 0)),
                  pl.BlockSpec((1, FOX_HEADS * VT_ROWS, s), lambda i, j: (i, VT_FOX_BLOCK, 0)),
                  pl.BlockSpec(masks.shape, lambda i, j: (0, 0, 0))],
        out_specs=pl.BlockSpec((1, tq, width), lambda i, j: (i, j, 0)),
        out_shape=jax.ShapeDtypeStruct((b, s, width), BF16),
        scratch_shapes=[pltpu.VMEM((FOX_HEADS, tq, LANES), BF16),
                        pltpu.VMEM((2, FOX_HEADS, t, tq), F32),
                        pltpu.VMEM((2, FOX_HEADS, SUBLANES, tq), F32),
                        pltpu.VMEM((FOX_HEADS, SUBLANES, tq), F32),
                        pltpu.VMEM((FOX_HEADS, VT_ROWS, tq), F32)],
        compiler_params=_params(2),
    )(fox_qk, fox_qk, key_terms, vt, masks)


def _split3(x):
    hi = x.astype(BF16)
    rest = x - hi.astype(F32)
    mid = rest.astype(BF16)
    low = (rest - mid.astype(F32)).astype(BF16)
    return hi, mid, low


def _compress_kernel(x_ref, pe_ref, w1a_ref, w1b_ref, w2_ref, o_ref):
    x = x_ref[0]
    n_rows = x.shape[0]

    def mm3(lhs, w_ref):
        return sum(jnp.dot(lhs, w_ref[piece], preferred_element_type=F32) for piece in range(3))

    first = mm3(x, w1a_ref)
    second = mm3(x, w1b_ref)
    pe_term = sum(mm3(piece, w1a_ref) for piece in _split3(pe_ref[0])) \
        + sum(mm3(piece, w1b_ref) for piece in _split3(pe_ref[1]))
    pre = first + pltpu.roll(second, n_rows - 1, 0) + pe_term[0:1, :]
    hid = 0.5 * pre * (1.0 + jnp.tanh(math.sqrt(2.0 / math.pi) * (pre + 0.044715 * (pre * pre * pre))))
    o_ref[0] = jnp.dot(hid, w2_ref[...], precision=HIGHEST, preferred_element_type=F32)


def _compress_weights(cmp_pos, cmp_w1, cmp_w2):
    depth = cmp_w1.shape[0]
    half = CMP_LEN // 2
    feat = CMP_STRIDE * LANES
    eye = jnp.eye(2, dtype=F32)
    w1 = cmp_w1.astype(F32).reshape(depth, 2, CMP_LEN, HEAD_DIM, CMP_HIDDEN)

    def pieces(w):
        big = jnp.einsum('nwldj,hg->nwlhdgj', w, eye).reshape(depth, 2, feat, 2 * CMP_HIDDEN)
        return jnp.stack(_split3(big), axis=2)

    w2 = jnp.einsum('nwjd,hg->nwhjgd', cmp_w2.astype(F32), eye).reshape(depth, 2, 2 * CMP_HIDDEN, LANES)
    w2 = jnp.concatenate([w2, jnp.roll(w2, HEAD_DIM, axis=3)], axis=3)
    pe = jnp.broadcast_to(cmp_pos.astype(F32).reshape(depth, 2, 2, half, 1, HEAD_DIM),
                          (depth, 2, 2, half, 2, HEAD_DIM))
    pe = jnp.broadcast_to(pe.reshape(depth, 2, 2, 1, feat), (depth, 2, 2, 8, feat))
    return pe, pieces(w1[:, :, :half]), pieces(w1[:, :, half:]), w2


def _compress(x, weights, layer, branch):
    b, n_rows, feat = x.shape
    pe, w1a, w1b, w2 = weights

    def picked(shape):
        return pl.BlockSpec((None, None) + shape, lambda i: (layer, branch) + (0,) * len(shape))

    return pl.pallas_call(
        _compress_kernel,
        grid=(b,),
        in_specs=[pl.BlockSpec((1, n_rows, feat), lambda i: (i, 0, 0)),
                  picked((2, 8, feat)),
                  picked((3, feat, 2 * CMP_HIDDEN)),
                  picked((3, feat, 2 * CMP_HIDDEN)),
                  picked((2 * CMP_HIDDEN, 2 * LANES))],
        out_specs=pl.BlockSpec((1, n_rows, 2 * LANES), lambda i: (i, 0, 0)),
        out_shape=jax.ShapeDtypeStruct((b, n_rows, 2 * LANES), F32),
        compiler_params=_params(1),
    )(x, pe, w1a, w1b, w2)


def _select_kernel(q_ref, kc_ref, vct_ref, bias_ref, o_ref, mb_ref, count_scr, psum_scr):
    tq = SEL_TQ
    i = pl.program_id(0)
    lo = _lane_lo((tq, LANES))
    n_rows = kc_ref.shape[1]
    lo_k = _lane_lo((n_rows, LANES))
    n_grp = n_rows // SUBLANES

    k_own = kc_ref[0, :, 0:LANES]
    k_swapped = kc_ref[0, :, LANES:2 * LANES]
    hi = k_own.astype(BF16)
    low = (k_swapped - k_swapped.astype(BF16).astype(F32)).astype(BF16)
    k_sel = (jnp.where(lo_k, hi, low), jnp.where(lo_k, low, hi))

    def scores(g):
        qg = q_ref[0, :, g * LANES:(g + 1) * LANES]
        swapped = pltpu.roll(qg.astype(F32), HEAD_DIM, 1).astype(BF16)
        q_dup = (jnp.where(lo, qg, swapped), jnp.where(lo, swapped, qg))
        return [bias_ref[2 * g + half, 0] + _dot_nt(k_sel[half], q_dup[half]) for half in range(2)]

    query = i * tq + lax.broadcasted_iota(jnp.int32, (SUBLANES, tq), 1)
    has_keys = query >= CMP_LEN - 1
    p_sum = [None, None]

    def softmax_pv(g, sts):
        pair = []
        for half in range(2):
            s3 = sts[half].reshape(n_grp, SUBLANES, tq)
            m = _all_sublanes(s3.max(axis=0), jnp.maximum)
            e = jnp.exp2(s3 - m[None])
            inv = jnp.where(has_keys, 1.0 / _all_sublanes(e.sum(axis=0), jnp.add), 0.0)
            p = e * inv[None]
            p_sum[half] = p if p_sum[half] is None else p_sum[half] + p
            pair.append(jnp.dot(vct_ref[0, half * HEAD_DIM:(half + 1) * HEAD_DIM, :],
                                p.reshape(n_rows, tq).astype(BF16), preferred_element_type=F32))
        o_ref[0, :, g * LANES:(g + 1) * LANES] = jnp.concatenate(pair, axis=0).T.astype(o_ref.dtype)

    n_groups = NSA_HEADS // 2
    pending = scores(0)
    for g in range(n_groups):
        current = pending
        if g + 1 < n_groups:
            pending = scores(g + 1)
        softmax_pv(g, current)

    n_blk = HEAD_DIM
    blk_grp = n_blk // SUBLANES
    sub = lax.broadcasted_iota(jnp.int32, (SUBLANES, tq), 0)
    q_blk = query // SLC_BLOCK
    kind = []
    for r in range(blk_grp):
        blk = sub + r * SUBLANES
        behind = q_blk - blk
        forced = jnp.where(blk == 0, 1, 0) + jnp.where(behind == 0, 1, 0) + jnp.where(behind == 1, 1, 0)
        kind.append(jnp.where(behind < 0, 2, jnp.minimum(forced, 1)))
    masks = []
    per_blk = SLC_BLOCK // CMP_STRIDE
    n_real = n_rows // per_blk
    n_lane_chunks = tq // LANES
    psum_scr[:, 0:SUBLANES, :] = jnp.zeros((n_lane_chunks, SUBLANES, LANES), F32)
    for half in (1, 0):
        p_rows = p_sum[half].reshape(n_rows, tq)
        for c in range(n_lane_chunks):
            psum_scr[c, SUBLANES:SUBLANES + n_rows, :] = p_rows[:, c * LANES:(c + 1) * LANES]

        def every_fourth(offset):
            return jnp.concatenate([psum_scr[c, pl.ds(SUBLANES + offset, n_real, stride=per_blk), :]
                                    for c in range(n_lane_chunks)], axis=1)

        imp = (0.5 * (every_fourth(-1) + every_fourth(3))
               + (every_fourth(0) + every_fourth(1) + every_fourth(2)))
        if n_real < n_blk:
            imp = jnp.concatenate([imp, jnp.zeros((n_blk - n_real, tq), F32)], axis=0)
        rows = [jnp.where(kind[r] == 2, NEG, jnp.where(kind[r] == 1, FORCE, imp[r * SUBLANES:(r + 1) * SUBLANES, :]))
                for r in range(blk_grp)]
        count_scr[...] = jnp.zeros(count_scr.shape, jnp.int32)
        for r_other in range(blk_grp):
            @pl.when(r_other * SUBLANES * SLC_BLOCK < (i + 1) * tq)
            def _(r_other=r_other, rows=rows):
                counts = [None] * blk_grp
                for s_other in range(SUBLANES):
                    row = jnp.broadcast_to(rows[r_other][s_other:s_other + 1, :], (SUBLANES, tq))
                    for r in range(blk_grp):
                        if r > r_other:
                            beats = jnp.where(row >= rows[r], 1, 0)
                        elif r < r_other:
                            beats = jnp.where(row > rows[r], 1, 0)
                        else:
                            beats = jnp.where(sub > s_other, jnp.where(row >= rows[r], 1, 0),
                                              jnp.where(row > rows[r], 1, 0))
                        counts[r] = beats if counts[r] is None else counts[r] + beats
                for r in range(blk_grp):
                    count_scr[r] = count_scr[r] + counts[r]
        masks.extend(jnp.where(count_scr[r] < TOPK, 0.0, NEG) for r in range(blk_grp))
    for c in range(tq // LANES):
        mb_ref[0, c * LANES:(c + 1) * LANES, :] = jnp.concatenate(
            [mk[:, c * LANES:(c + 1) * LANES] for mk in masks], axis=0).T.astype(BF16)


def _select(nsa_q, k_cmp, v_cmp, bias_c):
    b, s, width = nsa_q.shape
    n_rows = k_cmp.shape[1]
    n_blk = HEAD_DIM
    assert CMP_LEN == 2 * CMP_STRIDE and SLC_BLOCK == 4 * CMP_STRIDE
    tq = SEL_TQ
    vct = v_cmp[:, :, 0:LANES].transpose(0, 2, 1).astype(BF16)
    return pl.pallas_call(
        _select_kernel,
        grid=(s // tq, b),
        in_specs=[pl.BlockSpec((1, tq, width), lambda j, i: (i, j, 0)),
                  pl.BlockSpec((1, n_rows, 2 * LANES), lambda j, i: (i, 0, 0)),
                  pl.BlockSpec((1, LANES, n_rows), lambda j, i: (i, 0, 0)),
                  pl.BlockSpec((NSA_HEADS, 1, n_rows, tq), lambda j, i: (0, j, 0, 0))],
        out_specs=[pl.BlockSpec((1, tq, width), lambda j, i: (i, j, 0)),
                   pl.BlockSpec((1, tq, LANES), lambda j, i: (i, j, 0))],
        out_shape=[jax.ShapeDtypeStruct((b, s, width), BF16),
                   jax.ShapeDtypeStruct((b, s, LANES), BF16)],
        scratch_shapes=[pltpu.VMEM((n_blk // SUBLANES, SUBLANES, tq), jnp.int32),
                        pltpu.VMEM((tq // LANES, SUBLANES + n_rows, LANES), F32)],
        compiler_params=_params(2),
    )(nsa_q, k_cmp, vct, bias_c)


def _slc_kernel(q_ref, mb_ref, k_ref, e2_ref, vt_ref, bias_ref, o_ref, qs_scr, s_scr, tmax_scr, m_scr, acc_scr, *,
                n_near):
    t = FLASH_T
    tq = q_ref.shape[1]
    ratio = tq // t
    i = pl.program_id(1)
    lo = _lane_lo((t, LANES))
    lo_q = _lane_lo((tq, LANES))
    n_groups = NSA_HEADS // 2
    mb = mb_ref[0]
    for g in range(n_groups):
        qg = q_ref[0, :, g * LANES:(g + 1) * LANES]
        qs_scr[2 * g] = jnp.where(lo_q, qg, mb)
        qs_scr[2 * g + 1] = jnp.where(lo_q, mb, qg)
    _flash_init(m_scr, acc_scr)

    def qk_scores(j):
        start = pl.multiple_of(j * t, t)
        k_tile = k_ref[0, pl.ds(start, t), :]
        e_tile = e2_ref[pl.ds(start, t), :]
        k_sel = (jnp.where(lo, k_tile, e_tile), jnp.where(lo, e_tile, k_tile))
        return [functools.partial(lambda k, pos: _dot_nt(k, qs_scr[pos]), k_sel[pos % 2], pos)
                for pos in range(NSA_HEADS)]

    def bias_tile(pos, j):
        tiles = []
        for chunk in range(ratio):
            behind = ratio * i + chunk - j
            tiles.append(bias_ref[pos, jnp.where(behind < 0, n_near + 1, jnp.minimum(behind, n_near))])
        return tuple(tiles)

    def vt_slab(pos, j):
        kv = pos % 2
        return vt_ref[0, kv * VT_ROWS:(kv + 1) * VT_ROWS, pl.ds(pl.multiple_of(j * t, t), t)]

    _flash_pipeline(ratio * i + ratio - 1, NSA_HEADS, qk_scores, bias_tile, vt_slab, s_scr, tmax_scr, m_scr, acc_scr)
    _flash_finish(o_ref, n_groups, acc_scr)


def _slc_attention(nsa_q, mask_bias, k2, e2, vt, bias):
    b, s, width = nsa_q.shape
    t, tq = FLASH_T, SLC_TQ
    n_near = bias.shape[1] - 2
    return pl.pallas_call(
        functools.partial(_slc_kernel, n_near=n_near),
        grid=(b, s // tq),
        in_specs=[pl.BlockSpec((1, tq, width), lambda i, j: (i, j, 0)),
                  pl.BlockSpec((1, tq, LANES), lambda i, j: (i, j, 0)),
                  pl.BlockSpec((1, s, LANES), lambda i, j: (i, 0, 0)),
                  pl.BlockSpec((s, LANES), lambda i, j: (0, 0), pipeline_mode=pl.Buffered(1)),
                  pl.BlockSpec((1, 2 * VT_ROWS, s), lambda i, j: (i, VT_SLC_BLOCK, 0)),
                  pl.BlockSpec(bias.shape, lambda i, j: (0, 0, 0, 0), pipeline_mode=pl.Buffered(1))],
        out_specs=pl.BlockSpec((1, tq, width), lambda i, j: (i, j, 0)),
        out_shape=jax.ShapeDtypeStruct((b, s, width), BF16),
        scratch_shapes=[pltpu.VMEM((NSA_HEADS, tq, LANES), BF16),
                        pltpu.VMEM((2, NSA_HEADS, t, tq), F32),
                        pltpu.VMEM((2, NSA_HEADS, SUBLANES, tq), F32),
                        pltpu.VMEM((NSA_HEADS, SUBLANES, tq), F32),
                        pltpu.VMEM((NSA_HEADS, VT_ROWS, tq), F32)],
        compiler_params=_params(2),
    )(nsa_q, mask_bias, k2, e2, vt, bias)


def _block_onehot(s_len):
    blk = np.arange(s_len)[:, None] // SLC_BLOCK
    lane = np.arange(LANES)[None, :] % HEAD_DIM
    return jnp.asarray((blk == lane).astype(np.float32), dtype=BF16)


def _mix_ffn_kernel(x_ref, mod_ref, swa_ref, fox_ref, cmp_ref, slc_ref, win_ref, misc_ref, expand_ref, gn_ref,
                    w_ref, post_ref, pre_ref, wg_ref, wu_ref, wd_ref, fpost_ref, o_ref):
    n_swa = SWA_HEADS * HEAD_DIM
    n_fox = FOX_HEADS * HEAD_DIM
    n_nsa = NSA_HEADS * HEAD_DIM
    gate = jax.nn.sigmoid(misc_ref[0])
    gate_hi = gate.astype(BF16)
    gate_lo = (gate - gate_hi.astype(F32)).astype(BF16)
    gates = (jnp.dot(gate_hi, expand_ref[...], preferred_element_type=F32)
             + jnp.dot(gate_lo, expand_ref[...], preferred_element_type=F32))
    o_nsa = (gates[:, 0:n_nsa] * cmp_ref[0].astype(F32) + gates[:, n_nsa:2 * n_nsa] * slc_ref[0].astype(F32)
             + gates[:, 2 * n_nsa:3 * n_nsa] * win_ref[0].astype(F32))
    a = _rms(swa_ref[0].astype(F32), gn_ref[:, 0:n_swa]).astype(BF16)
    b = _rms(fox_ref[0].astype(F32), gn_ref[:, n_swa:n_swa + n_fox]).astype(BF16)
    c = _rms(o_nsa, gn_ref[:, n_swa + n_fox:]).astype(BF16)
    y = (jnp.dot(a, w_ref[0:n_swa, :], preferred_element_type=F32)
         + jnp.dot(b, w_ref[n_swa:n_swa + n_fox, :], preferred_element_type=F32)
         + jnp.dot(c, w_ref[n_swa + n_fox:, :], preferred_element_type=F32))
    x = x_ref[0] + mod_ref[0, 2:3, :] * _rms(y, post_ref[...])

    h = (_rms(x, pre_ref[...]) * (1.0 + mod_ref[0, 4:5, :]) + mod_ref[0, 3:4, :]).astype(BF16)
    y = jnp.zeros(x.shape, F32)
    for chunk in range(wg_ref.shape[1] // FFN_CHUNK):
        cols = slice(chunk * FFN_CHUNK, (chunk + 1) * FFN_CHUNK)
        gate = jnp.dot(h, wg_ref[:, cols], preferred_element_type=F32)
        up = jnp.dot(h, wu_ref[:, cols], preferred_element_type=F32)
        act = (gate * jax.nn.sigmoid(gate) * up).astype(BF16)
        y = y + jnp.dot(act, wd_ref[cols, :], preferred_element_type=F32)
    o_ref[0] = x + mod_ref[0, 5:6, :] * _rms(y, fpost_ref[...])


def _gate_expansion():
    expand = np.zeros((LANES, 3 * NSA_HEADS * HEAD_DIM), np.float32)
    for branch in range(3):
        for p in range(NSA_HEADS):
            col = (branch * NSA_HEADS + p) * HEAD_DIM
            expand[GATE_LANE + 8 * branch + p, col:col + HEAD_DIM] = 1.0
    return jnp.asarray(expand, dtype=BF16)


def _mix_ffn(x, mod, o_swa, o_fox, o_cmp, o_slc, o_win, misc, gn, w, post, pre, wg, wu, wd, fpost, layer):
    b, s, d = x.shape
    expand = _gate_expansion()
    hidden = wg.shape[2]

    def rows(width):
        return pl.BlockSpec((1, ROW_TILE, width), lambda i, j: (i, j, 0))

    def whole(shape):
        return pl.BlockSpec(shape, lambda i, j: (0,) * len(shape), pipeline_mode=pl.Buffered(1))

    def of_layer(shape):
        return pl.BlockSpec((None,) + shape, lambda i, j: (layer,) + (0,) * len(shape),
                            pipeline_mode=pl.Buffered(1))

    vec = pl.BlockSpec((None, 1, d), lambda i, j: (layer, 0, 0))
    return pl.pallas_call(
        _mix_ffn_kernel,
        grid=(b, s // ROW_TILE),
        in_specs=[rows(d),
                  pl.BlockSpec((None, 1, ADA_CHUNKS, d), lambda i, j: (layer, i, 0, 0)),
                  rows(o_swa.shape[2]), rows(o_fox.shape[2]), rows(o_cmp.shape[2]), rows(o_slc.shape[2]),
                  rows(o_win.shape[2]), rows(LANES),
                  whole(expand.shape), vec, of_layer((d, d)), vec, vec,
                  of_layer((d, hidden)), of_layer((d, hidden)), of_layer((hidden, d)),
                  vec],
        out_specs=rows(d),
        out_shape=jax.ShapeDtypeStruct((b, s, d), F32),
        compiler_params=_params(2),
    )(x, mod, o_swa, o_fox, o_cmp, o_slc, o_win, misc, expand, gn, w, post, pre, wg, wu, wd, fpost)


def _forget_lanes():
    lanes, heads = [], []
    for h in range(FOX_HEADS):
        base = (h // 2) * LANES + (HEAD_DIM if h % 2 == 0 else 0)
        for j in range(KEY_BIAS_TERMS):
            lanes.append(base + j)
            heads.append(h)
    return np.array(lanes), np.array(heads)


def _in_proj_layout():
    d = HEAD_DIM
    o_qa, o_ka, o_va, o_qb, o_kb, o_vb, o_fb, o_qc = 0, 256, 384, 512, 768, 1024, 1280, 1284
    o_kc, o_vc, o_ksl, o_vsl, o_kw, o_vw, o_gc = 1796, 1924, 2052, 2180, 2308, 2436, 2564
    scale = LOG2E / math.sqrt(d)

    def head_cols(base, heads):
        return np.concatenate([np.arange(base + h * d, base + (h + 1) * d) for h in heads])

    def span(base, width):
        return np.arange(base, base + width)

    cols = [head_cols(o_qa, SWA_POS), span(o_ka, 128),
            span(o_qb, 256), span(o_kb, 256),
            head_cols(o_qc, NSA_POS),
            span(o_kc, 128), span(o_vc, 128),
            span(o_ksl, 128), span(o_kw, 128),
            span(o_vb, 256), span(o_va, 128), span(o_vsl, 128), span(o_vw, 128)]
    scales = [np.full(256, scale), np.ones(128), np.full(256, scale), np.ones(256), np.full(512, scale),
              np.ones(256), np.ones(256), np.ones(640)]
    lanes, heads = _forget_lanes()
    misc_cols = np.zeros(SEG_MISC[1] - SEG_MISC[0], np.int64)
    misc_scale = np.zeros(SEG_MISC[1] - SEG_MISC[0])
    misc_cols[lanes] = o_fb + heads
    misc_scale[lanes] = 1.0
    for branch in range(3):
        for p, h in enumerate(NSA_POS):
            misc_cols[GATE_LANE + 8 * branch + p] = o_gc + h * 3 + branch
            misc_scale[GATE_LANE + 8 * branch + p] = 1.0
    cols.append(misc_cols)
    scales.append(misc_scale)
    return np.concatenate(cols), np.concatenate(scales).astype(np.float32)


def _head_perm(pos):
    return np.concatenate([np.arange(h * HEAD_DIM, (h + 1) * HEAD_DIM) for h in pos])


def kernel(x, c, rel_bias, ada_w, ada_b, attn_pre_norm, attn_post_norm, ffn_pre_norm, ffn_post_norm, w_in,
           forget_bias, swa_sinks, cmp_pos, cmp_w1, cmp_w2, group_norm, w_out, ffn_w_gate, ffn_w_up, ffn_w_down):
    b, s, d = x.shape
    depth = w_in.shape[0]
    hidden = ffn_w_gate.shape[2]
    assert s % (2 * FLASH_T) == 0 and s // SLC_BLOCK <= HEAD_DIM and hidden % FFN_CHUNK == 0

    cols, scales = _in_proj_layout()
    n_main = SEG_MISC[0]
    breaks = np.flatnonzero(np.diff(cols[:n_main]) != 1) + 1
    runs = np.split(cols[:n_main], breaks)
    w_main = (jnp.concatenate([w_in[:, :, int(r[0]):int(r[-1]) + 1] for r in runs], axis=2)
              * scales[:n_main]).astype(BF16)
    used = np.flatnonzero(scales[n_main:] != 0)
    w_misc = jnp.zeros((depth, d, SEG_MISC[1] - SEG_MISC[0]), w_in.dtype).at[:, :, used].set(
        w_in[:, :, cols[n_main:][used]]).astype(BF16)
    lanes, heads = _forget_lanes()
    fbias_all = jnp.zeros((depth, 1, SEG_MISC[1] - SEG_MISC[0]), F32).at[:, 0, lanes].set(
        forget_bias[:, heads].astype(F32))
    swa_perm = _head_perm(SWA_POS)
    nsa_perm = _head_perm(NSA_POS)
    n_swa, n_fox = SWA_HEADS * HEAD_DIM, FOX_HEADS * HEAD_DIM
    mix_perm = np.concatenate([swa_perm, n_swa + np.arange(n_fox), n_swa + n_fox + nsa_perm])
    w_out_all = w_out[:, mix_perm, :].astype(BF16)
    wg_all = ffn_w_gate.astype(BF16)
    wu_all = ffn_w_up.astype(BF16)
    wd_all = ffn_w_down.astype(BF16)
    cmp_weights = _compress_weights(cmp_pos, cmp_w1, cmp_w2)

    def stacked(v):
        return v.astype(F32).reshape(depth, 1, v.shape[1])

    gn_all = stacked(group_norm[:, mix_perm])
    attn_pre, attn_post = stacked(attn_pre_norm), stacked(attn_post_norm)
    ffn_pre, ffn_post = stacked(ffn_pre_norm), stacked(ffn_post_norm)
    sinks_all = swa_sinks[:, np.array(SWA_POS)].astype(F32)

    tab_swa = rel_bias[:, np.array(SWA_POS)].astype(F32)
    tab_nsa = rel_bias[:, SWA_HEADS + np.array(NSA_POS)].astype(F32)
    bias_swa = _bias_table(tab_swa, _band_buckets_t(SWA_TILE, SWA_WINDOW))
    bias_win = _bias_table(tab_nsa, _band_buckets_t(WIN_TILE, NSA_WINDOW))
    bias_slc = _bias_table(tab_nsa, _toeplitz_buckets_t(FLASH_T, _near_tiles(FLASH_T)), subtract_last=True)
    n_rows = s // CMP_STRIDE
    bias_cmp = _bias_table(tab_nsa, _cmp_buckets_t(s, n_rows))
    e2 = _block_onehot(s)

    mod_all = _adaln(c.astype(F32), ada_w.astype(F32), ada_b.astype(F32)).reshape(depth, b, ADA_CHUNKS, d)

    for layer in range(depth):
        swa_qk, fox_qk, nsa_q, kc, vc, k2, misc, vt = _in_proj(x, mod_all, attn_pre, w_main, w_misc, layer)
        o_swa = _banded_attention(swa_qk, swa_qk, 2, vt, VT_SWA_BLOCK, bias_swa, sinks=sinks_all, layer=layer)
        o_fox = _fox_attention(fox_qk, _fox_key_terms(misc, fbias_all, layer), vt)
        k_cmp = _compress(kc, cmp_weights, layer, 0)
        v_cmp = _compress(vc, cmp_weights, layer, 1)
        o_cmp, mask_bias = _select(nsa_q, k_cmp, v_cmp, bias_cmp)
        o_slc = _slc_attention(nsa_q, mask_bias, k2, e2, vt, bias_slc)
        o_win = _banded_attention(nsa_q, k2, 1, vt, VT_WIN_BLOCK, bias_win)
        x = _mix_ffn(x, mod_all, o_swa, o_fox, o_cmp, o_slc, o_win, misc, gn_all, w_out_all, attn_post, ffn_pre,
                     wg_all, wu_all, wd_all, ffn_post, layer)
    return x
```

```python
import functools
import math

import numpy as np
import jax
import jax.numpy as jnp
from jax import lax
from jax.experimental import pallas as pl
from jax.experimental.pallas import tpu as pltpu

F32 = jnp.float32
BF16 = jnp.bfloat16
HIGHEST = lax.Precision.HIGHEST

LANES = 128
SUBLANES = 8
VMEM_LIMIT = 56 * 1024 * 1024

HEAD_DIM = 64
SWA_HEADS = 4
SWA_WINDOW = 128
FOX_HEADS = 4
NSA_HEADS = 8
CMP_LEN = 32
CMP_STRIDE = 16
CMP_HIDDEN = 2 * HEAD_DIM
SLC_BLOCK = 64
TOPK = 16
NSA_WINDOW = 512
REL_BUCKETS = 32
REL_MAX_DISTANCE = 1024
ZERO_BUCKET = -2
RMS_EPS = 1e-6
NEG = -1e30
FORCE = 1e30
ADA_CHUNKS = 6
LOG2E = math.log2(math.e)

SWA_POS = (0, 2, 1, 3)
NSA_POS = (0, 4, 1, 5, 2, 6, 3, 7)

SWA_TILE = 256
WIN_TILE = 256
FLASH_T = 256
FOX_TQ = 512
SEL_TQ = 512
IN_ROW_TILE = 1024
ROW_TILE = 512
FFN_CHUNK = 256
VT_ROWS = HEAD_DIM + 16
KEY_BIAS_TERMS = 3

SEG_SWA = (0, 384)
SEG_FOX = (384, 896)
SEG_NSAQ = (896, 1408)
SEG_KC = (1408, 1536)
SEG_VC = (1536, 1664)
SEG_K2 = (1664, 1920)
SEG_V = (1920, 2560)
SEG_MISC = (2560, 2816)
GATE_LANE = 8
VT_FOX_BLOCK, VT_SWA_BLOCK, VT_SLC_BLOCK, VT_WIN_BLOCK = 0, 2, 3, 4


def _params(n_grid, vmem=VMEM_LIMIT):
    return pltpu.CompilerParams(dimension_semantics=("parallel",) * n_grid, vmem_limit_bytes=vmem)


def _dot_nt(a, b):
    return lax.dot_general(a, b, (((1,), (1,)), ((), ())), preferred_element_type=F32)


def _lane_lo(shape):
    return lax.broadcasted_iota(jnp.int32, shape, len(shape) - 1) < HEAD_DIM


def _adaln_kernel(c_ref, w_ref, b_ref, o_ref):
    c = c_ref[...]
    act = c * jax.nn.sigmoid(c)
    o_ref[0] = jnp.dot(act, w_ref[0], precision=HIGHEST, preferred_element_type=F32) + b_ref[0]


def _adaln(c, ada_w, ada_b):
    depth, d, n = ada_w.shape
    b = c.shape[0]
    return pl.pallas_call(
        _adaln_kernel,
        grid=(depth, n // d),
        in_specs=[pl.BlockSpec((b, d), lambda l, j: (0, 0)),
                  pl.BlockSpec((1, d, d), lambda l, j: (l, 0, j)),
                  pl.BlockSpec((1, 1, d), lambda l, j: (l, 0, j))],
        out_specs=pl.BlockSpec((1, b, d), lambda l, j: (l, 0, j)),
        out_shape=jax.ShapeDtypeStruct((depth, b, n), F32),
        compiler_params=_params(2),
    )(c, ada_w, ada_b.reshape(depth, 1, n))


def _t5_bucket(dist):
    n = jnp.maximum(dist, 0)
    max_exact = REL_BUCKETS // 2
    nf = jnp.maximum(n, 1).astype(jnp.float32)
    large = max_exact + (jnp.log(nf / max_exact) / math.log(REL_MAX_DISTANCE / max_exact)
                         * (REL_BUCKETS - max_exact)).astype(jnp.int32)
    large = jnp.minimum(large, REL_BUCKETS - 1)
    return jnp.where(n < max_exact, n, large)


def _bias_table_kernel(tab_ref, bucket_ref, o_ref, *, subtract_last):
    n_heads = o_ref.shape[0]
    values = [[(tab_ref[k, h] - (tab_ref[REL_BUCKETS - 1, h] if subtract_last else 0.0)) * LOG2E
               for h in range(n_heads)] for k in range(REL_BUCKETS)]

    def rows(chunk, carry):
        r0 = pl.multiple_of(chunk * SUBLANES, SUBLANES)
        bucket = bucket_ref[0, pl.ds(r0, SUBLANES), :]
        accs = [jnp.where(bucket == ZERO_BUCKET, 0.0, NEG) for _ in range(n_heads)]
        for k in range(REL_BUCKETS):
            hit = bucket == k
            for h in range(n_heads):
                accs[h] = jnp.where(hit, values[k][h], accs[h])
        for h in range(n_heads):
            o_ref[h, 0, pl.ds(r0, SUBLANES), :] = accs[h]
        return carry

    lax.fori_loop(0, bucket_ref.shape[1] // SUBLANES, rows, 0)


def _bias_table(table, bucket, subtract_last=False):
    n_heads = table.shape[1]
    n, r, c = bucket.shape
    return pl.pallas_call(
        functools.partial(_bias_table_kernel, subtract_last=subtract_last),
        grid=(n,),
        in_specs=[pl.BlockSpec(memory_space=pltpu.SMEM),
                  pl.BlockSpec((1, r, c), lambda i: (i, 0, 0))],
        out_specs=pl.BlockSpec((n_heads, 1, r, c), lambda i: (0, i, 0, 0)),
        out_shape=jax.ShapeDtypeStruct((n_heads, n, r, c), F32),
        compiler_params=_params(1),
    )(table, bucket)


def _band_buckets_t(tile, window):
    n_back = -(-(window - 1) // tile)
    t = jnp.arange(n_back + 1)[:, None, None]
    key = jnp.arange(tile)[None, :, None]
    query = jnp.arange(tile)[None, None, :]
    dist = query + (n_back - t) * tile - key
    return jnp.where((dist >= 0) & (dist < window), _t5_bucket(dist), -1).astype(jnp.int32)


def _toeplitz_buckets_t(tile, n_tiles):
    m = jnp.arange(n_tiles)[:, None, None]
    key = jnp.arange(tile)[None, :, None]
    query = jnp.arange(tile)[None, None, :]
    dist = m * tile + query - key
    near = jnp.where(dist >= 0, _t5_bucket(dist), -1).astype(jnp.int32)
    return jnp.concatenate([near, jnp.full((1, tile, tile), ZERO_BUCKET, jnp.int32)])


def _cmp_buckets_t(s_len, n_rows):
    n_c = n_rows - 1
    tile = jnp.arange(s_len // SEL_TQ)[:, None, None]
    n = jnp.arange(n_rows)[None, :, None]
    t = tile * SEL_TQ + jnp.arange(SEL_TQ)[None, None, :]
    dist = t - (n * CMP_STRIDE + CMP_LEN - 1)
    return jnp.where((dist >= 0) & (n < n_c), _t5_bucket(dist), -1).astype(jnp.int32)


def _near_tiles(tile):
    max_exact = REL_BUCKETS // 2
    first_const = math.ceil(max_exact * (REL_MAX_DISTANCE / max_exact) ** ((max_exact - 1) / max_exact)) + 1
    m = 1
    while m * tile - (tile - 1) < first_const:
        m += 1
    return m


def _rms(x, gain):
    return x * lax.rsqrt(jnp.mean(x * x, axis=-1, keepdims=True) + RMS_EPS) * gain


def _in_proj_kernel(x_ref, mod_ref, gain_ref, w_ref, wm_ref, swa_ref, fox_ref, nsaq_ref, kc_ref, vc_ref, k2_ref,
                    misc_ref, vt_ref, pack_scr):
    x = x_ref[0]
    h = _rms(x, gain_ref[...]) * (1.0 + mod_ref[0, 1:2, :]) + mod_ref[0, 0:1, :]
    hb = h.astype(BF16)

    def seg(bounds):
        if bounds == SEG_MISC:
            return jnp.dot(hb, wm_ref[...], preferred_element_type=F32)
        return jnp.dot(hb, w_ref[:, bounds[0]:bounds[1]], preferred_element_type=F32)

    swa_ref[0] = seg(SEG_SWA).astype(BF16)
    fox_ref[0] = seg(SEG_FOX).astype(BF16)
    nsaq_ref[0] = seg(SEG_NSAQ).astype(BF16)
    for slot, (bounds, out_ref) in enumerate(((SEG_KC, kc_ref), (SEG_VC, vc_ref))):
        pack_scr[slot] = seg(bounds)
        for tok in range(CMP_STRIDE):
            out_ref[0, :, tok * LANES:(tok + 1) * LANES] = pack_scr[
                slot, pl.ds(tok, x.shape[0] // CMP_STRIDE, stride=CMP_STRIDE), :].astype(BF16)
    k2_ref[0] = seg(SEG_K2).astype(BF16)
    misc_ref[0] = seg(SEG_MISC)

    rows = x.shape[0]
    extra_row = lax.broadcasted_iota(jnp.int32, (VT_ROWS - HEAD_DIM, rows), 0)
    extra = jnp.where(extra_row == 0, 1.0, 0.0).astype(BF16)
    values = seg(SEG_V)
    for c in range(values.shape[1] // LANES):
        vt = values[:, c * LANES:(c + 1) * LANES].T.astype(BF16)
        for half in range(2):
            base = (2 * c + half) * VT_ROWS
            vt_ref[0, base:base + HEAD_DIM, :] = vt[half * HEAD_DIM:(half + 1) * HEAD_DIM, :]
            vt_ref[0, base + HEAD_DIM:base + VT_ROWS, :] = extra


def _in_proj(x, mod, gain, w, w_misc, layer):
    b, s, d = x.shape
    n = w.shape[2]

    tile = IN_ROW_TILE

    def rows(width, dtype):
        return (pl.BlockSpec((1, tile, width), lambda i, j: (i, j, 0)), jax.ShapeDtypeStruct((b, s, width), dtype))

    packed = (pl.BlockSpec((1, tile // CMP_STRIDE, CMP_STRIDE * LANES), lambda i, j: (i, j, 0)),
              jax.ShapeDtypeStruct((b, s // CMP_STRIDE, CMP_STRIDE * LANES), BF16))
    vt_rows = (SEG_V[1] - SEG_V[0]) // HEAD_DIM * VT_ROWS
    outs = [rows(SEG_SWA[1] - SEG_SWA[0], BF16), rows(SEG_FOX[1] - SEG_FOX[0], BF16),
            rows(SEG_NSAQ[1] - SEG_NSAQ[0], BF16), packed, packed, rows(SEG_K2[1] - SEG_K2[0], BF16),
            rows(SEG_MISC[1] - SEG_MISC[0], F32),
            (pl.BlockSpec((1, vt_rows, tile), lambda i, j: (i, 0, j)),
             jax.ShapeDtypeStruct((b, vt_rows, s), BF16))]
    return pl.pallas_call(
        _in_proj_kernel,
        grid=(b, s // tile),
        in_specs=[pl.BlockSpec((1, tile, d), lambda i, j: (i, j, 0)),
                  pl.BlockSpec((None, 1, ADA_CHUNKS, d), lambda i, j: (layer, i, 0, 0)),
                  pl.BlockSpec((None, 1, d), lambda i, j: (layer, 0, 0)),
                  pl.BlockSpec((None, d, n), lambda i, j: (layer, 0, 0)),
                  pl.BlockSpec((None, d, w_misc.shape[2]), lambda i, j: (layer, 0, 0))],
        out_specs=[spec for spec, _ in outs],
        out_shape=[shape for _, shape in outs],
        scratch_shapes=[pltpu.VMEM((2, tile, LANES), F32)],
        compiler_params=_params(2),
    )(x, mod, gain, w, w_misc)


def _banded_kernel(*refs, n_back, n_groups, has_sink, t, layer):
    if has_sink:
        sink_ref, q_ref, k_ref, vt_ref, bias_ref, o_ref = refs
    else:
        q_ref, k_ref, vt_ref, bias_ref, o_ref = refs
    i = pl.program_id(1)
    lo = _lane_lo((t, LANES))
    n_tiles = n_back + 1

    def run(all_valid):
        starts = [pl.multiple_of(jnp.maximum(i - n_back + tt, 0) * t, t) for tt in range(n_tiles)]
        k_tiles = [k_ref[0, pl.ds(start, t), :] for start in starts]

        def scores(g):
            qg = q_ref[0, :, g * LANES:(g + 1) * LANES]
            zero = jnp.zeros_like(qg)
            qms = (jnp.where(lo, qg, zero), jnp.where(lo, zero, qg))
            return [[bias_ref[2 * g + half, tt] + _dot_nt(k_tiles[tt], qms[half]) for tt in range(n_tiles)]
                    for half in range(2)]

        def softmax_pv(g, sts):
            pair = []
            for half in range(2):
                tiles = sts[half]
                if not all_valid:
                    tiles = [jnp.where(i - n_back + tt >= 0, st, NEG) if tt < n_back else st
                             for tt, st in enumerate(tiles)]
                m = None
                for st in tiles:
                    part = st.reshape(t // SUBLANES, SUBLANES, t).max(axis=0)
                    m = part if m is None else jnp.maximum(m, part)
                m = _all_sublanes(m, jnp.maximum)
                if has_sink:
                    sink = sink_ref[layer, 2 * g + half] * LOG2E
                    m = jnp.maximum(m, sink)
                acc = None
                for tt, st in enumerate(tiles):
                    p = jnp.exp2((st.reshape(t // SUBLANES, SUBLANES, t) - m[None]).reshape(t, t).astype(BF16))
                    part = jnp.dot(vt_ref[0, half * VT_ROWS:(half + 1) * VT_ROWS, pl.ds(starts[tt], t)], p,
                                   preferred_element_type=F32)
                    acc = part if acc is None else acc + part
                denom = _all_sublanes(acc[HEAD_DIM:HEAD_DIM + SUBLANES, :], jnp.add)
                if has_sink:
                    denom = denom + jnp.exp2(sink - m)
                out = acc[0:HEAD_DIM, :].reshape(HEAD_DIM // SUBLANES, SUBLANES, t) / denom[None]
                pair.append(out.reshape(HEAD_DIM, t))
            o_ref[0, :, g * LANES:(g + 1) * LANES] = jnp.concatenate(pair, axis=0).T.astype(o_ref.dtype)

        pending = scores(0)
        for g in range(n_groups):
            current = pending
            if g + 1 < n_groups:
                pending = scores(g + 1)
            softmax_pv(g, current)

    @pl.when(i >= n_back)
    def _():
        run(True)

    @pl.when(i < n_back)
    def _():
        run(False)


def _banded_attention(q_arr, k_arr, k_blk, vt, vt_blk, bias, sinks=None, layer=0):
    b, s, _ = q_arr.shape
    n_pos, n_tiles, t = bias.shape[0], bias.shape[1], bias.shape[2]
    width = n_pos * HEAD_DIM
    in_specs = [pl.BlockSpec((1, t, width), lambda i, j: (i, j, 0)),
                pl.BlockSpec((1, s, LANES), lambda i, j: (i, 0, k_blk)),
                pl.BlockSpec((1, 2 * VT_ROWS, s), lambda i, j: (i, vt_blk, 0)),
                pl.BlockSpec(bias.shape, lambda i, j: (0, 0, 0, 0))]
    args = [q_arr, k_arr, vt, bias]
    if sinks is not None:
        in_specs = [pl.BlockSpec(memory_space=pltpu.SMEM)] + in_specs
        args = [sinks] + args
    return pl.pallas_call(
        functools.partial(_banded_kernel, n_back=n_tiles - 1, n_groups=n_pos // 2, has_sink=sinks is not None, t=t,
                          layer=layer),
        grid=(b, s // t),
        in_specs=in_specs,
        out_specs=pl.BlockSpec((1, t, width), lambda i, j: (i, j, 0)),
        out_shape=jax.ShapeDtypeStruct((b, s, width), BF16),
        compiler_params=_params(2),
    )(*args)


def _all_sublanes(x, op):
    for shift in (4, 2, 1):
        x = op(x, pltpu.roll(x, shift, 0))
    return x


def _flash_init(m_scr, acc_scr):
    m_scr[...] = jnp.full(m_scr.shape, NEG, F32)
    acc_scr[...] = jnp.zeros(acc_scr.shape, F32)


def _flash_update(h, st_ref, tile_max, vt_h, m_scr, acc_scr):
    tk, tq = st_ref.shape
    m_prev = m_scr[h]
    m_new = _all_sublanes(jnp.maximum(m_prev, tile_max), jnp.maximum)
    alpha = jnp.exp2(m_prev - m_new)
    p = jnp.exp2((st_ref[...].reshape(tk // SUBLANES, SUBLANES, tq) - m_new[None]).reshape(tk, tq).astype(BF16))
    acc = acc_scr[h].reshape(VT_ROWS // SUBLANES, SUBLANES, tq) * alpha[None]
    acc_scr[h] = acc.reshape(VT_ROWS, tq) + jnp.dot(vt_h, p, preferred_element_type=F32)
    m_scr[h] = m_new


def _flash_finish(o_ref, n_groups, acc_scr):
    tq = acc_scr.shape[2]
    for g in range(n_groups):
        pair = []
        for h in (2 * g, 2 * g + 1):
            denom = _all_sublanes(acc_scr[h, HEAD_DIM:HEAD_DIM + SUBLANES, :], jnp.add)
            out = acc_scr[h, 0:HEAD_DIM, :].reshape(HEAD_DIM // SUBLANES, SUBLANES, tq) / denom[None]
            pair.append(out.reshape(HEAD_DIM, tq))
        o_ref[0, :, g * LANES:(g + 1) * LANES] = jnp.concatenate(pair, axis=0).T.astype(o_ref.dtype)


def _flash_pipeline(i, n_heads, qk_scores, bias_tile, vt_slab, s_scr, tmax_scr, m_scr, acc_scr):
    def qk_head(thunk, h, j, slot):
        st = thunk() + bias_tile(h, j)
        s_scr[slot, h] = st
        tmax_scr[slot, h] = st.reshape(st.shape[0] // SUBLANES, SUBLANES, st.shape[1]).max(axis=0)

    def softmax_head(h, j, slot):
        _flash_update(h, s_scr.at[slot, h], tmax_scr[slot, h], vt_slab(h, j), m_scr, acc_scr)

    def softmax_all(j, slot):
        for h in range(n_heads):
            softmax_head(h, j, slot)

    def stage(j_qk, slot_qk, j_sm, slot_sm):
        thunks = qk_scores(j_qk)
        for h in range(n_heads):
            qk_head(thunks[h], h, j_qk, slot_qk)
            softmax_head(h, j_sm, slot_sm)

    for h, thunk in enumerate(qk_scores(0)):
        qk_head(thunk, h, 0, 0)

    def body(trip, carry):
        j = 2 * trip
        stage(j + 1, 1, j, 0)
        stage(j + 2, 0, j + 1, 1)
        return carry

    lax.fori_loop(0, i // 2, body, 0)
    last = 2 * (i // 2)

    @pl.when(i % 2 == 0)
    def _():
        softmax_all(last, 0)

    @pl.when(i % 2 == 1)
    def _():
        stage(last + 1, 1, last, 0)
        softmax_all(last + 1, 1)


def _fox_aug_kernel(misc_ref, fbias_ref, tri_ref, o_ref):
    s_len, width = misc_ref.shape[1], misc_ref.shape[2]
    term = lax.broadcasted_iota(jnp.int32, (LANES, width), 1) % HEAD_DIM
    carry = jnp.zeros((1, width), F32)
    for c in range(s_len // LANES):
        z = misc_ref[0, c * LANES:(c + 1) * LANES, :] + fbias_ref[...]
        log_f = jnp.minimum(z, 0.0) - jnp.log1p(jnp.exp(-jnp.abs(z)))
        cum = jnp.dot(tri_ref[...], log_f, precision=HIGHEST, preferred_element_type=F32) + carry
        carry = cum[LANES - 1:LANES, :]
        x = cum * (-LOG2E)
        hi = x.astype(BF16).astype(F32)
        rest = x - hi
        mid = rest.astype(BF16).astype(F32)
        low = rest - mid
        out = jnp.where(term == 0, hi, jnp.where(term == 1, mid, jnp.where(term == 2, low, 0.0)))
        o_ref[0, c * LANES:(c + 1) * LANES, :] = out.astype(BF16)


def _fox_key_terms(misc, fbias, layer):
    b, s, width = misc.shape
    tri = jnp.asarray(np.tril(np.ones((LANES, LANES), np.float32)))
    return pl.pallas_call(
        _fox_aug_kernel,
        grid=(b,),
        in_specs=[pl.BlockSpec((1, s, width), lambda i: (i, 0, 0)),
                  pl.BlockSpec((None, 1, width), lambda i: (layer, 0, 0)),
                  pl.BlockSpec((LANES, LANES), lambda i: (0, 0))],
        out_specs=pl.BlockSpec((1, s, width), lambda i: (i, 0, 0)),
        out_shape=jax.ShapeDtypeStruct((b, s, width), BF16),
        compiler_params=_params(1),
    )(misc, fbias, tri)


def _fox_kernel(q_ref, k_ref, aug_ref, vt_ref, mask_ref, o_ref, qs_scr, s_scr, tmax_scr, m_scr, acc_scr):
    t = FLASH_T
    tq = q_ref.shape[1]
    ratio = tq // t
    i = pl.program_id(1)
    lo = _lane_lo((t, LANES))
    lo_q = _lane_lo((tq, LANES))
    lane = lax.broadcasted_iota(jnp.int32, (tq, LANES), 1)
    ones = jnp.where(lane % HEAD_DIM < KEY_BIAS_TERMS, 1.0, 0.0).astype(BF16)
    n_groups = FOX_HEADS // 2
    for g in range(n_groups):
        qg = q_ref[0, :, g * LANES:(g + 1) * LANES]
        qs_scr[2 * g] = jnp.where(lo_q, qg, ones)
        qs_scr[2 * g + 1] = jnp.where(lo_q, ones, qg)
    _flash_init(m_scr, acc_scr)

    def qk_scores(j):
        start = pl.multiple_of(j * t, t)
        scores = []
        for g in range(n_groups):
            k_tile = k_ref[0, pl.ds(start, t), g * LANES:(g + 1) * LANES]
            a_tile = aug_ref[0, pl.ds(start, t), g * LANES:(g + 1) * LANES]
            k_sel = (jnp.where(lo, k_tile, a_tile), jnp.where(lo, a_tile, k_tile))
            for half in range(2):
                scores.append(functools.partial(lambda k, h: _dot_nt(k, qs_scr[h]), k_sel[half], 2 * g + half))
        return scores

    def bias_tile(h, j):
        return mask_ref[jnp.clip(j - ratio * i + 1, 0, ratio)]

    def vt_slab(h, j):
        return vt_ref[0, h * VT_ROWS:(h + 1) * VT_ROWS, pl.ds(pl.multiple_of(j * t, t), t)]

    _flash_pipeline(ratio * i + ratio - 1, FOX_HEADS, qk_scores, bias_tile, vt_slab, s_scr, tmax_scr, m_scr, acc_scr)
    _flash_finish(o_ref, n_groups, acc_scr)


def _fox_attention(fox_qk, key_terms, vt):
    b, s, _ = fox_qk.shape
    width = FOX_HEADS * HEAD_DIM
    t, tq = FLASH_T, FOX_TQ
    ratio = tq // t
    key = np.arange(t)[:, None]
    query = np.arange(tq)[None, :]
    masks = [np.zeros((t, tq))] + [np.where(r * t + key <= query, 0.0, NEG) for r in range(ratio)]
    masks = jnp.asarray(np.stack(masks).astype(np.float32))
    return pl.pallas_call(
        _fox_kernel,
        grid=(b, s // tq),
        in_specs=[pl.BlockSpec((1, tq, width), lambda i, j: (i, j, 0)),
                  pl.BlockSpec((1, s, width), lambda i, j: (i, 0, 1)),
                  pl.BlockSpec((1, s, width), lambda i, j: (i, 0, 0)),
                  pl.BlockSpec((1, FOX_HEADS * VT_ROWS, s), lambda i, j: (i, VT_FOX_BLOCK, 0)),
                  pl.BlockSpec(masks.shape, lambda i, j: (0, 0, 0))],
        out_specs=pl.BlockSpec((1, tq, width), lambda i, j: (i, j, 0)),
        out_shape=jax.ShapeDtypeStruct((b, s, width), BF16),
        scratch_shapes=[pltpu.VMEM((FOX_HEADS, tq, LANES), BF16),
                        pltpu.VMEM((2, FOX_HEADS, t, tq), F32),
                        pltpu.VMEM((2, FOX_HEADS, SUBLANES, tq), F32),
                        pltpu.VMEM((FOX_HEADS, SUBLANES, tq), F32),
                        pltpu.VMEM((FOX_HEADS, VT_ROWS, tq), F32)],
        compiler_params=_params(2),
    )(fox_qk, fox_qk, key_terms, vt, masks)


def _split3(x):
    hi = x.astype(BF16)
    rest = x - hi.astype(F32)
    mid = rest.astype(BF16)
    low = (rest - mid.astype(F32)).astype(BF16)
    return hi, mid, low


def _compress_kernel(x_ref, pe_ref, w1a_ref, w1b_ref, w2_ref, o_ref):
    x = x_ref[0]
    n_rows = x.shape[0]

    def mm3(lhs, w_ref):
        return sum(jnp.dot(lhs, w_ref[piece], preferred_element_type=F32) for piece in range(3))

    first = mm3(x, w1a_ref)
    second = mm3(x, w1b_ref)
    pe_term = sum(mm3(piece, w1a_ref) for piece in _split3(pe_ref[0])) \
        + sum(mm3(piece, w1b_ref) for piece in _split3(pe_ref[1]))
    pre = first + pltpu.roll(second, n_rows - 1, 0) + pe_term[0:1, :]
    hid = 0.5 * pre * (1.0 + jnp.tanh(math.sqrt(2.0 / math.pi) * (pre + 0.044715 * (pre * pre * pre))))
    o_ref[0] = jnp.dot(hid, w2_ref[...], precision=HIGHEST, preferred_element_type=F32)


def _compress_weights(cmp_pos, cmp_w1, cmp_w2):
    depth = cmp_w1.shape[0]
    half = CMP_LEN // 2
    feat = CMP_STRIDE * LANES
    eye = jnp.eye(2, dtype=F32)
    w1 = cmp_w1.astype(F32).reshape(depth, 2, CMP_LEN, HEAD_DIM, CMP_HIDDEN)

    def pieces(w):
        big = jnp.einsum('nwldj,hg->nwlhdgj', w, eye).reshape(depth, 2, feat, 2 * CMP_HIDDEN)
        return jnp.stack(_split3(big), axis=2)

    w2 = jnp.einsum('nwjd,hg->nwhjgd', cmp_w2.astype(F32), eye).reshape(depth, 2, 2 * CMP_HIDDEN, LANES)
    w2 = jnp.concatenate([w2, jnp.roll(w2, HEAD_DIM, axis=3)], axis=3)
    pe = jnp.broadcast_to(cmp_pos.astype(F32).reshape(depth, 2, 2, half, 1, HEAD_DIM),
                          (depth, 2, 2, half, 2, HEAD_DIM))
    pe = jnp.broadcast_to(pe.reshape(depth, 2, 2, 1, feat), (depth, 2, 2, 8, feat))
    return pe, pieces(w1[:, :, :half]), pieces(w1[:, :, half:]), w2


def _compress(x, weights, layer, branch):
    b, n_rows, feat = x.shape
    pe, w1a, w1b, w2 = weights

    def picked(shape):
        return pl.BlockSpec((None, None) + shape, lambda i: (layer, branch) + (0,) * len(shape))

    return pl.pallas_call(
        _compress_kernel,
        grid=(b,),
        in_specs=[pl.BlockSpec((1, n_rows, feat), lambda i: (i, 0, 0)),
                  picked((2, 8, feat)),
                  picked((3, feat, 2 * CMP_HIDDEN)),
                  picked((3, feat, 2 * CMP_HIDDEN)),
                  picked((2 * CMP_HIDDEN, 2 * LANES))],
        out_specs=pl.BlockSpec((1, n_rows, 2 * LANES), lambda i: (i, 0, 0)),
        out_shape=jax.ShapeDtypeStruct((b, n_rows, 2 * LANES), F32),
        compiler_params=_params(1),
    )(x, pe, w1a, w1b, w2)


def _select_kernel(q_ref, kc_ref, vct_ref, bias_ref, o_ref, mb_ref, count_scr, psum_scr):
    tq = SEL_TQ
    i = pl.program_id(0)
    lo = _lane_lo((tq, LANES))
    n_rows = kc_ref.shape[1]
    lo_k = _lane_lo((n_rows, LANES))
    n_grp = n_rows // SUBLANES

    k_own = kc_ref[0, :, 0:LANES]
    k_swapped = kc_ref[0, :, LANES:2 * LANES]
    hi = k_own.astype(BF16)
    low = (k_swapped - k_swapped.astype(BF16).astype(F32)).astype(BF16)
    k_sel = (jnp.where(lo_k, hi, low), jnp.where(lo_k, low, hi))

    def scores(g):
        qg = q_ref[0, :, g * LANES:(g + 1) * LANES]
        swapped = pltpu.roll(qg.astype(F32), HEAD_DIM, 1).astype(BF16)
        q_dup = (jnp.where(lo, qg, swapped), jnp.where(lo, swapped, qg))
        return [bias_ref[2 * g + half, 0] + _dot_nt(k_sel[half], q_dup[half]) for half in range(2)]

    query = i * tq + lax.broadcasted_iota(jnp.int32, (SUBLANES, tq), 1)
    has_keys = query >= CMP_LEN - 1
    p_sum = [None, None]

    def softmax_pv(g, sts):
        pair = []
        for half in range(2):
            s3 = sts[half].reshape(n_grp, SUBLANES, tq)
            m = _all_sublanes(s3.max(axis=0), jnp.maximum)
            e = jnp.exp2(s3 - m[None])
            inv = jnp.where(has_keys, 1.0 / _all_sublanes(e.sum(axis=0), jnp.add), 0.0)
            p = e * inv[None]
            p_sum[half] = p if p_sum[half] is None else p_sum[half] + p
            pair.append(jnp.dot(vct_ref[0, half * HEAD_DIM:(half + 1) * HEAD_DIM, :],
                                p.reshape(n_rows, tq).astype(BF16), preferred_element_type=F32))
        o_ref[0, :, g * LANES:(g + 1) * LANES] = jnp.concatenate(pair, axis=0).T.astype(o_ref.dtype)

    n_groups = NSA_HEADS // 2
    pending = scores(0)
    for g in range(n_groups):
        current = pending
        if g + 1 < n_groups:
            pending = scores(g + 1)
        softmax_pv(g, current)

    n_blk = HEAD_DIM
    blk_grp = n_blk // SUBLANES
    sub = lax.broadcasted_iota(jnp.int32, (SUBLANES, tq), 0)
    q_blk = query // SLC_BLOCK
    kind = []
    for r in range(blk_grp):
        blk = sub + r * SUBLANES
        behind = q_blk - blk
        forced = jnp.where(blk == 0, 1, 0) + jnp.where(behind == 0, 1, 0) + jnp.where(behind == 1, 1, 0)
        kind.append(jnp.where(behind < 0, 2, jnp.minimum(forced, 1)))
    masks = []
    per_blk = SLC_BLOCK // CMP_STRIDE
    n_real = n_rows // per_blk
    n_lane_chunks = tq // LANES
    psum_scr[:, 0:SUBLANES, :] = jnp.zeros((n_lane_chunks, SUBLANES, LANES), F32)
    for half in (1, 0):
        p_rows = p_sum[half].reshape(n_rows, tq)
        for c in range(n_lane_chunks):
            psum_scr[c, SUBLANES:SUBLANES + n_rows, :] = p_rows[:, c * LANES:(c + 1) * LANES]

        def every_fourth(offset):
            return jnp.concatenate([psum_scr[c, pl.ds(SUBLANES + offset, n_real, stride=per_blk), :]
                                    for c in range(n_lane_chunks)], axis=1)

        imp = (0.5 * (every_fourth(-1) + every_fourth(3))
               + (every_fourth(0) + every_fourth(1) + every_fourth(2)))
        if n_real < n_blk:
            imp = jnp.concatenate([imp, jnp.zeros((n_blk - n_real, tq), F32)], axis=0)
        rows = [jnp.where(kind[r] == 2, NEG, jnp.where(kind[r] == 1, FORCE, imp[r * SUBLANES:(r + 1) * SUBLANES, :]))
                for r in range(blk_grp)]
        count_scr[...] = jnp.zeros(count_scr.shape, jnp.int32)
        for r_other in range(blk_grp):
            @pl.when(r_other * SUBLANES * SLC_BLOCK < (i + 1) * tq)
            def _(r_other=r_other, rows=rows):
                counts = [None] * blk_grp
                for s_other in range(SUBLANES):
                    row = jnp.broadcast_to(rows[r_other][s_other:s_other + 1, :], (SUBLANES, tq))
                    for r in range(blk_grp):
                        if r > r_other:
                            beats = jnp.where(row >= rows[r], 1, 0)
                        elif r < r_other:
                            beats = jnp.where(row > rows[r], 1, 0)
                        else:
                            beats = jnp.where(sub > s_other, jnp.where(row >= rows[r], 1, 0),
                                              jnp.where(row > rows[r], 1, 0))
                        counts[r] = beats if counts[r] is None else counts[r] + beats
                for r in range(blk_grp):
                    count_scr[r] = count_scr[r] + counts[r]
        masks.extend(jnp.where(count_scr[r] < TOPK, 0.0, NEG) for r in range(blk_grp))
    for c in range(tq // LANES):
        mb_ref[0, c * LANES:(c + 1) * LANES, :] = jnp.concatenate(
            [mk[:, c * LANES:(c + 1) * LANES] for mk in masks], axis=0).T.astype(BF16)


def _select(nsa_q, k_cmp, v_cmp, bias_c):
    b, s, width = nsa_q.shape
    n_rows = k_cmp.shape[1]
    n_blk = HEAD_DIM
    assert CMP_LEN == 2 * CMP_STRIDE and SLC_BLOCK == 4 * CMP_STRIDE
    tq = SEL_TQ
    vct = v_cmp[:, :, 0:LANES].transpose(0, 2, 1).astype(BF16)
    return pl.pallas_call(
        _select_kernel,
        grid=(s // tq, b),
        in_specs=[pl.BlockSpec((1, tq, width), lambda j, i: (i, j, 0)),
                  pl.BlockSpec((1, n_rows, 2 * LANES), lambda j, i: (i, 0, 0)),
                  pl.BlockSpec((1, LANES, n_rows), lambda j, i: (i, 0, 0)),
                  pl.BlockSpec((NSA_HEADS, 1, n_rows, tq), lambda j, i: (0, j, 0, 0))],
        out_specs=[pl.BlockSpec((1, tq, width), lambda j, i: (i, j, 0)),
                   pl.BlockSpec((1, tq, LANES), lambda j, i: (i, j, 0))],
        out_shape=[jax.ShapeDtypeStruct((b, s, width), BF16),
                   jax.ShapeDtypeStruct((b, s, LANES), BF16)],
        scratch_shapes=[pltpu.VMEM((n_blk // SUBLANES, SUBLANES, tq), jnp.int32),
                        pltpu.VMEM((tq // LANES, SUBLANES + n_rows, LANES), F32)],
        compiler_params=_params(2),
    )(nsa_q, k_cmp, vct, bias_c)


def _slc_kernel(q_ref, mb_ref, k_ref, e2_ref, vt_ref, bias_ref, o_ref, qs_scr, s_scr, tmax_scr, m_scr, acc_scr, *,
                n_near):
    t = FLASH_T
    i = pl.program_id(1)
    lo = _lane_lo((t, LANES))
    n_groups = NSA_HEADS // 2
    mb = mb_ref[0]
    for g in range(n_groups):
        qg = q_ref[0, :, g * LANES:(g + 1) * LANES]
        qs_scr[2 * g] = jnp.where(lo, qg, mb)
        qs_scr[2 * g + 1] = jnp.where(lo, mb, qg)
    _flash_init(m_scr, acc_scr)

    def qk_scores(j):
        start = pl.multiple_of(j * t, t)
        k_tile = k_ref[0, pl.ds(start, t), :]
        e_tile = e2_ref[pl.ds(start, t), :]
        k_sel = (jnp.where(lo, k_tile, e_tile), jnp.where(lo, e_tile, k_tile))
        return [functools.partial(lambda k, pos: _dot_nt(k, qs_scr[pos]), k_sel[pos % 2], pos)
                for pos in range(NSA_HEADS)]

    def bias_tile(pos, j):
        return bias_ref[pos, jnp.minimum(i - j, n_near)]

    def vt_slab(pos, j):
        kv = pos % 2
        return vt_ref[0, kv * VT_ROWS:(kv + 1) * VT_ROWS, pl.ds(pl.multiple_of(j * t, t), t)]

    _flash_pipeline(i, NSA_HEADS, qk_scores, bias_tile, vt_slab, s_scr, tmax_scr, m_scr, acc_scr)
    _flash_finish(o_ref, n_groups, acc_scr)


def _slc_attention(nsa_q, mask_bias, k2, e2, vt, bias):
    b, s, width = nsa_q.shape
    t = FLASH_T
    n_near = bias.shape[1] - 1
    return pl.pallas_call(
        functools.partial(_slc_kernel, n_near=n_near),
        grid=(b, s // t),
        in_specs=[pl.BlockSpec((1, t, width), lambda i, j: (i, j, 0)),
                  pl.BlockSpec((1, t, LANES), lambda i, j: (i, j, 0)),
                  pl.BlockSpec((1, s, LANES), lambda i, j: (i, 0, 0)),
                  pl.BlockSpec((s, LANES), lambda i, j: (0, 0)),
                  pl.BlockSpec((1, 2 * VT_ROWS, s), lambda i, j: (i, VT_SLC_BLOCK, 0)),
                  pl.BlockSpec(bias.shape, lambda i, j: (0, 0, 0, 0))],
        out_specs=pl.BlockSpec((1, t, width), lambda i, j: (i, j, 0)),
        out_shape=jax.ShapeDtypeStruct((b, s, width), BF16),
        scratch_shapes=[pltpu.VMEM((NSA_HEADS, t, LANES), BF16),
                        pltpu.VMEM((2, NSA_HEADS, t, t), F32),
                        pltpu.VMEM((2, NSA_HEADS, SUBLANES, t), F32),
                        pltpu.VMEM((NSA_HEADS, SUBLANES, t), F32),
                        pltpu.VMEM((NSA_HEADS, VT_ROWS, t), F32)],
        compiler_params=_params(2),
    )(nsa_q, mask_bias, k2, e2, vt, bias)


def _block_onehot(s_len):
    blk = np.arange(s_len)[:, None] // SLC_BLOCK
    lane = np.arange(LANES)[None, :] % HEAD_DIM
    return jnp.asarray((blk == lane).astype(np.float32), dtype=BF16)


def _mix_ffn_kernel(x_ref, mod_ref, swa_ref, fox_ref, cmp_ref, slc_ref, win_ref, misc_ref, expand_ref, gn_ref,
                    w_ref, post_ref, pre_ref, wg_ref, wu_ref, wd_ref, fpost_ref, o_ref):
    n_swa = SWA_HEADS * HEAD_DIM
    n_fox = FOX_HEADS * HEAD_DIM
    n_nsa = NSA_HEADS * HEAD_DIM
    gate = jax.nn.sigmoid(misc_ref[0])
    gate_hi = gate.astype(BF16)
    gate_lo = (gate - gate_hi.astype(F32)).astype(BF16)
    gates = (jnp.dot(gate_hi, expand_ref[...], preferred_element_type=F32)
             + jnp.dot(gate_lo, expand_ref[...], preferred_element_type=F32))
    o_nsa = (gates[:, 0:n_nsa] * cmp_ref[0].astype(F32) + gates[:, n_nsa:2 * n_nsa] * slc_ref[0].astype(F32)
             + gates[:, 2 * n_nsa:3 * n_nsa] * win_ref[0].astype(F32))
    a = _rms(swa_ref[0].astype(F32), gn_ref[:, 0:n_swa]).astype(BF16)
    b = _rms(fox_ref[0].astype(F32), gn_ref[:, n_swa:n_swa + n_fox]).astype(BF16)
    c = _rms(o_nsa, gn_ref[:, n_swa + n_fox:]).astype(BF16)
    y = (jnp.dot(a, w_ref[0:n_swa, :], preferred_element_type=F32)
         + jnp.dot(b, w_ref[n_swa:n_swa + n_fox, :], preferred_element_type=F32)
         + jnp.dot(c, w_ref[n_swa + n_fox:, :], preferred_element_type=F32))
    x = x_ref[0] + mod_ref[0, 2:3, :] * _rms(y, post_ref[...])

    h = (_rms(x, pre_ref[...]) * (1.0 + mod_ref[0, 4:5, :]) + mod_ref[0, 3:4, :]).astype(BF16)
    y = jnp.zeros(x.shape, F32)
    for chunk in range(wg_ref.shape[1] // FFN_CHUNK):
        cols = slice(chunk * FFN_CHUNK, (chunk + 1) * FFN_CHUNK)
        gate = jnp.dot(h, wg_ref[:, cols], preferred_element_type=F32)
        up = jnp.dot(h, wu_ref[:, cols], preferred_element_type=F32)
        act = (gate * jax.nn.sigmoid(gate) * up).astype(BF16)
        y = y + jnp.dot(act, wd_ref[cols, :], preferred_element_type=F32)
    o_ref[0] = x + mod_ref[0, 5:6, :] * _rms(y, fpost_ref[...])


def _gate_expansion():
    expand = np.zeros((LANES, 3 * NSA_HEADS * HEAD_DIM), np.float32)
    for branch in range(3):
        for p in range(NSA_HEADS):
            col = (branch * NSA_HEADS + p) * HEAD_DIM
            expand[GATE_LANE + 8 * branch + p, col:col + HEAD_DIM] = 1.0
    return jnp.asarray(expand, dtype=BF16)


def _mix_ffn(x, mod, o_swa, o_fox, o_cmp, o_slc, o_win, misc, gn, w, post, pre, wg, wu, wd, fpost, layer):
    b, s, d = x.shape
    expand = _gate_expansion()
    hidden = wg.shape[2]

    def rows(width):
        return pl.BlockSpec((1, ROW_TILE, width), lambda i, j: (i, j, 0))

    def whole(shape):
        return pl.BlockSpec(shape, lambda i, j: (0,) * len(shape), pipeline_mode=pl.Buffered(1))

    def of_layer(shape):
        return pl.BlockSpec((None,) + shape, lambda i, j: (layer,) + (0,) * len(shape),
                            pipeline_mode=pl.Buffered(1))

    vec = pl.BlockSpec((None, 1, d), lambda i, j: (layer, 0, 0))
    return pl.pallas_call(
        _mix_ffn_kernel,
        grid=(b, s // ROW_TILE),
        in_specs=[rows(d),
                  pl.BlockSpec((None, 1, ADA_CHUNKS, d), lambda i, j: (layer, i, 0, 0)),
                  rows(o_swa.shape[2]), rows(o_fox.shape[2]), rows(o_cmp.shape[2]), rows(o_slc.shape[2]),
                  rows(o_win.shape[2]), rows(LANES),
                  whole(expand.shape), vec, of_layer((d, d)), vec, vec,
                  of_layer((d, hidden)), of_layer((d, hidden)), of_layer((hidden, d)),
                  vec],
        out_specs=rows(d),
        out_shape=jax.ShapeDtypeStruct((b, s, d), F32),
        compiler_params=_params(2),
    )(x, mod, o_swa, o_fox, o_cmp, o_slc, o_win, misc, expand, gn, w, post, pre, wg, wu, wd, fpost)


def _forget_lanes():
    lanes, heads = [], []
    for h in range(FOX_HEADS):
        base = (h // 2) * LANES + (HEAD_DIM if h % 2 == 0 else 0)
        for j in range(KEY_BIAS_TERMS):
            lanes.append(base + j)
            heads.append(h)
    return np.array(lanes), np.array(heads)


def _in_proj_layout():
    d = HEAD_DIM
    o_qa, o_ka, o_va, o_qb, o_kb, o_vb, o_fb, o_qc = 0, 256, 384, 512, 768, 1024, 1280, 1284
    o_kc, o_vc, o_ksl, o_vsl, o_kw, o_vw, o_gc = 1796, 1924, 2052, 2180, 2308, 2436, 2564
    scale = LOG2E / math.sqrt(d)

    def head_cols(base, heads):
        return np.concatenate([np.arange(base + h * d, base + (h + 1) * d) for h in heads])

    def span(base, width):
        return np.arange(base, base + width)

    cols = [head_cols(o_qa, SWA_POS), span(o_ka, 128),
            span(o_qb, 256), span(o_kb, 256),
            head_cols(o_qc, NSA_POS),
            span(o_kc, 128), span(o_vc, 128),
            span(o_ksl, 128), span(o_kw, 128),
            span(o_vb, 256), span(o_va, 128), span(o_vsl, 128), span(o_vw, 128)]
    scales = [np.full(256, scale), np.ones(128), np.full(256, scale), np.ones(256), np.full(512, scale),
              np.ones(256), np.ones(256), np.ones(640)]
    lanes, heads = _forget_lanes()
    misc_cols = np.zeros(SEG_MISC[1] - SEG_MISC[0], np.int64)
    misc_scale = np.zeros(SEG_MISC[1] - SEG_MISC[0])
    misc_cols[lanes] = o_fb + heads
    misc_scale[lanes] = 1.0
    for branch in range(3):
        for p, h in enumerate(NSA_POS):
            misc_cols[GATE_LANE + 8 * branch + p] = o_gc + h * 3 + branch
            misc_scale[GATE_LANE + 8 * branch + p] = 1.0
    cols.append(misc_cols)
    scales.append(misc_scale)
    return np.concatenate(cols), np.concatenate(scales).astype(np.float32)


def _head_perm(pos):
    return np.concatenate([np.arange(h * HEAD_DIM, (h + 1) * HEAD_DIM) for h in pos])


def kernel(x, c, rel_bias, ada_w, ada_b, attn_pre_norm, attn_post_norm, ffn_pre_norm, ffn_post_norm, w_in,
           forget_bias, swa_sinks, cmp_pos, cmp_w1, cmp_w2, group_norm, w_out, ffn_w_gate, ffn_w_up, ffn_w_down):
    b, s, d = x.shape
    depth = w_in.shape[0]
    hidden = ffn_w_gate.shape[2]
    assert s % (2 * FLASH_T) == 0 and s // SLC_BLOCK <= HEAD_DIM and hidden % FFN_CHUNK == 0

    cols, scales = _in_proj_layout()
    n_main = SEG_MISC[0]
    breaks = np.flatnonzero(np.diff(cols[:n_main]) != 1) + 1
    runs = np.split(cols[:n_main], breaks)
    w_main = (jnp.concatenate([w_in[:, :, int(r[0]):int(r[-1]) + 1] for r in runs], axis=2)
              * scales[:n_main]).astype(BF16)
    used = np.flatnonzero(scales[n_main:] != 0)
    w_misc = jnp.zeros((depth, d, SEG_MISC[1] - SEG_MISC[0]), w_in.dtype).at[:, :, used].set(
        w_in[:, :, cols[n_main:][used]]).astype(BF16)
    lanes, heads = _forget_lanes()
    fbias_all = jnp.zeros((depth, 1, SEG_MISC[1] - SEG_MISC[0]), F32).at[:, 0, lanes].set(
        forget_bias[:, heads].astype(F32))
    swa_perm = _head_perm(SWA_POS)
    nsa_perm = _head_perm(NSA_POS)
    n_swa, n_fox = SWA_HEADS * HEAD_DIM, FOX_HEADS * HEAD_DIM
    mix_perm = np.concatenate([swa_perm, n_swa + np.arange(n_fox), n_swa + n_fox + nsa_perm])
    w_out_all = w_out[:, mix_perm, :].astype(BF16)
    wg_all = ffn_w_gate.astype(BF16)
    wu_all = ffn_w_up.astype(BF16)
    wd_all = ffn_w_down.astype(BF16)
    cmp_weights = _compress_weights(cmp_pos, cmp_w1, cmp_w2)

    def stacked(v):
        return v.astype(F32).reshape(depth, 1, v.shape[1])

    gn_all = stacked(group_norm[:, mix_perm])
    attn_pre, attn_post = stacked(attn_pre_norm), stacked(attn_post_norm)
    ffn_pre, ffn_post = stacked(ffn_pre_norm), stacked(ffn_post_norm)
    sinks_all = swa_sinks[:, np.array(SWA_POS)].astype(F32)

    tab_swa = rel_bias[:, np.array(SWA_POS)].astype(F32)
    tab_nsa = rel_bias[:, SWA_HEADS + np.array(NSA_POS)].astype(F32)
    bias_swa = _bias_table(tab_swa, _band_buckets_t(SWA_TILE, SWA_WINDOW))
    bias_win = _bias_table(tab_nsa, _band_buckets_t(WIN_TILE, NSA_WINDOW))
    bias_slc = _bias_table(tab_nsa, _toeplitz_buckets_t(FLASH_T, _near_tiles(FLASH_T)), subtract_last=True)
    n_rows = s // CMP_STRIDE
    bias_cmp = _bias_table(tab_nsa, _cmp_buckets_t(s, n_rows))
    e2 = _block_onehot(s)

    mod_all = _adaln(c.astype(F32), ada_w.astype(F32), ada_b.astype(F32)).reshape(depth, b, ADA_CHUNKS, d)

    for layer in range(depth):
        swa_qk, fox_qk, nsa_q, kc, vc, k2, misc, vt = _in_proj(x, mod_all, attn_pre, w_main, w_misc, layer)
        o_swa = _banded_attention(swa_qk, swa_qk, 2, vt, VT_SWA_BLOCK, bias_swa, sinks=sinks_all, layer=layer)
        o_fox = _fox_attention(fox_qk, _fox_key_terms(misc, fbias_all, layer), vt)
        k_cmp = _compress(kc, cmp_weights, layer, 0)
        v_cmp = _compress(vc, cmp_weights, layer, 1)
        o_cmp, mask_bias = _select(nsa_q, k_cmp, v_cmp, bias_cmp)
        o_slc = _slc_attention(nsa_q, mask_bias, k2, e2, vt, bias_slc)
        o_win = _banded_attention(nsa_q, k2, 1, vt, VT_WIN_BLOCK, bias_win)
        x = _mix_ffn(x, mod_all, o_swa, o_fox, o_cmp, o_slc, o_win, misc, gn_all, w_out_all, attn_post, ffn_pre,
                     wg_all, wu_all, wd_all, ffn_post, layer)
    return x
```

```python
import functools
import math

import numpy as np
import jax
import jax.numpy as jnp
from jax import lax
from jax.experimental import pallas as pl
from jax.experimental.pallas import tpu as pltpu

F32 = jnp.float32
BF16 = jnp.bfloat16
HIGHEST = lax.Precision.HIGHEST

LANES = 128
SUBLANES = 8
VMEM_LIMIT = 56 * 1024 * 1024

HEAD_DIM = 64
SWA_HEADS = 4
SWA_WINDOW = 128
FOX_HEADS = 4
NSA_HEADS = 8
CMP_LEN = 32
CMP_STRIDE = 16
CMP_HIDDEN = 2 * HEAD_DIM
SLC_BLOCK = 64
TOPK = 16
NSA_WINDOW = 512
REL_BUCKETS = 32
REL_MAX_DISTANCE = 1024
ZERO_BUCKET = -2
RMS_EPS = 1e-6
NEG = -1e30
FORCE = 1e30
ADA_CHUNKS = 6
LOG2E = math.log2(math.e)

SWA_POS = (0, 2, 1, 3)
NSA_POS = (0, 4, 1, 5, 2, 6, 3, 7)

SWA_TILE = 256
WIN_TILE = 256
FLASH_T = 256
FOX_TQ = 512
SEL_TQ = 512
IN_ROW_TILE = 1024
ROW_TILE = 512
FFN_CHUNK = 256
VT_ROWS = HEAD_DIM + 16
KEY_BIAS_TERMS = 3

SEG_SWA = (0, 384)
SEG_FOX = (384, 896)
SEG_NSAQ = (896, 1408)
SEG_KC = (1408, 1536)
SEG_VC = (1536, 1664)
SEG_K2 = (1664, 1920)
SEG_V = (1920, 2560)
SEG_MISC = (2560, 2816)
GATE_LANE = 8
VT_FOX_BLOCK, VT_SWA_BLOCK, VT_SLC_BLOCK, VT_WIN_BLOCK = 0, 2, 3, 4


def _params(n_grid, vmem=VMEM_LIMIT):
    return pltpu.CompilerParams(dimension_semantics=("parallel",) * n_grid, vmem_limit_bytes=vmem)


def _dot_nt(a, b):
    return lax.dot_general(a, b, (((1,), (1,)), ((), ())), preferred_element_type=F32)


def _lane_lo(shape):
    return lax.broadcasted_iota(jnp.int32, shape, len(shape) - 1) < HEAD_DIM


def _adaln_kernel(c_ref, w_ref, b_ref, o_ref):
    c = c_ref[...]
    act = c * jax.nn.sigmoid(c)
    o_ref[0] = jnp.dot(act, w_ref[0], precision=HIGHEST, preferred_element_type=F32) + b_ref[0]


def _adaln(c, ada_w, ada_b):
    depth, d, n = ada_w.shape
    b = c.shape[0]
    return pl.pallas_call(
        _adaln_kernel,
        grid=(depth, n // d),
        in_specs=[pl.BlockSpec((b, d), lambda l, j: (0, 0)),
                  pl.BlockSpec((1, d, d), lambda l, j: (l, 0, j)),
                  pl.BlockSpec((1, 1, d), lambda l, j: (l, 0, j))],
        out_specs=pl.BlockSpec((1, b, d), lambda l, j: (l, 0, j)),
        out_shape=jax.ShapeDtypeStruct((depth, b, n), F32),
        compiler_params=_params(2),
    )(c, ada_w, ada_b.reshape(depth, 1, n))


def _t5_bucket(dist):
    n = jnp.maximum(dist, 0)
    max_exact = REL_BUCKETS // 2
    nf = jnp.maximum(n, 1).astype(jnp.float32)
    large = max_exact + (jnp.log(nf / max_exact) / math.log(REL_MAX_DISTANCE / max_exact)
                         * (REL_BUCKETS - max_exact)).astype(jnp.int32)
    large = jnp.minimum(large, REL_BUCKETS - 1)
    return jnp.where(n < max_exact, n, large)


def _bias_table_kernel(tab_ref, bucket_ref, o_ref, *, subtract_last):
    n_heads = o_ref.shape[0]
    values = [[(tab_ref[k, h] - (tab_ref[REL_BUCKETS - 1, h] if subtract_last else 0.0)) * LOG2E
               for h in range(n_heads)] for k in range(REL_BUCKETS)]

    def rows(chunk, carry):
        r0 = pl.multiple_of(chunk * SUBLANES, SUBLANES)
        bucket = bucket_ref[0, pl.ds(r0, SUBLANES), :]
        accs = [jnp.where(bucket == ZERO_BUCKET, 0.0, NEG) for _ in range(n_heads)]
        for k in range(REL_BUCKETS):
            hit = bucket == k
            for h in range(n_heads):
                accs[h] = jnp.where(hit, values[k][h], accs[h])
        for h in range(n_heads):
            o_ref[h, 0, pl.ds(r0, SUBLANES), :] = accs[h]
        return carry

    lax.fori_loop(0, bucket_ref.shape[1] // SUBLANES, rows, 0)


def _bias_table(table, bucket, subtract_last=False):
    n_heads = table.shape[1]
    n, r, c = bucket.shape
    return pl.pallas_call(
        functools.partial(_bias_table_kernel, subtract_last=subtract_last),
        grid=(n,),
        in_specs=[pl.BlockSpec(memory_space=pltpu.SMEM),
                  pl.BlockSpec((1, r, c), lambda i: (i, 0, 0))],
        out_specs=pl.BlockSpec((n_heads, 1, r, c), lambda i: (0, i, 0, 0)),
        out_shape=jax.ShapeDtypeStruct((n_heads, n, r, c), F32),
        compiler_params=_params(1),
    )(table, bucket)


def _band_buckets_t(tile, window):
    n_back = -(-(window - 1) // tile)
    t = jnp.arange(n_back + 1)[:, None, None]
    key = jnp.arange(tile)[None, :, None]
    query = jnp.arange(tile)[None, None, :]
    dist = query + (n_back - t) * tile - key
    return jnp.where((dist >= 0) & (dist < window), _t5_bucket(dist), -1).astype(jnp.int32)


def _toeplitz_buckets_t(tile, n_tiles):
    m = jnp.arange(n_tiles)[:, None, None]
    key = jnp.arange(tile)[None, :, None]
    query = jnp.arange(tile)[None, None, :]
    dist = m * tile + query - key
    near = jnp.where(dist >= 0, _t5_bucket(dist), -1).astype(jnp.int32)
    return jnp.concatenate([near, jnp.full((1, tile, tile), ZERO_BUCKET, jnp.int32)])


def _cmp_buckets_t(s_len, n_rows):
    n_c = n_rows - 1
    tile = jnp.arange(s_len // SEL_TQ)[:, None, None]
    n = jnp.arange(n_rows)[None, :, None]
    t = tile * SEL_TQ + jnp.arange(SEL_TQ)[None, None, :]
    dist = t - (n * CMP_STRIDE + CMP_LEN - 1)
    return jnp.where((dist >= 0) & (n < n_c), _t5_bucket(dist), -1).astype(jnp.int32)


def _near_tiles(tile):
    max_exact = REL_BUCKETS // 2
    first_const = math.ceil(max_exact * (REL_MAX_DISTANCE / max_exact) ** ((max_exact - 1) / max_exact)) + 1
    m = 1
    while m * tile - (tile - 1) < first_const:
        m += 1
    return m


def _rms(x, gain):
    return x * lax.rsqrt(jnp.mean(x * x, axis=-1, keepdims=True) + RMS_EPS) * gain


def _in_proj_kernel(x_ref, mod_ref, gain_ref, w_ref, wm_ref, swa_ref, fox_ref, nsaq_ref, kc_ref, vc_ref, k2_ref,
                    misc_ref, vt_ref, pack_scr):
    x = x_ref[0]
    h = _rms(x, gain_ref[...]) * (1.0 + mod_ref[0, 1:2, :]) + mod_ref[0, 0:1, :]
    hb = h.astype(BF16)

    def seg(bounds):
        if bounds == SEG_MISC:
            return jnp.dot(hb, wm_ref[...], preferred_element_type=F32)
        return jnp.dot(hb, w_ref[:, bounds[0]:bounds[1]], preferred_element_type=F32)

    swa_ref[0] = seg(SEG_SWA).astype(BF16)
    fox_ref[0] = seg(SEG_FOX).astype(BF16)
    nsaq_ref[0] = seg(SEG_NSAQ).astype(BF16)
    for slot, (bounds, out_ref) in enumerate(((SEG_KC, kc_ref), (SEG_VC, vc_ref))):
        pack_scr[slot] = seg(bounds)
        for tok in range(CMP_STRIDE):
            out_ref[0, :, tok * LANES:(tok + 1) * LANES] = pack_scr[
                slot, pl.ds(tok, x.shape[0] // CMP_STRIDE, stride=CMP_STRIDE), :].astype(BF16)
    k2_ref[0] = seg(SEG_K2).astype(BF16)
    misc_ref[0] = seg(SEG_MISC)

    rows = x.shape[0]
    extra_row = lax.broadcasted_iota(jnp.int32, (VT_ROWS - HEAD_DIM, rows), 0)
    extra = jnp.where(extra_row == 0, 1.0, 0.0).astype(BF16)
    values = seg(SEG_V)
    for c in range(values.shape[1] // LANES):
        vt = values[:, c * LANES:(c + 1) * LANES].T.astype(BF16)
        for half in range(2):
            base = (2 * c + half) * VT_ROWS
            vt_ref[0, base:base + HEAD_DIM, :] = vt[half * HEAD_DIM:(half + 1) * HEAD_DIM, :]
            vt_ref[0, base + HEAD_DIM:base + VT_ROWS, :] = extra


def _in_proj(x, mod, gain, w, w_misc, layer):
    b, s, d = x.shape
    n = w.shape[2]

    tile = IN_ROW_TILE

    def rows(width, dtype):
        return (pl.BlockSpec((1, tile, width), lambda i, j: (i, j, 0)), jax.ShapeDtypeStruct((b, s, width), dtype))

    packed = (pl.BlockSpec((1, tile // CMP_STRIDE, CMP_STRIDE * LANES), lambda i, j: (i, j, 0)),
              jax.ShapeDtypeStruct((b, s // CMP_STRIDE, CMP_STRIDE * LANES), BF16))
    vt_rows = (SEG_V[1] - SEG_V[0]) // HEAD_DIM * VT_ROWS
    outs = [rows(SEG_SWA[1] - SEG_SWA[0], BF16), rows(SEG_FOX[1] - SEG_FOX[0], BF16),
            rows(SEG_NSAQ[1] - SEG_NSAQ[0], BF16), packed, packed, rows(SEG_K2[1] - SEG_K2[0], BF16),
            rows(SEG_MISC[1] - SEG_MISC[0], F32),
            (pl.BlockSpec((1, vt_rows, tile), lambda i, j: (i, 0, j)),
             jax.ShapeDtypeStruct((b, vt_rows, s), BF16))]
    return pl.pallas_call(
        _in_proj_kernel,
        grid=(b, s // tile),
        in_specs=[pl.BlockSpec((1, tile, d), lambda i, j: (i, j, 0)),
                  pl.BlockSpec((None, 1, ADA_CHUNKS, d), lambda i, j: (layer, i, 0, 0)),
                  pl.BlockSpec((None, 1, d), lambda i, j: (layer, 0, 0)),
                  pl.BlockSpec((None, d, n), lambda i, j: (layer, 0, 0)),
                  pl.BlockSpec((None, d, w_misc.shape[2]), lambda i, j: (layer, 0, 0))],
        out_specs=[spec for spec, _ in outs],
        out_shape=[shape for _, shape in outs],
        scratch_shapes=[pltpu.VMEM((2, tile, LANES), F32)],
        compiler_params=_params(2),
    )(x, mod, gain, w, w_misc)


def _banded_kernel(*refs, n_back, n_groups, has_sink, t, layer):
    if has_sink:
        sink_ref, q_ref, k_ref, vt_ref, bias_ref, o_ref = refs
    else:
        q_ref, k_ref, vt_ref, bias_ref, o_ref = refs
    i = pl.program_id(1)
    lo = _lane_lo((t, LANES))
    n_tiles = n_back + 1

    def run(all_valid):
        starts = [pl.multiple_of(jnp.maximum(i - n_back + tt, 0) * t, t) for tt in range(n_tiles)]
        k_tiles = [k_ref[0, pl.ds(start, t), :] for start in starts]

        def scores(g):
            qg = q_ref[0, :, g * LANES:(g + 1) * LANES]
            zero = jnp.zeros_like(qg)
            qms = (jnp.where(lo, qg, zero), jnp.where(lo, zero, qg))
            return [[bias_ref[2 * g + half, tt] + _dot_nt(k_tiles[tt], qms[half]) for tt in range(n_tiles)]
                    for half in range(2)]

        def softmax_pv(g, sts):
            pair = []
            for half in range(2):
                tiles = sts[half]
                if not all_valid:
                    tiles = [jnp.where(i - n_back + tt >= 0, st, NEG) if tt < n_back else st
                             for tt, st in enumerate(tiles)]
                m = None
                for st in tiles:
                    part = st.reshape(t // SUBLANES, SUBLANES, t).max(axis=0)
                    m = part if m is None else jnp.maximum(m, part)
                m = _all_sublanes(m, jnp.maximum)
                if has_sink:
                    sink = sink_ref[layer, 2 * g + half] * LOG2E
                    m = jnp.maximum(m, sink)
                acc = None
                for tt, st in enumerate(tiles):
                    p = jnp.exp2((st.reshape(t // SUBLANES, SUBLANES, t) - m[None]).reshape(t, t).astype(BF16))
                    part = jnp.dot(vt_ref[0, half * VT_ROWS:(half + 1) * VT_ROWS, pl.ds(starts[tt], t)], p,
                                   preferred_element_type=F32)
                    acc = part if acc is None else acc + part
                denom = _all_sublanes(acc[HEAD_DIM:HEAD_DIM + SUBLANES, :], jnp.add)
                if has_sink:
                    denom = denom + jnp.exp2(sink - m)
                out = acc[0:HEAD_DIM, :].reshape(HEAD_DIM // SUBLANES, SUBLANES, t) / denom[None]
                pair.append(out.reshape(HEAD_DIM, t))
            o_ref[0, :, g * LANES:(g + 1) * LANES] = jnp.concatenate(pair, axis=0).T.astype(o_ref.dtype)

        pending = scores(0)
        for g in range(n_groups):
            current = pending
            if g + 1 < n_groups:
                pending = scores(g + 1)
            softmax_pv(g, current)

    @pl.when(i >= n_back)
    def _():
        run(True)

    @pl.when(i < n_back)
    def _():
        run(False)


def _banded_attention(q_arr, k_arr, k_blk, vt, vt_blk, bias, sinks=None, layer=0):
    b, s, _ = q_arr.shape
    n_pos, n_tiles, t = bias.shape[0], bias.shape[1], bias.shape[2]
    width = n_pos * HEAD_DIM
    in_specs = [pl.BlockSpec((1, t, width), lambda i, j: (i, j, 0)),
                pl.BlockSpec((1, s, LANES), lambda i, j: (i, 0, k_blk)),
                pl.BlockSpec((1, 2 * VT_ROWS, s), lambda i, j: (i, vt_blk, 0)),
                pl.BlockSpec(bias.shape, lambda i, j: (0, 0, 0, 0))]
    args = [q_arr, k_arr, vt, bias]
    if sinks is not None:
        in_specs = [pl.BlockSpec(memory_space=pltpu.SMEM)] + in_specs
        args = [sinks] + args
    return pl.pallas_call(
        functools.partial(_banded_kernel, n_back=n_tiles - 1, n_groups=n_pos // 2, has_sink=sinks is not None, t=t,
                          layer=layer),
        grid=(b, s // t),
        in_specs=in_specs,
        out_specs=pl.BlockSpec((1, t, width), lambda i, j: (i, j, 0)),
        out_shape=jax.ShapeDtypeStruct((b, s, width), BF16),
        compiler_params=_params(2),
    )(*args)


def _all_sublanes(x, op):
    for shift in (4, 2, 1):
        x = op(x, pltpu.roll(x, shift, 0))
    return x


def _flash_init(m_scr, acc_scr):
    m_scr[...] = jnp.full(m_scr.shape, NEG, F32)
    acc_scr[...] = jnp.zeros(acc_scr.shape, F32)


def _flash_update(h, st_ref, tile_max, vt_h, m_scr, acc_scr):
    tk, tq = st_ref.shape
    m_prev = m_scr[h]
    m_new = _all_sublanes(jnp.maximum(m_prev, tile_max), jnp.maximum)
    alpha = jnp.exp2(m_prev - m_new)
    p = jnp.exp2((st_ref[...].reshape(tk // SUBLANES, SUBLANES, tq) - m_new[None]).reshape(tk, tq).astype(BF16))
    acc = acc_scr[h].reshape(VT_ROWS // SUBLANES, SUBLANES, tq) * alpha[None]
    acc_scr[h] = acc.reshape(VT_ROWS, tq) + jnp.dot(vt_h, p, preferred_element_type=F32)
    m_scr[h] = m_new


def _flash_finish(o_ref, n_groups, acc_scr):
    tq = acc_scr.shape[2]
    for g in range(n_groups):
        pair = []
        for h in (2 * g, 2 * g + 1):
            denom = _all_sublanes(acc_scr[h, HEAD_DIM:HEAD_DIM + SUBLANES, :], jnp.add)
            out = acc_scr[h, 0:HEAD_DIM, :].reshape(HEAD_DIM // SUBLANES, SUBLANES, tq) / denom[None]
            pair.append(out.reshape(HEAD_DIM, tq))
        o_ref[0, :, g * LANES:(g + 1) * LANES] = jnp.concatenate(pair, axis=0).T.astype(o_ref.dtype)


def _flash_pipeline(i, n_heads, qk_scores, bias_tile, vt_slab, s_scr, tmax_scr, m_scr, acc_scr):
    def qk_head(thunk, h, j, slot):
        st = thunk() + bias_tile(h, j)
        s_scr[slot, h] = st
        tmax_scr[slot, h] = st.reshape(st.shape[0] // SUBLANES, SUBLANES, st.shape[1]).max(axis=0)

    def softmax_head(h, j, slot):
        _flash_update(h, s_scr.at[slot, h], tmax_scr[slot, h], vt_slab(h, j), m_scr, acc_scr)

    def softmax_all(j, slot):
        for h in range(n_heads):
            softmax_head(h, j, slot)

    def stage(j_qk, slot_qk, j_sm, slot_sm):
        thunks = qk_scores(j_qk)
        for h in range(n_heads):
            qk_head(thunks[h], h, j_qk, slot_qk)
            softmax_head(h, j_sm, slot_sm)

    for h, thunk in enumerate(qk_scores(0)):
        qk_head(thunk, h, 0, 0)

    def body(trip, carry):
        j = 2 * trip
        stage(j + 1, 1, j, 0)
        stage(j + 2, 0, j + 1, 1)
        return carry

    lax.fori_loop(0, i // 2, body, 0)
    last = 2 * (i // 2)

    @pl.when(i % 2 == 0)
    def _():
        softmax_all(last, 0)

    @pl.when(i % 2 == 1)
    def _():
        stage(last + 1, 1, last, 0)
        softmax_all(last + 1, 1)


def _fox_aug_kernel(misc_ref, fbias_ref, tri_ref, o_ref):
    s_len, width = misc_ref.shape[1], misc_ref.shape[2]
    term = lax.broadcasted_iota(jnp.int32, (LANES, width), 1) % HEAD_DIM
    carry = jnp.zeros((1, width), F32)
    for c in range(s_len // LANES):
        z = misc_ref[0, c * LANES:(c + 1) * LANES, :] + fbias_ref[...]
        log_f = jnp.minimum(z, 0.0) - jnp.log1p(jnp.exp(-jnp.abs(z)))
        cum = jnp.dot(tri_ref[...], log_f, precision=HIGHEST, preferred_element_type=F32) + carry
        carry = cum[LANES - 1:LANES, :]
        x = cum * (-LOG2E)
        hi = x.astype(BF16).astype(F32)
        rest = x - hi
        mid = rest.astype(BF16).astype(F32)
        low = rest - mid
        out = jnp.where(term == 0, hi, jnp.where(term == 1, mid, jnp.where(term == 2, low, 0.0)))
        o_ref[0, c * LANES:(c + 1) * LANES, :] = out.astype(BF16)


def _fox_key_terms(misc, fbias, layer):
    b, s, width = misc.shape
    tri = jnp.asarray(np.tril(np.ones((LANES, LANES), np.float32)))
    return pl.pallas_call(
        _fox_aug_kernel,
        grid=(b,),
        in_specs=[pl.BlockSpec((1, s, width), lambda i: (i, 0, 0)),
                  pl.BlockSpec((None, 1, width), lambda i: (layer, 0, 0)),
                  pl.BlockSpec((LANES, LANES), lambda i: (0, 0))],
        out_specs=pl.BlockSpec((1, s, width), lambda i: (i, 0, 0)),
        out_shape=jax.ShapeDtypeStruct((b, s, width), BF16),
        compiler_params=_params(1),
    )(misc, fbias, tri)


def _fox_kernel(q_ref, k_ref, aug_ref, vt_ref, mask_ref, o_ref, qs_scr, s_scr, tmax_scr, m_scr, acc_scr):
    t = FLASH_T
    tq = q_ref.shape[1]
    ratio = tq // t
    i = pl.program_id(1)
    lo = _lane_lo((t, LANES))
    lo_q = _lane_lo((tq, LANES))
    lane = lax.broadcasted_iota(jnp.int32, (tq, LANES), 1)
    ones = jnp.where(lane % HEAD_DIM < KEY_BIAS_TERMS, 1.0, 0.0).astype(BF16)
    n_groups = FOX_HEADS // 2
    for g in range(n_groups):
        qg = q_ref[0, :, g * LANES:(g + 1) * LANES]
        qs_scr[2 * g] = jnp.where(lo_q, qg, ones)
        qs_scr[2 * g + 1] = jnp.where(lo_q, ones, qg)
    _flash_init(m_scr, acc_scr)

    def qk_scores(j):
        start = pl.multiple_of(j * t, t)
        scores = []
        for g in range(n_groups):
            k_tile = k_ref[0, pl.ds(start, t), g * LANES:(g + 1) * LANES]
            a_tile = aug_ref[0, pl.ds(start, t), g * LANES:(g + 1) * LANES]
            k_sel = (jnp.where(lo, k_tile, a_tile), jnp.where(lo, a_tile, k_tile))
            for half in range(2):
                scores.append(functools.partial(lambda k, h: _dot_nt(k, qs_scr[h]), k_sel[half], 2 * g + half))
        return scores

    def bias_tile(h, j):
        return mask_ref[jnp.clip(j - ratio * i + 1, 0, ratio)]

    def vt_slab(h, j):
        return vt_ref[0, h * VT_ROWS:(h + 1) * VT_ROWS, pl.ds(pl.multiple_of(j * t, t), t)]

    _flash_pipeline(ratio * i + ratio - 1, FOX_HEADS, qk_scores, bias_tile, vt_slab, s_scr, tmax_scr, m_scr, acc_scr)
    _flash_finish(o_ref, n_groups, acc_scr)


def _fox_attention(fox_qk, key_terms, vt):
    b, s, _ = fox_qk.shape
    width = FOX_HEADS * HEAD_DIM
    t, tq = FLASH_T, FOX_TQ
    ratio = tq // t
    key = np.arange(t)[:, None]
    query = np.arange(tq)[None, :]
    masks = [np.zeros((t, tq))] + [np.where(r * t + key <= query, 0.0, NEG) for r in range(ratio)]
    masks = jnp.asarray(np.stack(masks).astype(np.float32))
    return pl.pallas_call(
        _fox_kernel,
        grid=(b, s // tq),
        in_specs=[pl.BlockSpec((1, tq, width), lambda i, j: (i, j, 0)),
                  pl.BlockSpec((1, s, width), lambda i, j: (i, 0, 1)),
                  pl.BlockSpec((1, s, width), lambda i, j: (i, 0, 0)),
                  pl.BlockSpec((1, FOX_HEADS * VT_ROWS, s), lambda i, j: (i, VT_FOX_BLOCK, 0)),
                  pl.BlockSpec(masks.shape, lambda i, j: (0, 0, 0))],
        out_specs=pl.BlockSpec((1, tq, width), lambda i, j: (i, j, 0)),
        out_shape=jax.ShapeDtypeStruct((b, s, width), BF16),
        scratch_shapes=[pltpu.VMEM((FOX_HEADS, tq, LANES), BF16),
                        pltpu.VMEM((2, FOX_HEADS, t, tq), F32),
                        pltpu.VMEM((2, FOX_HEADS, SUBLANES, tq), F32),
                        pltpu.VMEM((FOX_HEADS, SUBLANES, tq), F32),
                        pltpu.VMEM((FOX_HEADS, VT_ROWS, tq), F32)],
        compiler_params=_params(2),
    )(fox_qk, fox_qk, key_terms, vt, masks)


def _split3(x):
    hi = x.astype(BF16)
    rest = x - hi.astype(F32)
    mid = rest.astype(BF16)
    low = (rest - mid.astype(F32)).astype(BF16)
    return hi, mid, low


def _compress_kernel(x_ref, pe_ref, w1a_ref, w1b_ref, w2_ref, o_ref):
    x = x_ref[0]
    n_rows = x.shape[0]

    def mm3(lhs, w_ref):
        return sum(jnp.dot(lhs, w_ref[piece], preferred_element_type=F32) for piece in range(3))

    first = mm3(x, w1a_ref)
    second = mm3(x, w1b_ref)
    pe_term = sum(mm3(piece, w1a_ref) for piece in _split3(pe_ref[0])) \
        + sum(mm3(piece, w1b_ref) for piece in _split3(pe_ref[1]))
    pre = first + pltpu.roll(second, n_rows - 1, 0) + pe_term[0:1, :]
    hid = 0.5 * pre * (1.0 + jnp.tanh(math.sqrt(2.0 / math.pi) * (pre + 0.044715 * (pre * pre * pre))))
    o_ref[0] = jnp.dot(hid, w2_ref[...], precision=HIGHEST, preferred_element_type=F32)


def _compress_weights(cmp_pos, cmp_w1, cmp_w2):
    depth = cmp_w1.shape[0]
    half = CMP_LEN // 2
    feat = CMP_STRIDE * LANES
    eye = jnp.eye(2, dtype=F32)
    w1 = cmp_w1.astype(F32).reshape(depth, 2, CMP_LEN, HEAD_DIM, CMP_HIDDEN)

    def pieces(w):
        big = jnp.einsum('nwldj,hg->nwlhdgj', w, eye).reshape(depth, 2, feat, 2 * CMP_HIDDEN)
        return jnp.stack(_split3(big), axis=2)

    w2 = jnp.einsum('nwjd,hg->nwhjgd', cmp_w2.astype(F32), eye).reshape(depth, 2, 2 * CMP_HIDDEN, LANES)
    w2 = jnp.concatenate([w2, jnp.roll(w2, HEAD_DIM, axis=3)], axis=3)
    pe = jnp.broadcast_to(cmp_pos.astype(F32).reshape(depth, 2, 2, half, 1, HEAD_DIM),
                          (depth, 2, 2, half, 2, HEAD_DIM))
    pe = jnp.broadcast_to(pe.reshape(depth, 2, 2, 1, feat), (depth, 2, 2, 8, feat))
    return pe, pieces(w1[:, :, :half]), pieces(w1[:, :, half:]), w2


def _compress(x, weights, layer, branch):
    b, n_rows, feat = x.shape
    pe, w1a, w1b, w2 = weights

    def picked(shape):
        return pl.BlockSpec((None, None) + shape, lambda i: (layer, branch) + (0,) * len(shape))

    return pl.pallas_call(
        _compress_kernel,
        grid=(b,),
        in_specs=[pl.BlockSpec((1, n_rows, feat), lambda i: (i, 0, 0)),
                  picked((2, 8, feat)),
                  picked((3, feat, 2 * CMP_HIDDEN)),
                  picked((3, feat, 2 * CMP_HIDDEN)),
                  picked((2 * CMP_HIDDEN, 2 * LANES))],
        out_specs=pl.BlockSpec((1, n_rows, 2 * LANES), lambda i: (i, 0, 0)),
        out_shape=jax.ShapeDtypeStruct((b, n_rows, 2 * LANES), F32),
        compiler_params=_params(1),
    )(x, pe, w1a, w1b, w2)


def _select_kernel(q_ref, kc_ref, vct_ref, bias_ref, o_ref, mb_ref, count_scr, psum_scr):
    tq = SEL_TQ
    i = pl.program_id(0)
    lo = _lane_lo((tq, LANES))
    n_rows = kc_ref.shape[1]
    lo_k = _lane_lo((n_rows, LANES))
    n_grp = n_rows // SUBLANES

    k_own = kc_ref[0, :, 0:LANES]
    k_swapped = kc_ref[0, :, LANES:2 * LANES]
    hi = k_own.astype(BF16)
    low = (k_swapped - k_swapped.astype(BF16).astype(F32)).astype(BF16)
    k_sel = (jnp.where(lo_k, hi, low), jnp.where(lo_k, low, hi))

    def scores(g):
        qg = q_ref[0, :, g * LANES:(g + 1) * LANES]
        swapped = pltpu.roll(qg.astype(F32), HEAD_DIM, 1).astype(BF16)
        q_dup = (jnp.where(lo, qg, swapped), jnp.where(lo, swapped, qg))
        return [bias_ref[2 * g + half, 0] + _dot_nt(k_sel[half], q_dup[half]) for half in range(2)]

    query = i * tq + lax.broadcasted_iota(jnp.int32, (SUBLANES, tq), 1)
    has_keys = query >= CMP_LEN - 1
    p_sum = [None, None]

    def softmax_pv(g, sts):
        pair = []
        for half in range(2):
            s3 = sts[half].reshape(n_grp, SUBLANES, tq)
            m = _all_sublanes(s3.max(axis=0), jnp.maximum)
            e = jnp.exp2(s3 - m[None])
            inv = jnp.where(has_keys, 1.0 / _all_sublanes(e.sum(axis=0), jnp.add), 0.0)
            p = e * inv[None]
            p_sum[half] = p if p_sum[half] is None else p_sum[half] + p
            pair.append(jnp.dot(vct_ref[0, half * HEAD_DIM:(half + 1) * HEAD_DIM, :],
                                p.reshape(n_rows, tq).astype(BF16), preferred_element_type=F32))
        o_ref[0, :, g * LANES:(g + 1) * LANES] = jnp.concatenate(pair, axis=0).T.astype(o_ref.dtype)

    n_groups = NSA_HEADS // 2
    pending = scores(0)
    for g in range(n_groups):
        current = pending
        if g + 1 < n_groups:
            pending = scores(g + 1)
        softmax_pv(g, current)

    n_blk = HEAD_DIM
    blk_grp = n_blk // SUBLANES
    sub = lax.broadcasted_iota(jnp.int32, (SUBLANES, tq), 0)
    q_blk = query // SLC_BLOCK
    kind = []
    for r in range(blk_grp):
        blk = sub + r * SUBLANES
        behind = q_blk - blk
        forced = jnp.where(blk == 0, 1, 0) + jnp.where(behind == 0, 1, 0) + jnp.where(behind == 1, 1, 0)
        kind.append(jnp.where(behind < 0, 2, jnp.minimum(forced, 1)))
    masks = []
    per_blk = SLC_BLOCK // CMP_STRIDE
    n_real = n_rows // per_blk
    n_lane_chunks = tq // LANES
    psum_scr[:, 0:SUBLANES, :] = jnp.zeros((n_lane_chunks, SUBLANES, LANES), F32)
    for half in (1, 0):
        p_rows = p_sum[half].reshape(n_rows, tq)
        for c in range(n_lane_chunks):
            psum_scr[c, SUBLANES:SUBLANES + n_rows, :] = p_rows[:, c * LANES:(c + 1) * LANES]

        def every_fourth(offset):
            return jnp.concatenate([psum_scr[c, pl.ds(SUBLANES + offset, n_real, stride=per_blk), :]
                                    for c in range(n_lane_chunks)], axis=1)

        imp = (0.5 * (every_fourth(-1) + every_fourth(3))
               + (every_fourth(0) + every_fourth(1) + every_fourth(2)))
        if n_real < n_blk:
            imp = jnp.concatenate([imp, jnp.zeros((n_blk - n_real, tq), F32)], axis=0)
        rows = [jnp.where(kind[r] == 2, NEG, jnp.where(kind[r] == 1, FORCE, imp[r * SUBLANES:(r + 1) * SUBLANES, :]))
                for r in range(blk_grp)]
        count_scr[...] = jnp.zeros(count_scr.shape, jnp.int32)
        for r_other in range(blk_grp):
            @pl.when(r_other * SUBLANES * SLC_BLOCK < (i + 1) * tq)
            def _(r_other=r_other, rows=rows):
                counts = [None] * blk_grp
                for s_other in range(SUBLANES):
                    row = jnp.broadcast_to(rows[r_other][s_other:s_other + 1, :], (SUBLANES, tq))
                    for r in range(blk_grp):
                        if r > r_other:
                            beats = jnp.where(row >= rows[r], 1, 0)
                        elif r < r_other:
                            beats = jnp.where(row > rows[r], 1, 0)
                        else:
                            beats = jnp.where(sub > s_other, jnp.where(row >= rows[r], 1, 0),
                                              jnp.where(row > rows[r], 1, 0))
                        counts[r] = beats if counts[r] is None else counts[r] + beats
                for r in range(blk_grp):
                    count_scr[r] = count_scr[r] + counts[r]
        masks.extend(jnp.where(count_scr[r] < TOPK, 0.0, NEG) for r in range(blk_grp))
    for c in range(tq // LANES):
        mb_ref[0, c * LANES:(c + 1) * LANES, :] = jnp.concatenate(
            [mk[:, c * LANES:(c + 1) * LANES] for mk in masks], axis=0).T.astype(BF16)


def _select(nsa_q, k_cmp, v_cmp, bias_c):
    b, s, width = nsa_q.shape
    n_rows = k_cmp.shape[1]
    n_blk = HEAD_DIM
    assert CMP_LEN == 2 * CMP_STRIDE and SLC_BLOCK == 4 * CMP_STRIDE
    tq = SEL_TQ
    vct = v_cmp[:, :, 0:LANES].transpose(0, 2, 1).astype(BF16)
    return pl.pallas_call(
        _select_kernel,
        grid=(s // tq, b),
        in_specs=[pl.BlockSpec((1, tq, width), lambda j, i: (i, j, 0)),
                  pl.BlockSpec((1, n_rows, 2 * LANES), lambda j, i: (i, 0, 0)),
                  pl.BlockSpec((1, LANES, n_rows), lambda j, i: (i, 0, 0)),
                  pl.BlockSpec((NSA_HEADS, 1, n_rows, tq), lambda j, i: (0, j, 0, 0))],
        out_specs=[pl.BlockSpec((1, tq, width), lambda j, i: (i, j, 0)),
                   pl.BlockSpec((1, tq, LANES), lambda j, i: (i, j, 0))],
        out_shape=[jax.ShapeDtypeStruct((b, s, width), BF16),
                   jax.ShapeDtypeStruct((b, s, LANES), BF16)],
        scratch_shapes=[pltpu.VMEM((n_blk // SUBLANES, SUBLANES, tq), jnp.int32),
                        pltpu.VMEM((tq // LANES, SUBLANES + n_rows, LANES), F32)],
        compiler_params=_params(2),
    )(nsa_q, k_cmp, vct, bias_c)


def _slc_kernel(q_ref, mb_ref, k_ref, e2_ref, vt_ref, bias_ref, o_ref, qs_scr, s_scr, tmax_scr, m_scr, acc_scr, *,
                n_near):
    t = FLASH_T
    i = pl.program_id(1)
    lo = _lane_lo((t, LANES))
    n_groups = NSA_HEADS // 2
    mb = mb_ref[0]
    for g in range(n_groups):
        qg = q_ref[0, :, g * LANES:(g + 1) * LANES]
        qs_scr[2 * g] = jnp.where(lo, qg, mb)
        qs_scr[2 * g + 1] = jnp.where(lo, mb, qg)
    _flash_init(m_scr, acc_scr)

    def qk_scores(j):
        start = pl.multiple_of(j * t, t)
        k_tile = k_ref[0, pl.ds(start, t), :]
        e_tile = e2_ref[pl.ds(start, t), :]
        k_sel = (jnp.where(lo, k_tile, e_tile), jnp.where(lo, e_tile, k_tile))
        return [functools.partial(lambda k, pos: _dot_nt(k, qs_scr[pos]), k_sel[pos % 2], pos)
                for pos in range(NSA_HEADS)]

    def bias_tile(pos, j):
        return bias_ref[pos, jnp.minimum(i - j, n_near)]

    def vt_slab(pos, j):
        kv = pos % 2
        return vt_ref[0, kv * VT_ROWS:(kv + 1) * VT_ROWS, pl.ds(pl.multiple_of(j * t, t), t)]

    _flash_pipeline(i, NSA_HEADS, qk_scores, bias_tile, vt_slab, s_scr, tmax_scr, m_scr, acc_scr)
    _flash_finish(o_ref, n_groups, acc_scr)


def _slc_attention(nsa_q, mask_bias, k2, e2, vt, bias):
    b, s, width = nsa_q.shape
    t = FLASH_T
    n_near = bias.shape[1] - 1
    return pl.pallas_call(
        functools.partial(_slc_kernel, n_near=n_near),
        grid=(b, s // t),
        in_specs=[pl.BlockSpec((1, t, width), lambda i, j: (i, j, 0)),
                  pl.BlockSpec((1, t, LANES), lambda i, j: (i, j, 0)),
                  pl.BlockSpec((1, s, LANES), lambda i, j: (i, 0, 0)),
                  pl.BlockSpec((s, LANES), lambda i, j: (0, 0)),
                  pl.BlockSpec((1, 2 * VT_ROWS, s), lambda i, j: (i, VT_SLC_BLOCK, 0)),
                  pl.BlockSpec(bias.shape, lambda i, j: (0, 0, 0, 0))],
        out_specs=pl.BlockSpec((1, t, width), lambda i, j: (i, j, 0)),
        out_shape=jax.ShapeDtypeStruct((b, s, width), BF16),
        scratch_shapes=[pltpu.VMEM((NSA_HEADS, t, LANES), BF16),
                        pltpu.VMEM((2, NSA_HEADS, t, t), F32),
                        pltpu.VMEM((2, NSA_HEADS, SUBLANES, t), F32),
                        pltpu.VMEM((NSA_HEADS, SUBLANES, t), F32),
                        pltpu.VMEM((NSA_HEADS, VT_ROWS, t), F32)],
        compiler_params=_params(2),
    )(nsa_q, mask_bias, k2, e2, vt, bias)


def _win_kernel(q_ref, k_ref, vt_ref, bias_ref, o_ref, qs_scr, s_scr, tmax_scr, m_scr, acc_scr, *, n_back):
    t = WIN_TILE
    i = pl.program_id(1)
    lo = _lane_lo((t, LANES))
    n_groups = NSA_HEADS // 2
    for g in range(n_groups):
        qg = q_ref[0, :, g * LANES:(g + 1) * LANES]
        zero = jnp.zeros_like(qg)
        qs_scr[2 * g] = jnp.where(lo, qg, zero)
        qs_scr[2 * g + 1] = jnp.where(lo, zero, qg)
    _flash_init(m_scr, acc_scr)
    first = jnp.maximum(i - n_back, 0)

    def qk_scores(j):
        k_tile = k_ref[0, pl.ds(pl.multiple_of((first + j) * t, t), t), :]
        return [functools.partial(lambda pos: _dot_nt(k_tile, qs_scr[pos]), pos) for pos in range(NSA_HEADS)]

    def bias_tile(pos, j):
        return bias_ref[pos, first + j - i + n_back]

    def vt_slab(pos, j):
        kv = pos % 2
        return vt_ref[0, kv * VT_ROWS:(kv + 1) * VT_ROWS, pl.ds(pl.multiple_of((first + j) * t, t), t)]

    _flash_pipeline(i - first, NSA_HEADS, qk_scores, bias_tile, vt_slab, s_scr, tmax_scr, m_scr, acc_scr)
    _flash_finish(o_ref, n_groups, acc_scr)


def _win_attention(nsa_q, k2, vt, bias):
    b, s, width = nsa_q.shape
    t = bias.shape[2]
    return pl.pallas_call(
        functools.partial(_win_kernel, n_back=bias.shape[1] - 1),
        grid=(b, s // t),
        in_specs=[pl.BlockSpec((1, t, width), lambda i, j: (i, j, 0)),
                  pl.BlockSpec((1, s, LANES), lambda i, j: (i, 0, 1)),
                  pl.BlockSpec((1, 2 * VT_ROWS, s), lambda i, j: (i, VT_WIN_BLOCK, 0)),
                  pl.BlockSpec(bias.shape, lambda i, j: (0, 0, 0, 0))],
        out_specs=pl.BlockSpec((1, t, width), lambda i, j: (i, j, 0)),
        out_shape=jax.ShapeDtypeStruct((b, s, width), BF16),
        scratch_shapes=[pltpu.VMEM((NSA_HEADS, t, LANES), BF16),
                        pltpu.VMEM((2, NSA_HEADS, t, t), F32),
                        pltpu.VMEM((2, NSA_HEADS, SUBLANES, t), F32),
                        pltpu.VMEM((NSA_HEADS, SUBLANES, t), F32),
                        pltpu.VMEM((NSA_HEADS, VT_ROWS, t), F32)],
        compiler_params=_params(2),
    )(nsa_q, k2, vt, bias)


def _block_onehot(s_len):
    blk = np.arange(s_len)[:, None] // SLC_BLOCK
    lane = np.arange(LANES)[None, :] % HEAD_DIM
    return jnp.asarray((blk == lane).astype(np.float32), dtype=BF16)


def _mix_ffn_kernel(x_ref, mod_ref, swa_ref, fox_ref, cmp_ref, slc_ref, win_ref, misc_ref, expand_ref, gn_ref,
                    w_ref, post_ref, pre_ref, wg_ref, wu_ref, wd_ref, fpost_ref, o_ref):
    n_swa = SWA_HEADS * HEAD_DIM
    n_fox = FOX_HEADS * HEAD_DIM
    n_nsa = NSA_HEADS * HEAD_DIM
    gate = jax.nn.sigmoid(misc_ref[0])
    gate_hi = gate.astype(BF16)
    gate_lo = (gate - gate_hi.astype(F32)).astype(BF16)
    gates = (jnp.dot(gate_hi, expand_ref[...], preferred_element_type=F32)
             + jnp.dot(gate_lo, expand_ref[...], preferred_element_type=F32))
    o_nsa = (gates[:, 0:n_nsa] * cmp_ref[0].astype(F32) + gates[:, n_nsa:2 * n_nsa] * slc_ref[0].astype(F32)
             + gates[:, 2 * n_nsa:3 * n_nsa] * win_ref[0].astype(F32))
    a = _rms(swa_ref[0].astype(F32), gn_ref[:, 0:n_swa]).astype(BF16)
    b = _rms(fox_ref[0].astype(F32), gn_ref[:, n_swa:n_swa + n_fox]).astype(BF16)
    c = _rms(o_nsa, gn_ref[:, n_swa + n_fox:]).astype(BF16)
    y = (jnp.dot(a, w_ref[0:n_swa, :], preferred_element_type=F32)
         + jnp.dot(b, w_ref[n_swa:n_swa + n_fox, :], preferred_element_type=F32)
         + jnp.dot(c, w_ref[n_swa + n_fox:, :], preferred_element_type=F32))
    x = x_ref[0] + mod_ref[0, 2:3, :] * _rms(y, post_ref[...])

    h = (_rms(x, pre_ref[...]) * (1.0 + mod_ref[0, 4:5, :]) + mod_ref[0, 3:4, :]).astype(BF16)
    y = jnp.zeros(x.shape, F32)
    for chunk in range(wg_ref.shape[1] // FFN_CHUNK):
        cols = slice(chunk * FFN_CHUNK, (chunk + 1) * FFN_CHUNK)
        gate = jnp.dot(h, wg_ref[:, cols], preferred_element_type=F32)
        up = jnp.dot(h, wu_ref[:, cols], preferred_element_type=F32)
        act = (gate * jax.nn.sigmoid(gate) * up).astype(BF16)
        y = y + jnp.dot(act, wd_ref[cols, :], preferred_element_type=F32)
    o_ref[0] = x + mod_ref[0, 5:6, :] * _rms(y, fpost_ref[...])


def _gate_expansion():
    expand = np.zeros((LANES, 3 * NSA_HEADS * HEAD_DIM), np.float32)
    for branch in range(3):
        for p in range(NSA_HEADS):
            col = (branch * NSA_HEADS + p) * HEAD_DIM
            expand[GATE_LANE + 8 * branch + p, col:col + HEAD_DIM] = 1.0
    return jnp.asarray(expand, dtype=BF16)


def _mix_ffn(x, mod, o_swa, o_fox, o_cmp, o_slc, o_win, misc, gn, w, post, pre, wg, wu, wd, fpost, layer):
    b, s, d = x.shape
    expand = _gate_expansion()
    hidden = wg.shape[2]

    def rows(width):
        return pl.BlockSpec((1, ROW_TILE, width), lambda i, j: (i, j, 0))

    def whole(shape):
        return pl.BlockSpec(shape, lambda i, j: (0,) * len(shape), pipeline_mode=pl.Buffered(1))

    def of_layer(shape):
        return pl.BlockSpec((None,) + shape, lambda i, j: (layer,) + (0,) * len(shape),
                            pipeline_mode=pl.Buffered(1))

    vec = pl.BlockSpec((None, 1, d), lambda i, j: (layer, 0, 0))
    return pl.pallas_call(
        _mix_ffn_kernel,
        grid=(b, s // ROW_TILE),
        in_specs=[rows(d),
                  pl.BlockSpec((None, 1, ADA_CHUNKS, d), lambda i, j: (layer, i, 0, 0)),
                  rows(o_swa.shape[2]), rows(o_fox.shape[2]), rows(o_cmp.shape[2]), rows(o_slc.shape[2]),
                  rows(o_win.shape[2]), rows(LANES),
                  whole(expand.shape), vec, of_layer((d, d)), vec, vec,
                  of_layer((d, hidden)), of_layer((d, hidden)), of_layer((hidden, d)),
                  vec],
        out_specs=rows(d),
        out_shape=jax.ShapeDtypeStruct((b, s, d), F32),
        compiler_params=_params(2),
    )(x, mod, o_swa, o_fox, o_cmp, o_slc, o_win, misc, expand, gn, w, post, pre, wg, wu, wd, fpost)


def _forget_lanes():
    lanes, heads = [], []
    for h in range(FOX_HEADS):
        base = (h // 2) * LANES + (HEAD_DIM if h % 2 == 0 else 0)
        for j in range(KEY_BIAS_TERMS):
            lanes.append(base + j)
            heads.append(h)
    return np.array(lanes), np.array(heads)


def _in_proj_layout():
    d = HEAD_DIM
    o_qa, o_ka, o_va, o_qb, o_kb, o_vb, o_fb, o_qc = 0, 256, 384, 512, 768, 1024, 1280, 1284
    o_kc, o_vc, o_ksl, o_vsl, o_kw, o_vw, o_gc = 1796, 1924, 2052, 2180, 2308, 2436, 2564
    scale = LOG2E / math.sqrt(d)

    def head_cols(base, heads):
        return np.concatenate([np.arange(base + h * d, base + (h + 1) * d) for h in heads])

    def span(base, width):
        return np.arange(base, base + width)

    cols = [head_cols(o_qa, SWA_POS), span(o_ka, 128),
            span(o_qb, 256), span(o_kb, 256),
            head_cols(o_qc, NSA_POS),
            span(o_kc, 128), span(o_vc, 128),
            span(o_ksl, 128), span(o_kw, 128),
            span(o_vb, 256), span(o_va, 128), span(o_vsl, 128), span(o_vw, 128)]
    scales = [np.full(256, scale), np.ones(128), np.full(256, scale), np.ones(256), np.full(512, scale),
              np.ones(256), np.ones(256), np.ones(640)]
    lanes, heads = _forget_lanes()
    misc_cols = np.zeros(SEG_MISC[1] - SEG_MISC[0], np.int64)
    misc_scale = np.zeros(SEG_MISC[1] - SEG_MISC[0])
    misc_cols[lanes] = o_fb + heads
    misc_scale[lanes] = 1.0
    for branch in range(3):
        for p, h in enumerate(NSA_POS):
            misc_cols[GATE_LANE + 8 * branch + p] = o_gc + h * 3 + branch
            misc_scale[GATE_LANE + 8 * branch + p] = 1.0
    cols.append(misc_cols)
    scales.append(misc_scale)
    return np.concatenate(cols), np.concatenate(scales).astype(np.float32)


def _head_perm(pos):
    return np.concatenate([np.arange(h * HEAD_DIM, (h + 1) * HEAD_DIM) for h in pos])


def kernel(x, c, rel_bias, ada_w, ada_b, attn_pre_norm, attn_post_norm, ffn_pre_norm, ffn_post_norm, w_in,
           forget_bias, swa_sinks, cmp_pos, cmp_w1, cmp_w2, group_norm, w_out, ffn_w_gate, ffn_w_up, ffn_w_down):
    b, s, d = x.shape
    depth = w_in.shape[0]
    hidden = ffn_w_gate.shape[2]
    assert s % (2 * FLASH_T) == 0 and s // SLC_BLOCK <= HEAD_DIM and hidden % FFN_CHUNK == 0

    cols, scales = _in_proj_layout()
    n_main = SEG_MISC[0]
    breaks = np.flatnonzero(np.diff(cols[:n_main]) != 1) + 1
    runs = np.split(cols[:n_main], breaks)
    w_main = (jnp.concatenate([w_in[:, :, int(r[0]):int(r[-1]) + 1] for r in runs], axis=2)
              * scales[:n_main]).astype(BF16)
    used = np.flatnonzero(scales[n_main:] != 0)
    w_misc = jnp.zeros((depth, d, SEG_MISC[1] - SEG_MISC[0]), w_in.dtype).at[:, :, used].set(
        w_in[:, :, cols[n_main:][used]]).astype(BF16)
    lanes, heads = _forget_lanes()
    fbias_all = jnp.zeros((depth, 1, SEG_MISC[1] - SEG_MISC[0]), F32).at[:, 0, lanes].set(
        forget_bias[:, heads].astype(F32))
    swa_perm = _head_perm(SWA_POS)
    nsa_perm = _head_perm(NSA_POS)
    n_swa, n_fox = SWA_HEADS * HEAD_DIM, FOX_HEADS * HEAD_DIM
    mix_perm = np.concatenate([swa_perm, n_swa + np.arange(n_fox), n_swa + n_fox + nsa_perm])
    w_out_all = w_out[:, mix_perm, :].astype(BF16)
    wg_all = ffn_w_gate.astype(BF16)
    wu_all = ffn_w_up.astype(BF16)
    wd_all = ffn_w_down.astype(BF16)
    cmp_weights = _compress_weights(cmp_pos, cmp_w1, cmp_w2)

    def stacked(v):
        return v.astype(F32).reshape(depth, 1, v.shape[1])

    gn_all = stacked(group_norm[:, mix_perm])
    attn_pre, attn_post = stacked(attn_pre_norm), stacked(attn_post_norm)
    ffn_pre, ffn_post = stacked(ffn_pre_norm), stacked(ffn_post_norm)
    sinks_all = swa_sinks[:, np.array(SWA_POS)].astype(F32)

    tab_swa = rel_bias[:, np.array(SWA_POS)].astype(F32)
    tab_nsa = rel_bias[:, SWA_HEADS + np.array(NSA_POS)].astype(F32)
    bias_swa = _bias_table(tab_swa, _band_buckets_t(SWA_TILE, SWA_WINDOW))
    bias_win = _bias_table(tab_nsa, _band_buckets_t(WIN_TILE, NSA_WINDOW))
    bias_slc = _bias_table(tab_nsa, _toeplitz_buckets_t(FLASH_T, _near_tiles(FLASH_T)), subtract_last=True)
    n_rows = s // CMP_STRIDE
    bias_cmp = _bias_table(tab_nsa, _cmp_buckets_t(s, n_rows))
    e2 = _block_onehot(s)

    mod_all = _adaln(c.astype(F32), ada_w.astype(F32), ada_b.astype(F32)).reshape(depth, b, ADA_CHUNKS, d)

    for layer in range(depth):
        swa_qk, fox_qk, nsa_q, kc, vc, k2, misc, vt = _in_proj(x, mod_all, attn_pre, w_main, w_misc, layer)
        o_swa = _banded_attention(swa_qk, swa_qk, 2, vt, VT_SWA_BLOCK, bias_swa, sinks=sinks_all, layer=layer)
        o_fox = _fox_attention(fox_qk, _fox_key_terms(misc, fbias_all, layer), vt)
        k_cmp = _compress(kc, cmp_weights, layer, 0)
        v_cmp = _compress(vc, cmp_weights, layer, 1)
        o_cmp, mask_bias = _select(nsa_q, k_cmp, v_cmp, bias_cmp)
        o_slc = _slc_attention(nsa_q, mask_bias, k2, e2, vt, bias_slc)
        o_win = _win_attention(nsa_q, k2, vt, bias_win)
        x = _mix_ffn(x, mod_all, o_swa, o_fox, o_cmp, o_slc, o_win, misc, gn_all, w_out_all, attn_post, ffn_pre,
                     wg_all, wu_all, wd_all, ffn_post, layer)
    return x
```
